```python
import math
import jax, jax.numpy as jnp
from jax import lax
import numpy as np

D_MODEL = 2048
BATCH = 4
SEQ = 2048
DEPTH = 1
DEC_BATCH = 32
DEC_SEQ = 1
PAST_LEN = 8192
PAGE_SIZE = 128

N_ATTN_HEADS = 16
HEAD_DIM = 64
ATTN_WIDTH = N_ATTN_HEADS * HEAD_DIM
MOBA_BLOCK = 256
MOBA_TOPK = 3
Q_SWEEP = 16
D_INNER = D_MODEL
SSM_HEADDIM = 64
N_SSM_HEADS = D_INNER // SSM_HEADDIM
N_SSM_GROUPS = 8
D_STATE = 128
CONV_WIDTH = 4
CONV_DIM = D_INNER + 2 * N_SSM_GROUPS * D_STATE
SSD_CHUNK = 128
D_FF = -(-8 * D_MODEL // (3 * 256)) * 256
EPS = 1e-6
IN_PROJ_DIM = 3 * ATTN_WIDTH + D_INNER + CONV_DIM + N_SSM_HEADS + 2 * D_MODEL

kernel_name = "moba_mamba2_gated_parallel_decoder_step"


def _split_points():
    sizes = [ATTN_WIDTH, ATTN_WIDTH, ATTN_WIDTH, D_INNER, CONV_DIM, N_SSM_HEADS, D_MODEL, D_MODEL]
    return [int(s) for s in np.cumsum(sizes)[:-1]]


def rmsnorm(x, g):
    xf = x.astype(jnp.float32)
    y = xf * lax.rsqrt(jnp.mean(xf * xf, axis=-1, keepdims=True) + EPS)
    return (y * g.astype(jnp.float32)).astype(x.dtype)


def swiglu(h, w_gate, w_up, w_down):
    return (jax.nn.silu(h @ w_gate) * (h @ w_up)) @ w_down


def moba_attention(q, k, v, q_pos0):
    f32 = jnp.float32
    b, Lq, H, d = q.shape
    Lk = k.shape[1]
    nb = -(-Lk // MOBA_BLOCK)
    pad = nb * MOBA_BLOCK - Lk
    kp = jnp.pad(k, ((0, 0), (0, pad), (0, 0), (0, 0)))
    vp = jnp.pad(v, ((0, 0), (0, pad), (0, 0), (0, 0)))
    kb = kp.reshape(b, nb, MOBA_BLOCK, H, d).transpose(0, 3, 1, 2, 4)
    vb = vp.reshape(b, nb, MOBA_BLOCK, H, d).transpose(0, 3, 1, 2, 4)
    q_pos = q_pos0 + jnp.arange(Lq, dtype=jnp.int32)
    q_blk = q_pos // MOBA_BLOCK
    qh = q.transpose(0, 2, 1, 3)
    own = jnp.broadcast_to(q_blk[None, None, :, None], (b, H, Lq, 1))
    n_sel = min(MOBA_TOPK, nb - 1)
    if n_sel > 0:
        kmean = jnp.mean(kb.astype(f32), axis=3)
        gate = jnp.einsum('bhqd,bhnd->bhqn', qh.astype(f32), kmean)
        past = jnp.arange(nb, dtype=jnp.int32)[None, :] < q_blk[:, None]
        gate = jnp.where(past[None, None], gate, -jnp.inf)
        _, sel = lax.top_k(gate, n_sel)
        sel = sel.astype(jnp.int32)
        sel_ok = sel < q_blk[None, None, :, None]
        blk_idx = jnp.concatenate([sel, own], axis=-1)
        blk_ok = jnp.concatenate([sel_ok, jnp.ones_like(own, dtype=bool)], axis=-1)
    else:
        blk_idx = own
        blk_ok = jnp.ones_like(own, dtype=bool)

    qc = Q_SWEEP if Lq % Q_SWEEP == 0 else Lq
    nq = Lq // qc

    def to_chunks(t):
        return jnp.moveaxis(t.reshape(b, H, nq, qc, *t.shape[3:]), 2, 0)

    bi = jnp.arange(b)[:, None, None, None]
    hi = jnp.arange(H)[None, :, None, None]
    key_off = jnp.arange(MOBA_BLOCK, dtype=jnp.int32)
    scale = d ** -0.5

    def attend(args):
        qch, idx, ok, pos = args
        ks = kb[bi, hi, idx]
        vs = vb[bi, hi, idx]
        s = jnp.einsum('bhqd,bhqkjd->bhqkj', qch.astype(f32), ks.astype(f32)) * scale
        kpos = idx[..., None] * MOBA_BLOCK + key_off
        mask = ok[..., None] & (kpos <= pos[None, None, :, None, None])
        s = jnp.where(mask, s, -jnp.inf)
        p = jax.nn.softmax(s.reshape(b, H, qc, -1), axis=-1).reshape(s.shape)
        return jnp.einsum('bhqkj,bhqkjd->bhqd', p, vs.astype(f32))

    out = lax.map(attend, (to_chunks(qh), to_chunks(blk_idx), to_chunks(blk_ok), q_pos.reshape(nq, qc)))
    out = jnp.moveaxis(out, 0, 2).reshape(b, H, Lq, d).transpose(0, 2, 1, 3).reshape(b, Lq, H * d)
    return out.astype(q.dtype)


def causal_dwconv(xbc, prev, w, bias):
    L = xbc.shape[1]
    xp = jnp.concatenate([prev.astype(xbc.dtype), xbc], axis=1)
    y = bias + sum(w[i] * xp[:, i:i + L] for i in range(CONV_WIDTH))
    return y, xp[:, -(CONV_WIDTH - 1):]


def ssd_scan(x, dt, A, Bm, Cm, h0, chunk):
    f32 = jnp.float32
    b, L, H, P = x.shape
    G, N = Bm.shape[2], Bm.shape[3]
    R = H // G
    nc = L // chunk
    xc = x.astype(f32).reshape(b, nc, chunk, G, R, P)
    dtc = dt.astype(f32).reshape(b, nc, chunk, G, R)
    Bc = Bm.astype(f32).reshape(b, nc, chunk, G, N)
    Cc = Cm.astype(f32).reshape(b, nc, chunk, G, N)
    acum = jnp.cumsum(dtc * A.astype(f32).reshape(G, R), axis=2)
    seg = acum[:, :, :, None] - acum[:, :, None, :]
    causal = jnp.tril(jnp.ones((chunk, chunk), dtype=bool))[:, :, None, None]
    decay = jnp.exp(jnp.where(causal, seg, -jnp.inf))
    xdt = xc * dtc[..., None]
    cb = jnp.einsum('bcign,bcjgn->bcijg', Cc, Bc)
    y_diag = jnp.einsum('bcijg,bcijgr,bcjgrp->bcigrp', cb, decay, xdt)
    decay_end = jnp.exp(acum[:, :, -1:] - acum)
    states = jnp.einsum('bcjgn,bcjgr,bcjgrp->bcgrpn', Bc, decay_end, xdt)
    chunk_decay = jnp.exp(acum[:, :, -1])

    def step(h, inp):
        s, dcy = inp
        return dcy[..., None, None] * h + s, h

    h_init = h0.astype(f32).reshape(b, G, R, P, N)
    h_final, h_starts = lax.scan(step, h_init, (jnp.moveaxis(states, 1, 0), jnp.moveaxis(chunk_decay, 1, 0)))
    h_starts = jnp.moveaxis(h_starts, 0, 1)
    y_off = jnp.einsum('bcign,bcgrpn,bcigr->bcigrp', Cc, h_starts, jnp.exp(acum))
    return (y_diag + y_off).reshape(b, L, H, P), h_final.reshape(b, H, P, N)


def mamba2_branch(z, xbc, dt_raw, conv_prev, ssm_prev, conv_w, conv_b, dt_bias, a_log, d_skip, ssm_norm):
    f32 = jnp.float32
    b, L, _ = z.shape
    xbc_c, conv_new = causal_dwconv(xbc, conv_prev, conv_w, conv_b)
    xbc_c = jax.nn.silu(xbc_c)
    xs, Bm, Cm = jnp.split(xbc_c, [D_INNER, D_INNER + N_SSM_GROUPS * D_STATE], axis=-1)
    xs = xs.reshape(b, L, N_SSM_HEADS, SSM_HEADDIM)
    Bm = Bm.reshape(b, L, N_SSM_GROUPS, D_STATE)
    Cm = Cm.reshape(b, L, N_SSM_GROUPS, D_STATE)
    dt = jax.nn.softplus(dt_raw.astype(f32) + dt_bias.astype(f32))
    A = -jnp.exp(a_log.astype(f32))
    chunk = SSD_CHUNK if L % SSD_CHUNK == 0 else L
    y, h_new = ssd_scan(xs, dt, A, Bm, Cm, ssm_prev, chunk)
    y = y + d_skip.astype(f32)[:, None] * xs.astype(f32)
    u = (y.reshape(b, L, D_INNER) * jax.nn.silu(z.astype(f32))).reshape(b, L, N_SSM_GROUPS, D_INNER // N_SSM_GROUPS)
    u = u * lax.rsqrt(jnp.mean(u * u, axis=-1, keepdims=True) + EPS)
    u = u.reshape(b, L, D_INNER) * ssm_norm.astype(f32)
    return u.astype(z.dtype), conv_new, h_new.astype(ssm_prev.dtype)


def _layer(x, k_past, v_past, conv_prev, ssm_prev, q_pos0,
           norm_mix_pre, w_in, conv_w, conv_b, dt_bias, a_log, d_skip, ssm_norm,
           w_attn_out, w_ssm_out, w_out, norm_mix_post,
           norm_ffn_pre, w_gate, w_up, w_down, norm_ffn_post):
    b, L, _ = x.shape
    h = rmsnorm(x, norm_mix_pre)
    proj = h @ w_in
    q, k, v, z, xbc, dt_raw, g_attn, g_ssm = jnp.split(proj, _split_points(), axis=-1)
    q = q.reshape(b, L, N_ATTN_HEADS, HEAD_DIM)
    k = k.reshape(b, L, N_ATTN_HEADS, HEAD_DIM)
    v = v.reshape(b, L, N_ATTN_HEADS, HEAD_DIM)
    if k_past is None:
        k_all, v_all = k, v
    else:
        k_all = jnp.concatenate([k_past.astype(k.dtype), k], axis=1)
        v_all = jnp.concatenate([v_past.astype(v.dtype), v], axis=1)
    attn = moba_attention(q, k_all, v_all, q_pos0)
    ssm, conv_new, ssm_new = mamba2_branch(z, xbc, dt_raw, conv_prev, ssm_prev,
                                           conv_w, conv_b, dt_bias, a_log, d_skip, ssm_norm)
    merged = jax.nn.sigmoid(g_attn) * (attn @ w_attn_out) + jax.nn.sigmoid(g_ssm) * (ssm @ w_ssm_out)
    x = x + rmsnorm(merged @ w_out, norm_mix_post)
    f = swiglu(rmsnorm(x, norm_ffn_pre), w_gate, w_up, w_down)
    x = x + rmsnorm(f, norm_ffn_post)
    return x, k, v, conv_new, ssm_new


def setup_inputs(seed: int = 0) -> dict:
    key = jax.random.key(seed)
    ks = jax.random.split(key, 28)
    f32 = jnp.float32
    n_pages = PAST_LEN // PAGE_SIZE
    n_used = DEC_BATCH * n_pages
    n_phys = n_used + (n_used + 3) // 4

    def nrm(k, shape, scale=1.0):
        return scale * jax.random.normal(k, shape, f32)

    def gain(k, n):
        return 1.0 + 0.05 * jax.random.normal(k, (DEPTH, n), f32)

    page_table = jax.random.permutation(ks[6], n_phys)[:n_used].reshape(DEC_BATCH, n_pages).astype(jnp.int32)
    dt0 = jnp.exp(jax.random.uniform(ks[11], (DEPTH, N_SSM_HEADS), f32, math.log(1e-3), math.log(1e-1)))
    dt_bias = dt0 + jnp.log(-jnp.expm1(-dt0))
    a_log = jnp.log(jax.random.uniform(ks[12], (DEPTH, N_SSM_HEADS), f32, 1.0, 16.0))
    return {
        "x_prompt": nrm(ks[0], (BATCH, SEQ, D_MODEL)),
        "x_sample": nrm(ks[1], (DEC_BATCH, DEC_SEQ, D_MODEL)),
        "cache_k": nrm(ks[2], (DEPTH, n_phys, PAGE_SIZE, N_ATTN_HEADS, HEAD_DIM)),
        "cache_v": nrm(ks[3], (DEPTH, n_phys, PAGE_SIZE, N_ATTN_HEADS, HEAD_DIM)),
        "state_conv": nrm(ks[4], (DEPTH, DEC_BATCH, CONV_WIDTH - 1, CONV_DIM)),
        "state_ssm": nrm(ks[5], (DEPTH, DEC_BATCH, N_SSM_HEADS, SSM_HEADDIM, D_STATE), 0.3),
        "page_table": page_table,
        "norm_mix_pre": gain(ks[7], D_MODEL),
        "w_in": nrm(ks[8], (DEPTH, D_MODEL, IN_PROJ_DIM), D_MODEL ** -0.5),
        "conv_w": nrm(ks[9], (DEPTH, CONV_WIDTH, CONV_DIM), CONV_WIDTH ** -0.5),
        "conv_b": nrm(ks[10], (DEPTH, CONV_DIM), 0.01),
        "dt_bias": dt_bias,
        "a_log": a_log,
        "d_skip": 1.0 + 0.05 * jax.random.normal(ks[13], (DEPTH, N_SSM_HEADS), f32),
        "ssm_norm": gain(ks[14], D_INNER),
        "w_attn_out": nrm(ks[15], (DEPTH, ATTN_WIDTH, D_MODEL), ATTN_WIDTH ** -0.5),
        "w_ssm_out": nrm(ks[16], (DEPTH, D_INNER, D_MODEL), D_INNER ** -0.5),
        "w_out": nrm(ks[17], (DEPTH, D_MODEL, D_MODEL), D_MODEL ** -0.5),
        "norm_mix_post": gain(ks[18], D_MODEL),
        "norm_ffn_pre": gain(ks[19], D_MODEL),
        "w_gate": nrm(ks[20], (DEPTH, D_MODEL, D_FF), D_MODEL ** -0.5),
        "w_up": nrm(ks[21], (DEPTH, D_MODEL, D_FF), D_MODEL ** -0.5),
        "w_down": nrm(ks[22], (DEPTH, D_FF, D_MODEL), D_FF ** -0.5),
        "norm_ffn_post": gain(ks[23], D_MODEL),
    }


def reference(x_prompt, x_sample, cache_k, cache_v, state_conv, state_ssm, page_table,
              norm_mix_pre, w_in, conv_w, conv_b, dt_bias, a_log, d_skip, ssm_norm,
              w_attn_out, w_ssm_out, w_out, norm_mix_post,
              norm_ffn_pre, w_gate, w_up, w_down, norm_ffn_post):
    b_p = x_prompt.shape[0]
    b_s = x_sample.shape[0]
    n_pages = page_table.shape[1]
    past_len = n_pages * PAGE_SIZE
    y_prompt, y_sample = x_prompt, x_sample
    kp_list, vp_list, cp_list, sp_list = [], [], [], []
    ks_list, vs_list, cs_list, ss_list = [], [], [], []
    for l in range(DEPTH):
        lw = (norm_mix_pre[l], w_in[l], conv_w[l], conv_b[l], dt_bias[l], a_log[l], d_skip[l], ssm_norm[l],
              w_attn_out[l], w_ssm_out[l], w_out[l], norm_mix_post[l],
              norm_ffn_pre[l], w_gate[l], w_up[l], w_down[l], norm_ffn_post[l])
        conv0 = jnp.zeros((b_p, CONV_WIDTH - 1, CONV_DIM), x_prompt.dtype)
        ssm0 = jnp.zeros((b_p, N_SSM_HEADS, SSM_HEADDIM, D_STATE), state_ssm.dtype)
        y_prompt, k_p, v_p, c_p, s_p = _layer(y_prompt, None, None, conv0, ssm0, 0, *lw)
        k_past = cache_k[l][page_table].reshape(b_s, past_len, N_ATTN_HEADS, HEAD_DIM)
        v_past = cache_v[l][page_table].reshape(b_s, past_len, N_ATTN_HEADS, HEAD_DIM)
        y_sample, k_s, v_s, c_s, s_s = _layer(y_sample, k_past, v_past, state_conv[l], state_ssm[l], past_len, *lw)
        kp_list.append(k_p); vp_list.append(v_p); cp_list.append(c_p); sp_list.append(s_p)
        ks_list.append(k_s); vs_list.append(v_s); cs_list.append(c_s); ss_list.append(s_s)
    k_prompt = jnp.stack(kp_list)
    v_prompt = jnp.stack(vp_list)
    conv_prompt = jnp.stack(cp_list)
    ssm_prompt = jnp.stack(sp_list)
    k_sample = jnp.stack(ks_list)
    v_sample = jnp.stack(vs_list)
    conv_sample = jnp.stack(cs_list)
    ssm_sample = jnp.stack(ss_list)
    return (y_prompt, y_sample, k_prompt, v_prompt, conv_prompt, ssm_prompt, k_sample, v_sample, conv_sample, ssm_sample)
```

```python
import functools
import math

import jax
import jax.numpy as jnp
from jax import lax
from jax.experimental import pallas as pl
from jax.experimental.pallas import tpu as pltpu

F32 = jnp.float32
BF16 = jnp.bfloat16

EPS = 1e-6
HEAD_DIM = 64
MOBA_BLOCK = 256
MOBA_TOPK = 3
PAGE_SIZE = 128
SSM_HEADDIM = 64
N_SSM_GROUPS = 8
D_STATE = 128
CONV_WIDTH = 4
SSD_CHUNK = 128

LANES = 128
VMEM_LIMIT = 56 * 1024 * 1024
PROJ_TN = 512
FFN_TF = 512
SCAN_PAGES = 8

_NT = (((1,), (1,)), ((), ()))
_TN = (((0,), (0,)), ((), ()))


def _cparams(sem):
    return pltpu.CompilerParams(dimension_semantics=sem, vmem_limit_bytes=VMEM_LIMIT)


def _rms(x, g):
    return x * lax.rsqrt(jnp.mean(x * x, axis=-1, keepdims=True) + EPS) * g


def _silu(x):
    return x * jax.nn.sigmoid(x)


def _softplus(x):
    return jnp.maximum(x, 0.0) + jnp.log1p(jnp.exp(-jnp.abs(x)))


def _in_proj_kernel(x_ref, g_ref, wm_ref, wg_ref, wdt_ref,
                    q_ref, k_ref, v_ref, z_ref, xbc_ref, dt_ref, ga_ref, gs_ref,
                    h_scr, *, segs):
    j = pl.program_id(1)

    @pl.when(j == 0)
    def _():
        h = _rms(x_ref[...], g_ref[...]).astype(BF16)
        h_scr[...] = h
        dt_ref[...] = jnp.dot(h, wdt_ref[...], preferred_element_type=F32)

    outs = (q_ref, k_ref, v_ref, z_ref, xbc_ref, ga_ref, gs_ref)
    for out_ref, (lo, hi, main) in zip(outs, segs):
        w_ref = wm_ref if main else wg_ref

        @pl.when((j >= lo) & (j < hi))
        def _(out_ref=out_ref, w_ref=w_ref):
            out_ref[...] = jnp.dot(h_scr[...], w_ref[...], preferred_element_type=F32)


def _in_proj(x, g, w_main, w_gates, w_dt, widths, tm):
    M, D = x.shape
    tn = PROJ_TN
    n_main = w_main.shape[1] // tn
    n_gate = w_gates.shape[1] // tn
    segs, lo = [], 0
    for i, w in enumerate(widths):
        assert w % tn == 0
        segs.append((lo, lo + w // tn, i < 5))
        lo += w // tn
    assert segs[4][1] == n_main and lo == n_main + n_gate

    def omap(lo, hi):
        return lambda i, j: (i, jnp.clip(j - lo, 0, hi - lo - 1))

    out_shape = [jax.ShapeDtypeStruct((M, w), F32) for w in widths[:5]]
    out_specs = [pl.BlockSpec((tm, tn), omap(s[0], s[1])) for s in segs[:5]]
    out_shape.append(jax.ShapeDtypeStruct((M, LANES), F32))
    out_specs.append(pl.BlockSpec((tm, LANES), lambda i, j: (i, 0)))
    out_shape += [jax.ShapeDtypeStruct((M, w), F32) for w in widths[5:]]
    out_specs += [pl.BlockSpec((tm, tn), omap(s[0], s[1])) for s in segs[5:]]

    return pl.pallas_call(
        functools.partial(_in_proj_kernel, segs=tuple(segs)),
        grid=(M // tm, n_main + n_gate),
        in_specs=[
            pl.BlockSpec((tm, D), lambda i, j: (i, 0)),
            pl.BlockSpec((1, D), lambda i, j: (0, 0)),
            pl.BlockSpec((D, tn), lambda i, j: (0, jnp.minimum(j, n_main - 1))),
            pl.BlockSpec((D, tn), lambda i, j: (0, jnp.clip(j - n_main, 0, n_gate - 1))),
            pl.BlockSpec((D, LANES), lambda i, j: (0, 0)),
        ],
        out_specs=out_specs,
        out_shape=out_shape,
        scratch_shapes=[pltpu.VMEM((tm, D), BF16)],
        compiler_params=_cparams(("arbitrary", "arbitrary")),
        name="in_proj",
    )(x, g, w_main, w_gates, w_dt)


def _moba_prompt_kernel(q_ref, k_ref, v_ref, o_ref, *, L, scale):
    BS = MOBA_BLOCK
    nb = L // BS
    q = q_ref[...]
    k = k_ref[...]
    kb = k.astype(BF16)
    vb = v_ref[...].astype(BF16)
    lane = lax.broadcasted_iota(jnp.int32, (1, LANES), 1)
    head_lanes = (lane < HEAD_DIM, lane >= HEAD_DIM)

    kmean = jnp.concatenate(
        [jnp.mean(k[n * BS:(n + 1) * BS], axis=0, keepdims=True) for n in range(nb)], axis=0)

    blk = lax.broadcasted_iota(jnp.int32, (nb, L), 0)
    qblk = lax.broadcasted_iota(jnp.int32, (nb, L), 1) // BS
    past = blk < qblk
    row = lax.broadcasted_iota(jnp.int32, (BS, BS), 0)
    col = lax.broadcasted_iota(jnp.int32, (BS, BS), 1)
    causal = col <= row
    eye = (row == col).astype(BF16)

    outs = []
    for hm in head_lanes:
        gate = lax.dot_general(jnp.where(hm, kmean, 0.0), q, _NT,
                               precision=lax.Precision.HIGHEST, preferred_element_type=F32)
        gate = jnp.where(past, gate, -jnp.inf)
        rank = jnp.zeros((nb, L), jnp.int32)
        for m in range(nb):
            gm = gate[m:m + 1, :]
            beats = (gm > gate) | ((gm == gate) & (m < blk))
            rank = rank + beats.astype(jnp.int32)
        sel_t = (past & (rank < MOBA_TOPK)).astype(BF16)

        qh = jnp.where(hm, q, 0.0).astype(BF16)
        o_blocks = []
        for qb in range(nb):
            qs = slice(qb * BS, (qb + 1) * BS)
            nk = (qb + 1) * BS
            s = lax.dot_general(qh[qs], kb[:nk], _NT, preferred_element_type=F32) * scale
            parts = []
            if qb > 0:
                sel_q = lax.dot_general(eye, sel_t[:, qs], _NT, preferred_element_type=F32)
                for n in range(qb):
                    ok = sel_q[:, n:n + 1] > 0.5
                    parts.append(jnp.where(ok, s[:, n * BS:(n + 1) * BS], -jnp.inf))
            parts.append(jnp.where(causal, s[:, qb * BS:], -jnp.inf))
            s = jnp.concatenate(parts, axis=1) if len(parts) > 1 else parts[0]
            m = jnp.max(s, axis=1, keepdims=True)
            p = jnp.exp(s - m)
            l = jnp.sum(p, axis=1, keepdims=True)
            o = jnp.dot(p.astype(BF16), vb[:nk], preferred_element_type=F32)
            o_blocks.append(o / l)
        outs.append(jnp.concatenate(o_blocks, axis=0))
    o_ref[...] = jnp.where(head_lanes[0], outs[0], outs[1]).astype(o_ref.dtype)


def _moba_prompt(q, k, v, batch, L):
    M, W = q.shape
    assert L % MOBA_BLOCK == 0 and W % LANES == 0 and LANES == 2 * HEAD_DIM
    spec = pl.BlockSpec((L, LANES), lambda b, hp: (b, hp))
    return pl.pallas_call(
        functools.partial(_moba_prompt_kernel, L=L, scale=HEAD_DIM ** -0.5),
        grid=(batch, W // LANES),
        in_specs=[spec, spec, spec],
        out_specs=spec,
        out_shape=jax.ShapeDtypeStruct((M, W), BF16),
        compiler_params=_cparams(("arbitrary", "arbitrary")),
        name="moba_prompt",
    )(q, k, v)


def _ssd_prompt_kernel(xbc_ref, z_ref, dt_ref, cw_ref, cb_ref, dtb_r_ref, dtb_c_ref,
                       alog_r_ref, alog_c_ref, dskip_ref, norm_ref,
                       u_ref, conv_ref, st_ref, xs_scr, *, d_inner):
    Q = SSD_CHUNK
    P = SSM_HEADDIM
    N = D_STATE
    G = N_SSM_GROUPS
    R = d_inner // P // G
    assert R % 2 == 0 and 2 * P == LANES and N == LANES
    c = pl.program_id(1)
    tail = CONV_WIDTH - 1

    @pl.when(c == 0)
    def _():
        xs_scr[0:8, :] = jnp.zeros((8, xs_scr.shape[1]), F32)
        st_ref[...] = jnp.zeros(st_ref.shape, F32)

    xs_scr[8:8 + Q, :] = xbc_ref[...]
    acc = cw_ref[0:1, :] * xs_scr[8 - tail:8 - tail + Q, :]
    for i in range(1, CONV_WIDTH):
        acc = acc + cw_ref[i:i + 1, :] * xs_scr[8 - tail + i:8 - tail + i + Q, :]
    xc = _silu(cb_ref[...] + acc)
    last_rows = xbc_ref[Q - tail:Q, :]
    xs_scr[8 - tail:8, :] = last_rows
    conv_ref[0] = last_rows

    raw = dt_ref[...]
    dt = _softplus(raw + dtb_r_ref[...])
    dt_t = _softplus(raw.T + dtb_c_ref[...])
    a = dt * (-jnp.exp(alog_r_ref[...]))
    a_t = dt_t * (-jnp.exp(alog_c_ref[...]))
    ri = lax.broadcasted_iota(jnp.int32, (Q, Q), 0)
    ci = lax.broadcasted_iota(jnp.int32, (Q, Q), 1)
    causal = ci <= ri
    acum = jnp.dot(causal.astype(F32), a, precision=lax.Precision.HIGHEST,
                   preferred_element_type=F32)
    acum_t = jnp.dot(a_t, (ri <= ci).astype(F32), precision=lax.Precision.HIGHEST,
                     preferred_element_type=F32)
    e_acum = jnp.exp(acum)
    d_end = jnp.exp(acum[Q - 1:Q, :] - acum)
    lane = lax.broadcasted_iota(jnp.int32, (1, LANES), 1)
    first = lane < P

    def pair_cols(t, h0):
        return jnp.where(first, t[:, h0:h0 + 1], t[:, h0 + 1:h0 + 2])

    y_parts = []
    for g in range(G):
        b_g = xc[:, d_inner + g * N:d_inner + (g + 1) * N].astype(BF16)
        c_g = xc[:, d_inner + G * N + g * N:d_inner + G * N + (g + 1) * N].astype(BF16)
        cb = lax.dot_general(c_g, b_g, _NT, preferred_element_type=F32)
        st_g = st_ref[0, g * R:(g + 1) * R].reshape(R * P, N)
        y_off = lax.dot_general(c_g, st_g.astype(BF16), _NT, preferred_element_type=F32)
        xdtd_parts, cd_parts = [], []
        for pr in range(R // 2):
            h0 = g * R + 2 * pr
            x_p = xc[:, h0 * P:h0 * P + LANES]
            xdt = x_p * pair_cols(dt, h0)
            xdt_b = xdt.astype(BF16)
            yd = []
            for hh in (h0, h0 + 1):
                seg = acum[:, hh:hh + 1] - acum_t[hh:hh + 1, :]
                dec = jnp.exp(jnp.where(causal, seg, -jnp.inf))
                yd.append(jnp.dot((cb * dec).astype(BF16), xdt_b, preferred_element_type=F32))
                cd_parts.append(jnp.broadcast_to(jnp.exp(acum_t[hh:hh + 1, Q - 1:Q]), (P, N)))
            y_p = (jnp.where(first, yd[0], yd[1])
                   + y_off[:, 2 * pr * P:2 * pr * P + LANES] * pair_cols(e_acum, h0)
                   + dskip_ref[:, h0 * P:h0 * P + LANES] * x_p)
            y_parts.append(y_p)
            xdtd_parts.append((xdt * pair_cols(d_end, h0)).astype(BF16))
        xdtd = jnp.concatenate(xdtd_parts, axis=1)
        s_new = lax.dot_general(xdtd, b_g, _TN, preferred_element_type=F32)
        st_new = jnp.concatenate(cd_parts, axis=0) * st_g + s_new
        st_ref[0, g * R:(g + 1) * R] = st_new.reshape(R, P, N)

    y = jnp.concatenate(y_parts, axis=1)
    u = y * _silu(z_ref[...])
    gw = d_inner // G
    u_parts = []
    for g in range(G):
        ug = u[:, g * gw:(g + 1) * gw]
        u_parts.append(ug * lax.rsqrt(jnp.mean(ug * ug, axis=-1, keepdims=True) + EPS))
    u_ref[...] = (jnp.concatenate(u_parts, axis=1) * norm_ref[...]).astype(u_ref.dtype)


def _ssd_prompt(xbc, z, dt_raw, conv_w, conv_b, dtb_r, dtb_c, alog_r, alog_c, dskip, norm, batch, L):
    M, conv_dim = xbc.shape
    d_inner = z.shape[1]
    n_heads = d_inner // SSM_HEADDIM
    assert L % SSD_CHUNK == 0
    nc = L // SSD_CHUNK
    Q = SSD_CHUNK
    tail = CONV_WIDTH - 1
    row = lambda b, c: (b * nc + c, 0)
    const = lambda b, c: (0, 0)
    return pl.pallas_call(
        functools.partial(_ssd_prompt_kernel, d_inner=d_inner),
        grid=(batch, nc),
        in_specs=[
            pl.BlockSpec((Q, conv_dim), row),
            pl.BlockSpec((Q, d_inner), row),
            pl.BlockSpec((Q, LANES), row),
            pl.BlockSpec((CONV_WIDTH, conv_dim), const),
            pl.BlockSpec((1, conv_dim), const),
            pl.BlockSpec((1, LANES), const),
            pl.BlockSpec((LANES, 1), const),
            pl.BlockSpec((1, LANES), const),
            pl.BlockSpec((LANES, 1), const),
            pl.BlockSpec((1, d_inner), const),
            pl.BlockSpec((1, d_inner), const),
        ],
        out_specs=[
            pl.BlockSpec((Q, d_inner), row),
            pl.BlockSpec((1, tail, conv_dim), lambda b, c: (b, 0, 0)),
            pl.BlockSpec((1, n_heads, SSM_HEADDIM, D_STATE), lambda b, c: (b, 0, 0, 0)),
        ],
        out_shape=[
            jax.ShapeDtypeStruct((M, d_inner), BF16),
            jax.ShapeDtypeStruct((batch, tail, conv_dim), F32),
            jax.ShapeDtypeStruct((batch, n_heads, SSM_HEADDIM, D_STATE), F32),
        ],
        scratch_shapes=[pltpu.VMEM((8 + Q, conv_dim), F32)],
        compiler_params=_cparams(("arbitrary", "arbitrary")),
        name="ssd_prompt",
    )(xbc, z, dt_raw, conv_w, conv_b, dtb_r, dtb_c, alog_r, alog_c, dskip, norm)


def _ssd_step_kernel(xbc_ref, z_ref, dt_ref, cprev_ref, sprev_ref, cw_ref, cb_ref, dtb_ref,
                     alog_ref, dskip_ref, norm_ref, u_ref, conv_ref, st_ref, *, d_inner):
    P = SSM_HEADDIM
    N = D_STATE
    G = N_SSM_GROUPS
    n_heads = d_inner // P
    R = n_heads // G
    assert R % 2 == 0 and 2 * P == LANES and N == LANES
    tail = CONV_WIDTH - 1
    x_new = xbc_ref[0]
    prev = cprev_ref[0]
    acc = cw_ref[0:1, :] * prev[0:1, :]
    for i in range(1, tail):
        acc = acc + cw_ref[i:i + 1, :] * prev[i:i + 1, :]
    acc = acc + cw_ref[tail:tail + 1, :] * x_new
    xc = _silu(cb_ref[...] + acc)
    conv_ref[0, 0:tail - 1, :] = prev[1:tail, :]
    conv_ref[0, tail - 1:tail, :] = x_new

    dt = _softplus(dt_ref[0] + dtb_ref[...])
    decay = jnp.exp(dt * (-jnp.exp(alog_ref[...])))
    ri = lax.broadcasted_iota(jnp.int32, (LANES, LANES), 0)
    ci = lax.broadcasted_iota(jnp.int32, (LANES, LANES), 1)
    eye = ri == ci
    first_rows = lax.broadcasted_iota(jnp.int32, (LANES, 1), 0) < P

    y_parts = []
    for hp in range(n_heads // 2):
        h0 = 2 * hp
        g = h0 // R
        b_g = xc[:, d_inner + g * N:d_inner + (g + 1) * N]
        c_g = xc[:, d_inner + G * N + g * N:d_inner + G * N + (g + 1) * N]
        x_row = xc[:, h0 * P:h0 * P + LANES]
        x_col = jnp.sum(jnp.where(eye, jnp.broadcast_to(x_row, (LANES, LANES)), 0.0),
                        axis=1, keepdims=True)
        dt_col = jnp.where(first_rows, dt[:, h0:h0 + 1], dt[:, h0 + 1:h0 + 2])
        dec_col = jnp.where(first_rows, decay[:, h0:h0 + 1], decay[:, h0 + 1:h0 + 2])
        st = sprev_ref[0, h0:h0 + 2].reshape(2 * P, N)
        st_new = dec_col * st + (dt_col * x_col) * b_g
        st_ref[0, h0:h0 + 2] = st_new.reshape(2, P, N)
        y_col = jnp.sum(st_new * c_g, axis=1, keepdims=True)
        y_row = jnp.sum(jnp.where(eye, jnp.broadcast_to(y_col, (LANES, LANES)), 0.0),
                        axis=0, keepdims=True)
        y_parts.append(y_row + dskip_ref[:, h0 * P:h0 * P + LANES] * x_row)
    y = jnp.concatenate(y_parts, axis=1)
    u = y * _silu(z_ref[0])
    gw = d_inner // G
    u_parts = []
    for g in range(G):
        ug = u[:, g * gw:(g + 1) * gw]
        u_parts.append(ug * lax.rsqrt(jnp.mean(ug * ug, axis=-1, keepdims=True) + EPS))
    u_ref[0] = (jnp.concatenate(u_parts, axis=1) * norm_ref[...]).astype(u_ref.dtype)


def _ssd_step(xbc, z, dt_raw, conv_prev, ssm_prev, conv_w, conv_b, dtb_r, alog_r, dskip, norm):
    B, conv_dim = xbc.shape
    d_inner = z.shape[1]
    n_heads = d_inner // SSM_HEADDIM
    tail = CONV_WIDTH - 1
    const = lambda b: (0, 0)
    b3 = lambda b: (b, 0, 0)
    b4 = lambda b: (b, 0, 0, 0)
    u, conv_new, ssm_new = pl.pallas_call(
        functools.partial(_ssd_step_kernel, d_inner=d_inner),
        grid=(B,),
        in_specs=[
            pl.BlockSpec((1, 1, conv_dim), b3),
            pl.BlockSpec((1, 1, d_inner), b3),
            pl.BlockSpec((1, 1, LANES), b3),
            pl.BlockSpec((1, tail, conv_dim), b3),
            pl.BlockSpec((1, n_heads, SSM_HEADDIM, D_STATE), b4),
            pl.BlockSpec((CONV_WIDTH, conv_dim), const),
            pl.BlockSpec((1, conv_dim), const),
            pl.BlockSpec((1, LANES), const),
            pl.BlockSpec((1, LANES), const),
            pl.BlockSpec((1, d_inner), const),
            pl.BlockSpec((1, d_inner), const),
        ],
        out_specs=[
            pl.BlockSpec((1, 1, d_inner), b3),
            pl.BlockSpec((1, tail, conv_dim), b3),
            pl.BlockSpec((1, n_heads, SSM_HEADDIM, D_STATE), b4),
        ],
        out_shape=[
            jax.ShapeDtypeStruct((B, 1, d_inner), BF16),
            jax.ShapeDtypeStruct((B, tail, conv_dim), F32),
            jax.ShapeDtypeStruct((B, n_heads, SSM_HEADDIM, D_STATE), F32),
        ],
        compiler_params=_cparams(("arbitrary",)),
        name="ssd_step",
    )(xbc.reshape(B, 1, conv_dim), z.reshape(B, 1, d_inner), dt_raw.reshape(B, 1, LANES),
      conv_prev, ssm_prev, conv_w, conv_b, dtb_r, alog_r, dskip, norm)
    return u.reshape(B, d_inner), conv_new, ssm_new


def _moba_select_kernel(pt_ref, q_ref, *refs, n_blocks, n_sel):
    del pt_ref
    page_refs = refs[:SCAN_PAGES]
    idx_ref = refs[SCAN_PAGES]
    ksum = refs[SCAN_PAGES + 1]
    s = pl.program_id(1)
    ppb = MOBA_BLOCK // PAGE_SIZE
    bps = SCAN_PAGES // ppb
    for j in range(bps):
        tot = jnp.sum(page_refs[j * ppb][...], axis=0)
        for t in range(1, ppb):
            tot = tot + jnp.sum(page_refs[j * ppb + t][...], axis=0)
        ksum[s * bps + j] = tot

    @pl.when(s == pl.num_programs(1) - 1)
    def _():
        kmean = ksum[...] * (1.0 / MOBA_BLOCK)
        gate = jnp.sum(kmean * q_ref[...], axis=-1, keepdims=True)
        blk = lax.broadcasted_iota(jnp.int32, gate.shape, 0)
        rank = jnp.zeros(gate.shape, jnp.int32)
        for m in range(n_blocks):
            gm = gate[m:m + 1]
            beats = (gm > gate) | ((gm == gate) & (m < blk))
            rank = rank + beats.astype(jnp.int32)
        for r in range(n_sel):
            idx_ref[0, r] = jnp.sum(jnp.where(rank == r, blk, 0), axis=0)


def _moba_select(q3, cache_k, page_table, n_sel):
    B, H, d = q3.shape
    n_pages = page_table.shape[1]
    ppb = MOBA_BLOCK // PAGE_SIZE
    assert n_pages % SCAN_PAGES == 0 and SCAN_PAGES % ppb == 0
    n_blocks = n_pages // ppb

    def page_spec(p):
        return pl.BlockSpec((None, PAGE_SIZE, H, d),
                            lambda b, s, pt: (pt[b, s * SCAN_PAGES + p], 0, 0, 0))

    grid_spec = pltpu.PrefetchScalarGridSpec(
        num_scalar_prefetch=1,
        grid=(B, n_pages // SCAN_PAGES),
        in_specs=[pl.BlockSpec((1, H, d), lambda b, s, pt: (b, 0, 0))]
                 + [page_spec(p) for p in range(SCAN_PAGES)],
        out_specs=pl.BlockSpec((1, n_sel, H, 1), lambda b, s, pt: (b, 0, 0, 0)),
        scratch_shapes=[pltpu.VMEM((n_blocks, H, d), F32)],
    )
    return pl.pallas_call(
        functools.partial(_moba_select_kernel, n_blocks=n_blocks, n_sel=n_sel),
        grid_spec=grid_spec,
        out_shape=jax.ShapeDtypeStruct((B, n_sel, H, 1), jnp.int32),
        compiler_params=_cparams(("arbitrary", "arbitrary")),
        name="moba_select",
    )(page_table, q3, *([cache_k] * SCAN_PAGES))


def _moba_sample_kernel(pt_ref, idx_ref, q_ref, kn_ref, vn_ref, ck_ref, cv_ref, o_ref,
                        kbuf, vbuf, sem, *, n_sel, n_heads, scale):
    b = pl.program_id(0)
    ppb = MOBA_BLOCK // PAGE_SIZE

    def copies(h):
        out = []
        for r in range(n_sel):
            blk = idx_ref[b, r * n_heads + h]
            for t in range(ppb):
                page = pt_ref[b, blk * ppb + t]
                rows = pl.ds((r * ppb + t) * PAGE_SIZE, PAGE_SIZE)
                out.append(pltpu.make_async_copy(ck_ref.at[page, :, h, :], kbuf.at[h, rows, :], sem.at[0]))
                out.append(pltpu.make_async_copy(cv_ref.at[page, :, h, :], vbuf.at[h, rows, :], sem.at[1]))
        return out

    for h in range(n_heads):
        for cp in copies(h):
            cp.start()
    for h in range(n_heads):
        for cp in copies(h):
            cp.wait()

    def body(h, carry):
        qh = q_ref[0, h]
        kh = kbuf[h]
        vh = vbuf[h]
        s = jnp.sum(kh * qh, axis=-1, keepdims=True) * scale
        s_own = jnp.sum(kn_ref[0, h] * qh, axis=-1, keepdims=True) * scale
        m = jnp.maximum(jnp.max(s, axis=0, keepdims=True), s_own)
        p = jnp.exp(s - m)
        p_own = jnp.exp(s_own - m)
        l = jnp.sum(p, axis=0, keepdims=True) + p_own
        o = jnp.sum(p * vh, axis=0, keepdims=True) + p_own * vn_ref[0, h]
        o_ref[0, h] = o / l
        return carry

    lax.fori_loop(0, n_heads, body, 0)


def _moba_sample(q3, k3, v3, cache_k, cache_v, page_table, idx):
    B, H, d = q3.shape
    n_sel = idx.shape[1] // H
    new = pl.BlockSpec((1, H, 1, d), lambda b, pt, ix: (b, 0, 0, 0))
    q3, k3, v3 = (t.reshape(B, H, 1, d) for t in (q3, k3, v3))
    grid_spec = pltpu.PrefetchScalarGridSpec(
        num_scalar_prefetch=2,
        grid=(B,),
        in_specs=[new, new, new, pl.BlockSpec(memory_space=pl.ANY), pl.BlockSpec(memory_space=pl.ANY)],
        out_specs=new,
        scratch_shapes=[pltpu.VMEM((H, n_sel * MOBA_BLOCK, d), F32),
                        pltpu.VMEM((H, n_sel * MOBA_BLOCK, d), F32),
                        pltpu.SemaphoreType.DMA((2,))],
    )
    return pl.pallas_call(
        functools.partial(_moba_sample_kernel, n_sel=n_sel, n_heads=H, scale=d ** -0.5),
        grid_spec=grid_spec,
        out_shape=jax.ShapeDtypeStruct((B, H, 1, d), F32),
        compiler_params=_cparams(("arbitrary",)),
        name="moba_sample",
    )(page_table, idx, q3, k3, v3, cache_k, cache_v).reshape(B, H, d)


def _merge_kernel(attn_ref, ssm_ref, ga_ref, gs_ref, x_ref, wa_ref, ws_ref, wo_ref,
                  npost_ref, npre_ref, x1_ref, h2_ref):
    a = jnp.dot(attn_ref[...], wa_ref[...], preferred_element_type=F32)
    s = jnp.dot(ssm_ref[...], ws_ref[...], preferred_element_type=F32)
    merged = jax.nn.sigmoid(ga_ref[...]) * a + jax.nn.sigmoid(gs_ref[...]) * s
    o = jnp.dot(merged.astype(BF16), wo_ref[...], preferred_element_type=F32)
    x1 = x_ref[...] + _rms(o, npost_ref[...])
    x1_ref[...] = x1
    h2_ref[...] = _rms(x1, npre_ref[...]).astype(BF16)


def _merge(attn, ssm, ga, gs, x, wa, ws, wo, npost, npre, tm):
    M, D = x.shape
    rows = lambda w: pl.BlockSpec((tm, w), lambda i: (i, 0))
    whole = lambda a: pl.BlockSpec(a.shape, lambda i: (0, 0), pipeline_mode=pl.Buffered(1))
    return pl.pallas_call(
        _merge_kernel,
        grid=(M // tm,),
        in_specs=[rows(attn.shape[1]), rows(ssm.shape[1]), rows(D), rows(D), rows(D),
                  whole(wa), whole(ws), whole(wo), whole(npost), whole(npre)],
        out_specs=[rows(D), rows(D)],
        out_shape=[jax.ShapeDtypeStruct((M, D), F32), jax.ShapeDtypeStruct((M, D), BF16)],
        compiler_params=_cparams(("arbitrary",)),
        name="merge",
    )(attn, ssm, ga, gs, x, wa, ws, wo, npost, npre)


def _ffn_kernel(h_ref, x_ref, wg_ref, wu_ref, wd_ref, npost_ref, y_ref, acc):
    f = pl.program_id(1)
    h = h_ref[...]
    act = _silu(jnp.dot(h, wg_ref[...], preferred_element_type=F32)) * \
        jnp.dot(h, wu_ref[...], preferred_element_type=F32)
    part = jnp.dot(act.astype(BF16), wd_ref[...], preferred_element_type=F32)

    @pl.when(f == 0)
    def _():
        acc[...] = part

    @pl.when(f > 0)
    def _():
        acc[...] += part

    @pl.when(f == pl.num_programs(1) - 1)
    def _():
        y_ref[...] = x_ref[...] + _rms(acc[...], npost_ref[...])


def _ffn(h2, x1, wg, wu, wd, npost, tm):
    M, D = x1.shape
    d_ff = wg.shape[1]
    tf = FFN_TF
    assert d_ff % tf == 0
    return pl.pallas_call(
        _ffn_kernel,
        grid=(M // tm, d_ff // tf),
        in_specs=[
            pl.BlockSpec((tm, D), lambda i, f: (i, 0)),
            pl.BlockSpec((tm, D), lambda i, f: (i, 0)),
            pl.BlockSpec((D, tf), lambda i, f: (0, f)),
            pl.BlockSpec((D, tf), lambda i, f: (0, f)),
            pl.BlockSpec((tf, D), lambda i, f: (f, 0)),
            pl.BlockSpec((1, D), lambda i, f: (0, 0)),
        ],
        out_specs=pl.BlockSpec((tm, D), lambda i, f: (i, 0)),
        out_shape=jax.ShapeDtypeStruct((M, D), F32),
        scratch_shapes=[pltpu.VMEM((tm, D), F32)],
        compiler_params=_cparams(("arbitrary", "arbitrary")),
        name="ffn",
    )(h2, x1, wg, wu, wd, npost)


def _pad_lanes(v):
    row = jnp.zeros((1, LANES), F32).at[0, :v.shape[0]].set(v.astype(F32))
    return row, row.reshape(LANES, 1)


def kernel(x_prompt, x_sample, cache_k, cache_v, state_conv, state_ssm, page_table, norm_mix_pre, w_in, conv_w, conv_b, dt_bias, a_log, d_skip, ssm_norm, w_attn_out, w_ssm_out, w_out, norm_mix_post, norm_ffn_pre, w_gate, w_up, w_down, norm_ffn_post):
    depth = w_in.shape[0]
    assert depth == 1, "single trunk layer"
    b_p, seq, d_model = x_prompt.shape
    b_s, dec_seq, _ = x_sample.shape
    assert dec_seq == 1
    n_heads_attn = cache_k.shape[3]
    attn_w = n_heads_attn * cache_k.shape[4]
    assert cache_k.shape[4] == HEAD_DIM and cache_k.shape[2] == PAGE_SIZE
    conv_dim = conv_w.shape[2]
    n_ssm_heads = dt_bias.shape[1]
    d_inner = n_ssm_heads * SSM_HEADDIM
    assert conv_dim == d_inner + 2 * N_SSM_GROUPS * D_STATE and n_ssm_heads <= LANES
    n_pages = page_table.shape[1]
    assert (n_pages * PAGE_SIZE) % MOBA_BLOCK == 0
    n_sel = min(MOBA_TOPK, n_pages * PAGE_SIZE // MOBA_BLOCK)
    assert n_sel == MOBA_TOPK

    l = 0
    widths = (attn_w, attn_w, attn_w, d_inner, conv_dim, d_model, d_model)
    n_main = 3 * attn_w + d_inner + conv_dim
    w_l = w_in[l]
    w_main = w_l[:, :n_main].astype(BF16)
    w_gates = w_l[:, n_main + n_ssm_heads:].astype(BF16)
    w_dt = jnp.pad(w_l[:, n_main:n_main + n_ssm_heads], ((0, 0), (0, LANES - n_ssm_heads))).astype(BF16)
    wa = w_attn_out[l].astype(BF16)
    ws = w_ssm_out[l].astype(BF16)
    wo = w_out[l].astype(BF16)
    wg = w_gate[l].astype(BF16)
    wu = w_up[l].astype(BF16)
    wd = w_down[l].astype(BF16)
    g_pre = norm_mix_pre[l].reshape(1, d_model)
    g_post = norm_mix_post[l].reshape(1, d_model)
    g_fpre = norm_ffn_pre[l].reshape(1, d_model)
    g_fpost = norm_ffn_post[l].reshape(1, d_model)
    cw = conv_w[l]
    cb = conv_b[l].reshape(1, conv_dim)
    dtb_r, dtb_c = _pad_lanes(dt_bias[l])
    alog_r, alog_c = _pad_lanes(a_log[l])
    dskip = jnp.repeat(d_skip[l].astype(F32), SSM_HEADDIM).reshape(1, d_inner)
    snorm = ssm_norm[l].reshape(1, d_inner)

    def dense_tail(attn, ssm, ga, gs, x, tm_merge, tm_ffn):
        x1, h2 = _merge(attn, ssm, ga, gs, x, wa, ws, wo, g_post, g_fpre, tm_merge)
        return _ffn(h2, x1, wg, wu, wd, g_fpost, tm_ffn)

    m_p = b_p * seq
    xp = x_prompt.reshape(m_p, d_model)
    q, k, v, z, xbc, dt_raw, ga, gs = _in_proj(xp, g_pre, w_main, w_gates, w_dt, widths, tm=512)
    attn = _moba_prompt(q, k, v, b_p, seq)
    u, conv_p, ssm_p = _ssd_prompt(xbc, z, dt_raw, cw, cb, dtb_r, dtb_c, alog_r, alog_c, dskip, snorm, b_p, seq)
    y_p = dense_tail(attn, u, ga, gs, xp, 256, 512)

    xs = x_sample.reshape(b_s, d_model)
    q_s, k_s, v_s, z_s, xbc_s, dt_s, ga_s, gs_s = _in_proj(xs, g_pre, w_main, w_gates, w_dt, widths, tm=b_s)
    hd = (b_s, n_heads_attn, HEAD_DIM)
    q3, k3, v3 = q_s.reshape(hd), k_s.reshape(hd), v_s.reshape(hd)
    ck = cache_k.reshape(cache_k.shape[1:])
    cv = cache_v.reshape(cache_v.shape[1:])
    idx = _moba_select(q3, ck, page_table, n_sel)
    attn_s = _moba_sample(q3, k3, v3, ck, cv, page_table, idx.reshape(b_s, n_sel * n_heads_attn))
    u_s, conv_s, ssm_s = _ssd_step(xbc_s, z_s, dt_s, state_conv[l], state_ssm[l], cw, cb,
                                   dtb_r, alog_r, dskip, snorm)
    y_s = dense_tail(attn_s.reshape(b_s, attn_w).astype(BF16), u_s, ga_s, gs_s, xs, b_s, b_s)

    kv_p = (1, b_p, seq, n_heads_attn, HEAD_DIM)
    kv_s = (1, b_s, 1, n_heads_attn, HEAD_DIM)
    return (y_p.reshape(b_p, seq, d_model), y_s.reshape(b_s, 1, d_model),
            k.reshape(kv_p), v.reshape(kv_p), conv_p[None], ssm_p[None],
            k_s.reshape(kv_s), v_s.reshape(kv_s), conv_s[None], ssm_s[None])
```

```python
import functools
import math

import jax
import jax.numpy as jnp
from jax import lax
from jax.experimental import pallas as pl
from jax.experimental.pallas import tpu as pltpu

F32 = jnp.float32
BF16 = jnp.bfloat16

EPS = 1e-6
HEAD_DIM = 64
MOBA_BLOCK = 256
MOBA_TOPK = 3
PAGE_SIZE = 128
SSM_HEADDIM = 64
N_SSM_GROUPS = 8
D_STATE = 128
CONV_WIDTH = 4
SSD_CHUNK = 128

LANES = 128
SUBLANES = 8
VMEM_LIMIT = 56 * 1024 * 1024
PROJ_TN = 512
FFN_TF = 512
SCAN_PAGES = 8

_NT = (((1,), (1,)), ((), ()))
_TN = (((0,), (0,)), ((), ()))


def _cparams(sem):
    return pltpu.CompilerParams(dimension_semantics=sem, vmem_limit_bytes=VMEM_LIMIT)


def _rms(x, g):
    return x * lax.rsqrt(jnp.mean(x * x, axis=-1, keepdims=True) + EPS) * g


def _silu(x):
    return x * jax.nn.sigmoid(x)


def _softplus(x):
    return jnp.maximum(x, 0.0) + jnp.log1p(jnp.exp(-jnp.abs(x)))


def _in_proj_kernel(x_ref, g_ref, wm_ref, wg_ref, wdt_ref,
                    q_ref, k_ref, v_ref, z_ref, xbc_ref, dt_ref, ga_ref, gs_ref,
                    h_scr, *, segs):
    j = pl.program_id(1)

    @pl.when(j == 0)
    def _():
        h = _rms(x_ref[...], g_ref[...]).astype(BF16)
        h_scr[...] = h
        dt_ref[...] = jnp.dot(h, wdt_ref[...], preferred_element_type=F32)

    outs = (q_ref, k_ref, v_ref, z_ref, xbc_ref, ga_ref, gs_ref)
    for out_ref, (lo, hi, main) in zip(outs, segs):
        w_ref = wm_ref if main else wg_ref

        @pl.when((j >= lo) & (j < hi))
        def _(out_ref=out_ref, w_ref=w_ref):
            out_ref[...] = jnp.dot(h_scr[...], w_ref[...], preferred_element_type=F32)


def _in_proj(x, g, w_main, w_gates, w_dt, widths, tm):
    M, D = x.shape
    tn = PROJ_TN
    n_main = w_main.shape[1] // tn
    n_gate = w_gates.shape[1] // tn
    segs, lo = [], 0
    for i, w in enumerate(widths):
        assert w % tn == 0
        segs.append((lo, lo + w // tn, i < 5))
        lo += w // tn
    assert segs[4][1] == n_main and lo == n_main + n_gate

    def omap(lo, hi):
        return lambda i, j: (i, jnp.clip(j - lo, 0, hi - lo - 1))

    out_shape = [jax.ShapeDtypeStruct((M, w), F32) for w in widths[:5]]
    out_specs = [pl.BlockSpec((tm, tn), omap(s[0], s[1])) for s in segs[:5]]
    out_shape.append(jax.ShapeDtypeStruct((M, LANES), F32))
    out_specs.append(pl.BlockSpec((tm, LANES), lambda i, j: (i, 0)))
    out_shape += [jax.ShapeDtypeStruct((M, w), F32) for w in widths[5:]]
    out_specs += [pl.BlockSpec((tm, tn), omap(s[0], s[1])) for s in segs[5:]]

    return pl.pallas_call(
        functools.partial(_in_proj_kernel, segs=tuple(segs)),
        grid=(M // tm, n_main + n_gate),
        in_specs=[
            pl.BlockSpec((tm, D), lambda i, j: (i, 0)),
            pl.BlockSpec((1, D), lambda i, j: (0, 0)),
            pl.BlockSpec((D, tn), lambda i, j: (0, jnp.minimum(j, n_main - 1))),
            pl.BlockSpec((D, tn), lambda i, j: (0, jnp.clip(j - n_main, 0, n_gate - 1))),
            pl.BlockSpec((D, LANES), lambda i, j: (0, 0)),
        ],
        out_specs=out_specs,
        out_shape=out_shape,
        scratch_shapes=[pltpu.VMEM((tm, D), BF16)],
        compiler_params=_cparams(("arbitrary", "arbitrary")),
        name="in_proj",
    )(x, g, w_main, w_gates, w_dt)


def _moba_prompt_kernel(q_ref, k_ref, v_ref, o_ref, kt_ref, vt_ref, *, L, scale):
    BS = MOBA_BLOCK
    nb = L // BS
    q = q_ref[...]
    k = k_ref[...]
    v = v_ref[...]
    kb = k.astype(BF16)
    vb = v.astype(BF16)
    kt_ref[0] = k.T
    vt_ref[0] = v.T
    lane = lax.broadcasted_iota(jnp.int32, (1, LANES), 1)
    head_lanes = (lane < HEAD_DIM, lane >= HEAD_DIM)

    kmean = jnp.concatenate(
        [jnp.mean(k[n * BS:(n + 1) * BS], axis=0, keepdims=True) for n in range(nb)], axis=0)

    blk = lax.broadcasted_iota(jnp.int32, (nb, L), 0)
    qblk = lax.broadcasted_iota(jnp.int32, (nb, L), 1) // BS
    past = blk < qblk
    row = lax.broadcasted_iota(jnp.int32, (BS, BS), 0)
    col = lax.broadcasted_iota(jnp.int32, (BS, BS), 1)
    causal = col <= row
    eye = (row == col).astype(BF16)

    outs = []
    for hm in head_lanes:
        gate = lax.dot_general(jnp.where(hm, kmean, 0.0), q, _NT,
                               precision=lax.Precision.HIGHEST, preferred_element_type=F32)
        gate = jnp.where(past, gate, -jnp.inf)
        rank = jnp.zeros((nb, L), jnp.int32)
        for m in range(nb):
            gm = gate[m:m + 1, :]
            beats = (gm > gate) | ((gm == gate) & (m < blk))
            rank = rank + beats.astype(jnp.int32)
        sel_t = (past & (rank < MOBA_TOPK)).astype(BF16)

        qh = jnp.where(hm, q, 0.0).astype(BF16)
        o_blocks = []
        for qb in range(nb):
            qs = slice(qb * BS, (qb + 1) * BS)
            nk = (qb + 1) * BS
            s = lax.dot_general(qh[qs], kb[:nk], _NT, preferred_element_type=F32) * scale
            parts = []
            if qb > 0:
                sel_q = lax.dot_general(eye, sel_t[:, qs], _NT, preferred_element_type=F32)
                for n in range(qb):
                    ok = sel_q[:, n:n + 1] > 0.5
                    parts.append(jnp.where(ok, s[:, n * BS:(n + 1) * BS], -jnp.inf))
            parts.append(jnp.where(causal, s[:, qb * BS:], -jnp.inf))
            s = jnp.concatenate(parts, axis=1) if len(parts) > 1 else parts[0]
            m = jnp.max(s, axis=1, keepdims=True)
            p = jnp.exp(s - m)
            l = jnp.sum(p, axis=1, keepdims=True)
            o = jnp.dot(p.astype(BF16), vb[:nk], preferred_element_type=F32)
            o_blocks.append(o / l)
        outs.append(jnp.concatenate(o_blocks, axis=0))
    o_ref[...] = jnp.where(head_lanes[0], outs[0], outs[1]).astype(o_ref.dtype)


def _moba_prompt(q, k, v, batch, L):
    M, W = q.shape
    assert L % MOBA_BLOCK == 0 and W % LANES == 0 and LANES == 2 * HEAD_DIM
    spec = pl.BlockSpec((L, LANES), lambda b, hp: (b, hp))
    spec_t = pl.BlockSpec((1, LANES, L), lambda b, hp: (b, hp, 0))
    kv_t = jax.ShapeDtypeStruct((batch, W, L), F32)
    return pl.pallas_call(
        functools.partial(_moba_prompt_kernel, L=L, scale=HEAD_DIM ** -0.5),
        grid=(batch, W // LANES),
        in_specs=[spec, spec, spec],
        out_specs=[spec, spec_t, spec_t],
        out_shape=[jax.ShapeDtypeStruct((M, W), BF16), kv_t, kv_t],
        compiler_params=_cparams(("arbitrary", "arbitrary")),
        name="moba_prompt",
    )(q, k, v)


def _ssd_prompt_kernel(xbc_ref, z_ref, dt_ref, cw_ref, cb_ref, dtb_r_ref, dtb_c_ref,
                       alog_r_ref, alog_c_ref, dskip_ref, norm_ref,
                       u_ref, conv_ref, st_ref, xs_scr, *, d_inner):
    Q = SSD_CHUNK
    P = SSM_HEADDIM
    N = D_STATE
    G = N_SSM_GROUPS
    R = d_inner // P // G
    assert R % 2 == 0 and 2 * P == LANES and N == LANES
    c = pl.program_id(1)
    tail = CONV_WIDTH - 1

    @pl.when(c == 0)
    def _():
        xs_scr[0:8, :] = jnp.zeros((8, xs_scr.shape[1]), F32)
        st_ref[...] = jnp.zeros(st_ref.shape, F32)

    xs_scr[8:8 + Q, :] = xbc_ref[...]
    acc = cw_ref[0:1, :] * xs_scr[8 - tail:8 - tail + Q, :]
    for i in range(1, CONV_WIDTH):
        acc = acc + cw_ref[i:i + 1, :] * xs_scr[8 - tail + i:8 - tail + i + Q, :]
    xc = _silu(cb_ref[...] + acc)
    last_rows = xbc_ref[Q - tail:Q, :]
    xs_scr[8 - tail:8, :] = last_rows
    conv_ref[0] = last_rows

    raw = dt_ref[...]
    dt = _softplus(raw + dtb_r_ref[...])
    dt_t = _softplus(raw.T + dtb_c_ref[...])
    a = dt * (-jnp.exp(alog_r_ref[...]))
    a_t = dt_t * (-jnp.exp(alog_c_ref[...]))
    ri = lax.broadcasted_iota(jnp.int32, (Q, Q), 0)
    ci = lax.broadcasted_iota(jnp.int32, (Q, Q), 1)
    causal = ci <= ri
    acum = jnp.dot(causal.astype(F32), a, precision=lax.Precision.HIGHEST,
                   preferred_element_type=F32)
    acum_t = jnp.dot(a_t, (ri <= ci).astype(F32), precision=lax.Precision.HIGHEST,
                     preferred_element_type=F32)
    e_acum = jnp.exp(acum)
    d_end = jnp.exp(acum[Q - 1:Q, :] - acum)
    lane = lax.broadcasted_iota(jnp.int32, (1, LANES), 1)
    first = lane < P

    def pair_cols(t, h0):
        return jnp.where(first, t[:, h0:h0 + 1], t[:, h0 + 1:h0 + 2])

    y_parts = []
    for g in range(G):
        b_g = xc[:, d_inner + g * N:d_inner + (g + 1) * N].astype(BF16)
        c_g = xc[:, d_inner + G * N + g * N:d_inner + G * N + (g + 1) * N].astype(BF16)
        cb = lax.dot_general(c_g, b_g, _NT, preferred_element_type=F32)
        st_g = st_ref[0, g * R:(g + 1) * R].reshape(R * P, N)
        y_off = lax.dot_general(c_g, st_g.astype(BF16), _NT, preferred_element_type=F32)
        xdtd_parts, cd_parts = [], []
        for pr in range(R // 2):
            h0 = g * R + 2 * pr
            x_p = xc[:, h0 * P:h0 * P + LANES]
            xdt = x_p * pair_cols(dt, h0)
            xdt_b = xdt.astype(BF16)
            yd = []
            for hh in (h0, h0 + 1):
                seg = acum[:, hh:hh + 1] - acum_t[hh:hh + 1, :]
                dec = jnp.exp(jnp.where(causal, seg, -jnp.inf))
                yd.append(jnp.dot((cb * dec).astype(BF16), xdt_b, preferred_element_type=F32))
                cd_parts.append(jnp.broadcast_to(jnp.exp(acum_t[hh:hh + 1, Q - 1:Q]), (P, N)))
            y_p = (jnp.where(first, yd[0], yd[1])
                   + y_off[:, 2 * pr * P:2 * pr * P + LANES] * pair_cols(e_acum, h0)
                   + dskip_ref[:, h0 * P:h0 * P + LANES] * x_p)
            y_parts.append(y_p)
            xdtd_parts.append((xdt * pair_cols(d_end, h0)).astype(BF16))
        xdtd = jnp.concatenate(xdtd_parts, axis=1)
        s_new = lax.dot_general(xdtd, b_g, _TN, preferred_element_type=F32)
        st_new = jnp.concatenate(cd_parts, axis=0) * st_g + s_new
        st_ref[0, g * R:(g + 1) * R] = st_new.reshape(R, P, N)

    y = jnp.concatenate(y_parts, axis=1)
    u = y * _silu(z_ref[...])
    gw = d_inner // G
    u_parts = []
    for g in range(G):
        ug = u[:, g * gw:(g + 1) * gw]
        u_parts.append(ug * lax.rsqrt(jnp.mean(ug * ug, axis=-1, keepdims=True) + EPS))
    u_ref[...] = (jnp.concatenate(u_parts, axis=1) * norm_ref[...]).astype(u_ref.dtype)


def _ssd_prompt(xbc, z, dt_raw, conv_w, conv_b, dtb_r, dtb_c, alog_r, alog_c, dskip, norm, batch, L):
    M, conv_dim = xbc.shape
    d_inner = z.shape[1]
    n_heads = d_inner // SSM_HEADDIM
    assert L % SSD_CHUNK == 0
    nc = L // SSD_CHUNK
    Q = SSD_CHUNK
    tail = CONV_WIDTH - 1
    row = lambda b, c: (b * nc + c, 0)
    const = lambda b, c: (0, 0)
    return pl.pallas_call(
        functools.partial(_ssd_prompt_kernel, d_inner=d_inner),
        grid=(batch, nc),
        in_specs=[
            pl.BlockSpec((Q, conv_dim), row),
            pl.BlockSpec((Q, d_inner), row),
            pl.BlockSpec((Q, LANES), row),
            pl.BlockSpec((CONV_WIDTH, conv_dim), const),
            pl.BlockSpec((1, conv_dim), const),
            pl.BlockSpec((1, LANES), const),
            pl.BlockSpec((LANES, 1), const),
            pl.BlockSpec((1, LANES), const),
            pl.BlockSpec((LANES, 1), const),
            pl.BlockSpec((1, d_inner), const),
            pl.BlockSpec((1, d_inner), const),
        ],
        out_specs=[
            pl.BlockSpec((Q, d_inner), row),
            pl.BlockSpec((1, tail, conv_dim), lambda b, c: (b, 0, 0)),
            pl.BlockSpec((1, n_heads, SSM_HEADDIM, D_STATE), lambda b, c: (b, 0, 0, 0)),
        ],
        out_shape=[
            jax.ShapeDtypeStruct((M, d_inner), BF16),
            jax.ShapeDtypeStruct((batch, tail, conv_dim), F32),
            jax.ShapeDtypeStruct((batch, n_heads, SSM_HEADDIM, D_STATE), F32),
        ],
        scratch_shapes=[pltpu.VMEM((8 + Q, conv_dim), F32)],
        compiler_params=_cparams(("arbitrary", "arbitrary")),
        name="ssd_prompt",
    )(xbc, z, dt_raw, conv_w, conv_b, dtb_r, dtb_c, alog_r, alog_c, dskip, norm)


def _ssd_step_kernel(xbc_ref, z_ref, dt_ref, cprev_ref, sprev_ref, cw_ref, cb_ref, dtb_ref,
                     alog_ref, dskip_ref, norm_ref, u_ref, conv_ref, st_ref, *, d_inner):
    P = SSM_HEADDIM
    N = D_STATE
    G = N_SSM_GROUPS
    n_heads = d_inner // P
    R = n_heads // G
    assert R % 2 == 0 and 2 * P == LANES and N == LANES
    tail = CONV_WIDTH - 1
    x_new = xbc_ref[0]
    prev = cprev_ref[0]
    acc = cw_ref[0:1, :] * prev[0:1, :]
    for i in range(1, tail):
        acc = acc + cw_ref[i:i + 1, :] * prev[i:i + 1, :]
    acc = acc + cw_ref[tail:tail + 1, :] * x_new
    xc = _silu(cb_ref[...] + acc)
    conv_ref[0, 0:tail - 1, :] = prev[1:tail, :]
    conv_ref[0, tail - 1:tail, :] = x_new

    dt = _softplus(dt_ref[0] + dtb_ref[...])
    decay = jnp.exp(dt * (-jnp.exp(alog_ref[...])))
    ri = lax.broadcasted_iota(jnp.int32, (LANES, LANES), 0)
    ci = lax.broadcasted_iota(jnp.int32, (LANES, LANES), 1)
    eye = ri == ci
    first_rows = lax.broadcasted_iota(jnp.int32, (LANES, 1), 0) < P

    y_parts = []
    for hp in range(n_heads // 2):
        h0 = 2 * hp
        g = h0 // R
        b_g = xc[:, d_inner + g * N:d_inner + (g + 1) * N]
        c_g = xc[:, d_inner + G * N + g * N:d_inner + G * N + (g + 1) * N]
        x_row = xc[:, h0 * P:h0 * P + LANES]
        x_col = jnp.sum(jnp.where(eye, jnp.broadcast_to(x_row, (LANES, LANES)), 0.0),
                        axis=1, keepdims=True)
        dt_col = jnp.where(first_rows, dt[:, h0:h0 + 1], dt[:, h0 + 1:h0 + 2])
        dec_col = jnp.where(first_rows, decay[:, h0:h0 + 1], decay[:, h0 + 1:h0 + 2])
        st = sprev_ref[0, h0:h0 + 2].reshape(2 * P, N)
        st_new = dec_col * st + (dt_col * x_col) * b_g
        st_ref[0, h0:h0 + 2] = st_new.reshape(2, P, N)
        y_col = jnp.sum(st_new * c_g, axis=1, keepdims=True)
        y_row = jnp.sum(jnp.where(eye, jnp.broadcast_to(y_col, (LANES, LANES)), 0.0),
                        axis=0, keepdims=True)
        y_parts.append(y_row + dskip_ref[:, h0 * P:h0 * P + LANES] * x_row)
    y = jnp.concatenate(y_parts, axis=1)
    u = y * _silu(z_ref[0])
    gw = d_inner // G
    u_parts = []
    for g in range(G):
        ug = u[:, g * gw:(g + 1) * gw]
        u_parts.append(ug * lax.rsqrt(jnp.mean(ug * ug, axis=-1, keepdims=True) + EPS))
    u_ref[0] = (jnp.concatenate(u_parts, axis=1) * norm_ref[...]).astype(u_ref.dtype)


def _ssd_step(xbc, z, dt_raw, conv_prev, ssm_prev, conv_w, conv_b, dtb_r, alog_r, dskip, norm):
    B, conv_dim = xbc.shape
    d_inner = z.shape[1]
    n_heads = d_inner // SSM_HEADDIM
    tail = CONV_WIDTH - 1
    const = lambda b: (0, 0)
    b3 = lambda b: (b, 0, 0)
    b4 = lambda b: (b, 0, 0, 0)
    u, conv_new, ssm_new = pl.pallas_call(
        functools.partial(_ssd_step_kernel, d_inner=d_inner),
        grid=(B,),
        in_specs=[
            pl.BlockSpec((1, 1, conv_dim), b3),
            pl.BlockSpec((1, 1, d_inner), b3),
            pl.BlockSpec((1, 1, LANES), b3),
            pl.BlockSpec((1, tail, conv_dim), b3),
            pl.BlockSpec((1, n_heads, SSM_HEADDIM, D_STATE), b4),
            pl.BlockSpec((CONV_WIDTH, conv_dim), const),
            pl.BlockSpec((1, conv_dim), const),
            pl.BlockSpec((1, LANES), const),
            pl.BlockSpec((1, LANES), const),
            pl.BlockSpec((1, d_inner), const),
            pl.BlockSpec((1, d_inner), const),
        ],
        out_specs=[
            pl.BlockSpec((1, 1, d_inner), b3),
            pl.BlockSpec((1, tail, conv_dim), b3),
            pl.BlockSpec((1, n_heads, SSM_HEADDIM, D_STATE), b4),
        ],
        out_shape=[
            jax.ShapeDtypeStruct((B, 1, d_inner), BF16),
            jax.ShapeDtypeStruct((B, tail, conv_dim), F32),
            jax.ShapeDtypeStruct((B, n_heads, SSM_HEADDIM, D_STATE), F32),
        ],
        compiler_params=_cparams(("arbitrary",)),
        name="ssd_step",
    )(xbc.reshape(B, 1, conv_dim), z.reshape(B, 1, d_inner), dt_raw.reshape(B, 1, LANES),
      conv_prev, ssm_prev, conv_w, conv_b, dtb_r, alog_r, dskip, norm)
    return u.reshape(B, d_inner), conv_new, ssm_new


def _moba_select_kernel(pt_ref, q_ref, *refs, n_blocks, n_sel):
    del pt_ref
    page_refs = refs[:SCAN_PAGES]
    idx_ref = refs[SCAN_PAGES]
    gate_scr, qb_scr = refs[SCAN_PAGES + 1:]
    s = pl.program_id(1)
    H, d, T = qb_scr.shape
    ppb = MOBA_BLOCK // PAGE_SIZE
    bps = SCAN_PAGES // ppb

    @pl.when(s == 0)
    def _():
        qb_scr[...] = jnp.broadcast_to(q_ref[0], (H, d, T))

    qb = qb_scr[...]
    for j in range(bps):
        part = None
        for t in range(ppb):
            prod = page_refs[j * ppb + t][...] * qb
            pp = jnp.sum(prod.reshape(H, d // SUBLANES, SUBLANES, T), axis=1)
            part = pp if part is None else part + pp
        g = jnp.sum(jnp.sum(part, axis=1), axis=-1, keepdims=True)
        gate_scr[s * bps + j] = g * (1.0 / MOBA_BLOCK)

    @pl.when(s == pl.num_programs(1) - 1)
    def _():
        gate = gate_scr[...]
        blk = lax.broadcasted_iota(jnp.int32, gate.shape, 0)
        rank = jnp.zeros(gate.shape, jnp.int32)
        for m in range(n_blocks):
            gm = gate[m:m + 1]
            beats = (gm > gate) | ((gm == gate) & (m < blk))
            rank = rank + beats.astype(jnp.int32)
        for r in range(n_sel):
            idx_ref[0, r] = jnp.sum(jnp.where(rank == r, blk, 0), axis=0)


def _moba_select(q4, cache_kt, page_table, n_sel):
    B, H, d, _ = q4.shape
    n_pages = page_table.shape[1]
    ppb = MOBA_BLOCK // PAGE_SIZE
    assert n_pages % SCAN_PAGES == 0 and SCAN_PAGES % ppb == 0 and d % SUBLANES == 0
    n_blocks = n_pages // ppb

    def page_spec(p):
        return pl.BlockSpec((None, H, d, PAGE_SIZE),
                            lambda b, s, pt: (pt[b, s * SCAN_PAGES + p], 0, 0, 0))

    grid_spec = pltpu.PrefetchScalarGridSpec(
        num_scalar_prefetch=1,
        grid=(B, n_pages // SCAN_PAGES),
        in_specs=[pl.BlockSpec((1, H, d, 1), lambda b, s, pt: (b, 0, 0, 0))]
                 + [page_spec(p) for p in range(SCAN_PAGES)],
        out_specs=pl.BlockSpec((1, n_sel, H, 1), lambda b, s, pt: (b, 0, 0, 0)),
        scratch_shapes=[pltpu.VMEM((n_blocks, H, 1), F32), pltpu.VMEM((H, d, PAGE_SIZE), F32)],
    )
    return pl.pallas_call(
        functools.partial(_moba_select_kernel, n_blocks=n_blocks, n_sel=n_sel),
        grid_spec=grid_spec,
        out_shape=jax.ShapeDtypeStruct((B, n_sel, H, 1), jnp.int32),
        compiler_params=_cparams(("arbitrary", "arbitrary")),
        name="moba_select",
    )(page_table, q4, *([cache_kt] * SCAN_PAGES))


def _moba_sample_kernel(pt_ref, idx_ref, q_ref, kn_ref, vn_ref, ck_ref, cv_ref, o_ref,
                        kbuf, vbuf, sem, *, n_sel, n_heads, scale):
    b = pl.program_id(0)
    n_b = pl.num_programs(0)
    ppb = MOBA_BLOCK // PAGE_SIZE

    def copies(bb, slot):
        out = []
        for h in range(n_heads):
            for r in range(n_sel):
                blk = idx_ref[bb, r * n_heads + h]
                for t in range(ppb):
                    page = pt_ref[bb, blk * ppb + t]
                    j = r * ppb + t
                    out.append(pltpu.make_async_copy(ck_ref.at[page, h], kbuf.at[slot, h, j], sem.at[0, slot]))
                    out.append(pltpu.make_async_copy(cv_ref.at[page, h], vbuf.at[slot, h, j], sem.at[1, slot]))
        return out

    slot = b % 2

    @pl.when(b == 0)
    def _():
        for cp in copies(b, slot):
            cp.start()

    @pl.when(b + 1 < n_b)
    def _():
        for cp in copies(b + 1, 1 - slot):
            cp.start()

    for cp in copies(b, slot):
        cp.wait()

    def body(h, carry):
        qc = q_ref[0, h]
        kh = kbuf[slot, h]
        vh = vbuf[slot, h]
        s = jnp.sum(kh * qc, axis=1, keepdims=True) * scale
        s_own = jnp.sum(kn_ref[0, h] * qc, axis=0, keepdims=True) * scale
        m = jnp.maximum(jnp.max(jnp.max(s, axis=0), axis=-1, keepdims=True), s_own)
        p = jnp.exp(s - m)
        p_own = jnp.exp(s_own - m)
        l = jnp.sum(jnp.sum(p, axis=0), axis=-1, keepdims=True) + p_own
        o = jnp.sum(jnp.sum(vh * p, axis=0), axis=-1, keepdims=True) + p_own * vn_ref[0, h]
        o_ref[0, h] = o / l
        return carry

    lax.fori_loop(0, n_heads, body, 0)


def _moba_sample(q4, k4, v4, cache_kt, cache_vt, page_table, idx):
    B, H, d, _ = q4.shape
    n_sel = idx.shape[1] // H
    n_slabs = n_sel * (MOBA_BLOCK // PAGE_SIZE)
    new = pl.BlockSpec((1, H, d, 1), lambda b, pt, ix: (b, 0, 0, 0))
    grid_spec = pltpu.PrefetchScalarGridSpec(
        num_scalar_prefetch=2,
        grid=(B,),
        in_specs=[new, new, new, pl.BlockSpec(memory_space=pl.ANY), pl.BlockSpec(memory_space=pl.ANY)],
        out_specs=new,
        scratch_shapes=[pltpu.VMEM((2, H, n_slabs, d, PAGE_SIZE), F32),
                        pltpu.VMEM((2, H, n_slabs, d, PAGE_SIZE), F32),
                        pltpu.SemaphoreType.DMA((2, 2))],
    )
    return pl.pallas_call(
        functools.partial(_moba_sample_kernel, n_sel=n_sel, n_heads=H, scale=d ** -0.5),
        grid_spec=grid_spec,
        out_shape=jax.ShapeDtypeStruct((B, H, d, 1), F32),
        compiler_params=_cparams(("arbitrary",)),
        name="moba_sample",
    )(page_table, idx, q4, k4, v4, cache_kt, cache_vt)


def _merge_kernel(attn_ref, ssm_ref, ga_ref, gs_ref, x_ref, wa_ref, ws_ref, wo_ref,
                  npost_ref, npre_ref, x1_ref, h2_ref):
    a = jnp.dot(attn_ref[...], wa_ref[...], preferred_element_type=F32)
    s = jnp.dot(ssm_ref[...], ws_ref[...], preferred_element_type=F32)
    merged = jax.nn.sigmoid(ga_ref[...]) * a + jax.nn.sigmoid(gs_ref[...]) * s
    o = jnp.dot(merged.astype(BF16), wo_ref[...], preferred_element_type=F32)
    x1 = x_ref[...] + _rms(o, npost_ref[...])
    x1_ref[...] = x1
    h2_ref[...] = _rms(x1, npre_ref[...]).astype(BF16)


def _merge(attn, ssm, ga, gs, x, wa, ws, wo, npost, npre, tm):
    M, D = x.shape
    rows = lambda w: pl.BlockSpec((tm, w), lambda i: (i, 0))
    whole = lambda a: pl.BlockSpec(a.shape, lambda i: (0, 0), pipeline_mode=pl.Buffered(1))
    return pl.pallas_call(
        _merge_kernel,
        grid=(M // tm,),
        in_specs=[rows(attn.shape[1]), rows(ssm.shape[1]), rows(D), rows(D), rows(D),
                  whole(wa), whole(ws), whole(wo), whole(npost), whole(npre)],
        out_specs=[rows(D), rows(D)],
        out_shape=[jax.ShapeDtypeStruct((M, D), F32), jax.ShapeDtypeStruct((M, D), BF16)],
        compiler_params=_cparams(("arbitrary",)),
        name="merge",
    )(attn, ssm, ga, gs, x, wa, ws, wo, npost, npre)


def _ffn_kernel(h_ref, x_ref, wg_ref, wu_ref, wd_ref, npost_ref, y_ref, acc):
    f = pl.program_id(1)
    h = h_ref[...]
    act = _silu(jnp.dot(h, wg_ref[...], preferred_element_type=F32)) * \
        jnp.dot(h, wu_ref[...], preferred_element_type=F32)

    @pl.when(f == 0)
    def _():
        acc[...] = jnp.zeros(acc.shape, F32)

    acc[...] += jnp.dot(act.astype(BF16), wd_ref[...], preferred_element_type=F32)

    @pl.when(f == pl.num_programs(1) - 1)
    def _():
        y_ref[...] = x_ref[...] + _rms(acc[...], npost_ref[...])


def _ffn(h2, x1, wg, wu, wd, npost, tm):
    M, D = x1.shape
    d_ff = wg.shape[1]
    tf = FFN_TF
    assert d_ff % tf == 0
    return pl.pallas_call(
        _ffn_kernel,
        grid=(M // tm, d_ff // tf),
        in_specs=[
            pl.BlockSpec((tm, D), lambda i, f: (i, 0)),
            pl.BlockSpec((tm, D), lambda i, f: (i, 0)),
            pl.BlockSpec((D, tf), lambda i, f: (0, f)),
            pl.BlockSpec((D, tf), lambda i, f: (0, f)),
            pl.BlockSpec((tf, D), lambda i, f: (f, 0)),
            pl.BlockSpec((1, D), lambda i, f: (0, 0)),
        ],
        out_specs=pl.BlockSpec((tm, D), lambda i, f: (i, 0)),
        out_shape=jax.ShapeDtypeStruct((M, D), F32),
        scratch_shapes=[pltpu.VMEM((tm, D), F32)],
        compiler_params=_cparams(("arbitrary", "arbitrary")),
        name="ffn",
    )(h2, x1, wg, wu, wd, npost)


def _pad_lanes(v):
    row = jnp.zeros((1, LANES), F32).at[0, :v.shape[0]].set(v.astype(F32))
    return row, row.reshape(LANES, 1)


def kernel(x_prompt, x_sample, cache_k, cache_v, state_conv, state_ssm, page_table, norm_mix_pre, w_in, conv_w, conv_b, dt_bias, a_log, d_skip, ssm_norm, w_attn_out, w_ssm_out, w_out, norm_mix_post, norm_ffn_pre, w_gate, w_up, w_down, norm_ffn_post):
    depth = w_in.shape[0]
    assert depth == 1, "single trunk layer"
    b_p, seq, d_model = x_prompt.shape
    b_s, dec_seq, _ = x_sample.shape
    assert dec_seq == 1
    n_heads_attn = cache_k.shape[3]
    attn_w = n_heads_attn * cache_k.shape[4]
    assert cache_k.shape[4] == HEAD_DIM and cache_k.shape[2] == PAGE_SIZE
    conv_dim = conv_w.shape[2]
    n_ssm_heads = dt_bias.shape[1]
    d_inner = n_ssm_heads * SSM_HEADDIM
    assert conv_dim == d_inner + 2 * N_SSM_GROUPS * D_STATE and n_ssm_heads <= LANES
    n_pages = page_table.shape[1]
    assert (n_pages * PAGE_SIZE) % MOBA_BLOCK == 0
    n_sel = min(MOBA_TOPK, n_pages * PAGE_SIZE // MOBA_BLOCK)
    assert n_sel == MOBA_TOPK

    l = 0
    widths = (attn_w, attn_w, attn_w, d_inner, conv_dim, d_model, d_model)
    n_main = 3 * attn_w + d_inner + conv_dim
    w_l = w_in[l]
    w_main = w_l[:, :n_main].astype(BF16)
    w_gates = w_l[:, n_main + n_ssm_heads:].astype(BF16)
    w_dt = jnp.pad(w_l[:, n_main:n_main + n_ssm_heads], ((0, 0), (0, LANES - n_ssm_heads))).astype(BF16)
    wa = w_attn_out[l].astype(BF16)
    ws = w_ssm_out[l].astype(BF16)
    wo = w_out[l].astype(BF16)
    wg = w_gate[l].astype(BF16)
    wu = w_up[l].astype(BF16)
    wd = w_down[l].astype(BF16)
    g_pre = norm_mix_pre[l].reshape(1, d_model)
    g_post = norm_mix_post[l].reshape(1, d_model)
    g_fpre = norm_ffn_pre[l].reshape(1, d_model)
    g_fpost = norm_ffn_post[l].reshape(1, d_model)
    cw = conv_w[l]
    cb = conv_b[l].reshape(1, conv_dim)
    dtb_r, dtb_c = _pad_lanes(dt_bias[l])
    alog_r, alog_c = _pad_lanes(a_log[l])
    dskip = jnp.repeat(d_skip[l].astype(F32), SSM_HEADDIM).reshape(1, d_inner)
    snorm = ssm_norm[l].reshape(1, d_inner)

    def dense_tail(attn, ssm, ga, gs, x, tm_merge, tm_ffn):
        x1, h2 = _merge(attn, ssm, ga, gs, x, wa, ws, wo, g_post, g_fpre, tm_merge)
        return _ffn(h2, x1, wg, wu, wd, g_fpost, tm_ffn)

    m_p = b_p * seq
    xp = x_prompt.reshape(m_p, d_model)
    q, k, v, z, xbc, dt_raw, ga, gs = _in_proj(xp, g_pre, w_main, w_gates, w_dt, widths, tm=512)
    attn, kt, vt = _moba_prompt(q, k, v, b_p, seq)
    u, conv_p, ssm_p = _ssd_prompt(xbc, z, dt_raw, cw, cb, dtb_r, dtb_c, alog_r, alog_c, dskip, snorm, b_p, seq)
    y_p = dense_tail(attn, u, ga, gs, xp, 256, 512)

    xs = x_sample.reshape(b_s, d_model)
    q_s, k_s, v_s, z_s, xbc_s, dt_s, ga_s, gs_s = _in_proj(xs, g_pre, w_main, w_gates, w_dt, widths, tm=b_s)
    hd = (b_s, n_heads_attn, HEAD_DIM, 1)
    q4, k4, v4 = q_s.reshape(hd), k_s.reshape(hd), v_s.reshape(hd)
    ck = jnp.transpose(cache_k.reshape(cache_k.shape[1:]), (0, 2, 3, 1))
    cv = jnp.transpose(cache_v.reshape(cache_v.shape[1:]), (0, 2, 3, 1))
    idx = _moba_select(q4, ck, page_table, n_sel)
    attn_s = _moba_sample(q4, k4, v4, ck, cv, page_table, idx.reshape(b_s, n_sel * n_heads_attn))
    u_s, conv_s, ssm_s = _ssd_step(xbc_s, z_s, dt_s, state_conv[l], state_ssm[l], cw, cb,
                                   dtb_r, alog_r, dskip, snorm)
    y_s = dense_tail(attn_s.reshape(b_s, attn_w).astype(BF16), u_s, ga_s, gs_s, xs, b_s, b_s)

    def kv_prompt(t):
        return jnp.transpose(t.reshape(b_p, n_heads_attn, HEAD_DIM, seq), (0, 3, 1, 2))[None]

    kv_s = (1, b_s, 1, n_heads_attn, HEAD_DIM)
    return (y_p.reshape(b_p, seq, d_model), y_s.reshape(b_s, 1, d_model),
            kv_prompt(kt), kv_prompt(vt), conv_p[None], ssm_p[None],
            k_s.reshape(kv_s), v_s.reshape(kv_s), conv_s[None], ssm_s[None])
```

```python
import functools
import math

import jax
import jax.numpy as jnp
from jax import lax
from jax.experimental import pallas as pl
from jax.experimental.pallas import tpu as pltpu

F32 = jnp.float32
BF16 = jnp.bfloat16

EPS = 1e-6
HEAD_DIM = 64
MOBA_BLOCK = 256
MOBA_TOPK = 3
PAGE_SIZE = 128
SSM_HEADDIM = 64
N_SSM_GROUPS = 8
D_STATE = 128
CONV_WIDTH = 4
SSD_CHUNK = 128

LANES = 128
SUBLANES = 8
VMEM_LIMIT = 56 * 1024 * 1024
PROJ_TN = 1536
FFN_TF = 512
SCAN_PAGES = 16

_NT = (((1,), (1,)), ((), ()))
_TN = (((0,), (0,)), ((), ()))


def _cparams(sem):
    return pltpu.CompilerParams(dimension_semantics=sem, vmem_limit_bytes=VMEM_LIMIT)


def _rms(x, g):
    return x * lax.rsqrt(jnp.mean(x * x, axis=-1, keepdims=True) + EPS) * g


def _silu(x):
    return x * jax.nn.sigmoid(x)


def _softplus(x):
    return jnp.maximum(x, 0.0) + jnp.log1p(jnp.exp(-jnp.abs(x)))


def _eye(n):
    return lax.broadcasted_iota(jnp.int32, (n, n), 0) == lax.broadcasted_iota(jnp.int32, (n, n), 1)


def _row_to_col(row):
    n = row.shape[1]
    return jnp.sum(jnp.where(_eye(n), jnp.broadcast_to(row, (n, n)), 0.0), axis=1, keepdims=True)


def _col_to_row(col):
    n = col.shape[0]
    return jnp.sum(jnp.where(_eye(n), jnp.broadcast_to(col, (n, n)), 0.0), axis=0, keepdims=True)


def _in_proj_kernel(x_ref, g_ref, w_ref, o_ref, h_scr):
    @pl.when(pl.program_id(1) == 0)
    def _():
        h_scr[...] = _rms(x_ref[...], g_ref[...]).astype(BF16)

    o_ref[...] = jnp.dot(h_scr[...], w_ref[...], preferred_element_type=F32)


def _in_proj(x, g, w_tiles, tm):
    M, D = x.shape
    n_tiles, _, tn = w_tiles.shape
    return pl.pallas_call(
        _in_proj_kernel,
        grid=(M // tm, n_tiles),
        in_specs=[
            pl.BlockSpec((tm, D), lambda i, j: (i, 0)),
            pl.BlockSpec((1, D), lambda i, j: (0, 0)),
            pl.BlockSpec((None, D, tn), lambda i, j: (j, 0, 0)),
        ],
        out_specs=pl.BlockSpec((tm, tn), lambda i, j: (i, j)),
        out_shape=jax.ShapeDtypeStruct((M, n_tiles * tn), F32),
        scratch_shapes=[pltpu.VMEM((tm, D), BF16)],
        compiler_params=_cparams(("arbitrary", "arbitrary")),
        name="in_proj",
    )(x, g, w_tiles)


def _moba_prompt_kernel(q_ref, k_ref, v_ref, o_ref, kt_ref, vt_ref, *, L, scale):
    BS = MOBA_BLOCK
    nb = L // BS
    q = q_ref[...]
    k = k_ref[...]
    v = v_ref[...]
    kb = k.astype(BF16)
    vb = v.astype(BF16)
    kt_ref[0] = k.T
    vt_ref[0] = v.T
    lane = lax.broadcasted_iota(jnp.int32, (1, LANES), 1)
    head_lanes = (lane < HEAD_DIM, lane >= HEAD_DIM)

    kmean = jnp.concatenate(
        [jnp.mean(k[n * BS:(n + 1) * BS], axis=0, keepdims=True) for n in range(nb)], axis=0)

    blk = lax.broadcasted_iota(jnp.int32, (nb, L), 0)
    qblk = lax.broadcasted_iota(jnp.int32, (nb, L), 1) // BS
    past = blk < qblk
    row = lax.broadcasted_iota(jnp.int32, (BS, BS), 0)
    col = lax.broadcasted_iota(jnp.int32, (BS, BS), 1)
    causal = col <= row
    eye = (row == col).astype(BF16)

    outs = []
    for hm in head_lanes:
        gate = lax.dot_general(jnp.where(hm, kmean, 0.0), q, _NT,
                               precision=lax.Precision.HIGHEST, preferred_element_type=F32)
        gate = jnp.where(past, gate, -jnp.inf)
        rank = jnp.zeros((nb, L), jnp.int32)
        for m in range(nb):
            gm = gate[m:m + 1, :]
            beats = (gm > gate) | ((gm == gate) & (m < blk))
            rank = rank + beats.astype(jnp.int32)
        sel_t = (past & (rank < MOBA_TOPK)).astype(BF16)

        qh = jnp.where(hm, q, 0.0).astype(BF16)
        o_blocks = []
        for qb in range(nb):
            qs = slice(qb * BS, (qb + 1) * BS)
            nk = (qb + 1) * BS
            s = lax.dot_general(qh[qs], kb[:nk], _NT, preferred_element_type=F32) * scale
            parts = []
            if qb > 0:
                sel_q = lax.dot_general(eye, sel_t[:, qs], _NT, preferred_element_type=F32)
                for n in range(qb):
                    ok = sel_q[:, n:n + 1] > 0.5
                    parts.append(jnp.where(ok, s[:, n * BS:(n + 1) * BS], -jnp.inf))
            parts.append(jnp.where(causal, s[:, qb * BS:], -jnp.inf))
            s = jnp.concatenate(parts, axis=1) if len(parts) > 1 else parts[0]
            m = jnp.max(s, axis=1, keepdims=True)
            p = jnp.exp(s - m)
            l = jnp.sum(p, axis=1, keepdims=True)
            o = jnp.dot(p.astype(BF16), vb[:nk], preferred_element_type=F32)
            o_blocks.append(o / l)
        outs.append(jnp.concatenate(o_blocks, axis=0))
    o_ref[...] = jnp.where(head_lanes[0], outs[0], outs[1]).astype(o_ref.dtype)


def _col_block(off, width):
    assert off % width == 0
    return off // width


def _moba_prompt(proj, cols, W, batch, L):
    M = proj.shape[0]
    assert L % MOBA_BLOCK == 0 and W % LANES == 0 and LANES == 2 * HEAD_DIM

    def in_spec(name):
        c0 = _col_block(cols[name], LANES)
        return pl.BlockSpec((L, LANES), lambda b, hp: (b, c0 + hp))

    spec_t = pl.BlockSpec((1, LANES, L), lambda b, hp: (b, hp, 0))
    kv_t = jax.ShapeDtypeStruct((batch, W, L), F32)
    return pl.pallas_call(
        functools.partial(_moba_prompt_kernel, L=L, scale=HEAD_DIM ** -0.5),
        grid=(batch, W // LANES),
        in_specs=[in_spec("q"), in_spec("k"), in_spec("v")],
        out_specs=[pl.BlockSpec((L, LANES), lambda b, hp: (b, hp)), spec_t, spec_t],
        out_shape=[jax.ShapeDtypeStruct((M, W), BF16), kv_t, kv_t],
        compiler_params=_cparams(("arbitrary", "arbitrary")),
        name="moba_prompt",
    )(proj, proj, proj)


def _ssd_prompt_kernel(xbc_ref, z_ref, dt_ref, cw_ref, cb_ref, dtb_r_ref, dtb_c_ref,
                       alog_r_ref, alog_c_ref, dskip_ref, norm_ref,
                       u_ref, conv_ref, st_ref, xs_scr, *, d_inner):
    Q = SSD_CHUNK
    P = SSM_HEADDIM
    N = D_STATE
    G = N_SSM_GROUPS
    R = d_inner // P // G
    assert R % 2 == 0 and 2 * P == LANES and N == LANES
    c = pl.program_id(1)
    tail = CONV_WIDTH - 1

    @pl.when(c == 0)
    def _():
        xs_scr[0:8, :] = jnp.zeros((8, xs_scr.shape[1]), F32)
        st_ref[...] = jnp.zeros(st_ref.shape, F32)

    xs_scr[8:8 + Q, :] = xbc_ref[...]
    acc = cw_ref[0:1, :] * xs_scr[8 - tail:8 - tail + Q, :]
    for i in range(1, CONV_WIDTH):
        acc = acc + cw_ref[i:i + 1, :] * xs_scr[8 - tail + i:8 - tail + i + Q, :]
    xc = _silu(cb_ref[...] + acc)
    last_rows = xbc_ref[Q - tail:Q, :]
    xs_scr[8 - tail:8, :] = last_rows
    conv_ref[0] = last_rows

    raw = dt_ref[...]
    dt = _softplus(raw + dtb_r_ref[...])
    dt_t = _softplus(raw.T + dtb_c_ref[...])
    a = dt * (-jnp.exp(alog_r_ref[...]))
    a_t = dt_t * (-jnp.exp(alog_c_ref[...]))
    ri = lax.broadcasted_iota(jnp.int32, (Q, Q), 0)
    ci = lax.broadcasted_iota(jnp.int32, (Q, Q), 1)
    causal = ci <= ri
    acum = jnp.dot(causal.astype(F32), a, precision=lax.Precision.HIGHEST,
                   preferred_element_type=F32)
    acum_t = jnp.dot(a_t, (ri <= ci).astype(F32), precision=lax.Precision.HIGHEST,
                     preferred_element_type=F32)
    e_acum = jnp.exp(acum)
    d_end = jnp.exp(acum[Q - 1:Q, :] - acum)
    lane = lax.broadcasted_iota(jnp.int32, (1, LANES), 1)
    first = lane < P

    def pair_cols(t, h0):
        return jnp.where(first, t[:, h0:h0 + 1], t[:, h0 + 1:h0 + 2])

    y_parts = []
    for g in range(G):
        b_g = xc[:, d_inner + g * N:d_inner + (g + 1) * N].astype(BF16)
        c_g = xc[:, d_inner + G * N + g * N:d_inner + G * N + (g + 1) * N].astype(BF16)
        cb = lax.dot_general(c_g, b_g, _NT, preferred_element_type=F32)
        st_g = st_ref[0, g * R:(g + 1) * R].reshape(R * P, N)
        y_off = lax.dot_general(c_g, st_g.astype(BF16), _NT, preferred_element_type=F32)
        xdtd_parts, cd_parts = [], []
        for pr in range(R // 2):
            h0 = g * R + 2 * pr
            x_p = xc[:, h0 * P:h0 * P + LANES]
            xdt = x_p * pair_cols(dt, h0)
            xdt_b = xdt.astype(BF16)
            yd = []
            for hh in (h0, h0 + 1):
                seg = acum[:, hh:hh + 1] - acum_t[hh:hh + 1, :]
                dec = jnp.exp(jnp.where(causal, seg, -jnp.inf))
                yd.append(jnp.dot((cb * dec).astype(BF16), xdt_b, preferred_element_type=F32))
                cd_parts.append(jnp.broadcast_to(jnp.exp(acum_t[hh:hh + 1, Q - 1:Q]), (P, N)))
            y_p = (jnp.where(first, yd[0], yd[1])
                   + y_off[:, 2 * pr * P:2 * pr * P + LANES] * pair_cols(e_acum, h0)
                   + dskip_ref[:, h0 * P:h0 * P + LANES] * x_p)
            y_parts.append(y_p)
            xdtd_parts.append((xdt * pair_cols(d_end, h0)).astype(BF16))
        xdtd = jnp.concatenate(xdtd_parts, axis=1)
        s_new = lax.dot_general(xdtd, b_g, _TN, preferred_element_type=F32)
        st_new = jnp.concatenate(cd_parts, axis=0) * st_g + s_new
        st_ref[0, g * R:(g + 1) * R] = st_new.reshape(R, P, N)

    y = jnp.concatenate(y_parts, axis=1)
    u = y * _silu(z_ref[...])
    gw = d_inner // G
    u_parts = []
    for g in range(G):
        ug = u[:, g * gw:(g + 1) * gw]
        u_parts.append(ug * lax.rsqrt(jnp.mean(ug * ug, axis=-1, keepdims=True) + EPS))
    u_ref[...] = (jnp.concatenate(u_parts, axis=1) * norm_ref[...]).astype(u_ref.dtype)


def _ssd_prompt(proj, cols, conv_w, conv_b, dtb_r, dtb_c, alog_r, alog_c, dskip, norm, batch, L):
    M = proj.shape[0]
    conv_dim = conv_w.shape[1]
    d_inner = norm.shape[1]
    n_heads = d_inner // SSM_HEADDIM
    assert L % SSD_CHUNK == 0
    nc = L // SSD_CHUNK
    Q = SSD_CHUNK
    tail = CONV_WIDTH - 1
    row = lambda b, c: (b * nc + c, 0)
    const = lambda b, c: (0, 0)

    def in_spec(name, width):
        c0 = _col_block(cols[name], width)
        return pl.BlockSpec((Q, width), lambda b, c: (b * nc + c, c0))

    xbc = z = dt_raw = proj
    return pl.pallas_call(
        functools.partial(_ssd_prompt_kernel, d_inner=d_inner),
        grid=(batch, nc),
        in_specs=[
            in_spec("xbc", conv_dim),
            in_spec("z", d_inner),
            in_spec("dt", LANES),
            pl.BlockSpec((CONV_WIDTH, conv_dim), const),
            pl.BlockSpec((1, conv_dim), const),
            pl.BlockSpec((1, LANES), const),
            pl.BlockSpec((LANES, 1), const),
            pl.BlockSpec((1, LANES), const),
            pl.BlockSpec((LANES, 1), const),
            pl.BlockSpec((1, d_inner), const),
            pl.BlockSpec((1, d_inner), const),
        ],
        out_specs=[
            pl.BlockSpec((Q, d_inner), row),
            pl.BlockSpec((1, tail, conv_dim), lambda b, c: (b, 0, 0)),
            pl.BlockSpec((1, n_heads, SSM_HEADDIM, D_STATE), lambda b, c: (b, 0, 0, 0)),
        ],
        out_shape=[
            jax.ShapeDtypeStruct((M, d_inner), BF16),
            jax.ShapeDtypeStruct((batch, tail, conv_dim), F32),
            jax.ShapeDtypeStruct((batch, n_heads, SSM_HEADDIM, D_STATE), F32),
        ],
        scratch_shapes=[pltpu.VMEM((8 + Q, conv_dim), F32)],
        compiler_params=_cparams(("arbitrary", "arbitrary")),
        name="ssd_prompt",
    )(xbc, z, dt_raw, conv_w, conv_b, dtb_r, dtb_c, alog_r, alog_c, dskip, norm)


def _ssd_step_kernel(xbc_ref, z_ref, dt_ref, cprev_ref, sprev_ref, cw_ref, cb_ref, dtb_ref,
                     alog_ref, dskip_ref, norm_ref, u_ref, conv_ref, st_ref, *, d_inner):
    P = SSM_HEADDIM
    N = D_STATE
    G = N_SSM_GROUPS
    n_heads = d_inner // P
    R = n_heads // G
    assert R % 2 == 0 and 2 * P == LANES and N == LANES
    tail = CONV_WIDTH - 1
    x_new = xbc_ref[0]
    prev = cprev_ref[0]
    acc = cw_ref[0:1, :] * prev[0:1, :]
    for i in range(1, tail):
        acc = acc + cw_ref[i:i + 1, :] * prev[i:i + 1, :]
    acc = acc + cw_ref[tail:tail + 1, :] * x_new
    xc = _silu(cb_ref[...] + acc)
    conv_ref[0, 0:tail - 1, :] = prev[1:tail, :]
    conv_ref[0, tail - 1:tail, :] = x_new

    dt = _softplus(dt_ref[0] + dtb_ref[...])
    decay = jnp.exp(dt * (-jnp.exp(alog_ref[...])))
    ri = lax.broadcasted_iota(jnp.int32, (LANES, LANES), 0)
    ci = lax.broadcasted_iota(jnp.int32, (LANES, LANES), 1)
    eye = ri == ci
    first_rows = lax.broadcasted_iota(jnp.int32, (LANES, 1), 0) < P

    y_parts = []
    for hp in range(n_heads // 2):
        h0 = 2 * hp
        g = h0 // R
        b_g = xc[:, d_inner + g * N:d_inner + (g + 1) * N]
        c_g = xc[:, d_inner + G * N + g * N:d_inner + G * N + (g + 1) * N]
        x_row = xc[:, h0 * P:h0 * P + LANES]
        x_col = jnp.sum(jnp.where(eye, jnp.broadcast_to(x_row, (LANES, LANES)), 0.0),
                        axis=1, keepdims=True)
        dt_col = jnp.where(first_rows, dt[:, h0:h0 + 1], dt[:, h0 + 1:h0 + 2])
        dec_col = jnp.where(first_rows, decay[:, h0:h0 + 1], decay[:, h0 + 1:h0 + 2])
        st = sprev_ref[0, h0:h0 + 2].reshape(2 * P, N)
        st_new = dec_col * st + (dt_col * x_col) * b_g
        st_ref[0, h0:h0 + 2] = st_new.reshape(2, P, N)
        y_col = jnp.sum(st_new * c_g, axis=1, keepdims=True)
        y_row = jnp.sum(jnp.where(eye, jnp.broadcast_to(y_col, (LANES, LANES)), 0.0),
                        axis=0, keepdims=True)
        y_parts.append(y_row + dskip_ref[:, h0 * P:h0 * P + LANES] * x_row)
    y = jnp.concatenate(y_parts, axis=1)
    u = y * _silu(z_ref[0])
    gw = d_inner // G
    u_parts = []
    for g in range(G):
        ug = u[:, g * gw:(g + 1) * gw]
        u_parts.append(ug * lax.rsqrt(jnp.mean(ug * ug, axis=-1, keepdims=True) + EPS))
    u_ref[0] = (jnp.concatenate(u_parts, axis=1) * norm_ref[...]).astype(u_ref.dtype)


def _ssd_step(xbc, z, dt_raw, conv_prev, ssm_prev, conv_w, conv_b, dtb_r, alog_r, dskip, norm):
    B, conv_dim = xbc.shape
    d_inner = z.shape[1]
    n_heads = d_inner // SSM_HEADDIM
    tail = CONV_WIDTH - 1
    const = lambda b: (0, 0)
    b3 = lambda b: (b, 0, 0)
    b4 = lambda b: (b, 0, 0, 0)
    u, conv_new, ssm_new = pl.pallas_call(
        functools.partial(_ssd_step_kernel, d_inner=d_inner),
        grid=(B,),
        in_specs=[
            pl.BlockSpec((1, 1, conv_dim), b3),
            pl.BlockSpec((1, 1, d_inner), b3),
            pl.BlockSpec((1, 1, LANES), b3),
            pl.BlockSpec((1, tail, conv_dim), b3),
            pl.BlockSpec((1, n_heads, SSM_HEADDIM, D_STATE), b4),
            pl.BlockSpec((CONV_WIDTH, conv_dim), const),
            pl.BlockSpec((1, conv_dim), const),
            pl.BlockSpec((1, LANES), const),
            pl.BlockSpec((1, LANES), const),
            pl.BlockSpec((1, d_inner), const),
            pl.BlockSpec((1, d_inner), const),
        ],
        out_specs=[
            pl.BlockSpec((1, 1, d_inner), b3),
            pl.BlockSpec((1, tail, conv_dim), b3),
            pl.BlockSpec((1, n_heads, SSM_HEADDIM, D_STATE), b4),
        ],
        out_shape=[
            jax.ShapeDtypeStruct((B, 1, d_inner), BF16),
            jax.ShapeDtypeStruct((B, tail, conv_dim), F32),
            jax.ShapeDtypeStruct((B, n_heads, SSM_HEADDIM, D_STATE), F32),
        ],
        compiler_params=_cparams(("arbitrary",)),
        name="ssd_step",
    )(xbc.reshape(B, 1, conv_dim), z.reshape(B, 1, d_inner), dt_raw.reshape(B, 1, LANES),
      conv_prev, ssm_prev, conv_w, conv_b, dtb_r, alog_r, dskip, norm)
    return u.reshape(B, d_inner), conv_new, ssm_new


def _moba_select_kernel(pt_ref, q_ref, *refs, n_blocks, n_sel):
    del pt_ref
    page_refs = refs[:SCAN_PAGES]
    idx_ref = refs[SCAN_PAGES]
    gate_scr, qb_scr = refs[SCAN_PAGES + 1:]
    s = pl.program_id(1)
    H, d, T = qb_scr.shape
    ppb = MOBA_BLOCK // PAGE_SIZE
    bps = SCAN_PAGES // ppb

    @pl.when(s == 0)
    def _():
        for h in range(H):
            qb_scr[h] = jnp.broadcast_to(_row_to_col(q_ref[0, h]), (d, T))

    qb = qb_scr[...]
    for j in range(bps):
        part = None
        for t in range(ppb):
            prod = page_refs[j * ppb + t][...] * qb
            pp = jnp.sum(prod.reshape(H, d // SUBLANES, SUBLANES, T), axis=1)
            part = pp if part is None else part + pp
        g = jnp.sum(jnp.sum(part, axis=1), axis=-1, keepdims=True)
        gate_scr[s * bps + j] = g * (1.0 / MOBA_BLOCK)

    @pl.when(s == pl.num_programs(1) - 1)
    def _():
        gate = gate_scr[...]
        blk = lax.broadcasted_iota(jnp.int32, gate.shape, 0)
        rank = jnp.zeros(gate.shape, jnp.int32)
        for m in range(n_blocks):
            gm = gate[m:m + 1]
            beats = (gm > gate) | ((gm == gate) & (m < blk))
            rank = rank + beats.astype(jnp.int32)
        for r in range(n_sel):
            idx_ref[0, r] = jnp.sum(jnp.where(rank == r, blk, 0), axis=0)


def _moba_select(q4, cache_kt, page_table, n_sel):
    B, H, _, d = q4.shape
    n_pages = page_table.shape[1]
    ppb = MOBA_BLOCK // PAGE_SIZE
    assert n_pages % SCAN_PAGES == 0 and SCAN_PAGES % ppb == 0 and d % SUBLANES == 0
    n_blocks = n_pages // ppb

    def page_spec(p):
        return pl.BlockSpec((None, H, d, PAGE_SIZE),
                            lambda b, s, pt: (pt[b, s * SCAN_PAGES + p], 0, 0, 0))

    grid_spec = pltpu.PrefetchScalarGridSpec(
        num_scalar_prefetch=1,
        grid=(B, n_pages // SCAN_PAGES),
        in_specs=[pl.BlockSpec((1, H, 1, d), lambda b, s, pt: (b, 0, 0, 0))]
                 + [page_spec(p) for p in range(SCAN_PAGES)],
        out_specs=pl.BlockSpec((1, n_sel, H, 1), lambda b, s, pt: (b, 0, 0, 0)),
        scratch_shapes=[pltpu.VMEM((n_blocks, H, 1), F32), pltpu.VMEM((H, d, PAGE_SIZE), F32)],
    )
    return pl.pallas_call(
        functools.partial(_moba_select_kernel, n_blocks=n_blocks, n_sel=n_sel),
        grid_spec=grid_spec,
        out_shape=jax.ShapeDtypeStruct((B, n_sel, H, 1), jnp.int32),
        compiler_params=_cparams(("arbitrary", "arbitrary")),
        name="moba_select",
    )(page_table, q4, *([cache_kt] * SCAN_PAGES))


def _moba_sample_kernel(pt_ref, idx_ref, q_ref, kn_ref, vn_ref, ck_ref, cv_ref, o_ref,
                        kbuf, vbuf, sem, *, n_sel, n_heads, scale):
    b = pl.program_id(0)
    n_b = pl.num_programs(0)
    ppb = MOBA_BLOCK // PAGE_SIZE

    def copies(bb, slot):
        out = []
        for h in range(n_heads):
            for r in range(n_sel):
                blk = idx_ref[bb, r * n_heads + h]
                for t in range(ppb):
                    page = pt_ref[bb, blk * ppb + t]
                    j = r * ppb + t
                    out.append(pltpu.make_async_copy(ck_ref.at[page, h], kbuf.at[slot, h, j], sem.at[0, slot]))
                    out.append(pltpu.make_async_copy(cv_ref.at[page, h], vbuf.at[slot, h, j], sem.at[1, slot]))
        return out

    slot = b % 2

    @pl.when(b == 0)
    def _():
        for cp in copies(b, slot):
            cp.start()

    @pl.when(b + 1 < n_b)
    def _():
        for cp in copies(b + 1, 1 - slot):
            cp.start()

    for cp in copies(b, slot):
        cp.wait()

    def body(h, carry):
        q_row = q_ref[0, h]
        qc = _row_to_col(q_row)
        kh = kbuf[slot, h]
        vh = vbuf[slot, h]
        s = jnp.sum(kh * qc, axis=1, keepdims=True) * scale
        s_own = jnp.sum(kn_ref[0, h] * q_row, axis=-1, keepdims=True) * scale
        m = jnp.maximum(jnp.max(jnp.max(s, axis=0), axis=-1, keepdims=True), s_own)
        p = jnp.exp(s - m)
        p_own = jnp.exp(s_own - m)
        l = jnp.sum(jnp.sum(p, axis=0), axis=-1, keepdims=True) + p_own
        o_past = jnp.sum(jnp.sum(vh * p, axis=0), axis=-1, keepdims=True)
        o_ref[0, h] = (_col_to_row(o_past) + p_own * vn_ref[0, h]) / l
        return carry

    lax.fori_loop(0, n_heads, body, 0, unroll=True)


def _moba_sample(q4, k4, v4, cache_kt, cache_vt, page_table, idx):
    B, H, _, d = q4.shape
    n_sel = idx.shape[1] // H
    n_slabs = n_sel * (MOBA_BLOCK // PAGE_SIZE)
    new = pl.BlockSpec((1, H, 1, d), lambda b, pt, ix: (b, 0, 0, 0))
    grid_spec = pltpu.PrefetchScalarGridSpec(
        num_scalar_prefetch=2,
        grid=(B,),
        in_specs=[new, new, new, pl.BlockSpec(memory_space=pl.ANY), pl.BlockSpec(memory_space=pl.ANY)],
        out_specs=new,
        scratch_shapes=[pltpu.VMEM((2, H, n_slabs, d, PAGE_SIZE), F32),
                        pltpu.VMEM((2, H, n_slabs, d, PAGE_SIZE), F32),
                        pltpu.SemaphoreType.DMA((2, 2))],
    )
    return pl.pallas_call(
        functools.partial(_moba_sample_kernel, n_sel=n_sel, n_heads=H, scale=d ** -0.5),
        grid_spec=grid_spec,
        out_shape=jax.ShapeDtypeStruct((B, H, 1, d), F32),
        compiler_params=_cparams(("arbitrary",)),
        name="moba_sample",
    )(page_table, idx, q4, k4, v4, cache_kt, cache_vt)


def _merge_kernel(attn_ref, ssm_ref, ga_ref, gs_ref, x_ref, wa_ref, ws_ref, wo_ref,
                  npost_ref, npre_ref, x1_ref, h2_ref):
    a = jnp.dot(attn_ref[...], wa_ref[...], preferred_element_type=F32)
    s = jnp.dot(ssm_ref[...], ws_ref[...], preferred_element_type=F32)
    merged = jax.nn.sigmoid(ga_ref[...]) * a + jax.nn.sigmoid(gs_ref[...]) * s
    o = jnp.dot(merged.astype(BF16), wo_ref[...], preferred_element_type=F32)
    x1 = x_ref[...] + _rms(o, npost_ref[...])
    x1_ref[...] = x1
    h2_ref[...] = _rms(x1, npre_ref[...]).astype(BF16)


def _merge(attn, ssm, proj, cols, x, wa, ws, wo, npost, npre, tm):
    M, D = x.shape
    rows = lambda w: pl.BlockSpec((tm, w), lambda i: (i, 0))
    whole = lambda a: pl.BlockSpec(a.shape, lambda i: (0, 0), pipeline_mode=pl.Buffered(1))

    def gate_spec(name):
        c0 = _col_block(cols[name], D)
        return pl.BlockSpec((tm, D), lambda i: (i, c0))

    return pl.pallas_call(
        _merge_kernel,
        grid=(M // tm,),
        in_specs=[rows(attn.shape[1]), rows(ssm.shape[1]), gate_spec("ga"), gate_spec("gs"), rows(D),
                  whole(wa), whole(ws), whole(wo), whole(npost), whole(npre)],
        out_specs=[rows(D), rows(D)],
        out_shape=[jax.ShapeDtypeStruct((M, D), F32), jax.ShapeDtypeStruct((M, D), BF16)],
        compiler_params=_cparams(("arbitrary",)),
        name="merge",
    )(attn, ssm, proj, proj, x, wa, ws, wo, npost, npre)


def _ffn_kernel(h_ref, x_ref, wgu_ref, wd_ref, npost_ref, y_ref):
    f = pl.program_id(1)
    tf = wd_ref.shape[0]
    gu = jnp.dot(h_ref[...], wgu_ref[...], preferred_element_type=F32)
    act = _silu(gu[:, :tf]) * gu[:, tf:]

    @pl.when(f == 0)
    def _():
        y_ref[...] = jnp.zeros(y_ref.shape, F32)

    y_ref[...] += jnp.dot(act.astype(BF16), wd_ref[...], preferred_element_type=F32)

    @pl.when(f == pl.num_programs(1) - 1)
    def _():
        y_ref[...] = x_ref[...] + _rms(y_ref[...], npost_ref[...])


def _ffn(h2, x1, w_gu, w_d, npost, tm):
    M, D = x1.shape
    n_f, tf, _ = w_d.shape
    return pl.pallas_call(
        _ffn_kernel,
        grid=(M // tm, n_f),
        in_specs=[
            pl.BlockSpec((tm, D), lambda i, f: (i, 0), pipeline_mode=pl.Buffered(1)),
            pl.BlockSpec((tm, D), lambda i, f: (i, 0), pipeline_mode=pl.Buffered(1)),
            pl.BlockSpec((None, D, 2 * tf), lambda i, f: (f, 0, 0)),
            pl.BlockSpec((None, tf, D), lambda i, f: (f, 0, 0)),
            pl.BlockSpec((1, D), lambda i, f: (0, 0)),
        ],
        out_specs=pl.BlockSpec((tm, D), lambda i, f: (i, 0)),
        out_shape=jax.ShapeDtypeStruct((M, D), F32),
        compiler_params=_cparams(("arbitrary", "arbitrary")),
        name="ffn",
    )(h2, x1, w_gu, w_d, npost)


def _pad_lanes(v):
    row = jnp.zeros((1, LANES), F32).at[0, :v.shape[0]].set(v.astype(F32))
    return row, row.reshape(LANES, 1)


def kernel(x_prompt, x_sample, cache_k, cache_v, state_conv, state_ssm, page_table, norm_mix_pre, w_in, conv_w, conv_b, dt_bias, a_log, d_skip, ssm_norm, w_attn_out, w_ssm_out, w_out, norm_mix_post, norm_ffn_pre, w_gate, w_up, w_down, norm_ffn_post):
    depth = w_in.shape[0]
    assert depth == 1, "single trunk layer"
    b_p, seq, d_model = x_prompt.shape
    b_s, dec_seq, _ = x_sample.shape
    assert dec_seq == 1
    n_heads_attn = cache_k.shape[3]
    attn_w = n_heads_attn * cache_k.shape[4]
    assert cache_k.shape[4] == HEAD_DIM and cache_k.shape[2] == PAGE_SIZE
    conv_dim = conv_w.shape[2]
    n_ssm_heads = dt_bias.shape[1]
    d_inner = n_ssm_heads * SSM_HEADDIM
    assert conv_dim == d_inner + 2 * N_SSM_GROUPS * D_STATE and n_ssm_heads <= LANES
    n_pages = page_table.shape[1]
    assert (n_pages * PAGE_SIZE) % MOBA_BLOCK == 0
    n_sel = min(MOBA_TOPK, n_pages * PAGE_SIZE // MOBA_BLOCK)
    assert n_sel == MOBA_TOPK

    l = 0
    src, off = {}, 0
    for name, w in (("q", attn_w), ("k", attn_w), ("v", attn_w), ("z", d_inner), ("xbc", conv_dim),
                    ("dt", n_ssm_heads), ("ga", d_model), ("gs", d_model)):
        src[name] = (off, w)
        off += w
    assert off == w_in.shape[2]
    w_l = w_in[l]
    cols, parts, off = {}, [], 0
    for name in ("xbc", "z", "ga", "gs", "q", "k", "v", "dt"):
        s0, w = src[name]
        cols[name] = off
        parts.append(w_l[:, s0:s0 + w])
        off += w
    n_proj = -(-(cols["dt"] + LANES) // PROJ_TN) * PROJ_TN
    parts.append(jnp.zeros((d_model, n_proj - off), F32))
    w_proj = jnp.concatenate(parts, axis=1).astype(BF16)
    w_proj = jnp.transpose(w_proj.reshape(d_model, n_proj // PROJ_TN, PROJ_TN), (1, 0, 2))
    wa = w_attn_out[l].astype(BF16)
    ws = w_ssm_out[l].astype(BF16)
    wo = w_out[l].astype(BF16)
    d_ff = w_gate.shape[2]
    assert d_ff % FFN_TF == 0
    n_f = d_ff // FFN_TF
    tile_cols = lambda w: jnp.transpose(w.reshape(d_model, n_f, FFN_TF), (1, 0, 2))
    w_gu = jnp.concatenate([tile_cols(w_gate[l]), tile_cols(w_up[l])], axis=2).astype(BF16)
    w_d = w_down[l].astype(BF16).reshape(n_f, FFN_TF, d_model)
    g_pre = norm_mix_pre[l].reshape(1, d_model)
    g_post = norm_mix_post[l].reshape(1, d_model)
    g_fpre = norm_ffn_pre[l].reshape(1, d_model)
    g_fpost = norm_ffn_post[l].reshape(1, d_model)
    cw = conv_w[l]
    cb = conv_b[l].reshape(1, conv_dim)
    dtb_r, dtb_c = _pad_lanes(dt_bias[l])
    alog_r, alog_c = _pad_lanes(a_log[l])
    dskip = jnp.repeat(d_skip[l].astype(F32), SSM_HEADDIM).reshape(1, d_inner)
    snorm = ssm_norm[l].reshape(1, d_inner)

    def dense_tail(attn, ssm, proj, x, tm_merge, tm_ffn):
        x1, h2 = _merge(attn, ssm, proj, cols, x, wa, ws, wo, g_post, g_fpre, tm_merge)
        return _ffn(h2, x1, w_gu, w_d, g_fpost, tm_ffn)

    m_p = b_p * seq
    xp = x_prompt.reshape(m_p, d_model)
    proj = _in_proj(xp, g_pre, w_proj, tm=1024)
    attn, kt, vt = _moba_prompt(proj, cols, attn_w, b_p, seq)
    u, conv_p, ssm_p = _ssd_prompt(proj, cols, cw, cb, dtb_r, dtb_c, alog_r, alog_c, dskip, snorm, b_p, seq)
    y_p = dense_tail(attn, u, proj, xp, 256, 1024)

    xs = x_sample.reshape(b_s, d_model)
    proj_s = _in_proj(xs, g_pre, w_proj, tm=b_s)
    group = lambda name, w: proj_s[:, cols[name]:cols[name] + w]
    q_s, k_s, v_s = group("q", attn_w), group("k", attn_w), group("v", attn_w)
    z_s, xbc_s, dt_s = group("z", d_inner), group("xbc", conv_dim), group("dt", LANES)
    hd = (b_s, n_heads_attn, 1, HEAD_DIM)
    q4, k4, v4 = q_s.reshape(hd), k_s.reshape(hd), v_s.reshape(hd)
    ck = jnp.transpose(cache_k.reshape(cache_k.shape[1:]), (0, 2, 3, 1))
    cv = jnp.transpose(cache_v.reshape(cache_v.shape[1:]), (0, 2, 3, 1))
    idx = _moba_select(q4, ck, page_table, n_sel)
    attn_s = _moba_sample(q4, k4, v4, ck, cv, page_table, idx.reshape(b_s, n_sel * n_heads_attn))
    u_s, conv_s, ssm_s = _ssd_step(xbc_s, z_s, dt_s, state_conv[l], state_ssm[l], cw, cb,
                                   dtb_r, alog_r, dskip, snorm)
    y_s = dense_tail(attn_s.reshape(b_s, attn_w).astype(BF16), u_s, proj_s, xs, b_s, b_s)

    def kv_prompt(t):
        return jnp.transpose(t.reshape(b_p, n_heads_attn, HEAD_DIM, seq), (0, 3, 1, 2))[None]

    kv_s = (1, b_s, 1, n_heads_attn, HEAD_DIM)
    return (y_p.reshape(b_p, seq, d_model), y_s.reshape(b_s, 1, d_model),
            kv_prompt(kt), kv_prompt(vt), conv_p[None], ssm_p[None],
            k_s.reshape(kv_s), v_s.reshape(kv_s), conv_s[None], ssm_s[None])
```

```python
import functools
import math

import jax
import jax.numpy as jnp
from jax import lax
from jax.experimental import pallas as pl
from jax.experimental.pallas import tpu as pltpu

F32 = jnp.float32
BF16 = jnp.bfloat16

EPS = 1e-6
HEAD_DIM = 64
MOBA_BLOCK = 256
MOBA_TOPK = 3
PAGE_SIZE = 128
SSM_HEADDIM = 64
N_SSM_GROUPS = 8
D_STATE = 128
CONV_WIDTH = 4
SSD_CHUNK = 128

LANES = 128
SUBLANES = 8
VMEM_LIMIT = 56 * 1024 * 1024
PROJ_TN = 1536
FFN_TF = 512

NEG_BIG = -(2.0 ** 100)
ROW_GROUP = 16

_NT = (((1,), (1,)), ((), ()))
_TN = (((0,), (0,)), ((), ()))


def _cparams(sem):
    return pltpu.CompilerParams(dimension_semantics=sem, vmem_limit_bytes=VMEM_LIMIT)


def _rms(x, g):
    return x * lax.rsqrt(jnp.mean(x * x, axis=-1, keepdims=True) + EPS) * g


def _silu(x):
    return x * jax.nn.sigmoid(x)


def _softplus(x):
    return jnp.maximum(x, 0.0) + jnp.log1p(jnp.exp(-jnp.abs(x)))


def _eye(n):
    return lax.broadcasted_iota(jnp.int32, (n, n), 0) == lax.broadcasted_iota(jnp.int32, (n, n), 1)


def _row_to_col(row):
    n = row.shape[1]
    return jnp.sum(jnp.where(_eye(n), jnp.broadcast_to(row, (n, n)), 0.0), axis=1, keepdims=True)


def _col_to_row(col):
    n = col.shape[0]
    return jnp.sum(jnp.where(_eye(n), jnp.broadcast_to(col, (n, n)), 0.0), axis=0, keepdims=True)


def _in_proj_kernel(x_ref, g_ref, w_ref, o_ref, h_scr):
    @pl.when(pl.program_id(1) == 0)
    def _():
        h_scr[...] = _rms(x_ref[...], g_ref[...]).astype(BF16)

    o_ref[...] = jnp.dot(h_scr[...], w_ref[...], preferred_element_type=F32)


def _in_proj(x, g, w, tm):
    M, D = x.shape
    N = w.shape[1]
    tn = PROJ_TN
    assert N % tn == 0
    return pl.pallas_call(
        _in_proj_kernel,
        grid=(M // tm, N // tn),
        in_specs=[
            pl.BlockSpec((tm, D), lambda i, j: (i, 0)),
            pl.BlockSpec((1, D), lambda i, j: (0, 0)),
            pl.BlockSpec((D, tn), lambda i, j: (0, j)),
        ],
        out_specs=pl.BlockSpec((tm, tn), lambda i, j: (i, j)),
        out_shape=jax.ShapeDtypeStruct((M, N), F32),
        scratch_shapes=[pltpu.VMEM((tm, D), BF16)],
        compiler_params=_cparams(("arbitrary", "arbitrary")),
        name="in_proj",
    )(x, g, w)


def _moba_prompt_kernel(pt_ref, q_ref, k_ref, v_ref, qs_ref, ck_ref, o_ref, kt_ref, vt_ref, g_ref,
                        s_scr, p_scr, l_scr, o_scr, scan_buf, scan_sem, *, L, scale, n_pages):
    BS = MOBA_BLOCK
    nb = L // BS
    assert nb <= HEAD_DIM
    n_sub = 2 * nb
    ppb = BS // PAGE_SIZE
    ch = n_pages // n_sub
    assert n_pages % n_sub == 0 and ch % ppb == 0 and n_sub % 2 == 0
    step = pl.program_id(0) * pl.num_programs(1) + pl.program_id(1)
    n_steps = pl.num_programs(0) * pl.num_programs(1)
    _, n_h, d_h, T = qs_ref.shape

    def scan_copies(seq, j, slot):
        return [pltpu.make_async_copy(ck_ref.at[pt_ref[seq, j * ch + t]], scan_buf.at[slot, t],
                                      scan_sem.at[slot]) for t in range(ch)]

    def scan_chunk(j):
        slot = j % 2
        if j + 1 < n_sub:
            for cp in scan_copies(step, j + 1, 1 - slot):
                cp.start()
        else:
            @pl.when(step + 1 < n_steps)
            def _():
                for cp in scan_copies(step + 1, 0, 1 - slot):
                    cp.start()
        for cp in scan_copies(step, j, slot):
            cp.wait()
        for h in range(n_h):
            qh = qs_ref[0, h]
            for u in range(ch // ppb):
                part = None
                for t in range(ppb):
                    prod = scan_buf[slot, u * ppb + t, h] * qh
                    pp = jnp.sum(prod.reshape(d_h // SUBLANES, SUBLANES, T), axis=0)
                    part = pp if part is None else part + pp
                g_ref[0, j * (ch // ppb) + u, h:h + 1, :] = \
                    jnp.sum(part, axis=0, keepdims=True) * (1.0 / BS)

    @pl.when(step == 0)
    def _():
        for cp in scan_copies(step, 0, 0):
            cp.start()

    q = q_ref[...]
    k = k_ref[...]
    v = v_ref[...]
    vb = v.astype(BF16)
    kt_ref[0] = k.T
    vt_ref[0] = v.T
    lane = lax.broadcasted_iota(jnp.int32, (1, LANES), 1)
    first = lane < HEAD_DIM

    kmean = jnp.concatenate(
        [jnp.mean(k[n * BS:(n + 1) * BS], axis=0, keepdims=True) for n in range(nb)], axis=0)

    blk = lax.broadcasted_iota(jnp.int32, (nb, L), 0)
    qblk = lax.broadcasted_iota(jnp.int32, (nb, L), 1) // BS
    past = blk < qblk
    row = lax.broadcasted_iota(jnp.int32, (BS, BS), 0)
    col = lax.broadcasted_iota(jnp.int32, (BS, BS), 1)
    causal_bias = jnp.where(col <= row, 0.0, -jnp.inf)
    key_blk = lax.broadcasted_iota(jnp.int32, (L, LANES), 0) // BS
    key_lane = lax.broadcasted_iota(jnp.int32, (L, LANES), 1)
    c = scale * math.log2(math.e)

    for head, (own, off) in enumerate(((first, HEAD_DIM), (~first, 0))):
        gate = lax.dot_general(jnp.where(own, kmean, 0.0), q, _NT,
                               precision=lax.Precision.HIGHEST, preferred_element_type=F32)
        gate = jnp.where(past, gate, -jnp.inf)
        rank = jnp.zeros((nb, L), jnp.int32)
        for m in range(nb):
            gm = gate[m:m + 1, :]
            beats = (gm > gate) | ((gm == gate) & (m < blk))
            rank = rank + beats.astype(jnp.int32)
        allowed = (past & (rank < MOBA_TOPK)) | (blk == qblk)
        bias_t = jnp.where(allowed, 0.0, NEG_BIG)
        pieces = [jnp.zeros((off, L), F32), bias_t, jnp.zeros((LANES - off - nb, L), F32)]
        pad_t = jnp.concatenate([t for t in pieces if t.shape[0]], axis=0)
        q_aug = jnp.where(own, q * c, pad_t.T).astype(BF16)
        k_aug = jnp.where(own, k, (key_lane - off == key_blk).astype(F32)).astype(BF16)

        for qb in range(nb):
            qs = slice(qb * BS, (qb + 1) * BS)
            n_past = qb * BS
            nk = n_past + BS
            scan_chunk(head * nb + qb)
            if qb > 0:
                s_scr[:, :n_past] = lax.dot_general(q_aug[qs], k_aug[:n_past], _NT,
                                                    preferred_element_type=F32)
            s_scr[:, n_past:nk] = lax.dot_general(q_aug[qs], k_aug[n_past:nk], _NT,
                                                  preferred_element_type=F32) + causal_bias

            def softmax_rows(r, carry, nk=nk):
                rows = pl.ds(pl.multiple_of(r * ROW_GROUP, ROW_GROUP), ROW_GROUP)
                s = s_scr[rows, :nk]
                p = jnp.exp2(s - jnp.max(s, axis=1, keepdims=True))
                l_scr[rows, :] = jnp.broadcast_to(jnp.sum(p, axis=1, keepdims=True), (ROW_GROUP, LANES))
                p_scr[rows, :nk] = p.astype(BF16)
                return carry

            lax.fori_loop(0, BS // ROW_GROUP, softmax_rows, 0, unroll=True)
            o = jnp.dot(p_scr[:, :nk], vb[:nk], preferred_element_type=F32) / l_scr[...]
            if head == 0:
                o_scr[qs, :] = o
            else:
                o_ref[qs, :] = jnp.where(first, o_scr[qs, :], o).astype(o_ref.dtype)


def _col_block(off, width):
    assert off % width == 0
    return off // width


def _moba_prompt(proj, cols, W, batch, L, q_sample, cache_kt, page_table):
    M = proj.shape[0]
    B, H, d, T = q_sample.shape
    n_pages = page_table.shape[1]
    n_hp = W // LANES
    n_blocks = n_pages * PAGE_SIZE // MOBA_BLOCK
    assert L % MOBA_BLOCK == 0 and W % LANES == 0 and LANES == 2 * HEAD_DIM
    assert batch * n_hp == B, "one sample sequence is scanned per grid step"
    ch = n_pages // (2 * (L // MOBA_BLOCK))

    def in_spec(name):
        c0 = _col_block(cols[name], LANES)
        return pl.BlockSpec((L, LANES), lambda b, hp, pt: (b, c0 + hp))

    seq4 = lambda b, hp, pt: (b * n_hp + hp, 0, 0, 0)
    spec_t = pl.BlockSpec((1, LANES, L), lambda b, hp, pt: (b, hp, 0))
    kv_t = jax.ShapeDtypeStruct((batch, W, L), F32)
    grid_spec = pltpu.PrefetchScalarGridSpec(
        num_scalar_prefetch=1,
        grid=(batch, n_hp),
        in_specs=[in_spec("q"), in_spec("k"), in_spec("v"),
                  pl.BlockSpec((1, H, d, T), seq4), pl.BlockSpec(memory_space=pl.ANY)],
        out_specs=[pl.BlockSpec((L, LANES), lambda b, hp, pt: (b, hp)), spec_t, spec_t,
                   pl.BlockSpec((1, n_blocks, H, T), seq4)],
        scratch_shapes=[pltpu.VMEM((MOBA_BLOCK, L), F32), pltpu.VMEM((MOBA_BLOCK, L), BF16),
                        pltpu.VMEM((MOBA_BLOCK, LANES), F32), pltpu.VMEM((L, LANES), F32),
                        pltpu.VMEM((2, ch, H, d, T), F32), pltpu.SemaphoreType.DMA((2,))],
    )
    return pl.pallas_call(
        functools.partial(_moba_prompt_kernel, L=L, scale=HEAD_DIM ** -0.5, n_pages=n_pages),
        grid_spec=grid_spec,
        out_shape=[jax.ShapeDtypeStruct((M, W), BF16), kv_t, kv_t,
                   jax.ShapeDtypeStruct((B, n_blocks, H, T), F32)],
        compiler_params=_cparams(("arbitrary", "arbitrary")),
        name="moba_prompt",
    )(page_table, proj, proj, proj, q_sample, cache_kt)


def _ssd_prompt_kernel(xbc_ref, z_ref, dt_ref, cw_ref, cb_ref, dtb_r_ref, dtb_c_ref,
                       alog_r_ref, alog_c_ref, dskip_ref, norm_ref,
                       u_ref, conv_ref, st_ref, xs_scr, *, d_inner):
    Q = SSD_CHUNK
    P = SSM_HEADDIM
    N = D_STATE
    G = N_SSM_GROUPS
    R = d_inner // P // G
    assert R % 2 == 0 and 2 * P == LANES and N == LANES
    c = pl.program_id(1)
    tail = CONV_WIDTH - 1

    @pl.when(c == 0)
    def _():
        xs_scr[0:8, :] = jnp.zeros((8, xs_scr.shape[1]), F32)
        st_ref[...] = jnp.zeros(st_ref.shape, F32)

    xs_scr[8:8 + Q, :] = xbc_ref[...]
    acc = cw_ref[0:1, :] * xs_scr[8 - tail:8 - tail + Q, :]
    for i in range(1, CONV_WIDTH):
        acc = acc + cw_ref[i:i + 1, :] * xs_scr[8 - tail + i:8 - tail + i + Q, :]
    xc = _silu(cb_ref[...] + acc)
    last_rows = xbc_ref[Q - tail:Q, :]
    xs_scr[8 - tail:8, :] = last_rows
    conv_ref[0] = last_rows

    raw = dt_ref[...]
    dt = _softplus(raw + dtb_r_ref[...])
    dt_t = _softplus(raw.T + dtb_c_ref[...])
    a = dt * (-jnp.exp(alog_r_ref[...]))
    a_t = dt_t * (-jnp.exp(alog_c_ref[...]))
    ri = lax.broadcasted_iota(jnp.int32, (Q, Q), 0)
    ci = lax.broadcasted_iota(jnp.int32, (Q, Q), 1)
    causal = ci <= ri
    acum = jnp.dot(causal.astype(F32), a, precision=lax.Precision.HIGHEST,
                   preferred_element_type=F32)
    acum_t = jnp.dot(a_t, (ri <= ci).astype(F32), precision=lax.Precision.HIGHEST,
                     preferred_element_type=F32)
    e_acum = jnp.exp(acum)
    d_end = jnp.exp(acum[Q - 1:Q, :] - acum)
    lane = lax.broadcasted_iota(jnp.int32, (1, LANES), 1)
    first = lane < P

    def pair_cols(t, h0):
        return jnp.where(first, t[:, h0:h0 + 1], t[:, h0 + 1:h0 + 2])

    y_parts = []
    for g in range(G):
        b_g = xc[:, d_inner + g * N:d_inner + (g + 1) * N].astype(BF16)
        c_g = xc[:, d_inner + G * N + g * N:d_inner + G * N + (g + 1) * N].astype(BF16)
        cb = lax.dot_general(c_g, b_g, _NT, preferred_element_type=F32)
        st_g = st_ref[0, g * R:(g + 1) * R].reshape(R * P, N)
        y_off = lax.dot_general(c_g, st_g.astype(BF16), _NT, preferred_element_type=F32)
        xdtd_parts, cd_parts = [], []
        for pr in range(R // 2):
            h0 = g * R + 2 * pr
            x_p = xc[:, h0 * P:h0 * P + LANES]
            xdt = x_p * pair_cols(dt, h0)
            xdt_b = xdt.astype(BF16)
            yd = []
            for hh in (h0, h0 + 1):
                seg = acum[:, hh:hh + 1] - acum_t[hh:hh + 1, :]
                dec = jnp.exp(jnp.where(causal, seg, -jnp.inf))
                yd.append(jnp.dot((cb * dec).astype(BF16), xdt_b, preferred_element_type=F32))
                cd_parts.append(jnp.broadcast_to(jnp.exp(acum_t[hh:hh + 1, Q - 1:Q]), (P, N)))
            y_p = (jnp.where(first, yd[0], yd[1])
                   + y_off[:, 2 * pr * P:2 * pr * P + LANES] * pair_cols(e_acum, h0)
                   + dskip_ref[:, h0 * P:h0 * P + LANES] * x_p)
            y_parts.append(y_p)
            xdtd_parts.append((xdt * pair_cols(d_end, h0)).astype(BF16))
        xdtd = jnp.concatenate(xdtd_parts, axis=1)
        s_new = lax.dot_general(xdtd, b_g, _TN, preferred_element_type=F32)
        st_new = jnp.concatenate(cd_parts, axis=0) * st_g + s_new
        st_ref[0, g * R:(g + 1) * R] = st_new.reshape(R, P, N)

    y = jnp.concatenate(y_parts, axis=1)
    u = y * _silu(z_ref[...])
    gw = d_inner // G
    u_parts = []
    for g in range(G):
        ug = u[:, g * gw:(g + 1) * gw]
        u_parts.append(ug * lax.rsqrt(jnp.mean(ug * ug, axis=-1, keepdims=True) + EPS))
    u_ref[...] = (jnp.concatenate(u_parts, axis=1) * norm_ref[...]).astype(u_ref.dtype)


def _ssd_prompt(proj, cols, conv_w, conv_b, dtb_r, dtb_c, alog_r, alog_c, dskip, norm, batch, L):
    M = proj.shape[0]
    conv_dim = conv_w.shape[1]
    d_inner = norm.shape[1]
    n_heads = d_inner // SSM_HEADDIM
    assert L % SSD_CHUNK == 0
    nc = L // SSD_CHUNK
    Q = SSD_CHUNK
    tail = CONV_WIDTH - 1
    row = lambda b, c: (b * nc + c, 0)
    const = lambda b, c: (0, 0)

    def in_spec(name, width):
        c0 = _col_block(cols[name], width)
        return pl.BlockSpec((Q, width), lambda b, c: (b * nc + c, c0))

    xbc = z = dt_raw = proj
    return pl.pallas_call(
        functools.partial(_ssd_prompt_kernel, d_inner=d_inner),
        grid=(batch, nc),
        in_specs=[
            in_spec("xbc", conv_dim),
            in_spec("z", d_inner),
            in_spec("dt", LANES),
            pl.BlockSpec((CONV_WIDTH, conv_dim), const),
            pl.BlockSpec((1, conv_dim), const),
            pl.BlockSpec((1, LANES), const),
            pl.BlockSpec((LANES, 1), const),
            pl.BlockSpec((1, LANES), const),
            pl.BlockSpec((LANES, 1), const),
            pl.BlockSpec((1, d_inner), const),
            pl.BlockSpec((1, d_inner), const),
        ],
        out_specs=[
            pl.BlockSpec((Q, d_inner), row),
            pl.BlockSpec((1, tail, conv_dim), lambda b, c: (b, 0, 0)),
            pl.BlockSpec((1, n_heads, SSM_HEADDIM, D_STATE), lambda b, c: (b, 0, 0, 0)),
        ],
        out_shape=[
            jax.ShapeDtypeStruct((M, d_inner), BF16),
            jax.ShapeDtypeStruct((batch, tail, conv_dim), F32),
            jax.ShapeDtypeStruct((batch, n_heads, SSM_HEADDIM, D_STATE), F32),
        ],
        scratch_shapes=[pltpu.VMEM((8 + Q, conv_dim), F32)],
        compiler_params=_cparams(("arbitrary", "arbitrary")),
        name="ssd_prompt",
    )(xbc, z, dt_raw, conv_w, conv_b, dtb_r, dtb_c, alog_r, alog_c, dskip, norm)


def _ssd_step_kernel(xbc_ref, z_ref, dt_ref, cprev_ref, sprev_ref, cw_ref, cb_ref, dtb_ref,
                     alog_ref, dskip_ref, norm_ref, u_ref, conv_ref, st_ref, *, d_inner):
    P = SSM_HEADDIM
    N = D_STATE
    G = N_SSM_GROUPS
    n_heads = d_inner // P
    R = n_heads // G
    assert R % 2 == 0 and 2 * P == LANES and N == LANES
    tail = CONV_WIDTH - 1
    x_new = xbc_ref[0]
    prev = cprev_ref[0]
    acc = cw_ref[0:1, :] * prev[0:1, :]
    for i in range(1, tail):
        acc = acc + cw_ref[i:i + 1, :] * prev[i:i + 1, :]
    acc = acc + cw_ref[tail:tail + 1, :] * x_new
    xc = _silu(cb_ref[...] + acc)
    conv_ref[0, 0:tail - 1, :] = prev[1:tail, :]
    conv_ref[0, tail - 1:tail, :] = x_new

    dt = _softplus(dt_ref[0] + dtb_ref[...])
    decay = jnp.exp(dt * (-jnp.exp(alog_ref[...])))
    ri = lax.broadcasted_iota(jnp.int32, (LANES, LANES), 0)
    ci = lax.broadcasted_iota(jnp.int32, (LANES, LANES), 1)
    eye = ri == ci
    first_rows = lax.broadcasted_iota(jnp.int32, (LANES, 1), 0) < P

    y_parts = []
    for hp in range(n_heads // 2):
        h0 = 2 * hp
        g = h0 // R
        b_g = xc[:, d_inner + g * N:d_inner + (g + 1) * N]
        c_g = xc[:, d_inner + G * N + g * N:d_inner + G * N + (g + 1) * N]
        x_row = xc[:, h0 * P:h0 * P + LANES]
        x_col = jnp.sum(jnp.where(eye, jnp.broadcast_to(x_row, (LANES, LANES)), 0.0),
                        axis=1, keepdims=True)
        dt_col = jnp.where(first_rows, dt[:, h0:h0 + 1], dt[:, h0 + 1:h0 + 2])
        dec_col = jnp.where(first_rows, decay[:, h0:h0 + 1], decay[:, h0 + 1:h0 + 2])
        st = sprev_ref[0, h0:h0 + 2].reshape(2 * P, N)
        st_new = dec_col * st + (dt_col * x_col) * b_g
        st_ref[0, h0:h0 + 2] = st_new.reshape(2, P, N)
        y_col = jnp.sum(st_new * c_g, axis=1, keepdims=True)
        y_row = jnp.sum(jnp.where(eye, jnp.broadcast_to(y_col, (LANES, LANES)), 0.0),
                        axis=0, keepdims=True)
        y_parts.append(y_row + dskip_ref[:, h0 * P:h0 * P + LANES] * x_row)
    y = jnp.concatenate(y_parts, axis=1)
    u = y * _silu(z_ref[0])
    gw = d_inner // G
    u_parts = []
    for g in range(G):
        ug = u[:, g * gw:(g + 1) * gw]
        u_parts.append(ug * lax.rsqrt(jnp.mean(ug * ug, axis=-1, keepdims=True) + EPS))
    u_ref[0] = (jnp.concatenate(u_parts, axis=1) * norm_ref[...]).astype(u_ref.dtype)


def _ssd_step(xbc, z, dt_raw, conv_prev, ssm_prev, conv_w, conv_b, dtb_r, alog_r, dskip, norm):
    B, conv_dim = xbc.shape
    d_inner = z.shape[1]
    n_heads = d_inner // SSM_HEADDIM
    tail = CONV_WIDTH - 1
    const = lambda b: (0, 0)
    b3 = lambda b: (b, 0, 0)
    b4 = lambda b: (b, 0, 0, 0)
    u, conv_new, ssm_new = pl.pallas_call(
        functools.partial(_ssd_step_kernel, d_inner=d_inner),
        grid=(B,),
        in_specs=[
            pl.BlockSpec((1, 1, conv_dim), b3),
            pl.BlockSpec((1, 1, d_inner), b3),
            pl.BlockSpec((1, 1, LANES), b3),
            pl.BlockSpec((1, tail, conv_dim), b3),
            pl.BlockSpec((1, n_heads, SSM_HEADDIM, D_STATE), b4),
            pl.BlockSpec((CONV_WIDTH, conv_dim), const),
            pl.BlockSpec((1, conv_dim), const),
            pl.BlockSpec((1, LANES), const),
            pl.BlockSpec((1, LANES), const),
            pl.BlockSpec((1, d_inner), const),
            pl.BlockSpec((1, d_inner), const),
        ],
        out_specs=[
            pl.BlockSpec((1, 1, d_inner), b3),
            pl.BlockSpec((1, tail, conv_dim), b3),
            pl.BlockSpec((1, n_heads, SSM_HEADDIM, D_STATE), b4),
        ],
        out_shape=[
            jax.ShapeDtypeStruct((B, 1, d_inner), BF16),
            jax.ShapeDtypeStruct((B, tail, conv_dim), F32),
            jax.ShapeDtypeStruct((B, n_heads, SSM_HEADDIM, D_STATE), F32),
        ],
        compiler_params=_cparams(("arbitrary",)),
        name="ssd_step",
    )(xbc.reshape(B, 1, conv_dim), z.reshape(B, 1, d_inner), dt_raw.reshape(B, 1, LANES),
      conv_prev, ssm_prev, conv_w, conv_b, dtb_r, alog_r, dskip, norm)
    return u.reshape(B, d_inner), conv_new, ssm_new


def _moba_topk_kernel(g_ref, idx_ref, *, n_sel):
    gate = jnp.sum(g_ref[0], axis=-1, keepdims=True)
    n_blocks = gate.shape[0]
    blk = lax.broadcasted_iota(jnp.int32, gate.shape, 0)
    rank = jnp.zeros(gate.shape, jnp.int32)
    for m in range(n_blocks):
        gm = gate[m:m + 1]
        beats = (gm > gate) | ((gm == gate) & (m < blk))
        rank = rank + beats.astype(jnp.int32)
    for r in range(n_sel):
        idx_ref[0, r] = jnp.sum(jnp.where(rank == r, blk, 0), axis=0)


def _moba_topk(gates, n_sel):
    B, n_blocks, H, T = gates.shape
    return pl.pallas_call(
        functools.partial(_moba_topk_kernel, n_sel=n_sel),
        grid=(B,),
        in_specs=[pl.BlockSpec((1, n_blocks, H, T), lambda b: (b, 0, 0, 0))],
        out_specs=pl.BlockSpec((1, n_sel, H, 1), lambda b: (b, 0, 0, 0)),
        out_shape=jax.ShapeDtypeStruct((B, n_sel, H, 1), jnp.int32),
        compiler_params=_cparams(("arbitrary",)),
        name="moba_topk",
    )(gates)


def _moba_sample_kernel(pt_ref, idx_ref, q_ref, kn_ref, vn_ref, ck_ref, cv_ref, o_ref,
                        kbuf, vbuf, sem, *, n_sel, n_heads, scale):
    b = pl.program_id(0)
    n_b = pl.num_programs(0)
    ppb = MOBA_BLOCK // PAGE_SIZE

    def copies(bb, slot):
        out = []
        for h in range(n_heads):
            for r in range(n_sel):
                blk = idx_ref[bb, r * n_heads + h]
                for t in range(ppb):
                    page = pt_ref[bb, blk * ppb + t]
                    j = r * ppb + t
                    out.append(pltpu.make_async_copy(ck_ref.at[page, h], kbuf.at[slot, h, j], sem.at[0, slot]))
                    out.append(pltpu.make_async_copy(cv_ref.at[page, h], vbuf.at[slot, h, j], sem.at[1, slot]))
        return out

    slot = b % 2

    @pl.when(b == 0)
    def _():
        for cp in copies(b, slot):
            cp.start()

    @pl.when(b + 1 < n_b)
    def _():
        for cp in copies(b + 1, 1 - slot):
            cp.start()

    for cp in copies(b, slot):
        cp.wait()

    def body(h, carry):
        q_row = q_ref[0, h]
        qc = _row_to_col(q_row)
        kh = kbuf[slot, h]
        vh = vbuf[slot, h]
        s = jnp.sum(kh * qc, axis=1, keepdims=True) * scale
        s_own = jnp.sum(kn_ref[0, h] * q_row, axis=-1, keepdims=True) * scale
        m = jnp.maximum(jnp.max(jnp.max(s, axis=0), axis=-1, keepdims=True), s_own)
        p = jnp.exp(s - m)
        p_own = jnp.exp(s_own - m)
        l = jnp.sum(jnp.sum(p, axis=0), axis=-1, keepdims=True) + p_own
        o_past = jnp.sum(jnp.sum(vh * p, axis=0), axis=-1, keepdims=True)
        o_ref[0, h] = (_col_to_row(o_past) + p_own * vn_ref[0, h]) / l
        return carry

    lax.fori_loop(0, n_heads, body, 0, unroll=True)


def _moba_sample(q4, k4, v4, cache_kt, cache_vt, page_table, idx):
    B, H, _, d = q4.shape
    n_sel = idx.shape[1] // H
    n_slabs = n_sel * (MOBA_BLOCK // PAGE_SIZE)
    new = pl.BlockSpec((1, H, 1, d), lambda b, pt, ix: (b, 0, 0, 0))
    grid_spec = pltpu.PrefetchScalarGridSpec(
        num_scalar_prefetch=2,
        grid=(B,),
        in_specs=[new, new, new, pl.BlockSpec(memory_space=pl.ANY), pl.BlockSpec(memory_space=pl.ANY)],
        out_specs=new,
        scratch_shapes=[pltpu.VMEM((2, H, n_slabs, d, PAGE_SIZE), F32),
                        pltpu.VMEM((2, H, n_slabs, d, PAGE_SIZE), F32),
                        pltpu.SemaphoreType.DMA((2, 2))],
    )
    return pl.pallas_call(
        functools.partial(_moba_sample_kernel, n_sel=n_sel, n_heads=H, scale=d ** -0.5),
        grid_spec=grid_spec,
        out_shape=jax.ShapeDtypeStruct((B, H, 1, d), F32),
        compiler_params=_cparams(("arbitrary",)),
        name="moba_sample",
    )(page_table, idx, q4, k4, v4, cache_kt, cache_vt)


def _merge_kernel(attn_ref, ssm_ref, ga_ref, gs_ref, x_ref, wa_ref, ws_ref, wo_ref,
                  npost_ref, npre_ref, x1_ref, h2_ref):
    a = jnp.dot(attn_ref[...], wa_ref[...], preferred_element_type=F32)
    s = jnp.dot(ssm_ref[...], ws_ref[...], preferred_element_type=F32)
    merged = jax.nn.sigmoid(ga_ref[...]) * a + jax.nn.sigmoid(gs_ref[...]) * s
    o = jnp.dot(merged.astype(BF16), wo_ref[...], preferred_element_type=F32)
    x1 = x_ref[...] + _rms(o, npost_ref[...])
    x1_ref[...] = x1
    h2_ref[...] = _rms(x1, npre_ref[...]).astype(BF16)


def _merge(attn, ssm, proj, cols, x, wa, ws, wo, npost, npre, tm):
    M, D = x.shape
    rows = lambda w: pl.BlockSpec((tm, w), lambda i: (i, 0))
    whole = lambda a: pl.BlockSpec(a.shape, lambda i: (0, 0), pipeline_mode=pl.Buffered(1))

    def gate_spec(name):
        c0 = _col_block(cols[name], D)
        return pl.BlockSpec((tm, D), lambda i: (i, c0))

    return pl.pallas_call(
        _merge_kernel,
        grid=(M // tm,),
        in_specs=[rows(attn.shape[1]), rows(ssm.shape[1]), gate_spec("ga"), gate_spec("gs"), rows(D),
                  whole(wa), whole(ws), whole(wo), whole(npost), whole(npre)],
        out_specs=[rows(D), rows(D)],
        out_shape=[jax.ShapeDtypeStruct((M, D), F32), jax.ShapeDtypeStruct((M, D), BF16)],
        compiler_params=_cparams(("arbitrary",)),
        name="merge",
    )(attn, ssm, proj, proj, x, wa, ws, wo, npost, npre)


def _ffn_kernel(h_ref, x_ref, wg_ref, wu_ref, wd_ref, npost_ref, y_ref):
    f = pl.program_id(1)
    h = h_ref[...]
    act = _silu(jnp.dot(h, wg_ref[...], preferred_element_type=F32)) * \
        jnp.dot(h, wu_ref[...], preferred_element_type=F32)

    @pl.when(f == 0)
    def _():
        y_ref[...] = jnp.zeros(y_ref.shape, F32)

    y_ref[...] += jnp.dot(act.astype(BF16), wd_ref[...], preferred_element_type=F32)

    @pl.when(f == pl.num_programs(1) - 1)
    def _():
        y_ref[...] = x_ref[...] + _rms(y_ref[...], npost_ref[...])


def _ffn(h2, x1, w_g, w_u, w_d, npost, tm):
    M, D = x1.shape
    d_ff = w_g.shape[1]
    tf = FFN_TF
    assert d_ff % tf == 0
    return pl.pallas_call(
        _ffn_kernel,
        grid=(M // tm, d_ff // tf),
        in_specs=[
            pl.BlockSpec((tm, D), lambda i, f: (i, 0), pipeline_mode=pl.Buffered(1)),
            pl.BlockSpec((tm, D), lambda i, f: (i, 0), pipeline_mode=pl.Buffered(1)),
            pl.BlockSpec((D, tf), lambda i, f: (0, f)),
            pl.BlockSpec((D, tf), lambda i, f: (0, f)),
            pl.BlockSpec((tf, D), lambda i, f: (f, 0)),
            pl.BlockSpec((1, D), lambda i, f: (0, 0)),
        ],
        out_specs=pl.BlockSpec((tm, D), lambda i, f: (i, 0)),
        out_shape=jax.ShapeDtypeStruct((M, D), F32),
        compiler_params=_cparams(("arbitrary", "arbitrary")),
        name="ffn",
    )(h2, x1, w_g, w_u, w_d, npost)


def _pad_lanes(v):
    row = jnp.zeros((1, LANES), F32).at[0, :v.shape[0]].set(v.astype(F32))
    return row, row.reshape(LANES, 1)


def kernel(x_prompt, x_sample, cache_k, cache_v, state_conv, state_ssm, page_table, norm_mix_pre, w_in, conv_w, conv_b, dt_bias, a_log, d_skip, ssm_norm, w_attn_out, w_ssm_out, w_out, norm_mix_post, norm_ffn_pre, w_gate, w_up, w_down, norm_ffn_post):
    depth = w_in.shape[0]
    assert depth == 1, "single trunk layer"
    b_p, seq, d_model = x_prompt.shape
    b_s, dec_seq, _ = x_sample.shape
    assert dec_seq == 1
    n_heads_attn = cache_k.shape[3]
    attn_w = n_heads_attn * cache_k.shape[4]
    assert cache_k.shape[4] == HEAD_DIM and cache_k.shape[2] == PAGE_SIZE
    conv_dim = conv_w.shape[2]
    n_ssm_heads = dt_bias.shape[1]
    d_inner = n_ssm_heads * SSM_HEADDIM
    assert conv_dim == d_inner + 2 * N_SSM_GROUPS * D_STATE and n_ssm_heads <= LANES
    n_pages = page_table.shape[1]
    assert (n_pages * PAGE_SIZE) % MOBA_BLOCK == 0
    n_sel = min(MOBA_TOPK, n_pages * PAGE_SIZE // MOBA_BLOCK)
    assert n_sel == MOBA_TOPK

    l = 0
    src, off = {}, 0
    for name, w in (("q", attn_w), ("k", attn_w), ("v", attn_w), ("z", d_inner), ("xbc", conv_dim),
                    ("dt", n_ssm_heads), ("ga", d_model), ("gs", d_model)):
        src[name] = (off, w)
        off += w
    assert off == w_in.shape[2]
    w_l = w_in[l].astype(BF16)
    cols, parts, off = {}, [], 0
    for name in ("xbc", "z", "ga", "gs", "q", "k", "v", "dt"):
        s0, w = src[name]
        cols[name] = off
        parts.append(w_l[:, s0:s0 + w])
        off += w
    n_proj = -(-(cols["dt"] + LANES) // PROJ_TN) * PROJ_TN
    parts.append(jnp.zeros((d_model, n_proj - off), BF16))
    w_proj = jnp.concatenate(parts, axis=1)
    wa = w_attn_out[l].astype(BF16)
    ws = w_ssm_out[l].astype(BF16)
    wo = w_out[l].astype(BF16)
    w_g = w_gate[l].astype(BF16)
    w_u = w_up[l].astype(BF16)
    w_d = w_down[l].astype(BF16)
    g_pre = norm_mix_pre[l].reshape(1, d_model)
    g_post = norm_mix_post[l].reshape(1, d_model)
    g_fpre = norm_ffn_pre[l].reshape(1, d_model)
    g_fpost = norm_ffn_post[l].reshape(1, d_model)
    cw = conv_w[l]
    cb = conv_b[l].reshape(1, conv_dim)
    dtb_r, dtb_c = _pad_lanes(dt_bias[l])
    alog_r, alog_c = _pad_lanes(a_log[l])
    dskip = jnp.repeat(d_skip[l].astype(F32), SSM_HEADDIM).reshape(1, d_inner)
    snorm = ssm_norm[l].reshape(1, d_inner)

    def dense_tail(attn, ssm, proj, x, tm_merge, tm_ffn):
        x1, h2 = _merge(attn, ssm, proj, cols, x, wa, ws, wo, g_post, g_fpre, tm_merge)
        return _ffn(h2, x1, w_g, w_u, w_d, g_fpost, tm_ffn)

    xs = x_sample.reshape(b_s, d_model)
    proj_s = _in_proj(xs, g_pre, w_proj, tm=b_s)
    group = lambda name, w: proj_s[:, cols[name]:cols[name] + w]
    q_s, k_s, v_s = group("q", attn_w), group("k", attn_w), group("v", attn_w)
    z_s, xbc_s, dt_s = group("z", d_inner), group("xbc", conv_dim), group("dt", LANES)
    hd = (b_s, n_heads_attn, 1, HEAD_DIM)
    q4, k4, v4 = q_s.reshape(hd), k_s.reshape(hd), v_s.reshape(hd)
    q_bcast = jnp.broadcast_to(q_s.reshape(b_s, n_heads_attn, HEAD_DIM, 1),
                               (b_s, n_heads_attn, HEAD_DIM, PAGE_SIZE))
    ck = jnp.transpose(cache_k.reshape(cache_k.shape[1:]), (0, 2, 3, 1))
    cv = jnp.transpose(cache_v.reshape(cache_v.shape[1:]), (0, 2, 3, 1))

    m_p = b_p * seq
    xp = x_prompt.reshape(m_p, d_model)
    proj = _in_proj(xp, g_pre, w_proj, tm=1024)
    attn, kt, vt, gates = _moba_prompt(proj, cols, attn_w, b_p, seq, q_bcast, ck, page_table)
    u, conv_p, ssm_p = _ssd_prompt(proj, cols, cw, cb, dtb_r, dtb_c, alog_r, alog_c, dskip, snorm, b_p, seq)
    y_p = dense_tail(attn, u, proj, xp, 256, 1024)

    idx = _moba_topk(gates, n_sel)
    attn_s = _moba_sample(q4, k4, v4, ck, cv, page_table, idx.reshape(b_s, n_sel * n_heads_attn))
    u_s, conv_s, ssm_s = _ssd_step(xbc_s, z_s, dt_s, state_conv[l], state_ssm[l], cw, cb,
                                   dtb_r, alog_r, dskip, snorm)
    y_s = dense_tail(attn_s.reshape(b_s, attn_w).astype(BF16), u_s, proj_s, xs, b_s, b_s)

    def kv_prompt(t):
        return jnp.transpose(t.reshape(b_p, n_heads_attn, HEAD_DIM, seq), (0, 3, 1, 2))[None]

    kv_s = (1, b_s, 1, n_heads_attn, HEAD_DIM)
    return (y_p.reshape(b_p, seq, d_model), y_s.reshape(b_s, 1, d_model),
            kv_prompt(kt), kv_prompt(vt), conv_p[None], ssm_p[None],
            k_s.reshape(kv_s), v_s.reshape(kv_s), conv_s[None], ssm_s[None])
```

```python
import functools
import math

import jax
import jax.numpy as jnp
from jax import lax
from jax.experimental import pallas as pl
from jax.experimental.pallas import tpu as pltpu

F32 = jnp.float32
BF16 = jnp.bfloat16

EPS = 1e-6
HEAD_DIM = 64
MOBA_BLOCK = 256
MOBA_TOPK = 3
PAGE_SIZE = 128
SSM_HEADDIM = 64
N_SSM_GROUPS = 8
D_STATE = 128
CONV_WIDTH = 4
SSD_CHUNK = 128

LANES = 128
SUBLANES = 8
VMEM_LIMIT = 56 * 1024 * 1024
PROJ_TN = 768
FFN_TF = 512

NEG_BIG = -(2.0 ** 100)
ROW_GROUP = 16
SCAN_RING = 4

_NT = (((1,), (1,)), ((), ()))
_TN = (((0,), (0,)), ((), ()))


def _cparams(sem):
    return pltpu.CompilerParams(dimension_semantics=sem, vmem_limit_bytes=VMEM_LIMIT)


def _rms(x, g):
    return x * lax.rsqrt(jnp.mean(x * x, axis=-1, keepdims=True) + EPS) * g


def _silu(x):
    return x * jax.nn.sigmoid(x)


def _softplus(x):
    return jnp.maximum(x, 0.0) + jnp.log1p(jnp.exp(-jnp.abs(x)))


def _eye(n):
    return lax.broadcasted_iota(jnp.int32, (n, n), 0) == lax.broadcasted_iota(jnp.int32, (n, n), 1)


def _row_to_col(row):
    n = row.shape[1]
    return jnp.sum(jnp.where(_eye(n), jnp.broadcast_to(row, (n, n)), 0.0), axis=1, keepdims=True)


def _col_to_row(col):
    n = col.shape[0]
    return jnp.sum(jnp.where(_eye(n), jnp.broadcast_to(col, (n, n)), 0.0), axis=0, keepdims=True)


def _in_proj_kernel(x_ref, g_ref, wm_ref, wt_ref, o_ref, h_scr, *, n_main):
    j = pl.program_id(1)

    @pl.when(j == 0)
    def _():
        h_scr[...] = _rms(x_ref[...], g_ref[...]).astype(BF16)

    @pl.when(j < n_main)
    def _():
        o_ref[...] = jnp.dot(h_scr[...], wm_ref[...], preferred_element_type=F32)

    @pl.when(j >= n_main)
    def _():
        o_ref[...] = jnp.dot(h_scr[...], wt_ref[...], preferred_element_type=F32)


def _in_proj(x, g, w, n_main_cols, w_tail, tm):
    M, D = x.shape
    tn = PROJ_TN
    assert n_main_cols % tn == 0 and w_tail.shape[1] % tn == 0
    n_main = n_main_cols // tn
    n_tail = w_tail.shape[1] // tn
    return pl.pallas_call(
        functools.partial(_in_proj_kernel, n_main=n_main),
        grid=(M // tm, n_main + n_tail),
        in_specs=[
            pl.BlockSpec((tm, D), lambda i, j: (i, 0)),
            pl.BlockSpec((1, D), lambda i, j: (0, 0)),
            pl.BlockSpec((D, tn), lambda i, j: (0, jnp.minimum(j, n_main - 1))),
            pl.BlockSpec((D, tn), lambda i, j: (0, jnp.clip(j - n_main, 0, n_tail - 1))),
        ],
        out_specs=pl.BlockSpec((tm, tn), lambda i, j: (i, j)),
        out_shape=jax.ShapeDtypeStruct((M, (n_main + n_tail) * tn), F32),
        scratch_shapes=[pltpu.VMEM((tm, D), BF16)],
        compiler_params=_cparams(("arbitrary", "arbitrary")),
        name="in_proj",
    )(x, g, w, w_tail)


def _moba_prompt_kernel(pt_ref, q_ref, k_ref, v_ref, qs_ref, ck_ref, o_ref, kt_ref, vt_ref, g_ref,
                        s_scr, p_scr, l_scr, o_scr, scan_buf, scan_sem, *, L, scale, n_pages):
    BS = MOBA_BLOCK
    nb = L // BS
    assert nb <= HEAD_DIM
    n_sub = 2 * nb
    ppb = BS // PAGE_SIZE
    ch = n_pages // n_sub
    n_buf = scan_buf.shape[0]
    assert n_pages % n_sub == 0 and ch % ppb == 0 and n_sub % n_buf == 0
    step = pl.program_id(0) * pl.num_programs(1) + pl.program_id(1)
    n_steps = pl.num_programs(0) * pl.num_programs(1)
    _, n_h, d_h, T = qs_ref.shape

    def scan_start(j):
        seq, jj = (step, j) if j < n_sub else (step + 1, j - n_sub)

        def go():
            for t in range(ch):
                pltpu.make_async_copy(ck_ref.at[pt_ref[seq, jj * ch + t]], scan_buf.at[j % n_buf, t],
                                      scan_sem.at[j % n_buf]).start()

        if j < n_sub:
            go()
        else:
            pl.when(step + 1 < n_steps)(go)

    def scan_chunk(j):
        slot = j % n_buf
        scan_start(j + n_buf - 1)
        for t in range(ch):
            pltpu.make_async_copy(ck_ref.at[pt_ref[step, j * ch + t]], scan_buf.at[slot, t],
                                  scan_sem.at[slot]).wait()
        for h in range(n_h):
            qh = qs_ref[0, h]
            for u in range(ch // ppb):
                part = None
                for t in range(ppb):
                    prod = scan_buf[slot, u * ppb + t, h] * qh
                    pp = jnp.sum(prod.reshape(d_h // SUBLANES, SUBLANES, T), axis=0)
                    part = pp if part is None else part + pp
                n = j * (ch // ppb) + u
                g_ref[0, h:h + 1, n:n + 1] = jnp.sum(part, keepdims=True) * (1.0 / BS)

    @pl.when(step == 0)
    def _():
        for j in range(n_buf - 1):
            scan_start(j)

    q = q_ref[...]
    k = k_ref[...]
    v = v_ref[...]
    vb = v.astype(BF16)
    kt_ref[0] = k.T
    vt_ref[0] = v.T
    lane = lax.broadcasted_iota(jnp.int32, (1, LANES), 1)
    first = lane < HEAD_DIM

    kmean = jnp.concatenate(
        [jnp.mean(k[n * BS:(n + 1) * BS], axis=0, keepdims=True) for n in range(nb)], axis=0)

    blk = lax.broadcasted_iota(jnp.int32, (nb, L), 0)
    qblk = lax.broadcasted_iota(jnp.int32, (nb, L), 1) // BS
    past = blk < qblk
    row = lax.broadcasted_iota(jnp.int32, (BS, BS), 0)
    col = lax.broadcasted_iota(jnp.int32, (BS, BS), 1)
    causal_bias = jnp.where(col <= row, 0.0, -jnp.inf)
    key_blk = lax.broadcasted_iota(jnp.int32, (L, LANES), 0) // BS
    key_lane = lax.broadcasted_iota(jnp.int32, (L, LANES), 1)
    c = scale * math.log2(math.e)

    for head, (own, off) in enumerate(((first, HEAD_DIM), (~first, 0))):
        gate = lax.dot_general(jnp.where(own, kmean, 0.0), q, _NT,
                               precision=lax.Precision.HIGHEST, preferred_element_type=F32)
        gate = jnp.where(past, gate, -jnp.inf)
        rank = jnp.zeros((nb, L), jnp.int32)
        for m in range(nb):
            gm = gate[m:m + 1, :]
            beats = (gm > gate) | ((gm == gate) & (m < blk))
            rank = rank + beats.astype(jnp.int32)
        allowed = (past & (rank < MOBA_TOPK)) | (blk == qblk)
        bias_t = jnp.where(allowed, 0.0, NEG_BIG)
        pieces = [jnp.zeros((off, L), F32), bias_t, jnp.zeros((LANES - off - nb, L), F32)]
        pad_t = jnp.concatenate([t for t in pieces if t.shape[0]], axis=0)
        q_aug = jnp.where(own, q * c, pad_t.T).astype(BF16)
        k_aug = jnp.where(own, k, (key_lane - off == key_blk).astype(F32)).astype(BF16)

        for qb in range(nb):
            qs = slice(qb * BS, (qb + 1) * BS)
            n_past = qb * BS
            nk = n_past + BS
            scan_chunk(head * nb + qb)
            if qb > 0:
                s_scr[:, :n_past] = lax.dot_general(q_aug[qs], k_aug[:n_past], _NT,
                                                    preferred_element_type=F32)
            s_scr[:, n_past:nk] = lax.dot_general(q_aug[qs], k_aug[n_past:nk], _NT,
                                                  preferred_element_type=F32) + causal_bias

            def softmax_rows(r, carry, nk=nk):
                rows = pl.ds(pl.multiple_of(r * ROW_GROUP, ROW_GROUP), ROW_GROUP)
                s = s_scr[rows, :nk]
                p = jnp.exp2(s - jnp.max(s, axis=1, keepdims=True))
                l_scr[rows, :] = jnp.broadcast_to(jnp.sum(p, axis=1, keepdims=True), (ROW_GROUP, LANES))
                p_scr[rows, :nk] = p.astype(BF16)
                return carry

            lax.fori_loop(0, BS // ROW_GROUP, softmax_rows, 0, unroll=True)
            o = jnp.dot(p_scr[:, :nk], vb[:nk], preferred_element_type=F32) / l_scr[...]
            if head == 0:
                o_scr[qs, :] = o
            else:
                o_ref[qs, :] = jnp.where(first, o_scr[qs, :], o).astype(o_ref.dtype)


def _col_block(off, width):
    assert off % width == 0
    return off // width


def _moba_prompt(proj, cols, W, batch, L, q_sample, cache_kt, page_table):
    M = proj.shape[0]
    B, H, d, T = q_sample.shape
    n_pages = page_table.shape[1]
    n_hp = W // LANES
    n_blocks = n_pages * PAGE_SIZE // MOBA_BLOCK
    assert L % MOBA_BLOCK == 0 and W % LANES == 0 and LANES == 2 * HEAD_DIM
    assert batch * n_hp == B, "one sample sequence is scanned per grid step"
    ch = n_pages // (2 * (L // MOBA_BLOCK))

    def in_spec(name):
        c0 = _col_block(cols[name], LANES)
        return pl.BlockSpec((L, LANES), lambda b, hp, pt: (b, c0 + hp))

    seq4 = lambda b, hp, pt: (b * n_hp + hp, 0, 0, 0)
    spec_t = pl.BlockSpec((1, LANES, L), lambda b, hp, pt: (b, hp, 0))
    kv_t = jax.ShapeDtypeStruct((batch, W, L), F32)
    grid_spec = pltpu.PrefetchScalarGridSpec(
        num_scalar_prefetch=1,
        grid=(batch, n_hp),
        in_specs=[in_spec("q"), in_spec("k"), in_spec("v"),
                  pl.BlockSpec((1, H, d, T), seq4), pl.BlockSpec(memory_space=pl.ANY)],
        out_specs=[pl.BlockSpec((L, LANES), lambda b, hp, pt: (b, hp)), spec_t, spec_t,
                   pl.BlockSpec((1, H, n_blocks), lambda b, hp, pt: (b * n_hp + hp, 0, 0))],
        scratch_shapes=[pltpu.VMEM((MOBA_BLOCK, L), F32), pltpu.VMEM((MOBA_BLOCK, L), BF16),
                        pltpu.VMEM((MOBA_BLOCK, LANES), F32), pltpu.VMEM((L, LANES), F32),
                        pltpu.VMEM((SCAN_RING, ch, H, d, T), F32),
                        pltpu.SemaphoreType.DMA((SCAN_RING,))],
    )
    return pl.pallas_call(
        functools.partial(_moba_prompt_kernel, L=L, scale=HEAD_DIM ** -0.5, n_pages=n_pages),
        grid_spec=grid_spec,
        out_shape=[jax.ShapeDtypeStruct((M, W), BF16), kv_t, kv_t,
                   jax.ShapeDtypeStruct((B, H, n_blocks), F32)],
        compiler_params=_cparams(("arbitrary", "arbitrary")),
        name="moba_prompt",
    )(page_table, proj, proj, proj, q_sample, cache_kt)


def _ssd_prompt_kernel(*refs, d_inner, n_x, n_z):
    xbc_refs, z_refs = refs[:n_x], refs[n_x:n_x + n_z]
    (dt_ref, cw_ref, cb_ref, dtb_r_ref, dtb_c_ref, alog_r_ref, alog_c_ref, dskip_ref, norm_ref,
     u_ref, conv_ref, st_ref, xs_scr) = refs[n_x + n_z:]
    bw = xbc_refs[0].shape[1]
    Q = SSD_CHUNK
    P = SSM_HEADDIM
    N = D_STATE
    G = N_SSM_GROUPS
    R = d_inner // P // G
    assert R % 2 == 0 and 2 * P == LANES and N == LANES
    c = pl.program_id(1)
    tail = CONV_WIDTH - 1

    @pl.when(c == 0)
    def _():
        xs_scr[0:8, :] = jnp.zeros((8, xs_scr.shape[1]), F32)
        st_ref[...] = jnp.zeros(st_ref.shape, F32)

    for i, r in enumerate(xbc_refs):
        xs_scr[8:8 + Q, i * bw:(i + 1) * bw] = r[...]
    acc = cw_ref[0:1, :] * xs_scr[8 - tail:8 - tail + Q, :]
    for i in range(1, CONV_WIDTH):
        acc = acc + cw_ref[i:i + 1, :] * xs_scr[8 - tail + i:8 - tail + i + Q, :]
    xc = _silu(cb_ref[...] + acc)
    last_rows = xs_scr[8 + Q - tail:8 + Q, :]
    xs_scr[8 - tail:8, :] = last_rows
    conv_ref[0] = last_rows

    raw = dt_ref[...]
    dt = _softplus(raw + dtb_r_ref[...])
    dt_t = _softplus(raw.T + dtb_c_ref[...])
    a = dt * (-jnp.exp(alog_r_ref[...]))
    a_t = dt_t * (-jnp.exp(alog_c_ref[...]))
    ri = lax.broadcasted_iota(jnp.int32, (Q, Q), 0)
    ci = lax.broadcasted_iota(jnp.int32, (Q, Q), 1)
    causal = ci <= ri
    acum = jnp.dot(causal.astype(F32), a, precision=lax.Precision.HIGHEST,
                   preferred_element_type=F32)
    acum_t = jnp.dot(a_t, (ri <= ci).astype(F32), precision=lax.Precision.HIGHEST,
                     preferred_element_type=F32)
    e_acum = jnp.exp(acum)
    d_end = jnp.exp(acum[Q - 1:Q, :] - acum)
    lane = lax.broadcasted_iota(jnp.int32, (1, LANES), 1)
    first = lane < P

    def pair_cols(t, h0):
        return jnp.where(first, t[:, h0:h0 + 1], t[:, h0 + 1:h0 + 2])

    y_parts = []
    for g in range(G):
        b_g = xc[:, d_inner + g * N:d_inner + (g + 1) * N].astype(BF16)
        c_g = xc[:, d_inner + G * N + g * N:d_inner + G * N + (g + 1) * N].astype(BF16)
        cb = lax.dot_general(c_g, b_g, _NT, preferred_element_type=F32)
        st_g = st_ref[0, g * R:(g + 1) * R].reshape(R * P, N)
        y_off = lax.dot_general(c_g, st_g.astype(BF16), _NT, preferred_element_type=F32)
        xdtd_parts, cd_parts = [], []
        for pr in range(R // 2):
            h0 = g * R + 2 * pr
            x_p = xc[:, h0 * P:h0 * P + LANES]
            xdt = x_p * pair_cols(dt, h0)
            xdt_b = xdt.astype(BF16)
            yd = []
            for hh in (h0, h0 + 1):
                seg = acum[:, hh:hh + 1] - acum_t[hh:hh + 1, :]
                dec = jnp.exp(jnp.where(causal, seg, -jnp.inf))
                yd.append(jnp.dot((cb * dec).astype(BF16), xdt_b, preferred_element_type=F32))
                cd_parts.append(jnp.broadcast_to(jnp.exp(acum_t[hh:hh + 1, Q - 1:Q]), (P, N)))
            y_p = (jnp.where(first, yd[0], yd[1])
                   + y_off[:, 2 * pr * P:2 * pr * P + LANES] * pair_cols(e_acum, h0)
                   + dskip_ref[:, h0 * P:h0 * P + LANES] * x_p)
            y_parts.append(y_p)
            xdtd_parts.append((xdt * pair_cols(d_end, h0)).astype(BF16))
        xdtd = jnp.concatenate(xdtd_parts, axis=1)
        s_new = lax.dot_general(xdtd, b_g, _TN, preferred_element_type=F32)
        st_new = jnp.concatenate(cd_parts, axis=0) * st_g + s_new
        st_ref[0, g * R:(g + 1) * R] = st_new.reshape(R, P, N)

    y = jnp.concatenate(y_parts, axis=1)
    u = y * _silu(jnp.concatenate([r[...] for r in z_refs], axis=1))
    gw = d_inner // G
    u_parts = []
    for g in range(G):
        ug = u[:, g * gw:(g + 1) * gw]
        u_parts.append(ug * lax.rsqrt(jnp.mean(ug * ug, axis=-1, keepdims=True) + EPS))
    u_ref[...] = (jnp.concatenate(u_parts, axis=1) * norm_ref[...]).astype(u_ref.dtype)


def _ssd_prompt(proj, cols, conv_w, conv_b, dtb_r, dtb_c, alog_r, alog_c, dskip, norm, batch, L):
    M = proj.shape[0]
    conv_dim = conv_w.shape[1]
    d_inner = norm.shape[1]
    n_heads = d_inner // SSM_HEADDIM
    assert L % SSD_CHUNK == 0
    nc = L // SSD_CHUNK
    Q = SSD_CHUNK
    tail = CONV_WIDTH - 1
    row = lambda b, c: (b * nc + c, 0)
    const = lambda b, c: (0, 0)

    def in_spec(off, width):
        c0 = _col_block(off, width)
        return pl.BlockSpec((Q, width), lambda b, c: (b * nc + c, c0))

    bw = math.gcd(cols["xbc"], cols["z"], conv_dim, d_inner)
    assert bw % LANES == 0
    n_x, n_z = conv_dim // bw, d_inner // bw
    return pl.pallas_call(
        functools.partial(_ssd_prompt_kernel, d_inner=d_inner, n_x=n_x, n_z=n_z),
        grid=(batch, nc),
        in_specs=[in_spec(cols["xbc"] + i * bw, bw) for i in range(n_x)]
        + [in_spec(cols["z"] + i * bw, bw) for i in range(n_z)]
        + [
            in_spec(cols["dt"], LANES),
            pl.BlockSpec((CONV_WIDTH, conv_dim), const),
            pl.BlockSpec((1, conv_dim), const),
            pl.BlockSpec((1, LANES), const),
            pl.BlockSpec((LANES, 1), const),
            pl.BlockSpec((1, LANES), const),
            pl.BlockSpec((LANES, 1), const),
            pl.BlockSpec((1, d_inner), const),
            pl.BlockSpec((1, d_inner), const),
        ],
        out_specs=[
            pl.BlockSpec((Q, d_inner), row),
            pl.BlockSpec((1, tail, conv_dim), lambda b, c: (b, 0, 0)),
            pl.BlockSpec((1, n_heads, SSM_HEADDIM, D_STATE), lambda b, c: (b, 0, 0, 0)),
        ],
        out_shape=[
            jax.ShapeDtypeStruct((M, d_inner), BF16),
            jax.ShapeDtypeStruct((batch, tail, conv_dim), F32),
            jax.ShapeDtypeStruct((batch, n_heads, SSM_HEADDIM, D_STATE), F32),
        ],
        scratch_shapes=[pltpu.VMEM((8 + Q, conv_dim), F32)],
        compiler_params=_cparams(("arbitrary", "arbitrary")),
        name="ssd_prompt",
    )(*([proj] * (n_x + n_z + 1)), conv_w, conv_b, dtb_r, dtb_c, alog_r, alog_c, dskip, norm)


def _ssd_step_kernel(xbc_ref, z_ref, dt_ref, cprev_ref, sprev_ref, cw_ref, cb_ref, dtb_ref,
                     alog_ref, dskip_ref, norm_ref, u_ref, conv_ref, st_ref, *, d_inner):
    P = SSM_HEADDIM
    N = D_STATE
    G = N_SSM_GROUPS
    n_heads = d_inner // P
    R = n_heads // G
    assert R % 2 == 0 and 2 * P == LANES and N == LANES
    tail = CONV_WIDTH - 1
    x_new = xbc_ref[0]
    prev = cprev_ref[0]
    acc = cw_ref[0:1, :] * prev[0:1, :]
    for i in range(1, tail):
        acc = acc + cw_ref[i:i + 1, :] * prev[i:i + 1, :]
    acc = acc + cw_ref[tail:tail + 1, :] * x_new
    xc = _silu(cb_ref[...] + acc)
    conv_ref[0, 0:tail - 1, :] = prev[1:tail, :]
    conv_ref[0, tail - 1:tail, :] = x_new

    dt = _softplus(dt_ref[0] + dtb_ref[...])
    decay = jnp.exp(dt * (-jnp.exp(alog_ref[...])))
    ri = lax.broadcasted_iota(jnp.int32, (LANES, LANES), 0)
    ci = lax.broadcasted_iota(jnp.int32, (LANES, LANES), 1)
    eye = ri == ci
    first_rows = lax.broadcasted_iota(jnp.int32, (LANES, 1), 0) < P

    y_parts = []
    for hp in range(n_heads // 2):
        h0 = 2 * hp
        g = h0 // R
        b_g = xc[:, d_inner + g * N:d_inner + (g + 1) * N]
        c_g = xc[:, d_inner + G * N + g * N:d_inner + G * N + (g + 1) * N]
        x_row = xc[:, h0 * P:h0 * P + LANES]
        x_col = jnp.sum(jnp.where(eye, jnp.broadcast_to(x_row, (LANES, LANES)), 0.0),
                        axis=1, keepdims=True)
        dt_col = jnp.where(first_rows, dt[:, h0:h0 + 1], dt[:, h0 + 1:h0 + 2])
        dec_col = jnp.where(first_rows, decay[:, h0:h0 + 1], decay[:, h0 + 1:h0 + 2])
        st = sprev_ref[0, h0:h0 + 2].reshape(2 * P, N)
        st_new = dec_col * st + (dt_col * x_col) * b_g
        st_ref[0, h0:h0 + 2] = st_new.reshape(2, P, N)
        y_col = jnp.sum(st_new * c_g, axis=1, keepdims=True)
        y_row = jnp.sum(jnp.where(eye, jnp.broadcast_to(y_col, (LANES, LANES)), 0.0),
                        axis=0, keepdims=True)
        y_parts.append(y_row + dskip_ref[:, h0 * P:h0 * P + LANES] * x_row)
    y = jnp.concatenate(y_parts, axis=1)
    u = y * _silu(z_ref[0])
    gw = d_inner // G
    u_parts = []
    for g in range(G):
        ug = u[:, g * gw:(g + 1) * gw]
        u_parts.append(ug * lax.rsqrt(jnp.mean(ug * ug, axis=-1, keepdims=True) + EPS))
    u_ref[0] = (jnp.concatenate(u_parts, axis=1) * norm_ref[...]).astype(u_ref.dtype)


def _ssd_step(xbc, z, dt_raw, conv_prev, ssm_prev, conv_w, conv_b, dtb_r, alog_r, dskip, norm):
    B, conv_dim = xbc.shape
    d_inner = z.shape[1]
    n_heads = d_inner // SSM_HEADDIM
    tail = CONV_WIDTH - 1
    const = lambda b: (0, 0)
    b3 = lambda b: (b, 0, 0)
    b4 = lambda b: (b, 0, 0, 0)
    u, conv_new, ssm_new = pl.pallas_call(
        functools.partial(_ssd_step_kernel, d_inner=d_inner),
        grid=(B,),
        in_specs=[
            pl.BlockSpec((1, 1, conv_dim), b3),
            pl.BlockSpec((1, 1, d_inner), b3),
            pl.BlockSpec((1, 1, LANES), b3),
            pl.BlockSpec((1, tail, conv_dim), b3),
            pl.BlockSpec((1, n_heads, SSM_HEADDIM, D_STATE), b4),
            pl.BlockSpec((CONV_WIDTH, conv_dim), const),
            pl.BlockSpec((1, conv_dim), const),
            pl.BlockSpec((1, LANES), const),
            pl.BlockSpec((1, LANES), const),
            pl.BlockSpec((1, d_inner), const),
            pl.BlockSpec((1, d_inner), const),
        ],
        out_specs=[
            pl.BlockSpec((1, 1, d_inner), b3),
            pl.BlockSpec((1, tail, conv_dim), b3),
            pl.BlockSpec((1, n_heads, SSM_HEADDIM, D_STATE), b4),
        ],
        out_shape=[
            jax.ShapeDtypeStruct((B, 1, d_inner), BF16),
            jax.ShapeDtypeStruct((B, tail, conv_dim), F32),
            jax.ShapeDtypeStruct((B, n_heads, SSM_HEADDIM, D_STATE), F32),
        ],
        compiler_params=_cparams(("arbitrary",)),
        name="ssd_step",
    )(xbc.reshape(B, 1, conv_dim), z.reshape(B, 1, d_inner), dt_raw.reshape(B, 1, LANES),
      conv_prev, ssm_prev, conv_w, conv_b, dtb_r, alog_r, dskip, norm)
    return u.reshape(B, d_inner), conv_new, ssm_new


def _moba_topk_kernel(g_ref, idx_ref, *, n_sel):
    gate = g_ref[...]
    n_blocks = gate.shape[2]
    blk = lax.broadcasted_iota(jnp.int32, gate.shape, 2)
    rank = jnp.zeros(gate.shape, jnp.int32)
    for m in range(n_blocks):
        gm = gate[:, :, m:m + 1]
        beats = (gm > gate) | ((gm == gate) & (m < blk))
        rank = rank + beats.astype(jnp.int32)
    for r in range(n_sel):
        idx_ref[:, r] = jnp.sum(jnp.where(rank == r, blk, 0), axis=2, keepdims=True)


def _moba_topk(gates, n_sel):
    B, H, n_blocks = gates.shape
    return pl.pallas_call(
        functools.partial(_moba_topk_kernel, n_sel=n_sel),
        grid=(1,),
        in_specs=[pl.BlockSpec((B, H, n_blocks), lambda b: (0, 0, 0))],
        out_specs=pl.BlockSpec((B, n_sel, H, 1), lambda b: (0, 0, 0, 0)),
        out_shape=jax.ShapeDtypeStruct((B, n_sel, H, 1), jnp.int32),
        compiler_params=_cparams(("arbitrary",)),
        name="moba_topk",
    )(gates)


def _moba_sample_kernel(pt_ref, idx_ref, q_ref, kn_ref, vn_ref, ck_ref, cv_ref, o_ref,
                        kbuf, vbuf, sem, *, n_sel, n_heads, scale):
    b = pl.program_id(0)
    n_b = pl.num_programs(0)
    ppb = MOBA_BLOCK // PAGE_SIZE

    def copies(bb, slot):
        out = []
        for h in range(n_heads):
            for r in range(n_sel):
                blk = idx_ref[bb, r * n_heads + h]
                for t in range(ppb):
                    page = pt_ref[bb, blk * ppb + t]
                    j = r * ppb + t
                    out.append(pltpu.make_async_copy(ck_ref.at[page, h], kbuf.at[slot, h, j], sem.at[0, slot]))
                    out.append(pltpu.make_async_copy(cv_ref.at[page, h], vbuf.at[slot, h, j], sem.at[1, slot]))
        return out

    slot = b % 2

    @pl.when(b == 0)
    def _():
        for cp in copies(b, slot):
            cp.start()

    @pl.when(b + 1 < n_b)
    def _():
        for cp in copies(b + 1, 1 - slot):
            cp.start()

    for cp in copies(b, slot):
        cp.wait()

    def body(h, carry):
        q_row = q_ref[0, h]
        qc = _row_to_col(q_row)
        kh = kbuf[slot, h]
        vh = vbuf[slot, h]
        s = jnp.sum(kh * qc, axis=1, keepdims=True) * scale
        s_own = jnp.sum(kn_ref[0, h] * q_row, axis=-1, keepdims=True) * scale
        m = jnp.maximum(jnp.max(jnp.max(s, axis=0), axis=-1, keepdims=True), s_own)
        p = jnp.exp(s - m)
        p_own = jnp.exp(s_own - m)
        l = jnp.sum(jnp.sum(p, axis=0), axis=-1, keepdims=True) + p_own
        o_past = jnp.sum(jnp.sum(vh * p, axis=0), axis=-1, keepdims=True)
        o_ref[0, h] = (_col_to_row(o_past) + p_own * vn_ref[0, h]) / l
        return carry

    lax.fori_loop(0, n_heads, body, 0, unroll=True)


def _moba_sample(q4, k4, v4, cache_kt, cache_vt, page_table, idx):
    B, H, _, d = q4.shape
    n_sel = idx.shape[1] // H
    n_slabs = n_sel * (MOBA_BLOCK // PAGE_SIZE)
    new = pl.BlockSpec((1, H, 1, d), lambda b, pt, ix: (b, 0, 0, 0))
    grid_spec = pltpu.PrefetchScalarGridSpec(
        num_scalar_prefetch=2,
        grid=(B,),
        in_specs=[new, new, new, pl.BlockSpec(memory_space=pl.ANY), pl.BlockSpec(memory_space=pl.ANY)],
        out_specs=new,
        scratch_shapes=[pltpu.VMEM((2, H, n_slabs, d, PAGE_SIZE), F32),
                        pltpu.VMEM((2, H, n_slabs, d, PAGE_SIZE), F32),
                        pltpu.SemaphoreType.DMA((2, 2))],
    )
    return pl.pallas_call(
        functools.partial(_moba_sample_kernel, n_sel=n_sel, n_heads=H, scale=d ** -0.5),
        grid_spec=grid_spec,
        out_shape=jax.ShapeDtypeStruct((B, H, 1, d), F32),
        compiler_params=_cparams(("arbitrary",)),
        name="moba_sample",
    )(page_table, idx, q4, k4, v4, cache_kt, cache_vt)


def _merge_kernel(*refs, n_g):
    attn_ref, ssm_ref = refs[:2]
    ga_refs, gs_refs = refs[2:2 + n_g], refs[2 + n_g:2 + 2 * n_g]
    x_ref, wa_ref, ws_ref, wo_ref, npost_ref, npre_ref, x1_ref, h2_ref = refs[2 + 2 * n_g:]
    a = jnp.dot(attn_ref[...], wa_ref[...], preferred_element_type=F32)
    s = jnp.dot(ssm_ref[...], ws_ref[...], preferred_element_type=F32)
    ga = jnp.concatenate([r[...] for r in ga_refs], axis=1)
    gs = jnp.concatenate([r[...] for r in gs_refs], axis=1)
    merged = jax.nn.sigmoid(ga) * a + jax.nn.sigmoid(gs) * s
    o = jnp.dot(merged.astype(BF16), wo_ref[...], preferred_element_type=F32)
    x1 = x_ref[...] + _rms(o, npost_ref[...])
    x1_ref[...] = x1
    h2_ref[...] = _rms(x1, npre_ref[...]).astype(BF16)


def _merge(attn, ssm, proj, cols, x, wa, ws, wo, npost, npre, tm):
    M, D = x.shape
    rows = lambda w: pl.BlockSpec((tm, w), lambda i: (i, 0))
    whole = lambda a: pl.BlockSpec(a.shape, lambda i: (0, 0), pipeline_mode=pl.Buffered(1))

    bw = math.gcd(cols["ga"], cols["gs"], D)
    assert bw % LANES == 0
    n_g = D // bw

    def gate_specs(name):
        c0 = _col_block(cols[name], bw)
        return [pl.BlockSpec((tm, bw), lambda i, c=c0 + k: (i, c)) for k in range(n_g)]

    return pl.pallas_call(
        functools.partial(_merge_kernel, n_g=n_g),
        grid=(M // tm,),
        in_specs=[rows(attn.shape[1]), rows(ssm.shape[1])] + gate_specs("ga") + gate_specs("gs")
        + [rows(D), whole(wa), whole(ws), whole(wo), whole(npost), whole(npre)],
        out_specs=[rows(D), rows(D)],
        out_shape=[jax.ShapeDtypeStruct((M, D), F32), jax.ShapeDtypeStruct((M, D), BF16)],
        compiler_params=_cparams(("arbitrary",)),
        name="merge",
    )(attn, ssm, *([proj] * (2 * n_g)), x, wa, ws, wo, npost, npre)


def _ffn_kernel(h_ref, x_ref, wg_ref, wu_ref, wd_ref, npost_ref, y_ref):
    f = pl.program_id(1)
    h = h_ref[...]
    act = _silu(jnp.dot(h, wg_ref[...], preferred_element_type=F32)) * \
        jnp.dot(h, wu_ref[...], preferred_element_type=F32)

    @pl.when(f == 0)
    def _():
        y_ref[...] = jnp.zeros(y_ref.shape, F32)

    y_ref[...] += jnp.dot(act.astype(BF16), wd_ref[...], preferred_element_type=F32)

    @pl.when(f == pl.num_programs(1) - 1)
    def _():
        y_ref[...] = x_ref[...] + _rms(y_ref[...], npost_ref[...])


def _ffn(h2, x1, w_g, w_u, w_d, npost, tm):
    M, D = x1.shape
    d_ff = w_g.shape[1]
    tf = FFN_TF
    assert d_ff % tf == 0
    return pl.pallas_call(
        _ffn_kernel,
        grid=(M // tm, d_ff // tf),
        in_specs=[
            pl.BlockSpec((tm, D), lambda i, f: (i, 0), pipeline_mode=pl.Buffered(1)),
            pl.BlockSpec((tm, D), lambda i, f: (i, 0), pipeline_mode=pl.Buffered(1)),
            pl.BlockSpec((D, tf), lambda i, f: (0, f)),
            pl.BlockSpec((D, tf), lambda i, f: (0, f)),
            pl.BlockSpec((tf, D), lambda i, f: (f, 0)),
            pl.BlockSpec((1, D), lambda i, f: (0, 0)),
        ],
        out_specs=pl.BlockSpec((tm, D), lambda i, f: (i, 0)),
        out_shape=jax.ShapeDtypeStruct((M, D), F32),
        compiler_params=_cparams(("arbitrary", "arbitrary")),
        name="ffn",
    )(h2, x1, w_g, w_u, w_d, npost)


def _pad_lanes(v):
    row = jnp.zeros((1, LANES), F32).at[0, :v.shape[0]].set(v.astype(F32))
    return row, row.reshape(LANES, 1)


def kernel(x_prompt, x_sample, cache_k, cache_v, state_conv, state_ssm, page_table, norm_mix_pre, w_in, conv_w, conv_b, dt_bias, a_log, d_skip, ssm_norm, w_attn_out, w_ssm_out, w_out, norm_mix_post, norm_ffn_pre, w_gate, w_up, w_down, norm_ffn_post):
    depth = w_in.shape[0]
    assert depth == 1, "single trunk layer"
    b_p, seq, d_model = x_prompt.shape
    b_s, dec_seq, _ = x_sample.shape
    assert dec_seq == 1
    n_heads_attn = cache_k.shape[3]
    attn_w = n_heads_attn * cache_k.shape[4]
    assert cache_k.shape[4] == HEAD_DIM and cache_k.shape[2] == PAGE_SIZE
    conv_dim = conv_w.shape[2]
    n_ssm_heads = dt_bias.shape[1]
    d_inner = n_ssm_heads * SSM_HEADDIM
    assert conv_dim == d_inner + 2 * N_SSM_GROUPS * D_STATE and n_ssm_heads <= LANES
    n_pages = page_table.shape[1]
    assert (n_pages * PAGE_SIZE) % MOBA_BLOCK == 0
    n_sel = min(MOBA_TOPK, n_pages * PAGE_SIZE // MOBA_BLOCK)
    assert n_sel == MOBA_TOPK

    l = 0
    src, off = {}, 0
    for name, w in (("q", attn_w), ("k", attn_w), ("v", attn_w), ("z", d_inner), ("xbc", conv_dim),
                    ("dt", n_ssm_heads), ("ga", d_model), ("gs", d_model)):
        src[name] = (off, w)
        off += w
    assert off == w_in.shape[2]
    w_l = w_in[l].astype(BF16)
    n_main_cols = src["dt"][0]
    cols = {name: src[name][0] for name in ("q", "k", "v", "z", "xbc")}
    parts, off = [], n_main_cols
    for name in ("ga", "gs", "dt"):
        s0, w = src[name]
        cols[name] = off
        parts.append(w_l[:, s0:s0 + w])
        off += w
    n_proj = -(-(cols["dt"] + LANES) // PROJ_TN) * PROJ_TN
    parts.append(jnp.zeros((d_model, n_proj - off), BF16))
    w_tail = jnp.concatenate(parts, axis=1)
    wa = w_attn_out[l].astype(BF16)
    ws = w_ssm_out[l].astype(BF16)
    wo = w_out[l].astype(BF16)
    w_g = w_gate[l].astype(BF16)
    w_u = w_up[l].astype(BF16)
    w_d = w_down[l].astype(BF16)
    g_pre = norm_mix_pre[l].reshape(1, d_model)
    g_post = norm_mix_post[l].reshape(1, d_model)
    g_fpre = norm_ffn_pre[l].reshape(1, d_model)
    g_fpost = norm_ffn_post[l].reshape(1, d_model)
    cw = conv_w[l]
    cb = conv_b[l].reshape(1, conv_dim)
    dtb_r, dtb_c = _pad_lanes(dt_bias[l])
    alog_r, alog_c = _pad_lanes(a_log[l])
    dskip = jnp.repeat(d_skip[l].astype(F32), SSM_HEADDIM).reshape(1, d_inner)
    snorm = ssm_norm[l].reshape(1, d_inner)

    def dense_tail(attn, ssm, proj, x, tm_merge, tm_ffn):
        x1, h2 = _merge(attn, ssm, proj, cols, x, wa, ws, wo, g_post, g_fpre, tm_merge)
        return _ffn(h2, x1, w_g, w_u, w_d, g_fpost, tm_ffn)

    xs = x_sample.reshape(b_s, d_model)
    proj_s = _in_proj(xs, g_pre, w_l, n_main_cols, w_tail, tm=b_s)
    group = lambda name, w: proj_s[:, cols[name]:cols[name] + w]
    q_s, k_s, v_s = group("q", attn_w), group("k", attn_w), group("v", attn_w)
    z_s, xbc_s, dt_s = group("z", d_inner), group("xbc", conv_dim), group("dt", LANES)
    hd = (b_s, n_heads_attn, 1, HEAD_DIM)
    q4, k4, v4 = q_s.reshape(hd), k_s.reshape(hd), v_s.reshape(hd)
    q_bcast = jnp.broadcast_to(q_s.reshape(b_s, n_heads_attn, HEAD_DIM, 1),
                               (b_s, n_heads_attn, HEAD_DIM, PAGE_SIZE))
    ck = jnp.transpose(cache_k.reshape(cache_k.shape[1:]), (0, 2, 3, 1))
    cv = jnp.transpose(cache_v.reshape(cache_v.shape[1:]), (0, 2, 3, 1))

    m_p = b_p * seq
    xp = x_prompt.reshape(m_p, d_model)
    proj = _in_proj(xp, g_pre, w_l, n_main_cols, w_tail, tm=1024)
    attn, kt, vt, gates = _moba_prompt(proj, cols, attn_w, b_p, seq, q_bcast, ck, page_table)
    u, conv_p, ssm_p = _ssd_prompt(proj, cols, cw, cb, dtb_r, dtb_c, alog_r, alog_c, dskip, snorm, b_p, seq)
    y_p = dense_tail(attn, u, proj, xp, 256, 1024)

    idx = _moba_topk(gates, n_sel)
    attn_s = _moba_sample(q4, k4, v4, ck, cv, page_table, idx.reshape(b_s, n_sel * n_heads_attn))
    u_s, conv_s, ssm_s = _ssd_step(xbc_s, z_s, dt_s, state_conv[l], state_ssm[l], cw, cb,
                                   dtb_r, alog_r, dskip, snorm)
    y_s = dense_tail(attn_s.reshape(b_s, attn_w).astype(BF16), u_s, proj_s, xs, b_s, b_s)

    def kv_prompt(t):
        return jnp.transpose(t.reshape(b_p, n_heads_attn, HEAD_DIM, seq), (0, 3, 1, 2))[None]

    kv_s = (1, b_s, 1, n_heads_attn, HEAD_DIM)
    return (y_p.reshape(b_p, seq, d_model), y_s.reshape(b_s, 1, d_model),
            kv_prompt(kt), kv_prompt(vt), conv_p[None], ssm_p[None],
            k_s.reshape(kv_s), v_s.reshape(kv_s), conv_s[None], ssm_s[None])
```

```python
import functools
import math

import jax
import jax.numpy as jnp
from jax import lax
from jax.experimental import pallas as pl
from jax.experimental.pallas import tpu as pltpu

F32 = jnp.float32
BF16 = jnp.bfloat16

EPS = 1e-6
HEAD_DIM = 64
MOBA_BLOCK = 256
MOBA_TOPK = 3
PAGE_SIZE = 128
SSM_HEADDIM = 64
N_SSM_GROUPS = 8
D_STATE = 128
CONV_WIDTH = 4
SSD_CHUNK = 128

LANES = 128
SUBLANES = 8
VMEM_LIMIT = 56 * 1024 * 1024
PROJ_TN = 1536
FFN_TF = 512

NEG_BIG = -(2.0 ** 100)
ROW_GROUP = 16
SCAN_RING = 4

_NT = (((1,), (1,)), ((), ()))
_TN = (((0,), (0,)), ((), ()))


def _cparams(sem):
    return pltpu.CompilerParams(dimension_semantics=sem, vmem_limit_bytes=VMEM_LIMIT)


def _rms(x, g):
    return x * lax.rsqrt(jnp.mean(x * x, axis=-1, keepdims=True) + EPS) * g


def _silu(x):
    return x * jax.nn.sigmoid(x)


def _softplus(x):
    return jnp.maximum(x, 0.0) + jnp.log1p(jnp.exp(-jnp.abs(x)))


def _eye(n):
    return lax.broadcasted_iota(jnp.int32, (n, n), 0) == lax.broadcasted_iota(jnp.int32, (n, n), 1)


def _row_to_col(row):
    n = row.shape[1]
    return jnp.sum(jnp.where(_eye(n), jnp.broadcast_to(row, (n, n)), 0.0), axis=1, keepdims=True)


def _col_to_row(col):
    n = col.shape[0]
    return jnp.sum(jnp.where(_eye(n), jnp.broadcast_to(col, (n, n)), 0.0), axis=0, keepdims=True)


def _in_proj_kernel(x_ref, g_ref, wm_ref, wt_ref, o_ref, h_scr, *, n_main):
    j = pl.program_id(1)

    @pl.when(j == 0)
    def _():
        h_scr[...] = _rms(x_ref[...], g_ref[...]).astype(BF16)

    @pl.when(j < n_main)
    def _():
        o_ref[...] = jnp.dot(h_scr[...], wm_ref[...], preferred_element_type=F32)

    @pl.when(j >= n_main)
    def _():
        o_ref[...] = jnp.dot(h_scr[...], wt_ref[...], preferred_element_type=F32)


def _in_proj(x, g, w, n_main_cols, w_tail, tm):
    M, D = x.shape
    tn = PROJ_TN
    assert n_main_cols % tn == 0 and w_tail.shape[1] % tn == 0
    n_main = n_main_cols // tn
    n_tail = w_tail.shape[1] // tn
    return pl.pallas_call(
        functools.partial(_in_proj_kernel, n_main=n_main),
        grid=(M // tm, n_main + n_tail),
        in_specs=[
            pl.BlockSpec((tm, D), lambda i, j: (i, 0), pipeline_mode=pl.Buffered(1)),
            pl.BlockSpec((1, D), lambda i, j: (0, 0)),
            pl.BlockSpec((D, tn), lambda i, j: (0, jnp.minimum(j, n_main - 1))),
            pl.BlockSpec((D, tn), lambda i, j: (0, jnp.clip(j - n_main, 0, n_tail - 1))),
        ],
        out_specs=pl.BlockSpec((tm, tn), lambda i, j: (i, j)),
        out_shape=jax.ShapeDtypeStruct((M, (n_main + n_tail) * tn), F32),
        scratch_shapes=[pltpu.VMEM((tm, D), BF16)],
        compiler_params=_cparams(("arbitrary", "arbitrary")),
        name="in_proj",
    )(x, g, w, w_tail)


def _moba_prompt_kernel(pt_ref, q_ref, k_ref, v_ref, qs_ref, ck_ref, o_ref, kt_ref, vt_ref, g_ref,
                        s_scr, p_scr, l_scr, o_scr, scan_buf, scan_sem, *, L, scale, n_pages):
    BS = MOBA_BLOCK
    nb = L // BS
    assert nb <= HEAD_DIM
    n_sub = 2 * nb
    ppb = BS // PAGE_SIZE
    ch = n_pages // n_sub
    n_buf = scan_buf.shape[0]
    assert n_pages % n_sub == 0 and ch % ppb == 0 and n_sub % n_buf == 0
    step = pl.program_id(0) * pl.num_programs(1) + pl.program_id(1)
    n_steps = pl.num_programs(0) * pl.num_programs(1)
    _, n_h, d_h, T = qs_ref.shape

    def scan_start(j):
        seq, jj = (step, j) if j < n_sub else (step + 1, j - n_sub)

        def go():
            for t in range(ch):
                pltpu.make_async_copy(ck_ref.at[pt_ref[seq, jj * ch + t]], scan_buf.at[j % n_buf, t],
                                      scan_sem.at[j % n_buf]).start()

        if j < n_sub:
            go()
        else:
            pl.when(step + 1 < n_steps)(go)

    def scan_chunk(j):
        slot = j % n_buf
        scan_start(j + n_buf - 1)
        for t in range(ch):
            pltpu.make_async_copy(ck_ref.at[pt_ref[step, j * ch + t]], scan_buf.at[slot, t],
                                  scan_sem.at[slot]).wait()
        for h in range(n_h):
            qh = qs_ref[0, h]
            for u in range(ch // ppb):
                part = None
                for t in range(ppb):
                    prod = scan_buf[slot, u * ppb + t, h] * qh
                    pp = jnp.sum(prod.reshape(d_h // SUBLANES, SUBLANES, T), axis=0)
                    part = pp if part is None else part + pp
                n = j * (ch // ppb) + u
                g_ref[0, h:h + 1, n:n + 1] = jnp.sum(part, keepdims=True) * (1.0 / BS)

    @pl.when(step == 0)
    def _():
        for j in range(n_buf - 1):
            scan_start(j)

    q = q_ref[...]
    k = k_ref[...]
    v = v_ref[...]
    vb = v.astype(BF16)
    kt_ref[0] = k.T
    vt_ref[0] = v.T
    lane = lax.broadcasted_iota(jnp.int32, (1, LANES), 1)
    first = lane < HEAD_DIM

    kmean = jnp.concatenate(
        [jnp.mean(k[n * BS:(n + 1) * BS], axis=0, keepdims=True) for n in range(nb)], axis=0)

    blk = lax.broadcasted_iota(jnp.int32, (nb, L), 0)
    qblk = lax.broadcasted_iota(jnp.int32, (nb, L), 1) // BS
    past = blk < qblk
    row = lax.broadcasted_iota(jnp.int32, (BS, BS), 0)
    col = lax.broadcasted_iota(jnp.int32, (BS, BS), 1)
    causal_bias = jnp.where(col <= row, 0.0, -jnp.inf)
    key_blk = lax.broadcasted_iota(jnp.int32, (L, LANES), 0) // BS
    key_lane = lax.broadcasted_iota(jnp.int32, (L, LANES), 1)
    c = scale * math.log2(math.e)

    def scores(q_aug, k_aug, qb):
        qs = slice(qb * BS, (qb + 1) * BS)
        n_past = qb * BS
        if qb > 0:
            s_scr[qb % 2, :, :n_past] = lax.dot_general(q_aug[qs], k_aug[:n_past], _NT,
                                                        preferred_element_type=F32)
        s_scr[qb % 2, :, n_past:n_past + BS] = lax.dot_general(
            q_aug[qs], k_aug[n_past:n_past + BS], _NT, preferred_element_type=F32) + causal_bias

    def softmax(qb):
        nk = (qb + 1) * BS
        sb = qb % 2

        def rows(r, carry):
            rs = pl.ds(pl.multiple_of(r * ROW_GROUP, ROW_GROUP), ROW_GROUP)
            s = s_scr[sb, rs, :nk]
            p = jnp.exp2(s - jnp.max(s, axis=1, keepdims=True))
            l_scr[sb, rs, :] = jnp.broadcast_to(jnp.sum(p, axis=1, keepdims=True), (ROW_GROUP, LANES))
            p_scr[sb, rs, :nk] = p.astype(BF16)
            return carry

        lax.fori_loop(0, BS // ROW_GROUP, rows, 0, unroll=True)

    def weighted_values(head, qb):
        qs = slice(qb * BS, (qb + 1) * BS)
        nk = (qb + 1) * BS
        sb = qb % 2
        o = jnp.dot(p_scr[sb, :, :nk], vb[:nk], preferred_element_type=F32) / l_scr[sb]
        if head == 0:
            o_scr[qs, :] = o
        else:
            o_ref[qs, :] = jnp.where(first, o_scr[qs, :], o).astype(o_ref.dtype)

    for head, (own, off) in enumerate(((first, HEAD_DIM), (~first, 0))):
        gate = lax.dot_general(jnp.where(own, kmean, 0.0), q, _NT,
                               precision=lax.Precision.HIGHEST, preferred_element_type=F32)
        gate = jnp.where(past, gate, -jnp.inf)
        rank = jnp.zeros((nb, L), jnp.int32)
        for m in range(nb):
            gm = gate[m:m + 1, :]
            beats = (gm > gate) | ((gm == gate) & (m < blk))
            rank = rank + beats.astype(jnp.int32)
        allowed = (past & (rank < MOBA_TOPK)) | (blk == qblk)
        bias_t = jnp.where(allowed, 0.0, NEG_BIG)
        pieces = [jnp.zeros((off, L), F32), bias_t, jnp.zeros((LANES - off - nb, L), F32)]
        pad_t = jnp.concatenate([t for t in pieces if t.shape[0]], axis=0)
        q_aug = jnp.where(own, q * c, pad_t.T).astype(BF16)
        k_aug = jnp.where(own, k, (key_lane - off == key_blk).astype(F32)).astype(BF16)

        for qb in range(nb + 1):
            if qb < nb:
                scan_chunk(head * nb + qb)
            if qb == 0:
                scores(q_aug, k_aug, 0)
            if qb + 1 < nb:
                scores(q_aug, k_aug, qb + 1)
            if qb < nb:
                softmax(qb)
            if qb > 0:
                weighted_values(head, qb - 1)


def _col_block(off, width):
    assert off % width == 0
    return off // width


def _moba_prompt(proj, cols, W, batch, L, q_sample, cache_kt, page_table):
    M = proj.shape[0]
    B, H, d, T = q_sample.shape
    n_pages = page_table.shape[1]
    n_hp = W // LANES
    n_blocks = n_pages * PAGE_SIZE // MOBA_BLOCK
    assert L % MOBA_BLOCK == 0 and W % LANES == 0 and LANES == 2 * HEAD_DIM
    assert batch * n_hp == B, "one sample sequence is scanned per grid step"
    ch = n_pages // (2 * (L // MOBA_BLOCK))

    def in_spec(name):
        c0 = _col_block(cols[name], LANES)
        return pl.BlockSpec((L, LANES), lambda b, hp, pt: (b, c0 + hp))

    seq4 = lambda b, hp, pt: (b * n_hp + hp, 0, 0, 0)
    spec_t = pl.BlockSpec((1, LANES, L), lambda b, hp, pt: (b, hp, 0))
    kv_t = jax.ShapeDtypeStruct((batch, W, L), F32)
    grid_spec = pltpu.PrefetchScalarGridSpec(
        num_scalar_prefetch=1,
        grid=(batch, n_hp),
        in_specs=[in_spec("q"), in_spec("k"), in_spec("v"),
                  pl.BlockSpec((1, H, d, T), seq4), pl.BlockSpec(memory_space=pl.ANY)],
        out_specs=[pl.BlockSpec((L, LANES), lambda b, hp, pt: (b, hp)), spec_t, spec_t,
                   pl.BlockSpec((1, H, n_blocks), lambda b, hp, pt: (b * n_hp + hp, 0, 0))],
        scratch_shapes=[pltpu.VMEM((2, MOBA_BLOCK, L), F32), pltpu.VMEM((2, MOBA_BLOCK, L), BF16),
                        pltpu.VMEM((2, MOBA_BLOCK, LANES), F32), pltpu.VMEM((L, LANES), F32),
                        pltpu.VMEM((SCAN_RING, ch, H, d, T), F32),
                        pltpu.SemaphoreType.DMA((SCAN_RING,))],
    )
    return pl.pallas_call(
        functools.partial(_moba_prompt_kernel, L=L, scale=HEAD_DIM ** -0.5, n_pages=n_pages),
        grid_spec=grid_spec,
        out_shape=[jax.ShapeDtypeStruct((M, W), BF16), kv_t, kv_t,
                   jax.ShapeDtypeStruct((B, H, n_blocks), F32)],
        compiler_params=_cparams(("arbitrary", "arbitrary")),
        name="moba_prompt",
    )(page_table, proj, proj, proj, q_sample, cache_kt)


def _ssd_prompt_kernel(*refs, d_inner, n_x, n_z):
    xbc_refs, z_refs = refs[:n_x], refs[n_x:n_x + n_z]
    (dt_ref, cw_ref, cb_ref, dtb_r_ref, dtb_c_ref, alog_r_ref, alog_c_ref, dskip_ref, norm_ref,
     u_ref, conv_ref, st_ref, xs_scr) = refs[n_x + n_z:]
    bw = xbc_refs[0].shape[1]
    Q = SSD_CHUNK
    P = SSM_HEADDIM
    N = D_STATE
    G = N_SSM_GROUPS
    R = d_inner // P // G
    assert R % 2 == 0 and 2 * P == LANES and N == LANES
    c = pl.program_id(1)
    tail = CONV_WIDTH - 1

    @pl.when(c == 0)
    def _():
        xs_scr[...] = jnp.zeros(xs_scr.shape, F32)
        st_ref[...] = jnp.zeros(st_ref.shape, F32)

    x = jnp.concatenate([r[...] for r in xbc_refs], axis=1)
    prev = xs_scr[...]
    row8 = lax.broadcasted_iota(jnp.int32, (SUBLANES, 1), 0)

    def shifted(k):
        xk = pltpu.roll(x, k, axis=0)
        head = jnp.where(row8 < k, pltpu.roll(prev, k, axis=0), xk[0:SUBLANES])
        return jnp.concatenate([head, xk[SUBLANES:]], axis=0)

    acc = cw_ref[0:1, :] * shifted(tail)
    for i in range(1, tail):
        acc = acc + cw_ref[i:i + 1, :] * shifted(tail - i)
    acc = acc + cw_ref[tail:tail + 1, :] * x
    xc = _silu(cb_ref[...] + acc)
    xs_scr[...] = x[Q - SUBLANES:Q, :]
    conv_ref[0] = x[Q - tail:Q, :]

    raw = dt_ref[...]
    dt = _softplus(raw + dtb_r_ref[...])
    dt_t = _softplus(raw.T + dtb_c_ref[...])
    a = dt * (-jnp.exp(alog_r_ref[...]))
    a_t = dt_t * (-jnp.exp(alog_c_ref[...]))
    ri = lax.broadcasted_iota(jnp.int32, (Q, Q), 0)
    ci = lax.broadcasted_iota(jnp.int32, (Q, Q), 1)
    causal = ci <= ri
    acum = jnp.dot(causal.astype(F32), a, precision=lax.Precision.HIGHEST,
                   preferred_element_type=F32)
    acum_t = jnp.dot(a_t, (ri <= ci).astype(F32), precision=lax.Precision.HIGHEST,
                     preferred_element_type=F32)
    e_acum = jnp.exp(acum)
    d_end = jnp.exp(acum[Q - 1:Q, :] - acum)
    lane = lax.broadcasted_iota(jnp.int32, (1, LANES), 1)
    first = lane < P

    def pair_cols(t, h0):
        return jnp.where(first, t[:, h0:h0 + 1], t[:, h0 + 1:h0 + 2])

    y_parts = []
    for g in range(G):
        b_g = xc[:, d_inner + g * N:d_inner + (g + 1) * N].astype(BF16)
        c_g = xc[:, d_inner + G * N + g * N:d_inner + G * N + (g + 1) * N].astype(BF16)
        cb = lax.dot_general(c_g, b_g, _NT, preferred_element_type=F32)
        st_g = st_ref[0, g * R:(g + 1) * R].reshape(R * P, N)
        y_off = lax.dot_general(c_g, st_g.astype(BF16), _NT, preferred_element_type=F32)
        xdtd_parts, cd_parts = [], []
        for pr in range(R // 2):
            h0 = g * R + 2 * pr
            x_p = xc[:, h0 * P:h0 * P + LANES]
            xdt = x_p * pair_cols(dt, h0)
            xdt_b = xdt.astype(BF16)
            yd = []
            for hh in (h0, h0 + 1):
                seg = acum[:, hh:hh + 1] - acum_t[hh:hh + 1, :]
                dec = jnp.exp(jnp.where(causal, seg, -jnp.inf))
                yd.append(jnp.dot((cb * dec).astype(BF16), xdt_b, preferred_element_type=F32))
                cd_parts.append(jnp.broadcast_to(jnp.exp(acum_t[hh:hh + 1, Q - 1:Q]), (P, N)))
            y_p = (jnp.where(first, yd[0], yd[1])
                   + y_off[:, 2 * pr * P:2 * pr * P + LANES] * pair_cols(e_acum, h0)
                   + dskip_ref[:, h0 * P:h0 * P + LANES] * x_p)
            y_parts.append(y_p)
            xdtd_parts.append((xdt * pair_cols(d_end, h0)).astype(BF16))
        xdtd = jnp.concatenate(xdtd_parts, axis=1)
        s_new = lax.dot_general(xdtd, b_g, _TN, preferred_element_type=F32)
        st_new = jnp.concatenate(cd_parts, axis=0) * st_g + s_new
        st_ref[0, g * R:(g + 1) * R] = st_new.reshape(R, P, N)

    y = jnp.concatenate(y_parts, axis=1)
    u = y * _silu(jnp.concatenate([r[...] for r in z_refs], axis=1))
    gw = d_inner // G
    u_parts = []
    for g in range(G):
        ug = u[:, g * gw:(g + 1) * gw]
        u_parts.append(ug * lax.rsqrt(jnp.mean(ug * ug, axis=-1, keepdims=True) + EPS))
    u_ref[...] = (jnp.concatenate(u_parts, axis=1) * norm_ref[...]).astype(u_ref.dtype)


def _ssd_prompt(proj, cols, conv_w, conv_b, dtb_r, dtb_c, alog_r, alog_c, dskip, norm, batch, L):
    M = proj.shape[0]
    conv_dim = conv_w.shape[1]
    d_inner = norm.shape[1]
    n_heads = d_inner // SSM_HEADDIM
    assert L % SSD_CHUNK == 0
    nc = L // SSD_CHUNK
    Q = SSD_CHUNK
    tail = CONV_WIDTH - 1
    row = lambda b, c: (b * nc + c, 0)
    const = lambda b, c: (0, 0)

    def in_spec(off, width):
        c0 = _col_block(off, width)
        return pl.BlockSpec((Q, width), lambda b, c: (b * nc + c, c0))

    bw = math.gcd(cols["xbc"], cols["z"], conv_dim, d_inner)
    assert bw % LANES == 0
    n_x, n_z = conv_dim // bw, d_inner // bw
    return pl.pallas_call(
        functools.partial(_ssd_prompt_kernel, d_inner=d_inner, n_x=n_x, n_z=n_z),
        grid=(batch, nc),
        in_specs=[in_spec(cols["xbc"] + i * bw, bw) for i in range(n_x)]
        + [in_spec(cols["z"] + i * bw, bw) for i in range(n_z)]
        + [
            in_spec(cols["dt"], LANES),
            pl.BlockSpec((CONV_WIDTH, conv_dim), const),
            pl.BlockSpec((1, conv_dim), const),
            pl.BlockSpec((1, LANES), const),
            pl.BlockSpec((LANES, 1), const),
            pl.BlockSpec((1, LANES), const),
            pl.BlockSpec((LANES, 1), const),
            pl.BlockSpec((1, d_inner), const),
            pl.BlockSpec((1, d_inner), const),
        ],
        out_specs=[
            pl.BlockSpec((Q, d_inner), row),
            pl.BlockSpec((1, tail, conv_dim), lambda b, c: (b, 0, 0)),
            pl.BlockSpec((1, n_heads, SSM_HEADDIM, D_STATE), lambda b, c: (b, 0, 0, 0)),
        ],
        out_shape=[
            jax.ShapeDtypeStruct((M, d_inner), BF16),
            jax.ShapeDtypeStruct((batch, tail, conv_dim), F32),
            jax.ShapeDtypeStruct((batch, n_heads, SSM_HEADDIM, D_STATE), F32),
        ],
        scratch_shapes=[pltpu.VMEM((SUBLANES, conv_dim), F32)],
        compiler_params=_cparams(("arbitrary", "arbitrary")),
        name="ssd_prompt",
    )(*([proj] * (n_x + n_z + 1)), conv_w, conv_b, dtb_r, dtb_c, alog_r, alog_c, dskip, norm)


def _ssd_step_kernel(xbc_ref, z_ref, dt_ref, cprev_ref, sprev_ref, cw_ref, cb_ref, dtb_ref,
                     alog_ref, dskip_ref, norm_ref, u_ref, conv_ref, st_ref, *, d_inner):
    P = SSM_HEADDIM
    N = D_STATE
    G = N_SSM_GROUPS
    n_heads = d_inner // P
    R = n_heads // G
    assert R % 2 == 0 and 2 * P == LANES and N == LANES
    tail = CONV_WIDTH - 1
    x_new = xbc_ref[0]
    prev = cprev_ref[0]
    acc = cw_ref[0:1, :] * prev[0:1, :]
    for i in range(1, tail):
        acc = acc + cw_ref[i:i + 1, :] * prev[i:i + 1, :]
    acc = acc + cw_ref[tail:tail + 1, :] * x_new
    xc = _silu(cb_ref[...] + acc)
    conv_ref[0, 0:tail - 1, :] = prev[1:tail, :]
    conv_ref[0, tail - 1:tail, :] = x_new

    dt = _softplus(dt_ref[0] + dtb_ref[...])
    decay = jnp.exp(dt * (-jnp.exp(alog_ref[...])))
    ri = lax.broadcasted_iota(jnp.int32, (LANES, LANES), 0)
    ci = lax.broadcasted_iota(jnp.int32, (LANES, LANES), 1)
    eye = ri == ci
    first_rows = lax.broadcasted_iota(jnp.int32, (LANES, 1), 0) < P

    y_parts = []
    for hp in range(n_heads // 2):
        h0 = 2 * hp
        g = h0 // R
        b_g = xc[:, d_inner + g * N:d_inner + (g + 1) * N]
        c_g = xc[:, d_inner + G * N + g * N:d_inner + G * N + (g + 1) * N]
        x_row = xc[:, h0 * P:h0 * P + LANES]
        x_col = jnp.sum(jnp.where(eye, jnp.broadcast_to(x_row, (LANES, LANES)), 0.0),
                        axis=1, keepdims=True)
        dt_col = jnp.where(first_rows, dt[:, h0:h0 + 1], dt[:, h0 + 1:h0 + 2])
        dec_col = jnp.where(first_rows, decay[:, h0:h0 + 1], decay[:, h0 + 1:h0 + 2])
        st = sprev_ref[0, h0:h0 + 2].reshape(2 * P, N)
        st_new = dec_col * st + (dt_col * x_col) * b_g
        st_ref[0, h0:h0 + 2] = st_new.reshape(2, P, N)
        y_col = jnp.sum(st_new * c_g, axis=1, keepdims=True)
        y_row = jnp.sum(jnp.where(eye, jnp.broadcast_to(y_col, (LANES, LANES)), 0.0),
                        axis=0, keepdims=True)
        y_parts.append(y_row + dskip_ref[:, h0 * P:h0 * P + LANES] * x_row)
    y = jnp.concatenate(y_parts, axis=1)
    u = y * _silu(z_ref[0])
    gw = d_inner // G
    u_parts = []
    for g in range(G):
        ug = u[:, g * gw:(g + 1) * gw]
        u_parts.append(ug * lax.rsqrt(jnp.mean(ug * ug, axis=-1, keepdims=True) + EPS))
    u_ref[0] = (jnp.concatenate(u_parts, axis=1) * norm_ref[...]).astype(u_ref.dtype)


def _ssd_step(xbc, z, dt_raw, conv_prev, ssm_prev, conv_w, conv_b, dtb_r, alog_r, dskip, norm):
    B, conv_dim = xbc.shape
    d_inner = z.shape[1]
    n_heads = d_inner // SSM_HEADDIM
    tail = CONV_WIDTH - 1
    const = lambda b: (0, 0)
    b3 = lambda b: (b, 0, 0)
    b4 = lambda b: (b, 0, 0, 0)
    u, conv_new, ssm_new = pl.pallas_call(
        functools.partial(_ssd_step_kernel, d_inner=d_inner),
        grid=(B,),
        in_specs=[
            pl.BlockSpec((1, 1, conv_dim), b3),
            pl.BlockSpec((1, 1, d_inner), b3),
            pl.BlockSpec((1, 1, LANES), b3),
            pl.BlockSpec((1, tail, conv_dim), b3),
            pl.BlockSpec((1, n_heads, SSM_HEADDIM, D_STATE), b4),
            pl.BlockSpec((CONV_WIDTH, conv_dim), const),
            pl.BlockSpec((1, conv_dim), const),
            pl.BlockSpec((1, LANES), const),
            pl.BlockSpec((1, LANES), const),
            pl.BlockSpec((1, d_inner), const),
            pl.BlockSpec((1, d_inner), const),
        ],
        out_specs=[
            pl.BlockSpec((1, 1, d_inner), b3),
            pl.BlockSpec((1, tail, conv_dim), b3),
            pl.BlockSpec((1, n_heads, SSM_HEADDIM, D_STATE), b4),
        ],
        out_shape=[
            jax.ShapeDtypeStruct((B, 1, d_inner), BF16),
            jax.ShapeDtypeStruct((B, tail, conv_dim), F32),
            jax.ShapeDtypeStruct((B, n_heads, SSM_HEADDIM, D_STATE), F32),
        ],
        compiler_params=_cparams(("arbitrary",)),
        name="ssd_step",
    )(xbc.reshape(B, 1, conv_dim), z.reshape(B, 1, d_inner), dt_raw.reshape(B, 1, LANES),
      conv_prev, ssm_prev, conv_w, conv_b, dtb_r, alog_r, dskip, norm)
    return u.reshape(B, d_inner), conv_new, ssm_new


def _moba_topk_kernel(g_ref, idx_ref, *, n_sel):
    gate = g_ref[...]
    n_blocks = gate.shape[2]
    blk = lax.broadcasted_iota(jnp.int32, gate.shape, 2)
    rank = jnp.zeros(gate.shape, jnp.int32)
    for m in range(n_blocks):
        gm = gate[:, :, m:m + 1]
        beats = (gm > gate) | ((gm == gate) & (m < blk))
        rank = rank + beats.astype(jnp.int32)
    for r in range(n_sel):
        idx_ref[:, r] = jnp.sum(jnp.where(rank == r, blk, 0), axis=2, keepdims=True)


def _moba_topk(gates, n_sel):
    B, H, n_blocks = gates.shape
    return pl.pallas_call(
        functools.partial(_moba_topk_kernel, n_sel=n_sel),
        grid=(1,),
        in_specs=[pl.BlockSpec((B, H, n_blocks), lambda b: (0, 0, 0))],
        out_specs=pl.BlockSpec((B, n_sel, H, 1), lambda b: (0, 0, 0, 0)),
        out_shape=jax.ShapeDtypeStruct((B, n_sel, H, 1), jnp.int32),
        compiler_params=_cparams(("arbitrary",)),
        name="moba_topk",
    )(gates)


def _moba_sample_kernel(pt_ref, idx_ref, q_ref, kn_ref, vn_ref, ck_ref, cv_ref, o_ref,
                        kbuf, vbuf, sem, *, n_sel, n_heads, scale):
    b = pl.program_id(0)
    n_b = pl.num_programs(0)
    ppb = MOBA_BLOCK // PAGE_SIZE

    def copies(bb, slot):
        out = []
        for h in range(n_heads):
            for r in range(n_sel):
                blk = idx_ref[bb, r * n_heads + h]
                for t in range(ppb):
                    page = pt_ref[bb, blk * ppb + t]
                    j = r * ppb + t
                    out.append(pltpu.make_async_copy(ck_ref.at[page, h], kbuf.at[slot, h, j], sem.at[0, slot]))
                    out.append(pltpu.make_async_copy(cv_ref.at[page, h], vbuf.at[slot, h, j], sem.at[1, slot]))
        return out

    slot = b % 2

    @pl.when(b == 0)
    def _():
        for cp in copies(b, slot):
            cp.start()

    @pl.when(b + 1 < n_b)
    def _():
        for cp in copies(b + 1, 1 - slot):
            cp.start()

    for cp in copies(b, slot):
        cp.wait()

    def body(h, carry):
        q_row = q_ref[0, h]
        qc = _row_to_col(q_row)
        kh = kbuf[slot, h]
        vh = vbuf[slot, h]
        s = jnp.sum(kh * qc, axis=1, keepdims=True) * scale
        s_own = jnp.sum(kn_ref[0, h] * q_row, axis=-1, keepdims=True) * scale
        m = jnp.maximum(jnp.max(jnp.max(s, axis=0), axis=-1, keepdims=True), s_own)
        p = jnp.exp(s - m)
        p_own = jnp.exp(s_own - m)
        l = jnp.sum(jnp.sum(p, axis=0), axis=-1, keepdims=True) + p_own
        o_past = jnp.sum(jnp.sum(vh * p, axis=0), axis=-1, keepdims=True)
        o_ref[0, h] = (_col_to_row(o_past) + p_own * vn_ref[0, h]) / l
        return carry

    lax.fori_loop(0, n_heads, body, 0, unroll=True)


def _moba_sample(q4, k4, v4, cache_kt, cache_vt, page_table, idx):
    B, H, _, d = q4.shape
    n_sel = idx.shape[1] // H
    n_slabs = n_sel * (MOBA_BLOCK // PAGE_SIZE)
    new = pl.BlockSpec((1, H, 1, d), lambda b, pt, ix: (b, 0, 0, 0))
    grid_spec = pltpu.PrefetchScalarGridSpec(
        num_scalar_prefetch=2,
        grid=(B,),
        in_specs=[new, new, new, pl.BlockSpec(memory_space=pl.ANY), pl.BlockSpec(memory_space=pl.ANY)],
        out_specs=new,
        scratch_shapes=[pltpu.VMEM((2, H, n_slabs, d, PAGE_SIZE), F32),
                        pltpu.VMEM((2, H, n_slabs, d, PAGE_SIZE), F32),
                        pltpu.SemaphoreType.DMA((2, 2))],
    )
    return pl.pallas_call(
        functools.partial(_moba_sample_kernel, n_sel=n_sel, n_heads=H, scale=d ** -0.5),
        grid_spec=grid_spec,
        out_shape=jax.ShapeDtypeStruct((B, H, 1, d), F32),
        compiler_params=_cparams(("arbitrary",)),
        name="moba_sample",
    )(page_table, idx, q4, k4, v4, cache_kt, cache_vt)


def _merge_kernel(*refs, n_g):
    attn_ref, ssm_ref = refs[:2]
    ga_refs, gs_refs = refs[2:2 + n_g], refs[2 + n_g:2 + 2 * n_g]
    x_ref, wa_ref, ws_ref, wo_ref, npost_ref, npre_ref, x1_ref, h2_ref = refs[2 + 2 * n_g:]
    a = jnp.dot(attn_ref[...], wa_ref[...], preferred_element_type=F32)
    s = jnp.dot(ssm_ref[...], ws_ref[...], preferred_element_type=F32)
    ga = jnp.concatenate([r[...] for r in ga_refs], axis=1)
    gs = jnp.concatenate([r[...] for r in gs_refs], axis=1)
    merged = jax.nn.sigmoid(ga) * a + jax.nn.sigmoid(gs) * s
    o = jnp.dot(merged.astype(BF16), wo_ref[...], preferred_element_type=F32)
    x1 = x_ref[...] + _rms(o, npost_ref[...])
    x1_ref[...] = x1
    h2_ref[...] = _rms(x1, npre_ref[...]).astype(BF16)


def _merge(attn, ssm, proj, cols, x, wa, ws, wo, npost, npre, tm):
    M, D = x.shape
    rows = lambda w: pl.BlockSpec((tm, w), lambda i: (i, 0))
    whole = lambda a: pl.BlockSpec(a.shape, lambda i: (0, 0), pipeline_mode=pl.Buffered(1))

    bw = math.gcd(cols["ga"], cols["gs"], D)
    assert bw % LANES == 0
    n_g = D // bw

    def gate_specs(name):
        c0 = _col_block(cols[name], bw)
        return [pl.BlockSpec((tm, bw), lambda i, c=c0 + k: (i, c)) for k in range(n_g)]

    return pl.pallas_call(
        functools.partial(_merge_kernel, n_g=n_g),
        grid=(M // tm,),
        in_specs=[rows(attn.shape[1]), rows(ssm.shape[1])] + gate_specs("ga") + gate_specs("gs")
        + [rows(D), whole(wa), whole(ws), whole(wo), whole(npost), whole(npre)],
        out_specs=[rows(D), rows(D)],
        out_shape=[jax.ShapeDtypeStruct((M, D), F32), jax.ShapeDtypeStruct((M, D), BF16)],
        compiler_params=_cparams(("arbitrary",)),
        name="merge",
    )(attn, ssm, *([proj] * (2 * n_g)), x, wa, ws, wo, npost, npre)


def _ffn_kernel(h_ref, x_ref, wg_ref, wu_ref, wd_ref, npost_ref, y_ref):
    f = pl.program_id(1)
    h = h_ref[...]
    act = _silu(jnp.dot(h, wg_ref[...], preferred_element_type=F32)) * \
        jnp.dot(h, wu_ref[...], preferred_element_type=F32)

    @pl.when(f == 0)
    def _():
        y_ref[...] = jnp.zeros(y_ref.shape, F32)

    y_ref[...] += jnp.dot(act.astype(BF16), wd_ref[...], preferred_element_type=F32)

    @pl.when(f == pl.num_programs(1) - 1)
    def _():
        y_ref[...] = x_ref[...] + _rms(y_ref[...], npost_ref[...])


def _ffn(h2, x1, w_g, w_u, w_d, npost, tm):
    M, D = x1.shape
    d_ff = w_g.shape[1]
    tf = FFN_TF
    assert d_ff % tf == 0
    return pl.pallas_call(
        _ffn_kernel,
        grid=(M // tm, d_ff // tf),
        in_specs=[
            pl.BlockSpec((tm, D), lambda i, f: (i, 0), pipeline_mode=pl.Buffered(1)),
            pl.BlockSpec((tm, D), lambda i, f: (i, 0), pipeline_mode=pl.Buffered(1)),
            pl.BlockSpec((D, tf), lambda i, f: (0, f)),
            pl.BlockSpec((D, tf), lambda i, f: (0, f)),
            pl.BlockSpec((tf, D), lambda i, f: (f, 0)),
            pl.BlockSpec((1, D), lambda i, f: (0, 0)),
        ],
        out_specs=pl.BlockSpec((tm, D), lambda i, f: (i, 0)),
        out_shape=jax.ShapeDtypeStruct((M, D), F32),
        compiler_params=_cparams(("arbitrary", "arbitrary")),
        name="ffn",
    )(h2, x1, w_g, w_u, w_d, npost)


def _pad_lanes(v):
    row = jnp.zeros((1, LANES), F32).at[0, :v.shape[0]].set(v.astype(F32))
    return row, row.reshape(LANES, 1)


def kernel(x_prompt, x_sample, cache_k, cache_v, state_conv, state_ssm, page_table, norm_mix_pre, w_in, conv_w, conv_b, dt_bias, a_log, d_skip, ssm_norm, w_attn_out, w_ssm_out, w_out, norm_mix_post, norm_ffn_pre, w_gate, w_up, w_down, norm_ffn_post):
    depth = w_in.shape[0]
    assert depth == 1, "single trunk layer"
    b_p, seq, d_model = x_prompt.shape
    b_s, dec_seq, _ = x_sample.shape
    assert dec_seq == 1
    n_heads_attn = cache_k.shape[3]
    attn_w = n_heads_attn * cache_k.shape[4]
    assert cache_k.shape[4] == HEAD_DIM and cache_k.shape[2] == PAGE_SIZE
    conv_dim = conv_w.shape[2]
    n_ssm_heads = dt_bias.shape[1]
    d_inner = n_ssm_heads * SSM_HEADDIM
    assert conv_dim == d_inner + 2 * N_SSM_GROUPS * D_STATE and n_ssm_heads <= LANES
    n_pages = page_table.shape[1]
    assert (n_pages * PAGE_SIZE) % MOBA_BLOCK == 0
    n_sel = min(MOBA_TOPK, n_pages * PAGE_SIZE // MOBA_BLOCK)
    assert n_sel == MOBA_TOPK

    l = 0
    src, off = {}, 0
    for name, w in (("q", attn_w), ("k", attn_w), ("v", attn_w), ("z", d_inner), ("xbc", conv_dim),
                    ("dt", n_ssm_heads), ("ga", d_model), ("gs", d_model)):
        src[name] = (off, w)
        off += w
    assert off == w_in.shape[2]
    w_l = w_in[l].astype(BF16)
    n_main_cols = src["dt"][0]
    cols = {name: src[name][0] for name in ("q", "k", "v", "z", "xbc")}
    parts, off = [], n_main_cols
    for name in ("ga", "gs", "dt"):
        s0, w = src[name]
        cols[name] = off
        parts.append(w_l[:, s0:s0 + w])
        off += w
    n_proj = -(-(cols["dt"] + LANES) // PROJ_TN) * PROJ_TN
    parts.append(jnp.zeros((d_model, n_proj - off), BF16))
    w_tail = jnp.concatenate(parts, axis=1)
    wa = w_attn_out[l].astype(BF16)
    ws = w_ssm_out[l].astype(BF16)
    wo = w_out[l].astype(BF16)
    w_g = w_gate[l].astype(BF16)
    w_u = w_up[l].astype(BF16)
    w_d = w_down[l].astype(BF16)
    g_pre = norm_mix_pre[l].reshape(1, d_model)
    g_post = norm_mix_post[l].reshape(1, d_model)
    g_fpre = norm_ffn_pre[l].reshape(1, d_model)
    g_fpost = norm_ffn_post[l].reshape(1, d_model)
    cw = conv_w[l]
    cb = conv_b[l].reshape(1, conv_dim)
    dtb_r, dtb_c = _pad_lanes(dt_bias[l])
    alog_r, alog_c = _pad_lanes(a_log[l])
    dskip = jnp.repeat(d_skip[l].astype(F32), SSM_HEADDIM).reshape(1, d_inner)
    snorm = ssm_norm[l].reshape(1, d_inner)

    def dense_tail(attn, ssm, proj, x, tm_merge, tm_ffn):
        x1, h2 = _merge(attn, ssm, proj, cols, x, wa, ws, wo, g_post, g_fpre, tm_merge)
        return _ffn(h2, x1, w_g, w_u, w_d, g_fpost, tm_ffn)

    xs = x_sample.reshape(b_s, d_model)
    proj_s = _in_proj(xs, g_pre, w_l, n_main_cols, w_tail, tm=b_s)
    group = lambda name, w: proj_s[:, cols[name]:cols[name] + w]
    q_s, k_s, v_s = group("q", attn_w), group("k", attn_w), group("v", attn_w)
    z_s, xbc_s, dt_s = group("z", d_inner), group("xbc", conv_dim), group("dt", LANES)
    hd = (b_s, n_heads_attn, 1, HEAD_DIM)
    q4, k4, v4 = q_s.reshape(hd), k_s.reshape(hd), v_s.reshape(hd)
    q_bcast = jnp.broadcast_to(q_s.reshape(b_s, n_heads_attn, HEAD_DIM, 1),
                               (b_s, n_heads_attn, HEAD_DIM, PAGE_SIZE))
    ck = jnp.transpose(cache_k.reshape(cache_k.shape[1:]), (0, 2, 3, 1))
    cv = jnp.transpose(cache_v.reshape(cache_v.shape[1:]), (0, 2, 3, 1))

    m_p = b_p * seq
    xp = x_prompt.reshape(m_p, d_model)
    proj = _in_proj(xp, g_pre, w_l, n_main_cols, w_tail, tm=1024)
    attn, kt, vt, gates = _moba_prompt(proj, cols, attn_w, b_p, seq, q_bcast, ck, page_table)
    u, conv_p, ssm_p = _ssd_prompt(proj, cols, cw, cb, dtb_r, dtb_c, alog_r, alog_c, dskip, snorm, b_p, seq)
    y_p = dense_tail(attn, u, proj, xp, 256, 1024)

    idx = _moba_topk(gates, n_sel)
    attn_s = _moba_sample(q4, k4, v4, ck, cv, page_table, idx.reshape(b_s, n_sel * n_heads_attn))
    u_s, conv_s, ssm_s = _ssd_step(xbc_s, z_s, dt_s, state_conv[l], state_ssm[l], cw, cb,
                                   dtb_r, alog_r, dskip, snorm)
    y_s = dense_tail(attn_s.reshape(b_s, attn_w).astype(BF16), u_s, proj_s, xs, b_s, b_s)

    def kv_prompt(t):
        return jnp.transpose(t.reshape(b_p, n_heads_attn, HEAD_DIM, seq), (0, 3, 1, 2))[None]

    kv_s = (1, b_s, 1, n_heads_attn, HEAD_DIM)
    return (y_p.reshape(b_p, seq, d_model), y_s.reshape(b_s, 1, d_model),
            kv_prompt(kt), kv_prompt(vt), conv_p[None], ssm_p[None],
            k_s.reshape(kv_s), v_s.reshape(kv_s), conv_s[None], ssm_s[None])
```

```python
import functools
import math

import jax
import jax.numpy as jnp
from jax import lax
from jax.experimental import pallas as pl
from jax.experimental.pallas import tpu as pltpu

F32 = jnp.float32
BF16 = jnp.bfloat16

EPS = 1e-6
HEAD_DIM = 64
MOBA_BLOCK = 256
MOBA_TOPK = 3
PAGE_SIZE = 128
SSM_HEADDIM = 64
N_SSM_GROUPS = 8
D_STATE = 128
CONV_WIDTH = 4
SSD_CHUNK = 128

LANES = 128
SUBLANES = 8
VMEM_LIMIT = 56 * 1024 * 1024
PROJ_TN = 1536
PROJ_TN_F32 = 512
FFN_TF = 512

NEG_BIG = -(2.0 ** 100)
ROW_GROUP = 16
SCAN_RING = 4

_NT = (((1,), (1,)), ((), ()))
_TN = (((0,), (0,)), ((), ()))


def _cparams(sem):
    return pltpu.CompilerParams(dimension_semantics=sem, vmem_limit_bytes=VMEM_LIMIT)


def _rms(x, g):
    return x * lax.rsqrt(jnp.mean(x * x, axis=-1, keepdims=True) + EPS) * g


def _silu(x):
    return x * jax.nn.sigmoid(x)


def _softplus(x):
    return jnp.maximum(x, 0.0) + jnp.log1p(jnp.exp(-jnp.abs(x)))


def _eye(n):
    return lax.broadcasted_iota(jnp.int32, (n, n), 0) == lax.broadcasted_iota(jnp.int32, (n, n), 1)


def _row_to_col(row):
    n = row.shape[1]
    return jnp.sum(jnp.where(_eye(n), jnp.broadcast_to(row, (n, n)), 0.0), axis=1, keepdims=True)


def _col_to_row(col):
    n = col.shape[0]
    return jnp.sum(jnp.where(_eye(n), jnp.broadcast_to(col, (n, n)), 0.0), axis=0, keepdims=True)


def _in_proj_kernel(x_ref, g_ref, wm_ref, wt_ref, o_ref, *rest, n_main):
    h_scr = rest[-1]
    wm_out, wt_out = rest[:-1] if len(rest) == 3 else (None, None)
    j = pl.program_id(1)

    @pl.when(j == 0)
    def _():
        h_scr[...] = _rms(x_ref[...], g_ref[...]).astype(BF16)

    def tile(w_ref, copy_ref):
        w = w_ref[...].astype(BF16)
        if copy_ref is not None:
            copy_ref[...] = w
        o_ref[...] = jnp.dot(h_scr[...], w, preferred_element_type=F32)

    pl.when(j < n_main)(functools.partial(tile, wm_ref, wm_out))
    pl.when(j >= n_main)(functools.partial(tile, wt_ref, wt_out))


def _in_proj(x, g, w, n_main_cols, w_tail, tm, tn, emit_bf16=False):
    M, D = x.shape
    assert n_main_cols % tn == 0 and w_tail.shape[1] % tn == 0
    n_main = n_main_cols // tn
    n_tail = w_tail.shape[1] // tn
    main_map = lambda i, j: (0, jnp.minimum(j, n_main - 1))
    tail_map = lambda i, j: (0, jnp.clip(j - n_main, 0, n_tail - 1))
    out_specs = [pl.BlockSpec((tm, tn), lambda i, j: (i, j))]
    out_shape = [jax.ShapeDtypeStruct((M, (n_main + n_tail) * tn), F32)]
    if emit_bf16:
        assert M == tm
        out_specs += [pl.BlockSpec((D, tn), main_map), pl.BlockSpec((D, tn), tail_map)]
        out_shape += [jax.ShapeDtypeStruct((D, n_main_cols), BF16),
                      jax.ShapeDtypeStruct((D, w_tail.shape[1]), BF16)]
    outs = pl.pallas_call(
        functools.partial(_in_proj_kernel, n_main=n_main),
        grid=(M // tm, n_main + n_tail),
        in_specs=[
            pl.BlockSpec((tm, D), lambda i, j: (i, 0), pipeline_mode=pl.Buffered(1)),
            pl.BlockSpec((1, D), lambda i, j: (0, 0)),
            pl.BlockSpec((D, tn), main_map),
            pl.BlockSpec((D, tn), tail_map),
        ],
        out_specs=out_specs,
        out_shape=out_shape,
        scratch_shapes=[pltpu.VMEM((tm, D), BF16)],
        compiler_params=_cparams(("arbitrary", "arbitrary")),
        name="in_proj",
    )(x, g, w, w_tail)
    return outs if emit_bf16 else outs[0]


def _moba_prompt_kernel(pt_ref, q_ref, k_ref, v_ref, qs_ref, ck_ref, o_ref, kt_ref, vt_ref, g_ref,
                        s_scr, p_scr, l_scr, o_scr, scan_buf, scan_sem, *, L, scale, n_pages):
    BS = MOBA_BLOCK
    nb = L // BS
    assert nb <= HEAD_DIM
    n_sub = 2 * nb
    ppb = BS // PAGE_SIZE
    ch = n_pages // n_sub
    n_buf = scan_buf.shape[0]
    assert n_pages % n_sub == 0 and ch % ppb == 0 and n_sub % n_buf == 0
    step = pl.program_id(0) * pl.num_programs(1) + pl.program_id(1)
    n_steps = pl.num_programs(0) * pl.num_programs(1)
    _, n_h, d_h, T = qs_ref.shape

    def scan_start(j):
        seq, jj = (step, j) if j < n_sub else (step + 1, j - n_sub)

        def go():
            for t in range(ch):
                pltpu.make_async_copy(ck_ref.at[pt_ref[seq, jj * ch + t]], scan_buf.at[j % n_buf, t],
                                      scan_sem.at[j % n_buf]).start()

        if j < n_sub:
            go()
        else:
            pl.when(step + 1 < n_steps)(go)

    def scan_chunk(j):
        slot = j % n_buf
        scan_start(j + n_buf - 1)
        for t in range(ch):
            pltpu.make_async_copy(ck_ref.at[pt_ref[step, j * ch + t]], scan_buf.at[slot, t],
                                  scan_sem.at[slot]).wait()
        for h in range(n_h):
            qh = qs_ref[0, h]
            for u in range(ch // ppb):
                part = None
                for t in range(ppb):
                    prod = scan_buf[slot, u * ppb + t, h] * qh
                    pp = jnp.sum(prod.reshape(d_h // SUBLANES, SUBLANES, T), axis=0)
                    part = pp if part is None else part + pp
                n = j * (ch // ppb) + u
                g_ref[0, h:h + 1, n:n + 1] = jnp.sum(part, keepdims=True) * (1.0 / BS)

    @pl.when(step == 0)
    def _():
        for j in range(n_buf - 1):
            scan_start(j)

    q = q_ref[...]
    k = k_ref[...]
    v = v_ref[...]
    vb = v.astype(BF16)
    kt_ref[0] = k.T
    vt_ref[0] = v.T
    lane = lax.broadcasted_iota(jnp.int32, (1, LANES), 1)
    first = lane < HEAD_DIM

    kmean = jnp.concatenate(
        [jnp.mean(k[n * BS:(n + 1) * BS], axis=0, keepdims=True) for n in range(nb)], axis=0)

    blk = lax.broadcasted_iota(jnp.int32, (nb, L), 0)
    qblk = lax.broadcasted_iota(jnp.int32, (nb, L), 1) // BS
    past = blk < qblk
    row = lax.broadcasted_iota(jnp.int32, (BS, BS), 0)
    col = lax.broadcasted_iota(jnp.int32, (BS, BS), 1)
    causal_bias = jnp.where(col <= row, 0.0, -jnp.inf)
    key_blk = lax.broadcasted_iota(jnp.int32, (L, LANES), 0) // BS
    key_lane = lax.broadcasted_iota(jnp.int32, (L, LANES), 1)
    c = scale * math.log2(math.e)

    def scores(q_aug, k_aug, qb):
        qs = slice(qb * BS, (qb + 1) * BS)
        n_past = qb * BS
        if qb > 0:
            s_scr[qb % 2, :, :n_past] = lax.dot_general(q_aug[qs], k_aug[:n_past], _NT,
                                                        preferred_element_type=F32)
        s_scr[qb % 2, :, n_past:n_past + BS] = lax.dot_general(
            q_aug[qs], k_aug[n_past:n_past + BS], _NT, preferred_element_type=F32) + causal_bias

    def softmax(qb):
        nk = (qb + 1) * BS
        sb = qb % 2

        def rows(r, carry):
            rs = pl.ds(pl.multiple_of(r * ROW_GROUP, ROW_GROUP), ROW_GROUP)
            s = s_scr[sb, rs, :nk]
            p = jnp.exp2(s - jnp.max(s, axis=1, keepdims=True))
            l_scr[sb, rs, :] = jnp.broadcast_to(jnp.sum(p, axis=1, keepdims=True), (ROW_GROUP, LANES))
            p_scr[sb, rs, :nk] = p.astype(BF16)
            return carry

        lax.fori_loop(0, BS // ROW_GROUP, rows, 0, unroll=True)

    def weighted_values(head, qb):
        qs = slice(qb * BS, (qb + 1) * BS)
        nk = (qb + 1) * BS
        sb = qb % 2
        o = jnp.dot(p_scr[sb, :, :nk], vb[:nk], preferred_element_type=F32) / l_scr[sb]
        if head == 0:
            o_scr[qs, :] = o
        else:
            o_ref[qs, :] = jnp.where(first, o_scr[qs, :], o).astype(o_ref.dtype)

    for head, (own, off) in enumerate(((first, HEAD_DIM), (~first, 0))):
        gate = lax.dot_general(jnp.where(own, kmean, 0.0), q, _NT,
                               precision=lax.Precision.HIGHEST, preferred_element_type=F32)
        gate = jnp.where(past, gate, -jnp.inf)
        rank = jnp.zeros((nb, L), jnp.int32)
        for m in range(nb):
            gm = gate[m:m + 1, :]
            beats = (gm > gate) | ((gm == gate) & (m < blk))
            rank = rank + beats.astype(jnp.int32)
        allowed = (past & (rank < MOBA_TOPK)) | (blk == qblk)
        bias_t = jnp.where(allowed, 0.0, NEG_BIG)
        pieces = [jnp.zeros((off, L), F32), bias_t, jnp.zeros((LANES - off - nb, L), F32)]
        pad_t = jnp.concatenate([t for t in pieces if t.shape[0]], axis=0)
        q_aug = jnp.where(own, q * c, pad_t.T).astype(BF16)
        k_aug = jnp.where(own, k, (key_lane - off == key_blk).astype(F32)).astype(BF16)

        for qb in range(nb + 1):
            if qb < nb:
                scan_chunk(head * nb + qb)
            if qb == 0:
                scores(q_aug, k_aug, 0)
            if qb + 1 < nb:
                scores(q_aug, k_aug, qb + 1)
            if qb < nb:
                softmax(qb)
            if qb > 0:
                weighted_values(head, qb - 1)


def _col_block(off, width):
    assert off % width == 0
    return off // width


def _moba_prompt(proj, cols, W, batch, L, q_sample, cache_kt, page_table):
    M = proj.shape[0]
    B, H, d, T = q_sample.shape
    n_pages = page_table.shape[1]
    n_hp = W // LANES
    n_blocks = n_pages * PAGE_SIZE // MOBA_BLOCK
    assert L % MOBA_BLOCK == 0 and W % LANES == 0 and LANES == 2 * HEAD_DIM
    assert batch * n_hp == B, "one sample sequence is scanned per grid step"
    ch = n_pages // (2 * (L // MOBA_BLOCK))

    def in_spec(name):
        c0 = _col_block(cols[name], LANES)
        return pl.BlockSpec((L, LANES), lambda b, hp, pt: (b, c0 + hp))

    seq4 = lambda b, hp, pt: (b * n_hp + hp, 0, 0, 0)
    spec_t = pl.BlockSpec((1, LANES, L), lambda b, hp, pt: (b, hp, 0))
    kv_t = jax.ShapeDtypeStruct((batch, W, L), F32)
    grid_spec = pltpu.PrefetchScalarGridSpec(
        num_scalar_prefetch=1,
        grid=(batch, n_hp),
        in_specs=[in_spec("q"), in_spec("k"), in_spec("v"),
                  pl.BlockSpec((1, H, d, T), seq4), pl.BlockSpec(memory_space=pl.ANY)],
        out_specs=[pl.BlockSpec((L, LANES), lambda b, hp, pt: (b, hp)), spec_t, spec_t,
                   pl.BlockSpec((1, H, n_blocks), lambda b, hp, pt: (b * n_hp + hp, 0, 0))],
        scratch_shapes=[pltpu.VMEM((2, MOBA_BLOCK, L), F32), pltpu.VMEM((2, MOBA_BLOCK, L), BF16),
                        pltpu.VMEM((2, MOBA_BLOCK, LANES), F32), pltpu.VMEM((L, LANES), F32),
                        pltpu.VMEM((SCAN_RING, ch, H, d, T), F32),
                        pltpu.SemaphoreType.DMA((SCAN_RING,))],
    )
    return pl.pallas_call(
        functools.partial(_moba_prompt_kernel, L=L, scale=HEAD_DIM ** -0.5, n_pages=n_pages),
        grid_spec=grid_spec,
        out_shape=[jax.ShapeDtypeStruct((M, W), BF16), kv_t, kv_t,
                   jax.ShapeDtypeStruct((B, H, n_blocks), F32)],
        compiler_params=_cparams(("arbitrary", "arbitrary")),
        name="moba_prompt",
    )(page_table, proj, proj, proj, q_sample, cache_kt)


def _ssd_prompt_kernel(*refs, d_inner, n_x, n_z):
    xbc_refs, z_refs = refs[:n_x], refs[n_x:n_x + n_z]
    (dt_ref, cw_ref, cb_ref, dtb_r_ref, dtb_c_ref, alog_r_ref, alog_c_ref, dskip_ref, norm_ref,
     u_ref, conv_ref, st_ref, xs_scr) = refs[n_x + n_z:]
    bw = xbc_refs[0].shape[1]
    Q = SSD_CHUNK
    P = SSM_HEADDIM
    N = D_STATE
    G = N_SSM_GROUPS
    R = d_inner // P // G
    assert R % 2 == 0 and 2 * P == LANES and N == LANES
    c = pl.program_id(1)
    tail = CONV_WIDTH - 1

    @pl.when(c == 0)
    def _():
        xs_scr[...] = jnp.zeros(xs_scr.shape, F32)
        st_ref[...] = jnp.zeros(st_ref.shape, F32)

    x = jnp.concatenate([r[...] for r in xbc_refs], axis=1)
    prev = xs_scr[...]
    row8 = lax.broadcasted_iota(jnp.int32, (SUBLANES, 1), 0)

    def shifted(k):
        xk = pltpu.roll(x, k, axis=0)
        head = jnp.where(row8 < k, pltpu.roll(prev, k, axis=0), xk[0:SUBLANES])
        return jnp.concatenate([head, xk[SUBLANES:]], axis=0)

    acc = cw_ref[0:1, :] * shifted(tail)
    for i in range(1, tail):
        acc = acc + cw_ref[i:i + 1, :] * shifted(tail - i)
    acc = acc + cw_ref[tail:tail + 1, :] * x
    xc = _silu(cb_ref[...] + acc)
    xs_scr[...] = x[Q - SUBLANES:Q, :]
    conv_ref[0] = x[Q - tail:Q, :]

    raw = dt_ref[...]
    dt = _softplus(raw + dtb_r_ref[...])
    dt_t = _softplus(raw.T + dtb_c_ref[...])
    a = dt * (-jnp.exp(alog_r_ref[...]))
    a_t = dt_t * (-jnp.exp(alog_c_ref[...]))
    ri = lax.broadcasted_iota(jnp.int32, (Q, Q), 0)
    ci = lax.broadcasted_iota(jnp.int32, (Q, Q), 1)
    causal = ci <= ri
    acum = jnp.dot(causal.astype(F32), a, precision=lax.Precision.HIGHEST,
                   preferred_element_type=F32)
    acum_t = jnp.dot(a_t, (ri <= ci).astype(F32), precision=lax.Precision.HIGHEST,
                     preferred_element_type=F32)
    e_acum = jnp.exp(acum)
    d_end = jnp.exp(acum[Q - 1:Q, :] - acum)
    lane = lax.broadcasted_iota(jnp.int32, (1, LANES), 1)
    first = lane < P

    def pair_cols(t, h0):
        return jnp.where(first, t[:, h0:h0 + 1], t[:, h0 + 1:h0 + 2])

    y_parts = []
    for g in range(G):
        b_g = xc[:, d_inner + g * N:d_inner + (g + 1) * N].astype(BF16)
        c_g = xc[:, d_inner + G * N + g * N:d_inner + G * N + (g + 1) * N].astype(BF16)
        cb = lax.dot_general(c_g, b_g, _NT, preferred_element_type=F32)
        st_g = st_ref[0, g * R:(g + 1) * R].reshape(R * P, N)
        y_off = lax.dot_general(c_g, st_g.astype(BF16), _NT, preferred_element_type=F32)
        xdtd_parts, cd_parts = [], []
        for pr in range(R // 2):
            h0 = g * R + 2 * pr
            x_p = xc[:, h0 * P:h0 * P + LANES]
            xdt = x_p * pair_cols(dt, h0)
            xdt_b = xdt.astype(BF16)
            yd = []
            for hh in (h0, h0 + 1):
                seg = acum[:, hh:hh + 1] - acum_t[hh:hh + 1, :]
                dec = jnp.exp(jnp.where(causal, seg, -jnp.inf))
                yd.append(jnp.dot((cb * dec).astype(BF16), xdt_b, preferred_element_type=F32))
                cd_parts.append(jnp.broadcast_to(jnp.exp(acum_t[hh:hh + 1, Q - 1:Q]), (P, N)))
            y_p = (jnp.where(first, yd[0], yd[1])
                   + y_off[:, 2 * pr * P:2 * pr * P + LANES] * pair_cols(e_acum, h0)
                   + dskip_ref[:, h0 * P:h0 * P + LANES] * x_p)
            y_parts.append(y_p)
            xdtd_parts.append((xdt * pair_cols(d_end, h0)).astype(BF16))
        xdtd = jnp.concatenate(xdtd_parts, axis=1)
        s_new = lax.dot_general(xdtd, b_g, _TN, preferred_element_type=F32)
        st_new = jnp.concatenate(cd_parts, axis=0) * st_g + s_new
        st_ref[0, g * R:(g + 1) * R] = st_new.reshape(R, P, N)

    y = jnp.concatenate(y_parts, axis=1)
    u = y * _silu(jnp.concatenate([r[...] for r in z_refs], axis=1))
    gw = d_inner // G
    u_parts = []
    for g in range(G):
        ug = u[:, g * gw:(g + 1) * gw]
        u_parts.append(ug * lax.rsqrt(jnp.mean(ug * ug, axis=-1, keepdims=True) + EPS))
    u_ref[...] = (jnp.concatenate(u_parts, axis=1) * norm_ref[...]).astype(u_ref.dtype)


def _ssd_prompt(proj, cols, conv_w, conv_b, dtb_r, dtb_c, alog_r, alog_c, dskip, norm, batch, L):
    M = proj.shape[0]
    conv_dim = conv_w.shape[1]
    d_inner = norm.shape[1]
    n_heads = d_inner // SSM_HEADDIM
    assert L % SSD_CHUNK == 0
    nc = L // SSD_CHUNK
    Q = SSD_CHUNK
    tail = CONV_WIDTH - 1
    row = lambda b, c: (b * nc + c, 0)
    const = lambda b, c: (0, 0)

    def in_spec(off, width):
        c0 = _col_block(off, width)
        return pl.BlockSpec((Q, width), lambda b, c: (b * nc + c, c0))

    bw = math.gcd(cols["xbc"], cols["z"], conv_dim, d_inner)
    assert bw % LANES == 0
    n_x, n_z = conv_dim // bw, d_inner // bw
    return pl.pallas_call(
        functools.partial(_ssd_prompt_kernel, d_inner=d_inner, n_x=n_x, n_z=n_z),
        grid=(batch, nc),
        in_specs=[in_spec(cols["xbc"] + i * bw, bw) for i in range(n_x)]
        + [in_spec(cols["z"] + i * bw, bw) for i in range(n_z)]
        + [
            in_spec(cols["dt"], LANES),
            pl.BlockSpec((CONV_WIDTH, conv_dim), const),
            pl.BlockSpec((1, conv_dim), const),
            pl.BlockSpec((1, LANES), const),
            pl.BlockSpec((LANES, 1), const),
            pl.BlockSpec((1, LANES), const),
            pl.BlockSpec((LANES, 1), const),
            pl.BlockSpec((1, d_inner), const),
            pl.BlockSpec((1, d_inner), const),
        ],
        out_specs=[
            pl.BlockSpec((Q, d_inner), row),
            pl.BlockSpec((1, tail, conv_dim), lambda b, c: (b, 0, 0)),
            pl.BlockSpec((1, n_heads, SSM_HEADDIM, D_STATE), lambda b, c: (b, 0, 0, 0)),
        ],
        out_shape=[
            jax.ShapeDtypeStruct((M, d_inner), BF16),
            jax.ShapeDtypeStruct((batch, tail, conv_dim), F32),
            jax.ShapeDtypeStruct((batch, n_heads, SSM_HEADDIM, D_STATE), F32),
        ],
        scratch_shapes=[pltpu.VMEM((SUBLANES, conv_dim), F32)],
        compiler_params=_cparams(("arbitrary", "arbitrary")),
        name="ssd_prompt",
    )(*([proj] * (n_x + n_z + 1)), conv_w, conv_b, dtb_r, dtb_c, alog_r, alog_c, dskip, norm)


def _ssd_step_kernel(xbc_ref, z_ref, dt_ref, cprev_ref, sprev_ref, cw_ref, cb_ref, dtb_ref,
                     alog_ref, dskip_ref, norm_ref, u_ref, conv_ref, st_ref, *, d_inner):
    P = SSM_HEADDIM
    N = D_STATE
    G = N_SSM_GROUPS
    n_heads = d_inner // P
    R = n_heads // G
    assert R % 2 == 0 and 2 * P == LANES and N == LANES
    tail = CONV_WIDTH - 1
    x_new = xbc_ref[0]
    prev = cprev_ref[0]
    acc = cw_ref[0:1, :] * prev[0:1, :]
    for i in range(1, tail):
        acc = acc + cw_ref[i:i + 1, :] * prev[i:i + 1, :]
    acc = acc + cw_ref[tail:tail + 1, :] * x_new
    xc = _silu(cb_ref[...] + acc)
    conv_ref[0, 0:tail - 1, :] = prev[1:tail, :]
    conv_ref[0, tail - 1:tail, :] = x_new

    dt = _softplus(dt_ref[0] + dtb_ref[...])
    decay = jnp.exp(dt * (-jnp.exp(alog_ref[...])))
    ri = lax.broadcasted_iota(jnp.int32, (LANES, LANES), 0)
    ci = lax.broadcasted_iota(jnp.int32, (LANES, LANES), 1)
    eye = ri == ci
    first_rows = lax.broadcasted_iota(jnp.int32, (LANES, 1), 0) < P

    y_parts = []
    for hp in range(n_heads // 2):
        h0 = 2 * hp
        g = h0 // R
        b_g = xc[:, d_inner + g * N:d_inner + (g + 1) * N]
        c_g = xc[:, d_inner + G * N + g * N:d_inner + G * N + (g + 1) * N]
        x_row = xc[:, h0 * P:h0 * P + LANES]
        x_col = jnp.sum(jnp.where(eye, jnp.broadcast_to(x_row, (LANES, LANES)), 0.0),
                        axis=1, keepdims=True)
        dt_col = jnp.where(first_rows, dt[:, h0:h0 + 1], dt[:, h0 + 1:h0 + 2])
        dec_col = jnp.where(first_rows, decay[:, h0:h0 + 1], decay[:, h0 + 1:h0 + 2])
        st = sprev_ref[0, h0:h0 + 2].reshape(2 * P, N)
        st_new = dec_col * st + (dt_col * x_col) * b_g
        st_ref[0, h0:h0 + 2] = st_new.reshape(2, P, N)
        y_col = jnp.sum(st_new * c_g, axis=1, keepdims=True)
        y_row = jnp.sum(jnp.where(eye, jnp.broadcast_to(y_col, (LANES, LANES)), 0.0),
                        axis=0, keepdims=True)
        y_parts.append(y_row + dskip_ref[:, h0 * P:h0 * P + LANES] * x_row)
    y = jnp.concatenate(y_parts, axis=1)
    u = y * _silu(z_ref[0])
    gw = d_inner // G
    u_parts = []
    for g in range(G):
        ug = u[:, g * gw:(g + 1) * gw]
        u_parts.append(ug * lax.rsqrt(jnp.mean(ug * ug, axis=-1, keepdims=True) + EPS))
    u_ref[0] = (jnp.concatenate(u_parts, axis=1) * norm_ref[...]).astype(u_ref.dtype)


def _ssd_step(xbc, z, dt_raw, conv_prev, ssm_prev, conv_w, conv_b, dtb_r, alog_r, dskip, norm):
    B, conv_dim = xbc.shape
    d_inner = z.shape[1]
    n_heads = d_inner // SSM_HEADDIM
    tail = CONV_WIDTH - 1
    const = lambda b: (0, 0)
    b3 = lambda b: (b, 0, 0)
    b4 = lambda b: (b, 0, 0, 0)
    u, conv_new, ssm_new = pl.pallas_call(
        functools.partial(_ssd_step_kernel, d_inner=d_inner),
        grid=(B,),
        in_specs=[
            pl.BlockSpec((1, 1, conv_dim), b3),
            pl.BlockSpec((1, 1, d_inner), b3),
            pl.BlockSpec((1, 1, LANES), b3),
            pl.BlockSpec((1, tail, conv_dim), b3),
            pl.BlockSpec((1, n_heads, SSM_HEADDIM, D_STATE), b4),
            pl.BlockSpec((CONV_WIDTH, conv_dim), const),
            pl.BlockSpec((1, conv_dim), const),
            pl.BlockSpec((1, LANES), const),
            pl.BlockSpec((1, LANES), const),
            pl.BlockSpec((1, d_inner), const),
            pl.BlockSpec((1, d_inner), const),
        ],
        out_specs=[
            pl.BlockSpec((1, 1, d_inner), b3),
            pl.BlockSpec((1, tail, conv_dim), b3),
            pl.BlockSpec((1, n_heads, SSM_HEADDIM, D_STATE), b4),
        ],
        out_shape=[
            jax.ShapeDtypeStruct((B, 1, d_inner), BF16),
            jax.ShapeDtypeStruct((B, tail, conv_dim), F32),
            jax.ShapeDtypeStruct((B, n_heads, SSM_HEADDIM, D_STATE), F32),
        ],
        compiler_params=_cparams(("arbitrary",)),
        name="ssd_step",
    )(xbc.reshape(B, 1, conv_dim), z.reshape(B, 1, d_inner), dt_raw.reshape(B, 1, LANES),
      conv_prev, ssm_prev, conv_w, conv_b, dtb_r, alog_r, dskip, norm)
    return u.reshape(B, d_inner), conv_new, ssm_new


def _moba_topk_kernel(g_ref, idx_ref, *, n_sel):
    gate = g_ref[...]
    n_blocks = gate.shape[2]
    blk = lax.broadcasted_iota(jnp.int32, gate.shape, 2)
    rank = jnp.zeros(gate.shape, jnp.int32)
    for m in range(n_blocks):
        gm = gate[:, :, m:m + 1]
        beats = (gm > gate) | ((gm == gate) & (m < blk))
        rank = rank + beats.astype(jnp.int32)
    for r in range(n_sel):
        idx_ref[:, r] = jnp.sum(jnp.where(rank == r, blk, 0), axis=2, keepdims=True)


def _moba_topk(gates, n_sel):
    B, H, n_blocks = gates.shape
    return pl.pallas_call(
        functools.partial(_moba_topk_kernel, n_sel=n_sel),
        grid=(1,),
        in_specs=[pl.BlockSpec((B, H, n_blocks), lambda b: (0, 0, 0))],
        out_specs=pl.BlockSpec((B, n_sel, H, 1), lambda b: (0, 0, 0, 0)),
        out_shape=jax.ShapeDtypeStruct((B, n_sel, H, 1), jnp.int32),
        compiler_params=_cparams(("arbitrary",)),
        name="moba_topk",
    )(gates)


def _moba_sample_kernel(pt_ref, idx_ref, q_ref, kn_ref, vn_ref, ck_ref, cv_ref, o_ref,
                        kbuf, vbuf, sem, *, n_sel, n_heads, scale):
    b = pl.program_id(0)
    n_b = pl.num_programs(0)
    ppb = MOBA_BLOCK // PAGE_SIZE

    def copies(bb, slot):
        out = []
        for h in range(n_heads):
            for r in range(n_sel):
                blk = idx_ref[bb, r * n_heads + h]
                for t in range(ppb):
                    page = pt_ref[bb, blk * ppb + t]
                    j = r * ppb + t
                    out.append(pltpu.make_async_copy(ck_ref.at[page, h], kbuf.at[slot, h, j], sem.at[0, slot]))
                    out.append(pltpu.make_async_copy(cv_ref.at[page, h], vbuf.at[slot, h, j], sem.at[1, slot]))
        return out

    slot = b % 2

    @pl.when(b == 0)
    def _():
        for cp in copies(b, slot):
            cp.start()

    @pl.when(b + 1 < n_b)
    def _():
        for cp in copies(b + 1, 1 - slot):
            cp.start()

    for cp in copies(b, slot):
        cp.wait()

    def body(h, carry):
        q_row = q_ref[0, h]
        qc = _row_to_col(q_row)
        kh = kbuf[slot, h]
        vh = vbuf[slot, h]
        s = jnp.sum(kh * qc, axis=1, keepdims=True) * scale
        s_own = jnp.sum(kn_ref[0, h] * q_row, axis=-1, keepdims=True) * scale
        m = jnp.maximum(jnp.max(jnp.max(s, axis=0), axis=-1, keepdims=True), s_own)
        p = jnp.exp(s - m)
        p_own = jnp.exp(s_own - m)
        l = jnp.sum(jnp.sum(p, axis=0), axis=-1, keepdims=True) + p_own
        o_past = jnp.sum(jnp.sum(vh * p, axis=0), axis=-1, keepdims=True)
        o_ref[0, h] = (_col_to_row(o_past) + p_own * vn_ref[0, h]) / l
        return carry

    lax.fori_loop(0, n_heads, body, 0, unroll=True)


def _moba_sample(q4, k4, v4, cache_kt, cache_vt, page_table, idx):
    B, H, _, d = q4.shape
    n_sel = idx.shape[1] // H
    n_slabs = n_sel * (MOBA_BLOCK // PAGE_SIZE)
    new = pl.BlockSpec((1, H, 1, d), lambda b, pt, ix: (b, 0, 0, 0))
    grid_spec = pltpu.PrefetchScalarGridSpec(
        num_scalar_prefetch=2,
        grid=(B,),
        in_specs=[new, new, new, pl.BlockSpec(memory_space=pl.ANY), pl.BlockSpec(memory_space=pl.ANY)],
        out_specs=new,
        scratch_shapes=[pltpu.VMEM((2, H, n_slabs, d, PAGE_SIZE), F32),
                        pltpu.VMEM((2, H, n_slabs, d, PAGE_SIZE), F32),
                        pltpu.SemaphoreType.DMA((2, 2))],
    )
    return pl.pallas_call(
        functools.partial(_moba_sample_kernel, n_sel=n_sel, n_heads=H, scale=d ** -0.5),
        grid_spec=grid_spec,
        out_shape=jax.ShapeDtypeStruct((B, H, 1, d), F32),
        compiler_params=_cparams(("arbitrary",)),
        name="moba_sample",
    )(page_table, idx, q4, k4, v4, cache_kt, cache_vt)


def _merge_kernel(*refs, n_g):
    attn_ref, ssm_ref = refs[:2]
    ga_refs, gs_refs = refs[2:2 + n_g], refs[2 + n_g:2 + 2 * n_g]
    x_ref, wa_ref, ws_ref, wo_ref, npost_ref, npre_ref, x1_ref, h2_ref = refs[2 + 2 * n_g:]
    a = jnp.dot(attn_ref[...], wa_ref[...], preferred_element_type=F32)
    s = jnp.dot(ssm_ref[...], ws_ref[...], preferred_element_type=F32)
    ga = jnp.concatenate([r[...] for r in ga_refs], axis=1)
    gs = jnp.concatenate([r[...] for r in gs_refs], axis=1)
    merged = jax.nn.sigmoid(ga) * a + jax.nn.sigmoid(gs) * s
    o = jnp.dot(merged.astype(BF16), wo_ref[...], preferred_element_type=F32)
    x1 = x_ref[...] + _rms(o, npost_ref[...])
    x1_ref[...] = x1
    h2_ref[...] = _rms(x1, npre_ref[...]).astype(BF16)


def _merge(attn, ssm, proj, cols, x, wa, ws, wo, npost, npre, tm):
    M, D = x.shape
    rows = lambda w: pl.BlockSpec((tm, w), lambda i: (i, 0))
    whole = lambda a: pl.BlockSpec(a.shape, lambda i: (0, 0), pipeline_mode=pl.Buffered(1))

    bw = math.gcd(cols["ga"], cols["gs"], D)
    assert bw % LANES == 0
    n_g = D // bw

    def gate_specs(name):
        c0 = _col_block(cols[name], bw)
        return [pl.BlockSpec((tm, bw), lambda i, c=c0 + k: (i, c)) for k in range(n_g)]

    return pl.pallas_call(
        functools.partial(_merge_kernel, n_g=n_g),
        grid=(M // tm,),
        in_specs=[rows(attn.shape[1]), rows(ssm.shape[1])] + gate_specs("ga") + gate_specs("gs")
        + [rows(D), whole(wa), whole(ws), whole(wo), whole(npost), whole(npre)],
        out_specs=[rows(D), rows(D)],
        out_shape=[jax.ShapeDtypeStruct((M, D), F32), jax.ShapeDtypeStruct((M, D), BF16)],
        compiler_params=_cparams(("arbitrary",)),
        name="merge",
    )(attn, ssm, *([proj] * (2 * n_g)), x, wa, ws, wo, npost, npre)


def _ffn_kernel(h_ref, x_ref, wg_ref, wu_ref, wd_ref, npost_ref, y_ref, *copies):
    f = pl.program_id(1)
    h = h_ref[...]
    wg, wu, wd = (r[...].astype(BF16) for r in (wg_ref, wu_ref, wd_ref))
    for copy_ref, w in zip(copies, (wg, wu, wd)):
        copy_ref[...] = w
    act = _silu(jnp.dot(h, wg, preferred_element_type=F32)) * \
        jnp.dot(h, wu, preferred_element_type=F32)

    @pl.when(f == 0)
    def _():
        y_ref[...] = jnp.zeros(y_ref.shape, F32)

    y_ref[...] += jnp.dot(act.astype(BF16), wd, preferred_element_type=F32)

    @pl.when(f == pl.num_programs(1) - 1)
    def _():
        y_ref[...] = x_ref[...] + _rms(y_ref[...], npost_ref[...])


def _ffn(h2, x1, w_g, w_u, w_d, npost, tm, emit_bf16=False):
    M, D = x1.shape
    d_ff = w_g.shape[1]
    tf = FFN_TF
    assert d_ff % tf == 0
    w_specs = [pl.BlockSpec((D, tf), lambda i, f: (0, f)), pl.BlockSpec((D, tf), lambda i, f: (0, f)),
               pl.BlockSpec((tf, D), lambda i, f: (f, 0))]
    out_specs = [pl.BlockSpec((tm, D), lambda i, f: (i, 0))]
    out_shape = [jax.ShapeDtypeStruct((M, D), F32)]
    if emit_bf16:
        assert M == tm
        out_specs += w_specs
        out_shape += [jax.ShapeDtypeStruct(w.shape, BF16) for w in (w_g, w_u, w_d)]
    outs = pl.pallas_call(
        _ffn_kernel,
        grid=(M // tm, d_ff // tf),
        in_specs=[
            pl.BlockSpec((tm, D), lambda i, f: (i, 0), pipeline_mode=pl.Buffered(1)),
            pl.BlockSpec((tm, D), lambda i, f: (i, 0), pipeline_mode=pl.Buffered(1)),
            *w_specs,
            pl.BlockSpec((1, D), lambda i, f: (0, 0)),
        ],
        out_specs=out_specs,
        out_shape=out_shape,
        compiler_params=_cparams(("arbitrary", "arbitrary")),
        name="ffn",
    )(h2, x1, w_g, w_u, w_d, npost)
    return outs if emit_bf16 else outs[0]


def _pad_lanes(v):
    row = jnp.zeros((1, LANES), F32).at[0, :v.shape[0]].set(v.astype(F32))
    return row, row.reshape(LANES, 1)


def kernel(x_prompt, x_sample, cache_k, cache_v, state_conv, state_ssm, page_table, norm_mix_pre, w_in, conv_w, conv_b, dt_bias, a_log, d_skip, ssm_norm, w_attn_out, w_ssm_out, w_out, norm_mix_post, norm_ffn_pre, w_gate, w_up, w_down, norm_ffn_post):
    depth = w_in.shape[0]
    assert depth == 1, "single trunk layer"
    b_p, seq, d_model = x_prompt.shape
    b_s, dec_seq, _ = x_sample.shape
    assert dec_seq == 1
    n_heads_attn = cache_k.shape[3]
    attn_w = n_heads_attn * cache_k.shape[4]
    assert cache_k.shape[4] == HEAD_DIM and cache_k.shape[2] == PAGE_SIZE
    conv_dim = conv_w.shape[2]
    n_ssm_heads = dt_bias.shape[1]
    d_inner = n_ssm_heads * SSM_HEADDIM
    assert conv_dim == d_inner + 2 * N_SSM_GROUPS * D_STATE and n_ssm_heads <= LANES
    n_pages = page_table.shape[1]
    assert (n_pages * PAGE_SIZE) % MOBA_BLOCK == 0
    n_sel = min(MOBA_TOPK, n_pages * PAGE_SIZE // MOBA_BLOCK)
    assert n_sel == MOBA_TOPK

    l = 0
    src, off = {}, 0
    for name, w in (("q", attn_w), ("k", attn_w), ("v", attn_w), ("z", d_inner), ("xbc", conv_dim),
                    ("dt", n_ssm_heads), ("ga", d_model), ("gs", d_model)):
        src[name] = (off, w)
        off += w
    assert off == w_in.shape[2]
    w_l = w_in[l]
    n_main_cols = src["dt"][0]
    cols = {name: src[name][0] for name in ("q", "k", "v", "z", "xbc")}
    parts, off = [], n_main_cols
    for name in ("ga", "gs", "dt"):
        s0, w = src[name]
        cols[name] = off
        parts.append(w_l[:, s0:s0 + w])
        off += w
    n_proj = -(-(cols["dt"] + LANES) // PROJ_TN) * PROJ_TN
    parts.append(jnp.zeros((d_model, n_proj - off), F32))
    w_tail = jnp.concatenate(parts, axis=1)
    wa = w_attn_out[l].astype(BF16)
    ws = w_ssm_out[l].astype(BF16)
    wo = w_out[l].astype(BF16)
    g_pre = norm_mix_pre[l].reshape(1, d_model)
    g_post = norm_mix_post[l].reshape(1, d_model)
    g_fpre = norm_ffn_pre[l].reshape(1, d_model)
    g_fpost = norm_ffn_post[l].reshape(1, d_model)
    cw = conv_w[l]
    cb = conv_b[l].reshape(1, conv_dim)
    dtb_r, dtb_c = _pad_lanes(dt_bias[l])
    alog_r, alog_c = _pad_lanes(a_log[l])
    dskip = jnp.repeat(d_skip[l].astype(F32), SSM_HEADDIM).reshape(1, d_inner)
    snorm = ssm_norm[l].reshape(1, d_inner)


    xs = x_sample.reshape(b_s, d_model)
    proj_s, w_main_b, w_tail_b = _in_proj(xs, g_pre, w_l, n_main_cols, w_tail, tm=b_s,
                                          tn=PROJ_TN_F32, emit_bf16=True)
    group = lambda name, w: proj_s[:, cols[name]:cols[name] + w]
    q_s, k_s, v_s = group("q", attn_w), group("k", attn_w), group("v", attn_w)
    z_s, xbc_s, dt_s = group("z", d_inner), group("xbc", conv_dim), group("dt", LANES)
    hd = (b_s, n_heads_attn, 1, HEAD_DIM)
    q4, k4, v4 = q_s.reshape(hd), k_s.reshape(hd), v_s.reshape(hd)
    q_bcast = jnp.broadcast_to(q_s.reshape(b_s, n_heads_attn, HEAD_DIM, 1),
                               (b_s, n_heads_attn, HEAD_DIM, PAGE_SIZE))
    ck = jnp.transpose(cache_k.reshape(cache_k.shape[1:]), (0, 2, 3, 1))
    cv = jnp.transpose(cache_v.reshape(cache_v.shape[1:]), (0, 2, 3, 1))

    m_p = b_p * seq
    xp = x_prompt.reshape(m_p, d_model)
    proj = _in_proj(xp, g_pre, w_main_b, n_main_cols, w_tail_b, tm=1024, tn=PROJ_TN)
    attn, kt, vt, gates = _moba_prompt(proj, cols, attn_w, b_p, seq, q_bcast, ck, page_table)
    u, conv_p, ssm_p = _ssd_prompt(proj, cols, cw, cb, dtb_r, dtb_c, alog_r, alog_c, dskip, snorm, b_p, seq)

    idx = _moba_topk(gates, n_sel)
    attn_s = _moba_sample(q4, k4, v4, ck, cv, page_table, idx.reshape(b_s, n_sel * n_heads_attn))
    u_s, conv_s, ssm_s = _ssd_step(xbc_s, z_s, dt_s, state_conv[l], state_ssm[l], cw, cb,
                                   dtb_r, alog_r, dskip, snorm)
    x1_s, h2_s = _merge(attn_s.reshape(b_s, attn_w).astype(BF16), u_s, proj_s, cols, xs,
                        wa, ws, wo, g_post, g_fpre, b_s)
    y_s, w_g, w_u, w_d = _ffn(h2_s, x1_s, w_gate[l], w_up[l], w_down[l], g_fpost, b_s, emit_bf16=True)

    x1_p, h2_p = _merge(attn, u, proj, cols, xp, wa, ws, wo, g_post, g_fpre, 256)
    y_p = _ffn(h2_p, x1_p, w_g, w_u, w_d, g_fpost, 1024)

    def kv_prompt(t):
        return jnp.transpose(t.reshape(b_p, n_heads_attn, HEAD_DIM, seq), (0, 3, 1, 2))[None]

    kv_s = (1, b_s, 1, n_heads_attn, HEAD_DIM)
    return (y_p.reshape(b_p, seq, d_model), y_s.reshape(b_s, 1, d_model),
            kv_prompt(kt), kv_prompt(vt), conv_p[None], ssm_p[None],
            k_s.reshape(kv_s), v_s.reshape(kv_s), conv_s[None], ssm_s[None])
```

```python
import functools
import math

import jax
import jax.numpy as jnp
from jax import lax
from jax.experimental import pallas as pl
from jax.experimental.pallas import tpu as pltpu

F32 = jnp.float32
BF16 = jnp.bfloat16

EPS = 1e-6
HEAD_DIM = 64
MOBA_BLOCK = 256
MOBA_TOPK = 3
PAGE_SIZE = 128
SSM_HEADDIM = 64
N_SSM_GROUPS = 8
D_STATE = 128
CONV_WIDTH = 4
SSD_CHUNK = 128

LANES = 128
SUBLANES = 8
VMEM_LIMIT = 56 * 1024 * 1024
PROJ_TN = 1536
PROJ_TN_F32 = 512
FFN_TF = 512

NEG_BIG = -(2.0 ** 100)
ROW_GROUP = 16
SCAN_RING = 4

_NT = (((1,), (1,)), ((), ()))
_TN = (((0,), (0,)), ((), ()))


def _cparams(sem):
    return pltpu.CompilerParams(dimension_semantics=sem, vmem_limit_bytes=VMEM_LIMIT)


def _rms(x, g):
    return x * lax.rsqrt(jnp.mean(x * x, axis=-1, keepdims=True) + EPS) * g


def _silu(x):
    return x * jax.nn.sigmoid(x)


def _softplus(x):
    return jnp.maximum(x, 0.0) + jnp.log1p(jnp.exp(-jnp.abs(x)))


def _eye(n):
    return lax.broadcasted_iota(jnp.int32, (n, n), 0) == lax.broadcasted_iota(jnp.int32, (n, n), 1)


def _row_to_col(row):
    n = row.shape[1]
    return jnp.sum(jnp.where(_eye(n), jnp.broadcast_to(row, (n, n)), 0.0), axis=1, keepdims=True)


def _col_to_row(col):
    n = col.shape[0]
    return jnp.sum(jnp.where(_eye(n), jnp.broadcast_to(col, (n, n)), 0.0), axis=0, keepdims=True)


def _in_proj_kernel(x_ref, g_ref, w_ref, o_ref, *rest, pad):
    h_scr = rest[-1]
    j = pl.program_id(1)

    @pl.when(j == 0)
    def _():
        h_scr[...] = _rms(x_ref[...], g_ref[...]).astype(BF16)

    w = w_ref[...].astype(BF16)
    if pad is not None:
        rows = lax.broadcasted_iota(jnp.int32, (w.shape[0], 1), 0)
        w = jnp.where((j == pad[0]) & (rows >= pad[1]), jnp.zeros_like(w), w)
    if len(rest) == 2:
        rest[0][...] = w
    o_ref[...] = lax.dot_general(h_scr[...], w, _NT, preferred_element_type=F32)


def _in_proj(x, g, w_t, tm, tn, segments=None, pad=None):
    M, D = x.shape
    if segments is None:
        assert w_t.shape[0] % tn == 0
        n_tiles = w_t.shape[0] // tn
        w_spec = pl.BlockSpec((tn, D), lambda i, j: (j, 0))
    else:
        n_tiles = pad[0] + 1
        assert M == tm, "one row tile: every weight tile is visited, and copied, once"

        assert all(first_row % SUBLANES == 0 for _, first_row in segments) and tn % SUBLANES == 0

        def row_offset(i, j):
            off = 0
            for first_tile, first_row in segments:
                off = jnp.where(j >= first_tile, first_row + (j - first_tile) * tn, off)
            return pl.multiple_of(off, SUBLANES), 0

        w_spec = pl.BlockSpec((pl.Element(tn), pl.Element(D)), row_offset)
    out_specs = [pl.BlockSpec((tm, tn), lambda i, j: (i, j))]
    out_shape = [jax.ShapeDtypeStruct((M, n_tiles * tn), F32)]
    if segments is not None:
        out_specs.append(pl.BlockSpec((tn, D), lambda i, j: (j, 0)))
        out_shape.append(jax.ShapeDtypeStruct((n_tiles * tn, D), BF16))
    outs = pl.pallas_call(
        functools.partial(_in_proj_kernel, pad=pad),
        grid=(M // tm, n_tiles),
        in_specs=[pl.BlockSpec((tm, D), lambda i, j: (i, 0)),
                  pl.BlockSpec((1, D), lambda i, j: (0, 0)),
                  w_spec],
        out_specs=out_specs,
        out_shape=out_shape,
        scratch_shapes=[pltpu.VMEM((tm, D), BF16)],
        compiler_params=_cparams(("arbitrary", "arbitrary")),
        name="in_proj",
    )(x, g, w_t)
    return outs if segments is not None else outs[0]


def _moba_prompt_kernel(pt_ref, q_ref, k_ref, v_ref, qs_ref, ck_ref, o_ref, kt_ref, vt_ref, g_ref,
                        s_scr, p_scr, l_scr, o_scr, scan_buf, scan_sem, *, L, scale, n_pages):
    BS = MOBA_BLOCK
    nb = L // BS
    assert nb <= HEAD_DIM
    n_sub = 2 * nb
    ppb = BS // PAGE_SIZE
    ch = n_pages // n_sub
    n_buf = scan_buf.shape[0]
    assert n_pages % n_sub == 0 and ch % ppb == 0 and n_sub % n_buf == 0
    step = pl.program_id(0) * pl.num_programs(1) + pl.program_id(1)
    n_steps = pl.num_programs(0) * pl.num_programs(1)
    _, n_h, d_h, T = qs_ref.shape

    def scan_start(j):
        seq, jj = (step, j) if j < n_sub else (step + 1, j - n_sub)

        def go():
            for t in range(ch):
                pltpu.make_async_copy(ck_ref.at[pt_ref[seq, jj * ch + t]], scan_buf.at[j % n_buf, t],
                                      scan_sem.at[j % n_buf]).start()

        if j < n_sub:
            go()
        else:
            pl.when(step + 1 < n_steps)(go)

    def scan_chunk(j):
        slot = j % n_buf
        scan_start(j + n_buf - 1)
        for t in range(ch):
            pltpu.make_async_copy(ck_ref.at[pt_ref[step, j * ch + t]], scan_buf.at[slot, t],
                                  scan_sem.at[slot]).wait()
        for h in range(n_h):
            qh = qs_ref[0, h]
            for u in range(ch // ppb):
                part = None
                for t in range(ppb):
                    prod = scan_buf[slot, u * ppb + t, h] * qh
                    pp = jnp.sum(prod.reshape(d_h // SUBLANES, SUBLANES, T), axis=0)
                    part = pp if part is None else part + pp
                n = j * (ch // ppb) + u
                g_ref[0, h:h + 1, n:n + 1] = jnp.sum(part, keepdims=True) * (1.0 / BS)

    @pl.when(step == 0)
    def _():
        for j in range(n_buf - 1):
            scan_start(j)

    q = q_ref[...]
    k = k_ref[...]
    v = v_ref[...]
    vb = v.astype(BF16)
    kt_ref[0] = k.T
    vt_ref[0] = v.T
    lane = lax.broadcasted_iota(jnp.int32, (1, LANES), 1)
    first = lane < HEAD_DIM

    kmean = jnp.concatenate(
        [jnp.mean(k[n * BS:(n + 1) * BS], axis=0, keepdims=True) for n in range(nb)], axis=0)

    blk = lax.broadcasted_iota(jnp.int32, (nb, L), 0)
    qblk = lax.broadcasted_iota(jnp.int32, (nb, L), 1) // BS
    past = blk < qblk
    row = lax.broadcasted_iota(jnp.int32, (BS, BS), 0)
    col = lax.broadcasted_iota(jnp.int32, (BS, BS), 1)
    causal_bias = jnp.where(col <= row, 0.0, -jnp.inf)
    key_blk = lax.broadcasted_iota(jnp.int32, (L, LANES), 0) // BS
    key_lane = lax.broadcasted_iota(jnp.int32, (L, LANES), 1)
    c = scale * math.log2(math.e)

    def scores(q_aug, k_aug, qb):
        qs = slice(qb * BS, (qb + 1) * BS)
        n_past = qb * BS
        if qb > 0:
            s_scr[qb % 2, :, :n_past] = lax.dot_general(q_aug[qs], k_aug[:n_past], _NT,
                                                        preferred_element_type=F32)
        s_scr[qb % 2, :, n_past:n_past + BS] = lax.dot_general(
            q_aug[qs], k_aug[n_past:n_past + BS], _NT, preferred_element_type=F32) + causal_bias

    def softmax(qb):
        nk = (qb + 1) * BS
        sb = qb % 2

        def rows(r, carry):
            rs = pl.ds(pl.multiple_of(r * ROW_GROUP, ROW_GROUP), ROW_GROUP)
            s = s_scr[sb, rs, :nk]
            p = jnp.exp2(s - jnp.max(s, axis=1, keepdims=True))
            l_scr[sb, rs, :] = jnp.broadcast_to(jnp.sum(p, axis=1, keepdims=True), (ROW_GROUP, LANES))
            p_scr[sb, rs, :nk] = p.astype(BF16)
            return carry

        lax.fori_loop(0, BS // ROW_GROUP, rows, 0, unroll=True)

    def weighted_values(head, qb):
        qs = slice(qb * BS, (qb + 1) * BS)
        nk = (qb + 1) * BS
        sb = qb % 2
        o = jnp.dot(p_scr[sb, :, :nk], vb[:nk], preferred_element_type=F32) / l_scr[sb]
        if head == 0:
            o_scr[qs, :] = o
        else:
            o_ref[qs, :] = jnp.where(first, o_scr[qs, :], o).astype(o_ref.dtype)

    for head, (own, off) in enumerate(((first, HEAD_DIM), (~first, 0))):
        gate = lax.dot_general(jnp.where(own, kmean, 0.0), q, _NT,
                               precision=lax.Precision.HIGHEST, preferred_element_type=F32)
        gate = jnp.where(past, gate, -jnp.inf)
        rank = jnp.zeros((nb, L), jnp.int32)
        for m in range(nb):
            gm = gate[m:m + 1, :]
            beats = (gm > gate) | ((gm == gate) & (m < blk))
            rank = rank + beats.astype(jnp.int32)
        allowed = (past & (rank < MOBA_TOPK)) | (blk == qblk)
        bias_t = jnp.where(allowed, 0.0, NEG_BIG)
        pieces = [jnp.zeros((off, L), F32), bias_t, jnp.zeros((LANES - off - nb, L), F32)]
        pad_t = jnp.concatenate([t for t in pieces if t.shape[0]], axis=0)
        q_aug = jnp.where(own, q * c, pad_t.T).astype(BF16)
        k_aug = jnp.where(own, k, (key_lane - off == key_blk).astype(F32)).astype(BF16)

        for qb in range(nb + 1):
            if qb < nb:
                scan_chunk(head * nb + qb)
            if qb == 0:
                scores(q_aug, k_aug, 0)
            if qb + 1 < nb:
                scores(q_aug, k_aug, qb + 1)
            if qb < nb:
                softmax(qb)
            if qb > 0:
                weighted_values(head, qb - 1)


def _col_block(off, width):
    assert off % width == 0
    return off // width


def _moba_prompt(proj, cols, W, batch, L, q_sample, cache_kt, page_table):
    M = proj.shape[0]
    B, H, d, T = q_sample.shape
    n_pages = page_table.shape[1]
    n_hp = W // LANES
    n_blocks = n_pages * PAGE_SIZE // MOBA_BLOCK
    assert L % MOBA_BLOCK == 0 and W % LANES == 0 and LANES == 2 * HEAD_DIM
    assert batch * n_hp == B, "one sample sequence is scanned per grid step"
    ch = n_pages // (2 * (L // MOBA_BLOCK))

    def in_spec(name):
        c0 = _col_block(cols[name], LANES)
        return pl.BlockSpec((L, LANES), lambda b, hp, pt: (b, c0 + hp))

    seq4 = lambda b, hp, pt: (b * n_hp + hp, 0, 0, 0)
    spec_t = pl.BlockSpec((1, LANES, L), lambda b, hp, pt: (b, hp, 0))
    kv_t = jax.ShapeDtypeStruct((batch, W, L), F32)
    grid_spec = pltpu.PrefetchScalarGridSpec(
        num_scalar_prefetch=1,
        grid=(batch, n_hp),
        in_specs=[in_spec("q"), in_spec("k"), in_spec("v"),
                  pl.BlockSpec((1, H, d, T), seq4), pl.BlockSpec(memory_space=pl.ANY)],
        out_specs=[pl.BlockSpec((L, LANES), lambda b, hp, pt: (b, hp)), spec_t, spec_t,
                   pl.BlockSpec((1, H, n_blocks), lambda b, hp, pt: (b * n_hp + hp, 0, 0))],
        scratch_shapes=[pltpu.VMEM((2, MOBA_BLOCK, L), F32), pltpu.VMEM((2, MOBA_BLOCK, L), BF16),
                        pltpu.VMEM((2, MOBA_BLOCK, LANES), F32), pltpu.VMEM((L, LANES), F32),
                        pltpu.VMEM((SCAN_RING, ch, H, d, T), F32),
                        pltpu.SemaphoreType.DMA((SCAN_RING,))],
    )
    return pl.pallas_call(
        functools.partial(_moba_prompt_kernel, L=L, scale=HEAD_DIM ** -0.5, n_pages=n_pages),
        grid_spec=grid_spec,
        out_shape=[jax.ShapeDtypeStruct((M, W), BF16), kv_t, kv_t,
                   jax.ShapeDtypeStruct((B, H, n_blocks), F32)],
        compiler_params=_cparams(("arbitrary", "arbitrary")),
        name="moba_prompt",
    )(page_table, proj, proj, proj, q_sample, cache_kt)


def _ssd_prompt_kernel(*refs, d_inner, n_x, n_z):
    xbc_refs, z_refs = refs[:n_x], refs[n_x:n_x + n_z]
    (dt_ref, cw_ref, cb_ref, dtb_r_ref, dtb_c_ref, alog_r_ref, alog_c_ref, dskip_ref, norm_ref,
     u_ref, conv_ref, st_ref, xs_scr) = refs[n_x + n_z:]
    bw = xbc_refs[0].shape[1]
    Q = SSD_CHUNK
    P = SSM_HEADDIM
    N = D_STATE
    G = N_SSM_GROUPS
    R = d_inner // P // G
    assert R % 2 == 0 and 2 * P == LANES and N == LANES
    c = pl.program_id(1)
    tail = CONV_WIDTH - 1

    @pl.when(c == 0)
    def _():
        xs_scr[...] = jnp.zeros(xs_scr.shape, F32)
        st_ref[...] = jnp.zeros(st_ref.shape, F32)

    x = jnp.concatenate([r[...] for r in xbc_refs], axis=1)
    prev = xs_scr[...]
    row8 = lax.broadcasted_iota(jnp.int32, (SUBLANES, 1), 0)

    def shifted(k):
        xk = pltpu.roll(x, k, axis=0)
        head = jnp.where(row8 < k, pltpu.roll(prev, k, axis=0), xk[0:SUBLANES])
        return jnp.concatenate([head, xk[SUBLANES:]], axis=0)

    acc = cw_ref[0:1, :] * shifted(tail)
    for i in range(1, tail):
        acc = acc + cw_ref[i:i + 1, :] * shifted(tail - i)
    acc = acc + cw_ref[tail:tail + 1, :] * x
    xc = _silu(cb_ref[...] + acc)
    xs_scr[...] = x[Q - SUBLANES:Q, :]
    conv_ref[0] = x[Q - tail:Q, :]

    raw = dt_ref[...]
    dt = _softplus(raw + dtb_r_ref[...])
    dt_t = _softplus(raw.T + dtb_c_ref[...])
    a = dt * (-jnp.exp(alog_r_ref[...]))
    a_t = dt_t * (-jnp.exp(alog_c_ref[...]))
    ri = lax.broadcasted_iota(jnp.int32, (Q, Q), 0)
    ci = lax.broadcasted_iota(jnp.int32, (Q, Q), 1)
    causal = ci <= ri
    acum = jnp.dot(causal.astype(F32), a, precision=lax.Precision.HIGHEST,
                   preferred_element_type=F32)
    acum_t = jnp.dot(a_t, (ri <= ci).astype(F32), precision=lax.Precision.HIGHEST,
                     preferred_element_type=F32)
    e_acum = jnp.exp(acum)
    d_end = jnp.exp(acum[Q - 1:Q, :] - acum)
    lane = lax.broadcasted_iota(jnp.int32, (1, LANES), 1)
    first = lane < P

    def pair_cols(t, h0):
        return jnp.where(first, t[:, h0:h0 + 1], t[:, h0 + 1:h0 + 2])

    y_parts = []
    for g in range(G):
        b_g = xc[:, d_inner + g * N:d_inner + (g + 1) * N].astype(BF16)
        c_g = xc[:, d_inner + G * N + g * N:d_inner + G * N + (g + 1) * N].astype(BF16)
        cb = lax.dot_general(c_g, b_g, _NT, preferred_element_type=F32)
        st_g = st_ref[0, g * R:(g + 1) * R].reshape(R * P, N)
        y_off = lax.dot_general(c_g, st_g.astype(BF16), _NT, preferred_element_type=F32)
        xdtd_parts, cd_parts = [], []
        for pr in range(R // 2):
            h0 = g * R + 2 * pr
            x_p = xc[:, h0 * P:h0 * P + LANES]
            xdt = x_p * pair_cols(dt, h0)
            xdt_b = xdt.astype(BF16)
            yd = []
            for hh in (h0, h0 + 1):
                seg = acum[:, hh:hh + 1] - acum_t[hh:hh + 1, :]
                dec = jnp.exp(jnp.where(causal, seg, -jnp.inf))
                yd.append(jnp.dot((cb * dec).astype(BF16), xdt_b, preferred_element_type=F32))
                cd_parts.append(jnp.broadcast_to(jnp.exp(acum_t[hh:hh + 1, Q - 1:Q]), (P, N)))
            y_p = (jnp.where(first, yd[0], yd[1])
                   + y_off[:, 2 * pr * P:2 * pr * P + LANES] * pair_cols(e_acum, h0)
                   + dskip_ref[:, h0 * P:h0 * P + LANES] * x_p)
            y_parts.append(y_p)
            xdtd_parts.append((xdt * pair_cols(d_end, h0)).astype(BF16))
        xdtd = jnp.concatenate(xdtd_parts, axis=1)
        s_new = lax.dot_general(xdtd, b_g, _TN, preferred_element_type=F32)
        st_new = jnp.concatenate(cd_parts, axis=0) * st_g + s_new
        st_ref[0, g * R:(g + 1) * R] = st_new.reshape(R, P, N)

    y = jnp.concatenate(y_parts, axis=1)
    u = y * _silu(jnp.concatenate([r[...] for r in z_refs], axis=1))
    gw = d_inner // G
    u_parts = []
    for g in range(G):
        ug = u[:, g * gw:(g + 1) * gw]
        u_parts.append(ug * lax.rsqrt(jnp.mean(ug * ug, axis=-1, keepdims=True) + EPS))
    u_ref[...] = (jnp.concatenate(u_parts, axis=1) * norm_ref[...]).astype(u_ref.dtype)


def _ssd_prompt(proj, cols, conv_w, conv_b, dtb_r, dtb_c, alog_r, alog_c, dskip, norm, batch, L):
    M = proj.shape[0]
    conv_dim = conv_w.shape[1]
    d_inner = norm.shape[1]
    n_heads = d_inner // SSM_HEADDIM
    assert L % SSD_CHUNK == 0
    nc = L // SSD_CHUNK
    Q = SSD_CHUNK
    tail = CONV_WIDTH - 1
    row = lambda b, c: (b * nc + c, 0)
    const = lambda b, c: (0, 0)

    def in_spec(off, width):
        c0 = _col_block(off, width)
        return pl.BlockSpec((Q, width), lambda b, c: (b * nc + c, c0))

    bw = math.gcd(cols["xbc"], cols["z"], conv_dim, d_inner)
    assert bw % LANES == 0
    n_x, n_z = conv_dim // bw, d_inner // bw
    return pl.pallas_call(
        functools.partial(_ssd_prompt_kernel, d_inner=d_inner, n_x=n_x, n_z=n_z),
        grid=(batch, nc),
        in_specs=[in_spec(cols["xbc"] + i * bw, bw) for i in range(n_x)]
        + [in_spec(cols["z"] + i * bw, bw) for i in range(n_z)]
        + [
            in_spec(cols["dt"], LANES),
            pl.BlockSpec((CONV_WIDTH, conv_dim), const),
            pl.BlockSpec((1, conv_dim), const),
            pl.BlockSpec((1, LANES), const),
            pl.BlockSpec((LANES, 1), const),
            pl.BlockSpec((1, LANES), const),
            pl.BlockSpec((LANES, 1), const),
            pl.BlockSpec((1, d_inner), const),
            pl.BlockSpec((1, d_inner), const),
        ],
        out_specs=[
            pl.BlockSpec((Q, d_inner), row),
            pl.BlockSpec((1, tail, conv_dim), lambda b, c: (b, 0, 0)),
            pl.BlockSpec((1, n_heads, SSM_HEADDIM, D_STATE), lambda b, c: (b, 0, 0, 0)),
        ],
        out_shape=[
            jax.ShapeDtypeStruct((M, d_inner), BF16),
            jax.ShapeDtypeStruct((batch, tail, conv_dim), F32),
            jax.ShapeDtypeStruct((batch, n_heads, SSM_HEADDIM, D_STATE), F32),
        ],
        scratch_shapes=[pltpu.VMEM((SUBLANES, conv_dim), F32)],
        compiler_params=_cparams(("arbitrary", "arbitrary")),
        name="ssd_prompt",
    )(*([proj] * (n_x + n_z + 1)), conv_w, conv_b, dtb_r, dtb_c, alog_r, alog_c, dskip, norm)


def _ssd_step_kernel(xbc_ref, z_ref, dt_ref, cprev_ref, sprev_ref, cw_ref, cb_ref, dtb_ref,
                     alog_ref, dskip_ref, norm_ref, u_ref, conv_ref, st_ref, *, d_inner):
    P = SSM_HEADDIM
    N = D_STATE
    G = N_SSM_GROUPS
    n_heads = d_inner // P
    R = n_heads // G
    assert R % 2 == 0 and 2 * P == LANES and N == LANES
    tail = CONV_WIDTH - 1
    x_new = xbc_ref[0]
    prev = cprev_ref[0]
    acc = cw_ref[0:1, :] * prev[0:1, :]
    for i in range(1, tail):
        acc = acc + cw_ref[i:i + 1, :] * prev[i:i + 1, :]
    acc = acc + cw_ref[tail:tail + 1, :] * x_new
    xc = _silu(cb_ref[...] + acc)
    conv_ref[0, 0:tail - 1, :] = prev[1:tail, :]
    conv_ref[0, tail - 1:tail, :] = x_new

    dt = _softplus(dt_ref[0] + dtb_ref[...])
    decay = jnp.exp(dt * (-jnp.exp(alog_ref[...])))
    ri = lax.broadcasted_iota(jnp.int32, (LANES, LANES), 0)
    ci = lax.broadcasted_iota(jnp.int32, (LANES, LANES), 1)
    eye = ri == ci
    first_rows = lax.broadcasted_iota(jnp.int32, (LANES, 1), 0) < P

    y_parts = []
    for hp in range(n_heads // 2):
        h0 = 2 * hp
        g = h0 // R
        b_g = xc[:, d_inner + g * N:d_inner + (g + 1) * N]
        c_g = xc[:, d_inner + G * N + g * N:d_inner + G * N + (g + 1) * N]
        x_row = xc[:, h0 * P:h0 * P + LANES]
        x_col = jnp.sum(jnp.where(eye, jnp.broadcast_to(x_row, (LANES, LANES)), 0.0),
                        axis=1, keepdims=True)
        dt_col = jnp.where(first_rows, dt[:, h0:h0 + 1], dt[:, h0 + 1:h0 + 2])
        dec_col = jnp.where(first_rows, decay[:, h0:h0 + 1], decay[:, h0 + 1:h0 + 2])
        st = sprev_ref[0, h0:h0 + 2].reshape(2 * P, N)
        st_new = dec_col * st + (dt_col * x_col) * b_g
        st_ref[0, h0:h0 + 2] = st_new.reshape(2, P, N)
        y_col = jnp.sum(st_new * c_g, axis=1, keepdims=True)
        y_row = jnp.sum(jnp.where(eye, jnp.broadcast_to(y_col, (LANES, LANES)), 0.0),
                        axis=0, keepdims=True)
        y_parts.append(y_row + dskip_ref[:, h0 * P:h0 * P + LANES] * x_row)
    y = jnp.concatenate(y_parts, axis=1)
    u = y * _silu(z_ref[0])
    gw = d_inner // G
    u_parts = []
    for g in range(G):
        ug = u[:, g * gw:(g + 1) * gw]
        u_parts.append(ug * lax.rsqrt(jnp.mean(ug * ug, axis=-1, keepdims=True) + EPS))
    u_ref[0] = (jnp.concatenate(u_parts, axis=1) * norm_ref[...]).astype(u_ref.dtype)


def _ssd_step(xbc, z, dt_raw, conv_prev, ssm_prev, conv_w, conv_b, dtb_r, alog_r, dskip, norm):
    B, conv_dim = xbc.shape
    d_inner = z.shape[1]
    n_heads = d_inner // SSM_HEADDIM
    tail = CONV_WIDTH - 1
    const = lambda b: (0, 0)
    b3 = lambda b: (b, 0, 0)
    b4 = lambda b: (b, 0, 0, 0)
    u, conv_new, ssm_new = pl.pallas_call(
        functools.partial(_ssd_step_kernel, d_inner=d_inner),
        grid=(B,),
        in_specs=[
            pl.BlockSpec((1, 1, conv_dim), b3),
            pl.BlockSpec((1, 1, d_inner), b3),
            pl.BlockSpec((1, 1, LANES), b3),
            pl.BlockSpec((1, tail, conv_dim), b3),
            pl.BlockSpec((1, n_heads, SSM_HEADDIM, D_STATE), b4),
            pl.BlockSpec((CONV_WIDTH, conv_dim), const),
            pl.BlockSpec((1, conv_dim), const),
            pl.BlockSpec((1, LANES), const),
            pl.BlockSpec((1, LANES), const),
            pl.BlockSpec((1, d_inner), const),
            pl.BlockSpec((1, d_inner), const),
        ],
        out_specs=[
            pl.BlockSpec((1, 1, d_inner), b3),
            pl.BlockSpec((1, tail, conv_dim), b3),
            pl.BlockSpec((1, n_heads, SSM_HEADDIM, D_STATE), b4),
        ],
        out_shape=[
            jax.ShapeDtypeStruct((B, 1, d_inner), BF16),
            jax.ShapeDtypeStruct((B, tail, conv_dim), F32),
            jax.ShapeDtypeStruct((B, n_heads, SSM_HEADDIM, D_STATE), F32),
        ],
        compiler_params=_cparams(("arbitrary",)),
        name="ssd_step",
    )(xbc.reshape(B, 1, conv_dim), z.reshape(B, 1, d_inner), dt_raw.reshape(B, 1, LANES),
      conv_prev, ssm_prev, conv_w, conv_b, dtb_r, alog_r, dskip, norm)
    return u.reshape(B, d_inner), conv_new, ssm_new


def _moba_topk_kernel(g_ref, idx_ref, *, n_sel):
    gate = g_ref[...]
    n_blocks = gate.shape[2]
    blk = lax.broadcasted_iota(jnp.int32, gate.shape, 2)
    rank = jnp.zeros(gate.shape, jnp.int32)
    for m in range(n_blocks):
        gm = gate[:, :, m:m + 1]
        beats = (gm > gate) | ((gm == gate) & (m < blk))
        rank = rank + beats.astype(jnp.int32)
    for r in range(n_sel):
        idx_ref[:, r] = jnp.sum(jnp.where(rank == r, blk, 0), axis=2, keepdims=True)


def _moba_topk(gates, n_sel):
    B, H, n_blocks = gates.shape
    return pl.pallas_call(
        functools.partial(_moba_topk_kernel, n_sel=n_sel),
        grid=(1,),
        in_specs=[pl.BlockSpec((B, H, n_blocks), lambda b: (0, 0, 0))],
        out_specs=pl.BlockSpec((B, n_sel, H, 1), lambda b: (0, 0, 0, 0)),
        out_shape=jax.ShapeDtypeStruct((B, n_sel, H, 1), jnp.int32),
        compiler_params=_cparams(("arbitrary",)),
        name="moba_topk",
    )(gates)


def _moba_sample_kernel(pt_ref, idx_ref, q_ref, kn_ref, vn_ref, ck_ref, cv_ref, o_ref,
                        kbuf, vbuf, sem, *, n_sel, n_heads, scale):
    b = pl.program_id(0)
    n_b = pl.num_programs(0)
    ppb = MOBA_BLOCK // PAGE_SIZE

    def copies(bb, slot):
        out = []
        for h in range(n_heads):
            for r in range(n_sel):
                blk = idx_ref[bb, r * n_heads + h]
                for t in range(ppb):
                    page = pt_ref[bb, blk * ppb + t]
                    j = r * ppb + t
                    out.append(pltpu.make_async_copy(ck_ref.at[page, h], kbuf.at[slot, h, j], sem.at[0, slot]))
                    out.append(pltpu.make_async_copy(cv_ref.at[page, h], vbuf.at[slot, h, j], sem.at[1, slot]))
        return out

    slot = b % 2

    @pl.when(b == 0)
    def _():
        for cp in copies(b, slot):
            cp.start()

    @pl.when(b + 1 < n_b)
    def _():
        for cp in copies(b + 1, 1 - slot):
            cp.start()

    for cp in copies(b, slot):
        cp.wait()

    def body(h, carry):
        q_row = q_ref[0, h]
        qc = _row_to_col(q_row)
        kh = kbuf[slot, h]
        vh = vbuf[slot, h]
        s = jnp.sum(kh * qc, axis=1, keepdims=True) * scale
        s_own = jnp.sum(kn_ref[0, h] * q_row, axis=-1, keepdims=True) * scale
        m = jnp.maximum(jnp.max(jnp.max(s, axis=0), axis=-1, keepdims=True), s_own)
        p = jnp.exp(s - m)
        p_own = jnp.exp(s_own - m)
        l = jnp.sum(jnp.sum(p, axis=0), axis=-1, keepdims=True) + p_own
        o_past = jnp.sum(jnp.sum(vh * p, axis=0), axis=-1, keepdims=True)
        o_ref[0, h] = (_col_to_row(o_past) + p_own * vn_ref[0, h]) / l
        return carry

    lax.fori_loop(0, n_heads, body, 0, unroll=True)


def _moba_sample(q4, k4, v4, cache_kt, cache_vt, page_table, idx):
    B, H, _, d = q4.shape
    n_sel = idx.shape[1] // H
    n_slabs = n_sel * (MOBA_BLOCK // PAGE_SIZE)
    new = pl.BlockSpec((1, H, 1, d), lambda b, pt, ix: (b, 0, 0, 0))
    grid_spec = pltpu.PrefetchScalarGridSpec(
        num_scalar_prefetch=2,
        grid=(B,),
        in_specs=[new, new, new, pl.BlockSpec(memory_space=pl.ANY), pl.BlockSpec(memory_space=pl.ANY)],
        out_specs=new,
        scratch_shapes=[pltpu.VMEM((2, H, n_slabs, d, PAGE_SIZE), F32),
                        pltpu.VMEM((2, H, n_slabs, d, PAGE_SIZE), F32),
                        pltpu.SemaphoreType.DMA((2, 2))],
    )
    return pl.pallas_call(
        functools.partial(_moba_sample_kernel, n_sel=n_sel, n_heads=H, scale=d ** -0.5),
        grid_spec=grid_spec,
        out_shape=jax.ShapeDtypeStruct((B, H, 1, d), F32),
        compiler_params=_cparams(("arbitrary",)),
        name="moba_sample",
    )(page_table, idx, q4, k4, v4, cache_kt, cache_vt)


def _merge_kernel(*refs, n_g):
    attn_ref, ssm_ref = refs[:2]
    ga_refs, gs_refs = refs[2:2 + n_g], refs[2 + n_g:2 + 2 * n_g]
    x_ref, wa_ref, ws_ref, wo_ref, npost_ref, npre_ref, x1_ref, h2_ref = refs[2 + 2 * n_g:]
    a = jnp.dot(attn_ref[...], wa_ref[...], preferred_element_type=F32)
    s = jnp.dot(ssm_ref[...], ws_ref[...], preferred_element_type=F32)
    ga = jnp.concatenate([r[...] for r in ga_refs], axis=1)
    gs = jnp.concatenate([r[...] for r in gs_refs], axis=1)
    merged = jax.nn.sigmoid(ga) * a + jax.nn.sigmoid(gs) * s
    o = jnp.dot(merged.astype(BF16), wo_ref[...], preferred_element_type=F32)
    x1 = x_ref[...] + _rms(o, npost_ref[...])
    x1_ref[...] = x1
    h2_ref[...] = _rms(x1, npre_ref[...]).astype(BF16)


def _merge(attn, ssm, proj, cols, x, wa, ws, wo, npost, npre, tm):
    M, D = x.shape
    rows = lambda w: pl.BlockSpec((tm, w), lambda i: (i, 0))
    whole = lambda a: pl.BlockSpec(a.shape, lambda i: (0, 0), pipeline_mode=pl.Buffered(1))

    bw = math.gcd(cols["ga"], cols["gs"], D)
    assert bw % LANES == 0
    n_g = D // bw

    def gate_specs(name):
        c0 = _col_block(cols[name], bw)
        return [pl.BlockSpec((tm, bw), lambda i, c=c0 + k: (i, c)) for k in range(n_g)]

    return pl.pallas_call(
        functools.partial(_merge_kernel, n_g=n_g),
        grid=(M // tm,),
        in_specs=[rows(attn.shape[1]), rows(ssm.shape[1])] + gate_specs("ga") + gate_specs("gs")
        + [rows(D), whole(wa), whole(ws), whole(wo), whole(npost), whole(npre)],
        out_specs=[rows(D), rows(D)],
        out_shape=[jax.ShapeDtypeStruct((M, D), F32), jax.ShapeDtypeStruct((M, D), BF16)],
        compiler_params=_cparams(("arbitrary",)),
        name="merge",
    )(attn, ssm, *([proj] * (2 * n_g)), x, wa, ws, wo, npost, npre)


def _ffn_kernel(h_ref, x_ref, wg_ref, wu_ref, wd_ref, npost_ref, y_ref, *copies):
    f = pl.program_id(1)
    h = h_ref[...]
    wg, wu, wd = (r[...].astype(BF16) for r in (wg_ref, wu_ref, wd_ref))
    for copy_ref, w in zip(copies, (wg, wu, wd)):
        copy_ref[...] = w
    act = _silu(jnp.dot(h, wg, preferred_element_type=F32)) * \
        jnp.dot(h, wu, preferred_element_type=F32)

    @pl.when(f == 0)
    def _():
        y_ref[...] = jnp.zeros(y_ref.shape, F32)

    y_ref[...] += jnp.dot(act.astype(BF16), wd, preferred_element_type=F32)

    @pl.when(f == pl.num_programs(1) - 1)
    def _():
        y_ref[...] = x_ref[...] + _rms(y_ref[...], npost_ref[...])


def _ffn(h2, x1, w_g, w_u, w_d, npost, tm, emit_bf16=False):
    M, D = x1.shape
    d_ff = w_g.shape[1]
    tf = FFN_TF
    assert d_ff % tf == 0
    w_specs = [pl.BlockSpec((D, tf), lambda i, f: (0, f)), pl.BlockSpec((D, tf), lambda i, f: (0, f)),
               pl.BlockSpec((tf, D), lambda i, f: (f, 0))]
    out_specs = [pl.BlockSpec((tm, D), lambda i, f: (i, 0))]
    out_shape = [jax.ShapeDtypeStruct((M, D), F32)]
    if emit_bf16:
        assert M == tm
        out_specs += w_specs
        out_shape += [jax.ShapeDtypeStruct(w.shape, BF16) for w in (w_g, w_u, w_d)]
    outs = pl.pallas_call(
        _ffn_kernel,
        grid=(M // tm, d_ff // tf),
        in_specs=[
            pl.BlockSpec((tm, D), lambda i, f: (i, 0), pipeline_mode=pl.Buffered(1)),
            pl.BlockSpec((tm, D), lambda i, f: (i, 0), pipeline_mode=pl.Buffered(1)),
            *w_specs,
            pl.BlockSpec((1, D), lambda i, f: (0, 0)),
        ],
        out_specs=out_specs,
        out_shape=out_shape,
        compiler_params=_cparams(("arbitrary", "arbitrary")),
        name="ffn",
    )(h2, x1, w_g, w_u, w_d, npost)
    return outs if emit_bf16 else outs[0]


def _pad_lanes(v):
    row = jnp.zeros((1, LANES), F32).at[0, :v.shape[0]].set(v.astype(F32))
    return row, row.reshape(LANES, 1)


def kernel(x_prompt, x_sample, cache_k, cache_v, state_conv, state_ssm, page_table, norm_mix_pre, w_in, conv_w, conv_b, dt_bias, a_log, d_skip, ssm_norm, w_attn_out, w_ssm_out, w_out, norm_mix_post, norm_ffn_pre, w_gate, w_up, w_down, norm_ffn_post):
    depth = w_in.shape[0]
    assert depth == 1, "single trunk layer"
    b_p, seq, d_model = x_prompt.shape
    b_s, dec_seq, _ = x_sample.shape
    assert dec_seq == 1
    n_heads_attn = cache_k.shape[3]
    attn_w = n_heads_attn * cache_k.shape[4]
    assert cache_k.shape[4] == HEAD_DIM and cache_k.shape[2] == PAGE_SIZE
    conv_dim = conv_w.shape[2]
    n_ssm_heads = dt_bias.shape[1]
    d_inner = n_ssm_heads * SSM_HEADDIM
    assert conv_dim == d_inner + 2 * N_SSM_GROUPS * D_STATE and n_ssm_heads <= LANES
    n_pages = page_table.shape[1]
    assert (n_pages * PAGE_SIZE) % MOBA_BLOCK == 0
    n_sel = min(MOBA_TOPK, n_pages * PAGE_SIZE // MOBA_BLOCK)
    assert n_sel == MOBA_TOPK

    l = 0
    src, off = {}, 0
    for name, w in (("q", attn_w), ("k", attn_w), ("v", attn_w), ("z", d_inner), ("xbc", conv_dim),
                    ("dt", n_ssm_heads), ("ga", d_model), ("gs", d_model)):
        src[name] = (off, w)
        off += w
    assert off == w_in.shape[2]
    w_t = jnp.transpose(w_in[l])
    tn_f = PROJ_TN_F32
    cols = {name: src[name][0] for name in ("q", "k", "v", "z", "xbc")}
    segments, tile, off = [(0, 0)], src["dt"][0] // tn_f, src["dt"][0]
    assert off % tn_f == 0 and d_model % tn_f == 0
    for name in ("ga", "gs", "dt"):
        segments.append((tile, src[name][0]))
        cols[name] = off
        tile += -(-src[name][1] // tn_f)
        off = tile * tn_f
    assert n_ssm_heads <= LANES <= tn_f and src["dt"][0] + tn_f <= w_t.shape[0] and off % PROJ_TN == 0
    wa = w_attn_out[l].astype(BF16)
    ws = w_ssm_out[l].astype(BF16)
    wo = w_out[l].astype(BF16)
    g_pre = norm_mix_pre[l].reshape(1, d_model)
    g_post = norm_mix_post[l].reshape(1, d_model)
    g_fpre = norm_ffn_pre[l].reshape(1, d_model)
    g_fpost = norm_ffn_post[l].reshape(1, d_model)
    cw = conv_w[l]
    cb = conv_b[l].reshape(1, conv_dim)
    dtb_r, dtb_c = _pad_lanes(dt_bias[l])
    alog_r, alog_c = _pad_lanes(a_log[l])
    dskip = jnp.repeat(d_skip[l].astype(F32), SSM_HEADDIM).reshape(1, d_inner)
    snorm = ssm_norm[l].reshape(1, d_inner)


    xs = x_sample.reshape(b_s, d_model)
    proj_s, w_t_b = _in_proj(xs, g_pre, w_t, tm=b_s, tn=tn_f, segments=tuple(segments),
                             pad=(tile - 1, n_ssm_heads))
    group = lambda name, w: proj_s[:, cols[name]:cols[name] + w]
    q_s, k_s, v_s = group("q", attn_w), group("k", attn_w), group("v", attn_w)
    z_s, xbc_s, dt_s = group("z", d_inner), group("xbc", conv_dim), group("dt", LANES)
    hd = (b_s, n_heads_attn, 1, HEAD_DIM)
    q4, k4, v4 = q_s.reshape(hd), k_s.reshape(hd), v_s.reshape(hd)
    q_bcast = jnp.broadcast_to(q_s.reshape(b_s, n_heads_attn, HEAD_DIM, 1),
                               (b_s, n_heads_attn, HEAD_DIM, PAGE_SIZE))
    ck = jnp.transpose(cache_k.reshape(cache_k.shape[1:]), (0, 2, 3, 1))
    cv = jnp.transpose(cache_v.reshape(cache_v.shape[1:]), (0, 2, 3, 1))

    m_p = b_p * seq
    xp = x_prompt.reshape(m_p, d_model)
    proj = _in_proj(xp, g_pre, w_t_b, tm=1024, tn=PROJ_TN)
    attn, kt, vt, gates = _moba_prompt(proj, cols, attn_w, b_p, seq, q_bcast, ck, page_table)
    u, conv_p, ssm_p = _ssd_prompt(proj, cols, cw, cb, dtb_r, dtb_c, alog_r, alog_c, dskip, snorm, b_p, seq)

    idx = _moba_topk(gates, n_sel)
    attn_s = _moba_sample(q4, k4, v4, ck, cv, page_table, idx.reshape(b_s, n_sel * n_heads_attn))
    u_s, conv_s, ssm_s = _ssd_step(xbc_s, z_s, dt_s, state_conv[l], state_ssm[l], cw, cb,
                                   dtb_r, alog_r, dskip, snorm)
    x1_s, h2_s = _merge(attn_s.reshape(b_s, attn_w).astype(BF16), u_s, proj_s, cols, xs,
                        wa, ws, wo, g_post, g_fpre, b_s)
    y_s, w_g, w_u, w_d = _ffn(h2_s, x1_s, w_gate[l], w_up[l], w_down[l], g_fpost, b_s, emit_bf16=True)

    x1_p, h2_p = _merge(attn, u, proj, cols, xp, wa, ws, wo, g_post, g_fpre, 256)
    y_p = _ffn(h2_p, x1_p, w_g, w_u, w_d, g_fpost, 1024)

    def kv_prompt(t):
        return jnp.transpose(t.reshape(b_p, n_heads_attn, HEAD_DIM, seq), (0, 3, 1, 2))[None]

    kv_s = (1, b_s, 1, n_heads_attn, HEAD_DIM)
    return (y_p.reshape(b_p, seq, d_model), y_s.reshape(b_s, 1, d_model),
            kv_prompt(kt), kv_prompt(vt), conv_p[None], ssm_p[None],
            k_s.reshape(kv_s), v_s.reshape(kv_s), conv_s[None], ssm_s[None])
```

```python
import functools
import math

import jax
import jax.numpy as jnp
from jax import lax
from jax.experimental import pallas as pl
from jax.experimental.pallas import tpu as pltpu

F32 = jnp.float32
BF16 = jnp.bfloat16

EPS = 1e-6
HEAD_DIM = 64
MOBA_BLOCK = 256
MOBA_TOPK = 3
PAGE_SIZE = 128
SSM_HEADDIM = 64
N_SSM_GROUPS = 8
D_STATE = 128
CONV_WIDTH = 4
SSD_CHUNK = 128

LANES = 128
SUBLANES = 8
VMEM_LIMIT = 56 * 1024 * 1024
PROJ_TN = 1536
PROJ_TN_F32 = 512
FFN_TF = 512

NEG_BIG = -(2.0 ** 100)
ROW_GROUP = 16
SCAN_RING = 4

_NT = (((1,), (1,)), ((), ()))
_TN = (((0,), (0,)), ((), ()))


def _cparams(sem):
    return pltpu.CompilerParams(dimension_semantics=sem, vmem_limit_bytes=VMEM_LIMIT)


def _rms(x, g):
    return x * lax.rsqrt(jnp.mean(x * x, axis=-1, keepdims=True) + EPS) * g


def _silu(x):
    return x * jax.nn.sigmoid(x)


def _softplus(x):
    return jnp.maximum(x, 0.0) + jnp.log1p(jnp.exp(-jnp.abs(x)))


def _eye(n):
    return lax.broadcasted_iota(jnp.int32, (n, n), 0) == lax.broadcasted_iota(jnp.int32, (n, n), 1)


def _row_to_col(row):
    n = row.shape[1]
    return jnp.sum(jnp.where(_eye(n), jnp.broadcast_to(row, (n, n)), 0.0), axis=1, keepdims=True)


def _col_to_row(col):
    n = col.shape[0]
    return jnp.sum(jnp.where(_eye(n), jnp.broadcast_to(col, (n, n)), 0.0), axis=0, keepdims=True)


def _in_proj_kernel(x_ref, g_ref, w_ref, o_ref, *rest, pad):
    h_scr = rest[-1]
    j = pl.program_id(1)

    @pl.when(j == 0)
    def _():
        h_scr[...] = _rms(x_ref[...], g_ref[...]).astype(BF16)

    w = w_ref[...].astype(BF16)
    if pad is not None:
        rows = lax.broadcasted_iota(jnp.int32, (w.shape[0], 1), 0)
        w = jnp.where((j == pad[0]) & (rows >= pad[1]), jnp.zeros_like(w), w)
    if len(rest) == 2:
        rest[0][...] = w
    o_ref[...] = lax.dot_general(h_scr[...], w, _NT, preferred_element_type=F32)


def _in_proj(x, g, w_t, tm, tn, segments=None, pad=None):
    M, D = x.shape
    if segments is None:
        assert w_t.shape[0] % tn == 0
        n_tiles = w_t.shape[0] // tn
        w_spec = pl.BlockSpec((tn, D), lambda i, j: (j, 0))
    else:
        n_tiles = pad[0] + 1
        assert M == tm, "one row tile: every weight tile is visited, and copied, once"

        assert all(first_row % SUBLANES == 0 for _, first_row in segments) and tn % SUBLANES == 0

        def row_offset(i, j):
            off = 0
            for first_tile, first_row in segments:
                off = jnp.where(j >= first_tile, first_row + (j - first_tile) * tn, off)
            return pl.multiple_of(off, SUBLANES), 0

        w_spec = pl.BlockSpec((pl.Element(tn), pl.Element(D)), row_offset)
    out_specs = [pl.BlockSpec((tm, tn), lambda i, j: (i, j))]
    out_shape = [jax.ShapeDtypeStruct((M, n_tiles * tn), F32)]
    if segments is not None:
        out_specs.append(pl.BlockSpec((tn, D), lambda i, j: (j, 0)))
        out_shape.append(jax.ShapeDtypeStruct((n_tiles * tn, D), BF16))
    outs = pl.pallas_call(
        functools.partial(_in_proj_kernel, pad=pad),
        grid=(M // tm, n_tiles),
        in_specs=[pl.BlockSpec((tm, D), lambda i, j: (i, 0)),
                  pl.BlockSpec((1, D), lambda i, j: (0, 0)),
                  w_spec],
        out_specs=out_specs,
        out_shape=out_shape,
        scratch_shapes=[pltpu.VMEM((tm, D), BF16)],
        compiler_params=_cparams(("arbitrary", "arbitrary")),
        name="in_proj",
    )(x, g, w_t)
    return outs if segments is not None else outs[0]


def _moba_prompt_kernel(pt_ref, q_ref, k_ref, v_ref, qs_ref, ck_ref, o_ref, kt_ref, vt_ref, g_ref,
                        s_scr, p_scr, o_scr, scan_buf, scan_sem, *, L, scale, n_pages):
    BS = MOBA_BLOCK
    nb = L // BS
    assert nb <= HEAD_DIM
    n_sub = 2 * nb
    ppb = BS // PAGE_SIZE
    ch = n_pages // n_sub
    n_buf = scan_buf.shape[0]
    assert n_pages % n_sub == 0 and ch % ppb == 0 and n_sub % n_buf == 0
    step = pl.program_id(0) * pl.num_programs(1) + pl.program_id(1)
    n_steps = pl.num_programs(0) * pl.num_programs(1)
    _, n_h, d_h, T = qs_ref.shape

    def scan_start(j):
        seq, jj = (step, j) if j < n_sub else (step + 1, j - n_sub)

        def go():
            for t in range(ch):
                pltpu.make_async_copy(ck_ref.at[pt_ref[seq, jj * ch + t]], scan_buf.at[j % n_buf, t],
                                      scan_sem.at[j % n_buf]).start()

        if j < n_sub:
            go()
        else:
            pl.when(step + 1 < n_steps)(go)

    def scan_chunk(j):
        slot = j % n_buf
        scan_start(j + n_buf - 1)
        for t in range(ch):
            pltpu.make_async_copy(ck_ref.at[pt_ref[step, j * ch + t]], scan_buf.at[slot, t],
                                  scan_sem.at[slot]).wait()
        for h in range(n_h):
            qh = qs_ref[0, h]
            for u in range(ch // ppb):
                pages = scan_buf[slot, u * ppb, h]
                for t in range(1, ppb):
                    pages = pages + scan_buf[slot, u * ppb + t, h]
                part = jnp.sum((pages * qh).reshape(d_h // SUBLANES, SUBLANES, T), axis=0)
                n = j * (ch // ppb) + u
                g_ref[0, h:h + 1, n:n + 1] = jnp.sum(part, keepdims=True) * (1.0 / BS)

    @pl.when(step == 0)
    def _():
        for j in range(n_buf - 1):
            scan_start(j)

    q = q_ref[...]
    k = k_ref[...]
    v = v_ref[...]
    kt_ref[0] = k.T
    vt_ref[0] = v.T
    lane = lax.broadcasted_iota(jnp.int32, (1, LANES), 1)
    first = lane < HEAD_DIM

    kmean = jnp.concatenate(
        [jnp.mean(k[n * BS:(n + 1) * BS], axis=0, keepdims=True) for n in range(nb)], axis=0)

    blk = lax.broadcasted_iota(jnp.int32, (nb, L), 0)
    qblk = lax.broadcasted_iota(jnp.int32, (nb, L), 1) // BS
    past = blk < qblk
    row = lax.broadcasted_iota(jnp.int32, (BS, BS), 0)
    col = lax.broadcasted_iota(jnp.int32, (BS, BS), 1)
    causal_bias = jnp.where(col <= row, 0.0, -jnp.inf)
    key_blk = lax.broadcasted_iota(jnp.int32, (L, LANES), 0) // BS
    key_lane = lax.broadcasted_iota(jnp.int32, (L, LANES), 1)
    c = scale * math.log2(math.e)

    def scores(q_aug, k_aug, qb):
        qs = slice(qb * BS, (qb + 1) * BS)
        n_past = qb * BS
        if qb > 0:
            s_scr[qb % 2, :, :n_past] = lax.dot_general(q_aug[qs], k_aug[:n_past], _NT,
                                                        preferred_element_type=F32)
        s_scr[qb % 2, :, n_past:n_past + BS] = lax.dot_general(
            q_aug[qs], k_aug[n_past:n_past + BS], _NT, preferred_element_type=F32) + causal_bias

    def softmax(qb):
        nk = (qb + 1) * BS
        sb = qb % 2

        def rows(r, carry):
            rs = pl.ds(pl.multiple_of(r * ROW_GROUP, ROW_GROUP), ROW_GROUP)
            s = s_scr[sb, rs, :nk]
            p_scr[sb, rs, :nk] = jnp.exp2(s - jnp.max(s, axis=1, keepdims=True)).astype(BF16)
            return carry

        lax.fori_loop(0, BS // ROW_GROUP, rows, 0, unroll=True)

    def weighted_values(head, v_aug, qb):
        qs = slice(qb * BS, (qb + 1) * BS)
        nk = (qb + 1) * BS
        o = jnp.dot(p_scr[qb % 2, :, :nk], v_aug[:nk], preferred_element_type=F32)
        o = o / pltpu.roll(o, HEAD_DIM, axis=1)
        if head == 0:
            o_scr[qs, :] = o
        else:
            o_ref[qs, :] = jnp.where(first, o_scr[qs, :], o).astype(o_ref.dtype)

    for head, (own, off) in enumerate(((first, HEAD_DIM), (~first, 0))):
        gate = lax.dot_general(jnp.where(own, kmean, 0.0), q, _NT,
                               precision=lax.Precision.HIGHEST, preferred_element_type=F32)
        gate = jnp.where(past, gate, -jnp.inf)
        rank = jnp.zeros((nb, L), jnp.int32)
        for m in range(nb):
            gm = gate[m:m + 1, :]
            beats = (gm > gate) | ((gm == gate) & (m < blk))
            rank = rank + beats.astype(jnp.int32)
        allowed = (past & (rank < MOBA_TOPK)) | (blk == qblk)
        bias_t = jnp.where(allowed, 0.0, NEG_BIG)
        pieces = [jnp.zeros((off, L), F32), bias_t, jnp.zeros((LANES - off - nb, L), F32)]
        pad_t = jnp.concatenate([t for t in pieces if t.shape[0]], axis=0)
        q_aug = jnp.where(own, q * c, pad_t.T).astype(BF16)
        k_aug = jnp.where(own, k, (key_lane - off == key_blk).astype(F32)).astype(BF16)
        v_aug = jnp.where(own, v, 1.0).astype(BF16)

        for qb in range(nb + 1):
            if qb < nb:
                scan_chunk(head * nb + qb)
            if qb == 0:
                scores(q_aug, k_aug, 0)
            if qb + 1 < nb:
                scores(q_aug, k_aug, qb + 1)
            if qb < nb:
                softmax(qb)
            if qb > 0:
                weighted_values(head, v_aug, qb - 1)


def _col_block(off, width):
    assert off % width == 0
    return off // width


def _moba_prompt(proj, cols, W, batch, L, q_sample, cache_kt, page_table):
    M = proj.shape[0]
    B, H, d, T = q_sample.shape
    n_pages = page_table.shape[1]
    n_hp = W // LANES
    n_blocks = n_pages * PAGE_SIZE // MOBA_BLOCK
    assert L % MOBA_BLOCK == 0 and W % LANES == 0 and LANES == 2 * HEAD_DIM
    assert batch * n_hp == B, "one sample sequence is scanned per grid step"
    ch = n_pages // (2 * (L // MOBA_BLOCK))

    def in_spec(name):
        c0 = _col_block(cols[name], LANES)
        return pl.BlockSpec((L, LANES), lambda b, hp, pt: (b, c0 + hp))

    seq4 = lambda b, hp, pt: (b * n_hp + hp, 0, 0, 0)
    spec_t = pl.BlockSpec((1, LANES, L), lambda b, hp, pt: (b, hp, 0))
    kv_t = jax.ShapeDtypeStruct((batch, W, L), F32)
    grid_spec = pltpu.PrefetchScalarGridSpec(
        num_scalar_prefetch=1,
        grid=(batch, n_hp),
        in_specs=[in_spec("q"), in_spec("k"), in_spec("v"),
                  pl.BlockSpec((1, H, d, T), seq4), pl.BlockSpec(memory_space=pl.ANY)],
        out_specs=[pl.BlockSpec((L, LANES), lambda b, hp, pt: (b, hp)), spec_t, spec_t,
                   pl.BlockSpec((1, H, n_blocks), lambda b, hp, pt: (b * n_hp + hp, 0, 0))],
        scratch_shapes=[pltpu.VMEM((2, MOBA_BLOCK, L), F32), pltpu.VMEM((2, MOBA_BLOCK, L), BF16),
                        pltpu.VMEM((L, LANES), F32),
                        pltpu.VMEM((SCAN_RING, ch, H, d, T), F32),
                        pltpu.SemaphoreType.DMA((SCAN_RING,))],
    )
    return pl.pallas_call(
        functools.partial(_moba_prompt_kernel, L=L, scale=HEAD_DIM ** -0.5, n_pages=n_pages),
        grid_spec=grid_spec,
        out_shape=[jax.ShapeDtypeStruct((M, W), BF16), kv_t, kv_t,
                   jax.ShapeDtypeStruct((B, H, n_blocks), F32)],
        compiler_params=_cparams(("arbitrary", "arbitrary")),
        name="moba_prompt",
    )(page_table, proj, proj, proj, q_sample, cache_kt)


def _ssd_prompt_kernel(*refs, d_inner, n_x, n_z):
    xbc_refs, z_refs = refs[:n_x], refs[n_x:n_x + n_z]
    (dt_ref, cw_ref, cb_ref, dtb_r_ref, dtb_c_ref, alog_r_ref, alog_c_ref, dskip_ref, norm_ref,
     u_ref, conv_ref, st_ref, xs_scr) = refs[n_x + n_z:]
    bw = xbc_refs[0].shape[1]
    Q = SSD_CHUNK
    P = SSM_HEADDIM
    N = D_STATE
    G = N_SSM_GROUPS
    R = d_inner // P // G
    assert R % 2 == 0 and 2 * P == LANES and N == LANES
    c = pl.program_id(1)
    tail = CONV_WIDTH - 1

    @pl.when(c == 0)
    def _():
        xs_scr[...] = jnp.zeros(xs_scr.shape, F32)
        st_ref[...] = jnp.zeros(st_ref.shape, F32)

    x = jnp.concatenate([r[...] for r in xbc_refs], axis=1)
    prev = xs_scr[...]
    row8 = lax.broadcasted_iota(jnp.int32, (SUBLANES, 1), 0)

    def shifted(k):
        xk = pltpu.roll(x, k, axis=0)
        head = jnp.where(row8 < k, pltpu.roll(prev, k, axis=0), xk[0:SUBLANES])
        return jnp.concatenate([head, xk[SUBLANES:]], axis=0)

    acc = cw_ref[0:1, :] * shifted(tail)
    for i in range(1, tail):
        acc = acc + cw_ref[i:i + 1, :] * shifted(tail - i)
    acc = acc + cw_ref[tail:tail + 1, :] * x
    xc = _silu(cb_ref[...] + acc)
    xs_scr[...] = x[Q - SUBLANES:Q, :]
    conv_ref[0] = x[Q - tail:Q, :]

    raw = dt_ref[...]
    dt = _softplus(raw + dtb_r_ref[...])
    dt_t = _softplus(raw.T + dtb_c_ref[...])
    a = dt * (-jnp.exp(alog_r_ref[...]))
    a_t = dt_t * (-jnp.exp(alog_c_ref[...]))
    ri = lax.broadcasted_iota(jnp.int32, (Q, Q), 0)
    ci = lax.broadcasted_iota(jnp.int32, (Q, Q), 1)
    causal = ci <= ri
    acum = jnp.dot(causal.astype(F32), a, precision=lax.Precision.HIGHEST,
                   preferred_element_type=F32)
    acum_t = jnp.dot(a_t, (ri <= ci).astype(F32), precision=lax.Precision.HIGHEST,
                     preferred_element_type=F32)
    e_acum = jnp.exp(acum)
    d_end = jnp.exp(acum[Q - 1:Q, :] - acum)
    lane = lax.broadcasted_iota(jnp.int32, (1, LANES), 1)
    first = lane < P

    def pair_cols(t, h0):
        return jnp.where(first, t[:, h0:h0 + 1], t[:, h0 + 1:h0 + 2])

    y_parts = []
    for g in range(G):
        b_g = xc[:, d_inner + g * N:d_inner + (g + 1) * N].astype(BF16)
        c_g = xc[:, d_inner + G * N + g * N:d_inner + G * N + (g + 1) * N].astype(BF16)
        cb = lax.dot_general(c_g, b_g, _NT, preferred_element_type=F32)
        st_g = st_ref[0, g * R:(g + 1) * R].reshape(R * P, N)
        y_off = lax.dot_general(c_g, st_g.astype(BF16), _NT, preferred_element_type=F32)
        xdtd_parts, cd_parts = [], []
        for pr in range(R // 2):
            h0 = g * R + 2 * pr
            x_p = xc[:, h0 * P:h0 * P + LANES]
            xdt = x_p * pair_cols(dt, h0)
            xdt_b = xdt.astype(BF16)
            yd = []
            for hh in (h0, h0 + 1):
                seg = acum[:, hh:hh + 1] - acum_t[hh:hh + 1, :]
                dec = jnp.exp(jnp.where(causal, seg, -jnp.inf))
                yd.append(jnp.dot((cb * dec).astype(BF16), xdt_b, preferred_element_type=F32))
                cd_parts.append(jnp.broadcast_to(jnp.exp(acum_t[hh:hh + 1, Q - 1:Q]), (P, N)))
            y_p = (jnp.where(first, yd[0], yd[1])
                   + y_off[:, 2 * pr * P:2 * pr * P + LANES] * pair_cols(e_acum, h0)
                   + dskip_ref[:, h0 * P:h0 * P + LANES] * x_p)
            y_parts.append(y_p)
            xdtd_parts.append((xdt * pair_cols(d_end, h0)).astype(BF16))
        xdtd = jnp.concatenate(xdtd_parts, axis=1)
        s_new = lax.dot_general(xdtd, b_g, _TN, preferred_element_type=F32)
        st_new = jnp.concatenate(cd_parts, axis=0) * st_g + s_new
        st_ref[0, g * R:(g + 1) * R] = st_new.reshape(R, P, N)

    y = jnp.concatenate(y_parts, axis=1)
    u = y * _silu(jnp.concatenate([r[...] for r in z_refs], axis=1))
    gw = d_inner // G
    u_parts = []
    for g in range(G):
        ug = u[:, g * gw:(g + 1) * gw]
        u_parts.append(ug * lax.rsqrt(jnp.mean(ug * ug, axis=-1, keepdims=True) + EPS))
    u_ref[...] = (jnp.concatenate(u_parts, axis=1) * norm_ref[...]).astype(u_ref.dtype)


def _ssd_prompt(proj, cols, conv_w, conv_b, dtb_r, dtb_c, alog_r, alog_c, dskip, norm, batch, L):
    M = proj.shape[0]
    conv_dim = conv_w.shape[1]
    d_inner = norm.shape[1]
    n_heads = d_inner // SSM_HEADDIM
    assert L % SSD_CHUNK == 0
    nc = L // SSD_CHUNK
    Q = SSD_CHUNK
    tail = CONV_WIDTH - 1
    row = lambda b, c: (b * nc + c, 0)
    const = lambda b, c: (0, 0)

    def in_spec(off, width):
        c0 = _col_block(off, width)
        return pl.BlockSpec((Q, width), lambda b, c: (b * nc + c, c0))

    bw = math.gcd(cols["xbc"], cols["z"], conv_dim, d_inner)
    assert bw % LANES == 0
    n_x, n_z = conv_dim // bw, d_inner // bw
    return pl.pallas_call(
        functools.partial(_ssd_prompt_kernel, d_inner=d_inner, n_x=n_x, n_z=n_z),
        grid=(batch, nc),
        in_specs=[in_spec(cols["xbc"] + i * bw, bw) for i in range(n_x)]
        + [in_spec(cols["z"] + i * bw, bw) for i in range(n_z)]
        + [
            in_spec(cols["dt"], LANES),
            pl.BlockSpec((CONV_WIDTH, conv_dim), const),
            pl.BlockSpec((1, conv_dim), const),
            pl.BlockSpec((1, LANES), const),
            pl.BlockSpec((LANES, 1), const),
            pl.BlockSpec((1, LANES), const),
            pl.BlockSpec((LANES, 1), const),
            pl.BlockSpec((1, d_inner), const),
            pl.BlockSpec((1, d_inner), const),
        ],
        out_specs=[
            pl.BlockSpec((Q, d_inner), row),
            pl.BlockSpec((1, tail, conv_dim), lambda b, c: (b, 0, 0)),
            pl.BlockSpec((1, n_heads, SSM_HEADDIM, D_STATE), lambda b, c: (b, 0, 0, 0)),
        ],
        out_shape=[
            jax.ShapeDtypeStruct((M, d_inner), BF16),
            jax.ShapeDtypeStruct((batch, tail, conv_dim), F32),
            jax.ShapeDtypeStruct((batch, n_heads, SSM_HEADDIM, D_STATE), F32),
        ],
        scratch_shapes=[pltpu.VMEM((SUBLANES, conv_dim), F32)],
        compiler_params=_cparams(("arbitrary", "arbitrary")),
        name="ssd_prompt",
    )(*([proj] * (n_x + n_z + 1)), conv_w, conv_b, dtb_r, dtb_c, alog_r, alog_c, dskip, norm)


def _ssd_step_kernel(xbc_ref, z_ref, dt_ref, cprev_ref, sprev_ref, cw_ref, cb_ref, dtb_ref,
                     alog_ref, dskip_ref, norm_ref, u_ref, conv_ref, st_ref, *, d_inner):
    P = SSM_HEADDIM
    N = D_STATE
    G = N_SSM_GROUPS
    n_heads = d_inner // P
    R = n_heads // G
    assert R % 2 == 0 and 2 * P == LANES and N == LANES
    tail = CONV_WIDTH - 1
    x_new = xbc_ref[0]
    prev = cprev_ref[0]
    acc = cw_ref[0:1, :] * prev[0:1, :]
    for i in range(1, tail):
        acc = acc + cw_ref[i:i + 1, :] * prev[i:i + 1, :]
    acc = acc + cw_ref[tail:tail + 1, :] * x_new
    xc = _silu(cb_ref[...] + acc)
    conv_ref[0, 0:tail - 1, :] = prev[1:tail, :]
    conv_ref[0, tail - 1:tail, :] = x_new

    dt = _softplus(dt_ref[0] + dtb_ref[...])
    decay = jnp.exp(dt * (-jnp.exp(alog_ref[...])))
    first_rows = lax.broadcasted_iota(jnp.int32, (LANES, 1), 0) < P
    first = lax.broadcasted_iota(jnp.int32, (1, LANES), 1) < P
    pad_rows = lambda row: jnp.broadcast_to(row, (SUBLANES, row.shape[1])).astype(BF16)

    y_parts = []
    for hp in range(n_heads // 2):
        h0 = 2 * hp
        g = h0 // R
        b_g = xc[:, d_inner + g * N:d_inner + (g + 1) * N]
        c_g = xc[:, d_inner + G * N + g * N:d_inner + G * N + (g + 1) * N]
        x_row = xc[:, h0 * P:h0 * P + LANES]
        dt_row = jnp.where(first, dt[:, h0:h0 + 1], dt[:, h0 + 1:h0 + 2])
        dec_col = jnp.where(first_rows, decay[:, h0:h0 + 1], decay[:, h0 + 1:h0 + 2])
        st = sprev_ref[0, h0:h0 + 2].reshape(2 * P, N)
        upd = lax.dot_general(pad_rows(dt_row * x_row), pad_rows(b_g), _TN,
                              preferred_element_type=F32) * (1.0 / SUBLANES)
        st_new = dec_col * st + upd
        st_ref[0, h0:h0 + 2] = st_new.reshape(2, P, N)
        y_row = lax.dot_general(pad_rows(c_g), st_new.astype(BF16), _NT,
                                preferred_element_type=F32)[0:1]
        y_parts.append(y_row + dskip_ref[:, h0 * P:h0 * P + LANES] * x_row)
    y = jnp.concatenate(y_parts, axis=1)
    u = y * _silu(z_ref[0])
    gw = d_inner // G
    u_parts = []
    for g in range(G):
        ug = u[:, g * gw:(g + 1) * gw]
        u_parts.append(ug * lax.rsqrt(jnp.mean(ug * ug, axis=-1, keepdims=True) + EPS))
    u_ref[0] = (jnp.concatenate(u_parts, axis=1) * norm_ref[...]).astype(u_ref.dtype)


def _ssd_step(xbc, z, dt_raw, conv_prev, ssm_prev, conv_w, conv_b, dtb_r, alog_r, dskip, norm):
    B, conv_dim = xbc.shape
    d_inner = z.shape[1]
    n_heads = d_inner // SSM_HEADDIM
    tail = CONV_WIDTH - 1
    const = lambda b: (0, 0)
    b3 = lambda b: (b, 0, 0)
    b4 = lambda b: (b, 0, 0, 0)
    u, conv_new, ssm_new = pl.pallas_call(
        functools.partial(_ssd_step_kernel, d_inner=d_inner),
        grid=(B,),
        in_specs=[
            pl.BlockSpec((1, 1, conv_dim), b3),
            pl.BlockSpec((1, 1, d_inner), b3),
            pl.BlockSpec((1, 1, LANES), b3),
            pl.BlockSpec((1, tail, conv_dim), b3),
            pl.BlockSpec((1, n_heads, SSM_HEADDIM, D_STATE), b4),
            pl.BlockSpec((CONV_WIDTH, conv_dim), const),
            pl.BlockSpec((1, conv_dim), const),
            pl.BlockSpec((1, LANES), const),
            pl.BlockSpec((1, LANES), const),
            pl.BlockSpec((1, d_inner), const),
            pl.BlockSpec((1, d_inner), const),
        ],
        out_specs=[
            pl.BlockSpec((1, 1, d_inner), b3),
            pl.BlockSpec((1, tail, conv_dim), b3),
            pl.BlockSpec((1, n_heads, SSM_HEADDIM, D_STATE), b4),
        ],
        out_shape=[
            jax.ShapeDtypeStruct((B, 1, d_inner), BF16),
            jax.ShapeDtypeStruct((B, tail, conv_dim), F32),
            jax.ShapeDtypeStruct((B, n_heads, SSM_HEADDIM, D_STATE), F32),
        ],
        compiler_params=_cparams(("arbitrary",)),
        name="ssd_step",
    )(xbc.reshape(B, 1, conv_dim), z.reshape(B, 1, d_inner), dt_raw.reshape(B, 1, LANES),
      conv_prev, ssm_prev, conv_w, conv_b, dtb_r, alog_r, dskip, norm)
    return u.reshape(B, d_inner), conv_new, ssm_new


def _moba_topk_kernel(g_ref, idx_ref, *, n_sel):
    gate = g_ref[...]
    n_blocks = gate.shape[2]
    blk = lax.broadcasted_iota(jnp.int32, gate.shape, 2)
    rank = jnp.zeros(gate.shape, jnp.int32)
    for m in range(n_blocks):
        gm = gate[:, :, m:m + 1]
        beats = (gm > gate) | ((gm == gate) & (m < blk))
        rank = rank + beats.astype(jnp.int32)
    for r in range(n_sel):
        idx_ref[:, r] = jnp.sum(jnp.where(rank == r, blk, 0), axis=2, keepdims=True)


def _moba_topk(gates, n_sel):
    B, H, n_blocks = gates.shape
    return pl.pallas_call(
        functools.partial(_moba_topk_kernel, n_sel=n_sel),
        grid=(1,),
        in_specs=[pl.BlockSpec((B, H, n_blocks), lambda b: (0, 0, 0))],
        out_specs=pl.BlockSpec((B, n_sel, H, 1), lambda b: (0, 0, 0, 0)),
        out_shape=jax.ShapeDtypeStruct((B, n_sel, H, 1), jnp.int32),
        compiler_params=_cparams(("arbitrary",)),
        name="moba_topk",
    )(gates)


def _moba_sample_kernel(pt_ref, idx_ref, q_ref, kn_ref, vn_ref, ck_ref, cv_ref, o_ref,
                        kbuf, vbuf, sem, *, n_sel, n_heads, scale):
    b = pl.program_id(0)
    n_b = pl.num_programs(0)
    ppb = MOBA_BLOCK // PAGE_SIZE

    def copies(bb, slot):
        out = []
        for h in range(n_heads):
            for r in range(n_sel):
                blk = idx_ref[bb, r * n_heads + h]
                for t in range(ppb):
                    page = pt_ref[bb, blk * ppb + t]
                    j = r * ppb + t
                    out.append(pltpu.make_async_copy(ck_ref.at[page, h], kbuf.at[slot, h, j], sem.at[0, slot]))
                    out.append(pltpu.make_async_copy(cv_ref.at[page, h], vbuf.at[slot, h, j], sem.at[1, slot]))
        return out

    slot = b % 2

    @pl.when(b == 0)
    def _():
        for cp in copies(b, slot):
            cp.start()

    @pl.when(b + 1 < n_b)
    def _():
        for cp in copies(b + 1, 1 - slot):
            cp.start()

    for cp in copies(b, slot):
        cp.wait()

    def body(h, carry):
        q_row = q_ref[0, h]
        qc = _row_to_col(q_row)
        kh = kbuf[slot, h]
        vh = vbuf[slot, h]
        s = jnp.sum(kh * qc, axis=1, keepdims=True) * scale
        s_own = jnp.sum(kn_ref[0, h] * q_row, axis=-1, keepdims=True) * scale
        m = jnp.maximum(jnp.max(jnp.max(s, axis=0), axis=-1, keepdims=True), s_own)
        p = jnp.exp(s - m)
        p_own = jnp.exp(s_own - m)
        l = jnp.sum(jnp.sum(p, axis=0), axis=-1, keepdims=True) + p_own
        o_past = jnp.sum(jnp.sum(vh * p, axis=0), axis=-1, keepdims=True)
        o_ref[0, h] = (_col_to_row(o_past) + p_own * vn_ref[0, h]) / l
        return carry

    lax.fori_loop(0, n_heads, body, 0, unroll=True)


def _moba_sample(q4, k4, v4, cache_kt, cache_vt, page_table, idx):
    B, H, _, d = q4.shape
    n_sel = idx.shape[1] // H
    n_slabs = n_sel * (MOBA_BLOCK // PAGE_SIZE)
    new = pl.BlockSpec((1, H, 1, d), lambda b, pt, ix: (b, 0, 0, 0))
    grid_spec = pltpu.PrefetchScalarGridSpec(
        num_scalar_prefetch=2,
        grid=(B,),
        in_specs=[new, new, new, pl.BlockSpec(memory_space=pl.ANY), pl.BlockSpec(memory_space=pl.ANY)],
        out_specs=new,
        scratch_shapes=[pltpu.VMEM((2, H, n_slabs, d, PAGE_SIZE), F32),
                        pltpu.VMEM((2, H, n_slabs, d, PAGE_SIZE), F32),
                        pltpu.SemaphoreType.DMA((2, 2))],
    )
    return pl.pallas_call(
        functools.partial(_moba_sample_kernel, n_sel=n_sel, n_heads=H, scale=d ** -0.5),
        grid_spec=grid_spec,
        out_shape=jax.ShapeDtypeStruct((B, H, 1, d), F32),
        compiler_params=_cparams(("arbitrary",)),
        name="moba_sample",
    )(page_table, idx, q4, k4, v4, cache_kt, cache_vt)


def _merge_kernel(*refs, n_g):
    attn_ref, ssm_ref = refs[:2]
    ga_refs, gs_refs = refs[2:2 + n_g], refs[2 + n_g:2 + 2 * n_g]
    x_ref, wa_ref, ws_ref, wo_ref, npost_ref, npre_ref, x1_ref, h2_ref = refs[2 + 2 * n_g:]
    a = jnp.dot(attn_ref[...], wa_ref[...], preferred_element_type=F32)
    s = jnp.dot(ssm_ref[...], ws_ref[...], preferred_element_type=F32)
    ga = jnp.concatenate([r[...] for r in ga_refs], axis=1)
    gs = jnp.concatenate([r[...] for r in gs_refs], axis=1)
    merged = jax.nn.sigmoid(ga) * a + jax.nn.sigmoid(gs) * s
    o = jnp.dot(merged.astype(BF16), wo_ref[...], preferred_element_type=F32)
    x1 = x_ref[...] + _rms(o, npost_ref[...])
    x1_ref[...] = x1
    h2_ref[...] = _rms(x1, npre_ref[...]).astype(BF16)


def _merge(attn, ssm, proj, cols, x, wa, ws, wo, npost, npre, tm):
    M, D = x.shape
    rows = lambda w: pl.BlockSpec((tm, w), lambda i: (i, 0))
    whole = lambda a: pl.BlockSpec(a.shape, lambda i: (0, 0), pipeline_mode=pl.Buffered(1))

    bw = math.gcd(cols["ga"], cols["gs"], D)
    assert bw % LANES == 0
    n_g = D // bw

    def gate_specs(name):
        c0 = _col_block(cols[name], bw)
        return [pl.BlockSpec((tm, bw), lambda i, c=c0 + k: (i, c)) for k in range(n_g)]

    return pl.pallas_call(
        functools.partial(_merge_kernel, n_g=n_g),
        grid=(M // tm,),
        in_specs=[rows(attn.shape[1]), rows(ssm.shape[1])] + gate_specs("ga") + gate_specs("gs")
        + [rows(D), whole(wa), whole(ws), whole(wo), whole(npost), whole(npre)],
        out_specs=[rows(D), rows(D)],
        out_shape=[jax.ShapeDtypeStruct((M, D), F32), jax.ShapeDtypeStruct((M, D), BF16)],
        compiler_params=_cparams(("arbitrary",)),
        name="merge",
    )(attn, ssm, *([proj] * (2 * n_g)), x, wa, ws, wo, npost, npre)


def _ffn_kernel(h_ref, x_ref, wg_ref, wu_ref, wd_ref, npost_ref, y_ref, *copies):
    f = pl.program_id(1)
    h = h_ref[...]
    wg, wu, wd = (r[...].astype(BF16) for r in (wg_ref, wu_ref, wd_ref))
    for copy_ref, w in zip(copies, (wg, wu, wd)):
        copy_ref[...] = w
    act = _silu(jnp.dot(h, wg, preferred_element_type=F32)) * \
        jnp.dot(h, wu, preferred_element_type=F32)

    @pl.when(f == 0)
    def _():
        y_ref[...] = jnp.zeros(y_ref.shape, F32)

    y_ref[...] += jnp.dot(act.astype(BF16), wd, preferred_element_type=F32)

    @pl.when(f == pl.num_programs(1) - 1)
    def _():
        y_ref[...] = x_ref[...] + _rms(y_ref[...], npost_ref[...])


def _ffn(h2, x1, w_g, w_u, w_d, npost, tm, emit_bf16=False):
    M, D = x1.shape
    d_ff = w_g.shape[1]
    tf = FFN_TF
    assert d_ff % tf == 0
    w_specs = [pl.BlockSpec((D, tf), lambda i, f: (0, f)), pl.BlockSpec((D, tf), lambda i, f: (0, f)),
               pl.BlockSpec((tf, D), lambda i, f: (f, 0))]
    out_specs = [pl.BlockSpec((tm, D), lambda i, f: (i, 0))]
    out_shape = [jax.ShapeDtypeStruct((M, D), F32)]
    if emit_bf16:
        assert M == tm
        out_specs += w_specs
        out_shape += [jax.ShapeDtypeStruct(w.shape, BF16) for w in (w_g, w_u, w_d)]
    outs = pl.pallas_call(
        _ffn_kernel,
        grid=(M // tm, d_ff // tf),
        in_specs=[
            pl.BlockSpec((tm, D), lambda i, f: (i, 0), pipeline_mode=pl.Buffered(1)),
            pl.BlockSpec((tm, D), lambda i, f: (i, 0), pipeline_mode=pl.Buffered(1)),
            *w_specs,
            pl.BlockSpec((1, D), lambda i, f: (0, 0)),
        ],
        out_specs=out_specs,
        out_shape=out_shape,
        compiler_params=_cparams(("arbitrary", "arbitrary")),
        name="ffn",
    )(h2, x1, w_g, w_u, w_d, npost)
    return outs if emit_bf16 else outs[0]


def _pad_lanes(v):
    row = jnp.zeros((1, LANES), F32).at[0, :v.shape[0]].set(v.astype(F32))
    return row, row.reshape(LANES, 1)


def kernel(x_prompt, x_sample, cache_k, cache_v, state_conv, state_ssm, page_table, norm_mix_pre, w_in, conv_w, conv_b, dt_bias, a_log, d_skip, ssm_norm, w_attn_out, w_ssm_out, w_out, norm_mix_post, norm_ffn_pre, w_gate, w_up, w_down, norm_ffn_post):
    depth = w_in.shape[0]
    assert depth == 1, "single trunk layer"
    b_p, seq, d_model = x_prompt.shape
    b_s, dec_seq, _ = x_sample.shape
    assert dec_seq == 1
    n_heads_attn = cache_k.shape[3]
    attn_w = n_heads_attn * cache_k.shape[4]
    assert cache_k.shape[4] == HEAD_DIM and cache_k.shape[2] == PAGE_SIZE
    conv_dim = conv_w.shape[2]
    n_ssm_heads = dt_bias.shape[1]
    d_inner = n_ssm_heads * SSM_HEADDIM
    assert conv_dim == d_inner + 2 * N_SSM_GROUPS * D_STATE and n_ssm_heads <= LANES
    n_pages = page_table.shape[1]
    assert (n_pages * PAGE_SIZE) % MOBA_BLOCK == 0
    n_sel = min(MOBA_TOPK, n_pages * PAGE_SIZE // MOBA_BLOCK)
    assert n_sel == MOBA_TOPK

    l = 0
    src, off = {}, 0
    for name, w in (("q", attn_w), ("k", attn_w), ("v", attn_w), ("z", d_inner), ("xbc", conv_dim),
                    ("dt", n_ssm_heads), ("ga", d_model), ("gs", d_model)):
        src[name] = (off, w)
        off += w
    assert off == w_in.shape[2]
    w_t = jnp.transpose(w_in[l])
    tn_f = PROJ_TN_F32
    cols = {name: src[name][0] for name in ("q", "k", "v", "z", "xbc")}
    segments, tile, off = [(0, 0)], src["dt"][0] // tn_f, src["dt"][0]
    assert off % tn_f == 0 and d_model % tn_f == 0
    for name in ("ga", "gs", "dt"):
        segments.append((tile, src[name][0]))
        cols[name] = off
        tile += -(-src[name][1] // tn_f)
        off = tile * tn_f
    assert n_ssm_heads <= LANES <= tn_f and src["dt"][0] + tn_f <= w_t.shape[0] and off % PROJ_TN == 0
    wa = w_attn_out[l].astype(BF16)
    ws = w_ssm_out[l].astype(BF16)
    wo = w_out[l].astype(BF16)
    g_pre = norm_mix_pre[l].reshape(1, d_model)
    g_post = norm_mix_post[l].reshape(1, d_model)
    g_fpre = norm_ffn_pre[l].reshape(1, d_model)
    g_fpost = norm_ffn_post[l].reshape(1, d_model)
    cw = conv_w[l]
    cb = conv_b[l].reshape(1, conv_dim)
    dtb_r, dtb_c = _pad_lanes(dt_bias[l])
    alog_r, alog_c = _pad_lanes(a_log[l])
    dskip = jnp.repeat(d_skip[l].astype(F32), SSM_HEADDIM).reshape(1, d_inner)
    snorm = ssm_norm[l].reshape(1, d_inner)


    xs = x_sample.reshape(b_s, d_model)
    proj_s, w_t_b = _in_proj(xs, g_pre, w_t, tm=b_s, tn=tn_f, segments=tuple(segments),
                             pad=(tile - 1, n_ssm_heads))
    group = lambda name, w: proj_s[:, cols[name]:cols[name] + w]
    q_s, k_s, v_s = group("q", attn_w), group("k", attn_w), group("v", attn_w)
    z_s, xbc_s, dt_s = group("z", d_inner), group("xbc", conv_dim), group("dt", LANES)
    hd = (b_s, n_heads_attn, 1, HEAD_DIM)
    q4, k4, v4 = q_s.reshape(hd), k_s.reshape(hd), v_s.reshape(hd)
    q_bcast = jnp.broadcast_to(q_s.reshape(b_s, n_heads_attn, HEAD_DIM, 1),
                               (b_s, n_heads_attn, HEAD_DIM, PAGE_SIZE))
    ck = jnp.transpose(cache_k.reshape(cache_k.shape[1:]), (0, 2, 3, 1))
    cv = jnp.transpose(cache_v.reshape(cache_v.shape[1:]), (0, 2, 3, 1))

    m_p = b_p * seq
    xp = x_prompt.reshape(m_p, d_model)
    proj = _in_proj(xp, g_pre, w_t_b, tm=1024, tn=PROJ_TN)
    attn, kt, vt, gates = _moba_prompt(proj, cols, attn_w, b_p, seq, q_bcast, ck, page_table)
    u, conv_p, ssm_p = _ssd_prompt(proj, cols, cw, cb, dtb_r, dtb_c, alog_r, alog_c, dskip, snorm, b_p, seq)

    idx = _moba_topk(gates, n_sel)
    attn_s = _moba_sample(q4, k4, v4, ck, cv, page_table, idx.reshape(b_s, n_sel * n_heads_attn))
    u_s, conv_s, ssm_s = _ssd_step(xbc_s, z_s, dt_s, state_conv[l], state_ssm[l], cw, cb,
                                   dtb_r, alog_r, dskip, snorm)
    x1_s, h2_s = _merge(attn_s.reshape(b_s, attn_w).astype(BF16), u_s, proj_s, cols, xs,
                        wa, ws, wo, g_post, g_fpre, b_s)
    y_s, w_g, w_u, w_d = _ffn(h2_s, x1_s, w_gate[l], w_up[l], w_down[l], g_fpost, b_s, emit_bf16=True)

    x1_p, h2_p = _merge(attn, u, proj, cols, xp, wa, ws, wo, g_post, g_fpre, 256)
    y_p = _ffn(h2_p, x1_p, w_g, w_u, w_d, g_fpost, 1024)

    def kv_prompt(t):
        return jnp.transpose(t.reshape(b_p, n_heads_attn, HEAD_DIM, seq), (0, 3, 1, 2))[None]

    kv_s = (1, b_s, 1, n_heads_attn, HEAD_DIM)
    return (y_p.reshape(b_p, seq, d_model), y_s.reshape(b_s, 1, d_model),
            kv_prompt(kt), kv_prompt(vt), conv_p[None], ssm_p[None],
            k_s.reshape(kv_s), v_s.reshape(kv_s), conv_s[None], ssm_s[None])
```

```python
import functools
import math

import jax
import jax.numpy as jnp
from jax import lax
from jax.experimental import pallas as pl
from jax.experimental.pallas import tpu as pltpu

F32 = jnp.float32
BF16 = jnp.bfloat16

EPS = 1e-6
HEAD_DIM = 64
MOBA_BLOCK = 256
MOBA_TOPK = 3
PAGE_SIZE = 128
SSM_HEADDIM = 64
N_SSM_GROUPS = 8
D_STATE = 128
CONV_WIDTH = 4
SSD_CHUNK = 128

LANES = 128
SUBLANES = 8
VMEM_LIMIT = 56 * 1024 * 1024
PROJ_TN = 1536
PROJ_TN_F32 = 512
FFN_TF = 512
FFN_RESIDUAL_ROWS = 128

NEG_BIG = -(2.0 ** 100)
ROW_GROUP = 16
SCAN_RING = 8

_NT = (((1,), (1,)), ((), ()))
_TN = (((0,), (0,)), ((), ()))


def _cparams(sem):
    return pltpu.CompilerParams(dimension_semantics=sem, vmem_limit_bytes=VMEM_LIMIT)


def _rms(x, g):
    return x * lax.rsqrt(jnp.mean(x * x, axis=-1, keepdims=True) + EPS) * g


def _silu(x):
    return x * jax.nn.sigmoid(x)


def _softplus(x):
    return jnp.maximum(x, 0.0) + jnp.log1p(jnp.exp(-jnp.abs(x)))


def _eye(n):
    return lax.broadcasted_iota(jnp.int32, (n, n), 0) == lax.broadcasted_iota(jnp.int32, (n, n), 1)


def _row_to_col(row):
    n = row.shape[1]
    return jnp.sum(jnp.where(_eye(n), jnp.broadcast_to(row, (n, n)), 0.0), axis=1, keepdims=True)


def _col_to_row(col):
    n = col.shape[0]
    return jnp.sum(jnp.where(_eye(n), jnp.broadcast_to(col, (n, n)), 0.0), axis=0, keepdims=True)


def _in_proj_kernel(x_ref, g_ref, w_ref, o_ref, *rest, pad):
    h_scr = rest[-1]
    j = pl.program_id(1)

    @pl.when(j == 0)
    def _():
        h_scr[...] = _rms(x_ref[...], g_ref[...]).astype(BF16)

    w = w_ref[...].astype(BF16)
    if pad is not None:
        rows = lax.broadcasted_iota(jnp.int32, (w.shape[0], 1), 0)
        w = jnp.where((j == pad[0]) & (rows >= pad[1]), jnp.zeros_like(w), w)
    if len(rest) == 2:
        rest[0][...] = w
    o_ref[...] = lax.dot_general(h_scr[...], w, _NT, preferred_element_type=F32)


def _in_proj(x, g, w_t, tm, tn, segments=None, pad=None):
    M, D = x.shape
    if segments is None:
        assert w_t.shape[0] % tn == 0
        n_tiles = w_t.shape[0] // tn
        w_spec = pl.BlockSpec((tn, D), lambda i, j: (j, 0))
    else:
        n_tiles = pad[0] + 1
        assert M == tm, "one row tile: every weight tile is visited, and copied, once"

        assert all(first_row % SUBLANES == 0 for _, first_row in segments) and tn % SUBLANES == 0

        def row_offset(i, j):
            off = 0
            for first_tile, first_row in segments:
                off = jnp.where(j >= first_tile, first_row + (j - first_tile) * tn, off)
            return pl.multiple_of(off, SUBLANES), 0

        w_spec = pl.BlockSpec((pl.Element(tn), pl.Element(D)), row_offset)
    out_specs = [pl.BlockSpec((tm, tn), lambda i, j: (i, j))]
    out_shape = [jax.ShapeDtypeStruct((M, n_tiles * tn), F32)]
    if segments is not None:
        out_specs.append(pl.BlockSpec((tn, D), lambda i, j: (j, 0)))
        out_shape.append(jax.ShapeDtypeStruct((n_tiles * tn, D), BF16))
    outs = pl.pallas_call(
        functools.partial(_in_proj_kernel, pad=pad),
        grid=(M // tm, n_tiles),
        in_specs=[pl.BlockSpec((tm, D), lambda i, j: (i, 0)),
                  pl.BlockSpec((1, D), lambda i, j: (0, 0)),
                  w_spec],
        out_specs=out_specs,
        out_shape=out_shape,
        scratch_shapes=[pltpu.VMEM((tm, D), BF16)],
        compiler_params=_cparams(("arbitrary", "arbitrary")),
        name="in_proj",
    )(x, g, w_t)
    return outs if segments is not None else outs[0]


def _moba_prompt_kernel(pt_ref, q_ref, k_ref, v_ref, qs_ref, ck_ref, o_ref, kt_ref, vt_ref, g_ref,
                        s_scr, p_scr, o_scr, scan_buf, scan_sem, *, L, scale, n_pages):
    BS = MOBA_BLOCK
    nb = L // BS
    assert nb <= HEAD_DIM
    n_sub = 2 * nb
    ppb = BS // PAGE_SIZE
    ch = n_pages // n_sub
    n_buf = scan_buf.shape[0]
    assert n_pages % n_sub == 0 and ch % ppb == 0 and n_sub % n_buf == 0
    step = pl.program_id(0) * pl.num_programs(1) + pl.program_id(1)
    n_steps = pl.num_programs(0) * pl.num_programs(1)
    _, n_h, d_h, T = qs_ref.shape

    def scan_start(j):
        seq, jj = (step, j) if j < n_sub else (step + 1, j - n_sub)

        def go():
            for t in range(ch):
                pltpu.make_async_copy(ck_ref.at[pt_ref[seq, jj * ch + t]], scan_buf.at[j % n_buf, t],
                                      scan_sem.at[j % n_buf]).start()

        if j < n_sub:
            go()
        else:
            pl.when(step + 1 < n_steps)(go)

    def scan_chunk(j):
        slot = j % n_buf
        scan_start(j + n_buf - 1)
        for t in range(ch):
            pltpu.make_async_copy(ck_ref.at[pt_ref[step, j * ch + t]], scan_buf.at[slot, t],
                                  scan_sem.at[slot]).wait()
        for h in range(n_h):
            qh = qs_ref[0, h]
            for u in range(ch // ppb):
                pages = scan_buf[slot, u * ppb, h]
                for t in range(1, ppb):
                    pages = pages + scan_buf[slot, u * ppb + t, h]
                part = jnp.sum((pages * qh).reshape(d_h // SUBLANES, SUBLANES, T), axis=0)
                n = j * (ch // ppb) + u
                g_ref[0, h:h + 1, n:n + 1] = jnp.sum(part, keepdims=True) * (1.0 / BS)

    @pl.when(step == 0)
    def _():
        for j in range(n_buf - 1):
            scan_start(j)

    q = q_ref[...]
    k = k_ref[...]
    v = v_ref[...]
    kt_ref[0] = k.T
    vt_ref[0] = v.T
    lane = lax.broadcasted_iota(jnp.int32, (1, LANES), 1)
    first = lane < HEAD_DIM

    kmean = jnp.concatenate(
        [jnp.mean(k[n * BS:(n + 1) * BS], axis=0, keepdims=True) for n in range(nb)], axis=0)

    blk = lax.broadcasted_iota(jnp.int32, (nb, L), 0)
    qblk = lax.broadcasted_iota(jnp.int32, (nb, L), 1) // BS
    past = blk < qblk
    row = lax.broadcasted_iota(jnp.int32, (BS, BS), 0)
    col = lax.broadcasted_iota(jnp.int32, (BS, BS), 1)
    causal_bias = jnp.where(col <= row, 0.0, -jnp.inf)
    key_blk = lax.broadcasted_iota(jnp.int32, (L, LANES), 0) // BS
    key_lane = lax.broadcasted_iota(jnp.int32, (L, LANES), 1)
    c = scale * math.log2(math.e)

    def scores(q_aug, k_aug, qb):
        qs = slice(qb * BS, (qb + 1) * BS)
        n_past = qb * BS
        if qb > 0:
            s_scr[qb % 2, :, :n_past] = lax.dot_general(q_aug[qs], k_aug[:n_past], _NT,
                                                        preferred_element_type=F32)
        s_scr[qb % 2, :, n_past:n_past + BS] = lax.dot_general(
            q_aug[qs], k_aug[n_past:n_past + BS], _NT, preferred_element_type=F32) + causal_bias

    def softmax(qb):
        nk = (qb + 1) * BS
        sb = qb % 2

        def rows(r, carry):
            rs = pl.ds(pl.multiple_of(r * ROW_GROUP, ROW_GROUP), ROW_GROUP)
            s = s_scr[sb, rs, :nk]
            p_scr[sb, rs, :nk] = jnp.exp2(s - jnp.max(s, axis=1, keepdims=True)).astype(BF16)
            return carry

        lax.fori_loop(0, BS // ROW_GROUP, rows, 0, unroll=True)

    def weighted_values(head, v_aug, qb):
        qs = slice(qb * BS, (qb + 1) * BS)
        nk = (qb + 1) * BS
        o = jnp.dot(p_scr[qb % 2, :, :nk], v_aug[:nk], preferred_element_type=F32)
        o = o / pltpu.roll(o, HEAD_DIM, axis=1)
        if head == 0:
            o_scr[qs, :] = o
        else:
            o_ref[qs, :] = jnp.where(first, o_scr[qs, :], o).astype(o_ref.dtype)

    for head, (own, off) in enumerate(((first, HEAD_DIM), (~first, 0))):
        gate = lax.dot_general(jnp.where(own, kmean, 0.0), q, _NT,
                               precision=lax.Precision.HIGHEST, preferred_element_type=F32)
        gate = jnp.where(past, gate, -jnp.inf)
        rank = jnp.zeros((nb, L), jnp.int32)
        for m in range(nb):
            gm = gate[m:m + 1, :]
            beats = (gm > gate) | ((gm == gate) & (m < blk))
            rank = rank + beats.astype(jnp.int32)
        allowed = (past & (rank < MOBA_TOPK)) | (blk == qblk)
        bias_t = jnp.where(allowed, 0.0, NEG_BIG)
        pieces = [jnp.zeros((off, L), F32), bias_t, jnp.zeros((LANES - off - nb, L), F32)]
        pad_t = jnp.concatenate([t for t in pieces if t.shape[0]], axis=0)
        q_aug = jnp.where(own, q * c, pad_t.T).astype(BF16)
        k_aug = jnp.where(own, k, (key_lane - off == key_blk).astype(F32)).astype(BF16)
        v_aug = jnp.where(own, v, 1.0).astype(BF16)

        for qb in range(nb + 1):
            if qb < nb:
                scan_chunk(head * nb + qb)
            if qb == 0:
                scores(q_aug, k_aug, 0)
            if qb + 1 < nb:
                scores(q_aug, k_aug, qb + 1)
            if qb < nb:
                softmax(qb)
            if qb > 0:
                weighted_values(head, v_aug, qb - 1)


def _col_block(off, width):
    assert off % width == 0
    return off // width


def _moba_prompt(proj, cols, W, batch, L, q_sample, cache_kt, page_table):
    M = proj.shape[0]
    B, H, d, T = q_sample.shape
    n_pages = page_table.shape[1]
    n_hp = W // LANES
    n_blocks = n_pages * PAGE_SIZE // MOBA_BLOCK
    assert L % MOBA_BLOCK == 0 and W % LANES == 0 and LANES == 2 * HEAD_DIM
    assert batch * n_hp == B, "one sample sequence is scanned per grid step"
    ch = n_pages // (2 * (L // MOBA_BLOCK))

    def in_spec(name):
        c0 = _col_block(cols[name], LANES)
        return pl.BlockSpec((L, LANES), lambda b, hp, pt: (b, c0 + hp))

    seq4 = lambda b, hp, pt: (b * n_hp + hp, 0, 0, 0)
    spec_t = pl.BlockSpec((1, LANES, L), lambda b, hp, pt: (b, hp, 0))
    kv_t = jax.ShapeDtypeStruct((batch, W, L), F32)
    grid_spec = pltpu.PrefetchScalarGridSpec(
        num_scalar_prefetch=1,
        grid=(batch, n_hp),
        in_specs=[in_spec("q"), in_spec("k"), in_spec("v"),
                  pl.BlockSpec((1, H, d, T), seq4), pl.BlockSpec(memory_space=pl.ANY)],
        out_specs=[pl.BlockSpec((L, LANES), lambda b, hp, pt: (b, hp)), spec_t, spec_t,
                   pl.BlockSpec((1, H, n_blocks), lambda b, hp, pt: (b * n_hp + hp, 0, 0))],
        scratch_shapes=[pltpu.VMEM((2, MOBA_BLOCK, L), F32), pltpu.VMEM((2, MOBA_BLOCK, L), BF16),
                        pltpu.VMEM((L, LANES), F32),
                        pltpu.VMEM((SCAN_RING, ch, H, d, T), F32),
                        pltpu.SemaphoreType.DMA((SCAN_RING,))],
    )
    return pl.pallas_call(
        functools.partial(_moba_prompt_kernel, L=L, scale=HEAD_DIM ** -0.5, n_pages=n_pages),
        grid_spec=grid_spec,
        out_shape=[jax.ShapeDtypeStruct((M, W), BF16), kv_t, kv_t,
                   jax.ShapeDtypeStruct((B, H, n_blocks), F32)],
        compiler_params=_cparams(("arbitrary", "arbitrary")),
        name="moba_prompt",
    )(page_table, proj, proj, proj, q_sample, cache_kt)


def _ssd_prompt_kernel(*refs, d_inner, n_x, n_z):
    xbc_refs, z_refs = refs[:n_x], refs[n_x:n_x + n_z]
    (dt_ref, cw_ref, cb_ref, dtb_r_ref, dtb_c_ref, alog_r_ref, alog_c_ref, dskip_ref, norm_ref,
     u_ref, conv_ref, st_ref, xs_scr) = refs[n_x + n_z:]
    bw = xbc_refs[0].shape[1]
    Q = SSD_CHUNK
    P = SSM_HEADDIM
    N = D_STATE
    G = N_SSM_GROUPS
    R = d_inner // P // G
    assert R % 2 == 0 and 2 * P == LANES and N == LANES
    c = pl.program_id(1)
    tail = CONV_WIDTH - 1

    @pl.when(c == 0)
    def _():
        xs_scr[...] = jnp.zeros(xs_scr.shape, F32)
        st_ref[...] = jnp.zeros(st_ref.shape, F32)

    x = jnp.concatenate([r[...] for r in xbc_refs], axis=1)
    prev = xs_scr[...]
    row8 = lax.broadcasted_iota(jnp.int32, (SUBLANES, 1), 0)

    def shifted(k):
        xk = pltpu.roll(x, k, axis=0)
        head = jnp.where(row8 < k, pltpu.roll(prev, k, axis=0), xk[0:SUBLANES])
        return jnp.concatenate([head, xk[SUBLANES:]], axis=0)

    acc = cw_ref[0:1, :] * shifted(tail)
    for i in range(1, tail):
        acc = acc + cw_ref[i:i + 1, :] * shifted(tail - i)
    acc = acc + cw_ref[tail:tail + 1, :] * x
    xc = _silu(cb_ref[...] + acc)
    xs_scr[...] = x[Q - SUBLANES:Q, :]
    conv_ref[0] = x[Q - tail:Q, :]

    raw = dt_ref[...]
    dt = _softplus(raw + dtb_r_ref[...])
    dt_t = _softplus(raw.T + dtb_c_ref[...])
    a = dt * (-jnp.exp(alog_r_ref[...]))
    a_t = dt_t * (-jnp.exp(alog_c_ref[...]))
    ri = lax.broadcasted_iota(jnp.int32, (Q, Q), 0)
    ci = lax.broadcasted_iota(jnp.int32, (Q, Q), 1)
    causal = ci <= ri
    acum = jnp.dot(causal.astype(F32), a, precision=lax.Precision.HIGHEST,
                   preferred_element_type=F32)
    acum_t = jnp.dot(a_t, (ri <= ci).astype(F32), precision=lax.Precision.HIGHEST,
                     preferred_element_type=F32)
    e_acum = jnp.exp(acum)
    d_end = jnp.exp(acum[Q - 1:Q, :] - acum)
    lane = lax.broadcasted_iota(jnp.int32, (1, LANES), 1)
    first = lane < P

    def pair_cols(t, h0):
        return jnp.where(first, t[:, h0:h0 + 1], t[:, h0 + 1:h0 + 2])

    y_parts = []
    for g in range(G):
        b_g = xc[:, d_inner + g * N:d_inner + (g + 1) * N].astype(BF16)
        c_g = xc[:, d_inner + G * N + g * N:d_inner + G * N + (g + 1) * N].astype(BF16)
        cb = lax.dot_general(c_g, b_g, _NT, preferred_element_type=F32)
        st_g = st_ref[0, g * R:(g + 1) * R].reshape(R * P, N)
        y_off = lax.dot_general(c_g, st_g.astype(BF16), _NT, preferred_element_type=F32)
        xdtd_parts, cd_parts = [], []
        for pr in range(R // 2):
            h0 = g * R + 2 * pr
            x_p = xc[:, h0 * P:h0 * P + LANES]
            xdt = x_p * pair_cols(dt, h0)
            xdt_b = xdt.astype(BF16)
            yd = []
            for hh in (h0, h0 + 1):
                seg = acum[:, hh:hh + 1] - acum_t[hh:hh + 1, :]
                dec = jnp.exp(jnp.where(causal, seg, -jnp.inf))
                yd.append(jnp.dot((cb * dec).astype(BF16), xdt_b, preferred_element_type=F32))
                cd_parts.append(jnp.broadcast_to(jnp.exp(acum_t[hh:hh + 1, Q - 1:Q]), (P, N)))
            y_p = (jnp.where(first, yd[0], yd[1])
                   + y_off[:, 2 * pr * P:2 * pr * P + LANES] * pair_cols(e_acum, h0)
                   + dskip_ref[:, h0 * P:h0 * P + LANES] * x_p)
            y_parts.append(y_p)
            xdtd_parts.append((xdt * pair_cols(d_end, h0)).astype(BF16))
        xdtd = jnp.concatenate(xdtd_parts, axis=1)
        s_new = lax.dot_general(xdtd, b_g, _TN, preferred_element_type=F32)
        st_new = jnp.concatenate(cd_parts, axis=0) * st_g + s_new
        st_ref[0, g * R:(g + 1) * R] = st_new.reshape(R, P, N)

    y = jnp.concatenate(y_parts, axis=1)
    u = y * _silu(jnp.concatenate([r[...] for r in z_refs], axis=1))
    gw = d_inner // G
    u_parts = []
    for g in range(G):
        ug = u[:, g * gw:(g + 1) * gw]
        u_parts.append(ug * lax.rsqrt(jnp.mean(ug * ug, axis=-1, keepdims=True) + EPS))
    u_ref[...] = (jnp.concatenate(u_parts, axis=1) * norm_ref[...]).astype(u_ref.dtype)


def _ssd_prompt(proj, cols, conv_w, conv_b, dtb_r, dtb_c, alog_r, alog_c, dskip, norm, batch, L):
    M = proj.shape[0]
    conv_dim = conv_w.shape[1]
    d_inner = norm.shape[1]
    n_heads = d_inner // SSM_HEADDIM
    assert L % SSD_CHUNK == 0
    nc = L // SSD_CHUNK
    Q = SSD_CHUNK
    tail = CONV_WIDTH - 1
    row = lambda b, c: (b * nc + c, 0)
    const = lambda b, c: (0, 0)

    def in_spec(off, width):
        c0 = _col_block(off, width)
        return pl.BlockSpec((Q, width), lambda b, c: (b * nc + c, c0))

    bw = math.gcd(cols["xbc"], cols["z"], conv_dim, d_inner)
    assert bw % LANES == 0
    n_x, n_z = conv_dim // bw, d_inner // bw
    return pl.pallas_call(
        functools.partial(_ssd_prompt_kernel, d_inner=d_inner, n_x=n_x, n_z=n_z),
        grid=(batch, nc),
        in_specs=[in_spec(cols["xbc"] + i * bw, bw) for i in range(n_x)]
        + [in_spec(cols["z"] + i * bw, bw) for i in range(n_z)]
        + [
            in_spec(cols["dt"], LANES),
            pl.BlockSpec((CONV_WIDTH, conv_dim), const),
            pl.BlockSpec((1, conv_dim), const),
            pl.BlockSpec((1, LANES), const),
            pl.BlockSpec((LANES, 1), const),
            pl.BlockSpec((1, LANES), const),
            pl.BlockSpec((LANES, 1), const),
            pl.BlockSpec((1, d_inner), const),
            pl.BlockSpec((1, d_inner), const),
        ],
        out_specs=[
            pl.BlockSpec((Q, d_inner), row),
            pl.BlockSpec((1, tail, conv_dim), lambda b, c: (b, 0, 0)),
            pl.BlockSpec((1, n_heads, SSM_HEADDIM, D_STATE), lambda b, c: (b, 0, 0, 0)),
        ],
        out_shape=[
            jax.ShapeDtypeStruct((M, d_inner), BF16),
            jax.ShapeDtypeStruct((batch, tail, conv_dim), F32),
            jax.ShapeDtypeStruct((batch, n_heads, SSM_HEADDIM, D_STATE), F32),
        ],
        scratch_shapes=[pltpu.VMEM((SUBLANES, conv_dim), F32)],
        compiler_params=_cparams(("arbitrary", "arbitrary")),
        name="ssd_prompt",
    )(*([proj] * (n_x + n_z + 1)), conv_w, conv_b, dtb_r, dtb_c, alog_r, alog_c, dskip, norm)


def _ssd_step_kernel(xbc_ref, z_ref, dt_ref, cprev_ref, sprev_ref, cw_ref, cb_ref, dtb_ref,
                     alog_ref, dskip_ref, norm_ref, u_ref, conv_ref, st_ref, *, d_inner):
    P = SSM_HEADDIM
    N = D_STATE
    G = N_SSM_GROUPS
    n_heads = d_inner // P
    R = n_heads // G
    assert R % 2 == 0 and 2 * P == LANES and N == LANES
    tail = CONV_WIDTH - 1
    x_new = xbc_ref[0]
    prev = cprev_ref[0]
    acc = cw_ref[0:1, :] * prev[0:1, :]
    for i in range(1, tail):
        acc = acc + cw_ref[i:i + 1, :] * prev[i:i + 1, :]
    acc = acc + cw_ref[tail:tail + 1, :] * x_new
    xc = _silu(cb_ref[...] + acc)
    conv_ref[0, 0:tail - 1, :] = prev[1:tail, :]
    conv_ref[0, tail - 1:tail, :] = x_new

    dt = _softplus(dt_ref[0] + dtb_ref[...])
    decay = jnp.exp(dt * (-jnp.exp(alog_ref[...])))
    first_rows = lax.broadcasted_iota(jnp.int32, (LANES, 1), 0) < P
    first = lax.broadcasted_iota(jnp.int32, (1, LANES), 1) < P
    pad_rows = lambda row: jnp.broadcast_to(row, (SUBLANES, row.shape[1])).astype(BF16)

    y_parts = []
    for hp in range(n_heads // 2):
        h0 = 2 * hp
        g = h0 // R
        b_g = xc[:, d_inner + g * N:d_inner + (g + 1) * N]
        c_g = xc[:, d_inner + G * N + g * N:d_inner + G * N + (g + 1) * N]
        x_row = xc[:, h0 * P:h0 * P + LANES]
        dt_row = jnp.where(first, dt[:, h0:h0 + 1], dt[:, h0 + 1:h0 + 2])
        dec_col = jnp.where(first_rows, decay[:, h0:h0 + 1], decay[:, h0 + 1:h0 + 2])
        st = sprev_ref[0, h0:h0 + 2].reshape(2 * P, N)
        upd = lax.dot_general(pad_rows(dt_row * x_row), pad_rows(b_g), _TN,
                              preferred_element_type=F32) * (1.0 / SUBLANES)
        st_new = dec_col * st + upd
        st_ref[0, h0:h0 + 2] = st_new.reshape(2, P, N)
        y_row = lax.dot_general(pad_rows(c_g), st_new.astype(BF16), _NT,
                                preferred_element_type=F32)[0:1]
        y_parts.append(y_row + dskip_ref[:, h0 * P:h0 * P + LANES] * x_row)
    y = jnp.concatenate(y_parts, axis=1)
    u = y * _silu(z_ref[0])
    gw = d_inner // G
    u_parts = []
    for g in range(G):
        ug = u[:, g * gw:(g + 1) * gw]
        u_parts.append(ug * lax.rsqrt(jnp.mean(ug * ug, axis=-1, keepdims=True) + EPS))
    u_ref[0] = (jnp.concatenate(u_parts, axis=1) * norm_ref[...]).astype(u_ref.dtype)


def _ssd_step(xbc, z, dt_raw, conv_prev, ssm_prev, conv_w, conv_b, dtb_r, alog_r, dskip, norm):
    B, conv_dim = xbc.shape
    d_inner = z.shape[1]
    n_heads = d_inner // SSM_HEADDIM
    tail = CONV_WIDTH - 1
    const = lambda b: (0, 0)
    b3 = lambda b: (b, 0, 0)
    b4 = lambda b: (b, 0, 0, 0)
    u, conv_new, ssm_new = pl.pallas_call(
        functools.partial(_ssd_step_kernel, d_inner=d_inner),
        grid=(B,),
        in_specs=[
            pl.BlockSpec((1, 1, conv_dim), b3),
            pl.BlockSpec((1, 1, d_inner), b3),
            pl.BlockSpec((1, 1, LANES), b3),
            pl.BlockSpec((1, tail, conv_dim), b3),
            pl.BlockSpec((1, n_heads, SSM_HEADDIM, D_STATE), b4),
            pl.BlockSpec((CONV_WIDTH, conv_dim), const),
            pl.BlockSpec((1, conv_dim), const),
            pl.BlockSpec((1, LANES), const),
            pl.BlockSpec((1, LANES), const),
            pl.BlockSpec((1, d_inner), const),
            pl.BlockSpec((1, d_inner), const),
        ],
        out_specs=[
            pl.BlockSpec((1, 1, d_inner), b3),
            pl.BlockSpec((1, tail, conv_dim), b3),
            pl.BlockSpec((1, n_heads, SSM_HEADDIM, D_STATE), b4),
        ],
        out_shape=[
            jax.ShapeDtypeStruct((B, 1, d_inner), BF16),
            jax.ShapeDtypeStruct((B, tail, conv_dim), F32),
            jax.ShapeDtypeStruct((B, n_heads, SSM_HEADDIM, D_STATE), F32),
        ],
        compiler_params=_cparams(("arbitrary",)),
        name="ssd_step",
    )(xbc.reshape(B, 1, conv_dim), z.reshape(B, 1, d_inner), dt_raw.reshape(B, 1, LANES),
      conv_prev, ssm_prev, conv_w, conv_b, dtb_r, alog_r, dskip, norm)
    return u.reshape(B, d_inner), conv_new, ssm_new


def _moba_topk_kernel(g_ref, idx_ref, *, n_sel):
    gate = g_ref[...]
    n_blocks = gate.shape[2]
    blk = lax.broadcasted_iota(jnp.int32, gate.shape, 2)
    rank = jnp.zeros(gate.shape, jnp.int32)
    for m in range(n_blocks):
        gm = gate[:, :, m:m + 1]
        beats = (gm > gate) | ((gm == gate) & (m < blk))
        rank = rank + beats.astype(jnp.int32)
    for r in range(n_sel):
        idx_ref[:, r] = jnp.sum(jnp.where(rank == r, blk, 0), axis=2, keepdims=True)


def _moba_topk(gates, n_sel):
    B, H, n_blocks = gates.shape
    return pl.pallas_call(
        functools.partial(_moba_topk_kernel, n_sel=n_sel),
        grid=(1,),
        in_specs=[pl.BlockSpec((B, H, n_blocks), lambda b: (0, 0, 0))],
        out_specs=pl.BlockSpec((B, n_sel, H, 1), lambda b: (0, 0, 0, 0)),
        out_shape=jax.ShapeDtypeStruct((B, n_sel, H, 1), jnp.int32),
        compiler_params=_cparams(("arbitrary",)),
        name="moba_topk",
    )(gates)


def _moba_sample_kernel(pt_ref, idx_ref, q_ref, kn_ref, vn_ref, ck_ref, cv_ref, o_ref,
                        kbuf, vbuf, sem, *, n_sel, n_heads, scale):
    b = pl.program_id(0)
    n_b = pl.num_programs(0)
    ppb = MOBA_BLOCK // PAGE_SIZE

    def copies(bb, slot):
        out = []
        for h in range(n_heads):
            for r in range(n_sel):
                blk = idx_ref[bb, r * n_heads + h]
                for t in range(ppb):
                    page = pt_ref[bb, blk * ppb + t]
                    j = r * ppb + t
                    out.append(pltpu.make_async_copy(ck_ref.at[page, h], kbuf.at[slot, h, j], sem.at[0, slot]))
                    out.append(pltpu.make_async_copy(cv_ref.at[page, h], vbuf.at[slot, h, j], sem.at[1, slot]))
        return out

    slot = b % 2

    @pl.when(b == 0)
    def _():
        for cp in copies(b, slot):
            cp.start()

    @pl.when(b + 1 < n_b)
    def _():
        for cp in copies(b + 1, 1 - slot):
            cp.start()

    for cp in copies(b, slot):
        cp.wait()

    def body(h, carry):
        q_row = q_ref[0, h]
        qc = _row_to_col(q_row)
        kh = kbuf[slot, h]
        vh = vbuf[slot, h]
        s = jnp.sum(kh * qc, axis=1, keepdims=True) * scale
        s_own = jnp.sum(kn_ref[0, h] * q_row, axis=-1, keepdims=True) * scale
        m = jnp.maximum(jnp.max(jnp.max(s, axis=0), axis=-1, keepdims=True), s_own)
        p = jnp.exp(s - m)
        p_own = jnp.exp(s_own - m)
        l = jnp.sum(jnp.sum(p, axis=0), axis=-1, keepdims=True) + p_own
        o_past = jnp.sum(jnp.sum(vh * p, axis=0), axis=-1, keepdims=True)
        o_ref[0, h] = (_col_to_row(o_past) + p_own * vn_ref[0, h]) / l
        return carry

    lax.fori_loop(0, n_heads, body, 0, unroll=True)


def _moba_sample(q4, k4, v4, cache_kt, cache_vt, page_table, idx):
    B, H, _, d = q4.shape
    n_sel = idx.shape[1] // H
    n_slabs = n_sel * (MOBA_BLOCK // PAGE_SIZE)
    new = pl.BlockSpec((1, H, 1, d), lambda b, pt, ix: (b, 0, 0, 0))
    grid_spec = pltpu.PrefetchScalarGridSpec(
        num_scalar_prefetch=2,
        grid=(B,),
        in_specs=[new, new, new, pl.BlockSpec(memory_space=pl.ANY), pl.BlockSpec(memory_space=pl.ANY)],
        out_specs=new,
        scratch_shapes=[pltpu.VMEM((2, H, n_slabs, d, PAGE_SIZE), F32),
                        pltpu.VMEM((2, H, n_slabs, d, PAGE_SIZE), F32),
                        pltpu.SemaphoreType.DMA((2, 2))],
    )
    return pl.pallas_call(
        functools.partial(_moba_sample_kernel, n_sel=n_sel, n_heads=H, scale=d ** -0.5),
        grid_spec=grid_spec,
        out_shape=jax.ShapeDtypeStruct((B, H, 1, d), F32),
        compiler_params=_cparams(("arbitrary",)),
        name="moba_sample",
    )(page_table, idx, q4, k4, v4, cache_kt, cache_vt)


def _merge_kernel(*refs, n_g):
    attn_ref, ssm_ref = refs[:2]
    ga_refs, gs_refs = refs[2:2 + n_g], refs[2 + n_g:2 + 2 * n_g]
    x_ref, wa_ref, ws_ref, wo_ref, npost_ref, npre_ref, x1_ref, h2_ref = refs[2 + 2 * n_g:]
    a = jnp.dot(attn_ref[...], wa_ref[...], preferred_element_type=F32)
    s = jnp.dot(ssm_ref[...], ws_ref[...], preferred_element_type=F32)
    ga = jnp.concatenate([r[...] for r in ga_refs], axis=1)
    gs = jnp.concatenate([r[...] for r in gs_refs], axis=1)
    merged = jax.nn.sigmoid(ga) * a + jax.nn.sigmoid(gs) * s
    o = jnp.dot(merged.astype(BF16), wo_ref[...], preferred_element_type=F32)
    x1 = x_ref[...] + _rms(o, npost_ref[...])
    x1_ref[...] = x1
    h2_ref[...] = _rms(x1, npre_ref[...]).astype(BF16)


def _merge(attn, ssm, proj, cols, x, wa, ws, wo, npost, npre, tm):
    M, D = x.shape
    rows = lambda w: pl.BlockSpec((tm, w), lambda i: (i, 0))
    whole = lambda a: pl.BlockSpec(a.shape, lambda i: (0, 0), pipeline_mode=pl.Buffered(1))

    bw = math.gcd(cols["ga"], cols["gs"], D)
    assert bw % LANES == 0
    n_g = D // bw

    def gate_specs(name):
        c0 = _col_block(cols[name], bw)
        return [pl.BlockSpec((tm, bw), lambda i, c=c0 + k: (i, c)) for k in range(n_g)]

    return pl.pallas_call(
        functools.partial(_merge_kernel, n_g=n_g),
        grid=(M // tm,),
        in_specs=[rows(attn.shape[1]), rows(ssm.shape[1])] + gate_specs("ga") + gate_specs("gs")
        + [rows(D), whole(wa), whole(ws), whole(wo), whole(npost), whole(npre)],
        out_specs=[rows(D), rows(D)],
        out_shape=[jax.ShapeDtypeStruct((M, D), F32), jax.ShapeDtypeStruct((M, D), BF16)],
        compiler_params=_cparams(("arbitrary",)),
        name="merge",
    )(attn, ssm, *([proj] * (2 * n_g)), x, wa, ws, wo, npost, npre)


def _ffn_kernel(h_ref, x_hbm, wg_ref, wu_ref, wd_ref, npost_ref, y_ref, *rest):
    *copies, x_buf, x_sem = rest
    i = pl.program_id(0)
    f = pl.program_id(1)
    last = pl.num_programs(1) - 1
    tm = y_ref.shape[0]
    rc = x_buf.shape[1]

    def x_copy(c):
        return pltpu.make_async_copy(x_hbm.at[pl.ds(i * tm + c * rc, rc), :], x_buf.at[c % 2],
                                     x_sem.at[c % 2])

    @pl.when(f == last)
    def _():
        x_copy(0).start()

    h = h_ref[...]
    wg, wu, wd = (r[...].astype(BF16) for r in (wg_ref, wu_ref, wd_ref))
    for copy_ref, w in zip(copies, (wg, wu, wd)):
        copy_ref[...] = w
    act = _silu(jnp.dot(h, wg, preferred_element_type=F32)) * \
        jnp.dot(h, wu, preferred_element_type=F32)

    @pl.when(f == 0)
    def _():
        y_ref[...] = jnp.zeros(y_ref.shape, F32)

    y_ref[...] += jnp.dot(act.astype(BF16), wd, preferred_element_type=F32)

    @pl.when(f == last)
    def _():
        for c in range(tm // rc):
            if c + 1 < tm // rc:
                x_copy(c + 1).start()
            x_copy(c).wait()
            rows = slice(c * rc, (c + 1) * rc)
            y_ref[rows, :] = x_buf[c % 2] + _rms(y_ref[rows, :], npost_ref[...])


def _ffn(h2, x1, w_g, w_u, w_d, npost, tm, emit_bf16=False):
    M, D = x1.shape
    d_ff = w_g.shape[1]
    tf = FFN_TF
    rc = min(tm, FFN_RESIDUAL_ROWS)
    assert d_ff % tf == 0 and tm % rc == 0
    w_specs = [pl.BlockSpec((D, tf), lambda i, f: (0, f)), pl.BlockSpec((D, tf), lambda i, f: (0, f)),
               pl.BlockSpec((tf, D), lambda i, f: (f, 0))]
    out_specs = [pl.BlockSpec((tm, D), lambda i, f: (i, 0))]
    out_shape = [jax.ShapeDtypeStruct((M, D), F32)]
    if emit_bf16:
        assert M == tm
        out_specs += w_specs
        out_shape += [jax.ShapeDtypeStruct(w.shape, BF16) for w in (w_g, w_u, w_d)]
    outs = pl.pallas_call(
        _ffn_kernel,
        grid=(M // tm, d_ff // tf),
        in_specs=[
            pl.BlockSpec((tm, D), lambda i, f: (i, 0)),
            pl.BlockSpec(memory_space=pl.ANY),
            *w_specs,
            pl.BlockSpec((1, D), lambda i, f: (0, 0)),
        ],
        out_specs=out_specs,
        out_shape=out_shape,
        scratch_shapes=[pltpu.VMEM((2, rc, D), F32), pltpu.SemaphoreType.DMA((2,))],
        compiler_params=_cparams(("arbitrary", "arbitrary")),
        name="ffn",
    )(h2, x1, w_g, w_u, w_d, npost)
    return outs if emit_bf16 else outs[0]


def _pad_lanes(v):
    row = jnp.zeros((1, LANES), F32).at[0, :v.shape[0]].set(v.astype(F32))
    return row, row.reshape(LANES, 1)


def kernel(x_prompt, x_sample, cache_k, cache_v, state_conv, state_ssm, page_table, norm_mix_pre, w_in, conv_w, conv_b, dt_bias, a_log, d_skip, ssm_norm, w_attn_out, w_ssm_out, w_out, norm_mix_post, norm_ffn_pre, w_gate, w_up, w_down, norm_ffn_post):
    depth = w_in.shape[0]
    assert depth == 1, "single trunk layer"
    b_p, seq, d_model = x_prompt.shape
    b_s, dec_seq, _ = x_sample.shape
    assert dec_seq == 1
    n_heads_attn = cache_k.shape[3]
    attn_w = n_heads_attn * cache_k.shape[4]
    assert cache_k.shape[4] == HEAD_DIM and cache_k.shape[2] == PAGE_SIZE
    conv_dim = conv_w.shape[2]
    n_ssm_heads = dt_bias.shape[1]
    d_inner = n_ssm_heads * SSM_HEADDIM
    assert conv_dim == d_inner + 2 * N_SSM_GROUPS * D_STATE and n_ssm_heads <= LANES
    n_pages = page_table.shape[1]
    assert (n_pages * PAGE_SIZE) % MOBA_BLOCK == 0
    n_sel = min(MOBA_TOPK, n_pages * PAGE_SIZE // MOBA_BLOCK)
    assert n_sel == MOBA_TOPK

    l = 0
    src, off = {}, 0
    for name, w in (("q", attn_w), ("k", attn_w), ("v", attn_w), ("z", d_inner), ("xbc", conv_dim),
                    ("dt", n_ssm_heads), ("ga", d_model), ("gs", d_model)):
        src[name] = (off, w)
        off += w
    assert off == w_in.shape[2]
    w_t = jnp.transpose(w_in[l])
    tn_f = PROJ_TN_F32
    cols = {name: src[name][0] for name in ("q", "k", "v", "z", "xbc")}
    segments, tile, off = [(0, 0)], src["dt"][0] // tn_f, src["dt"][0]
    assert off % tn_f == 0 and d_model % tn_f == 0
    for name in ("ga", "gs", "dt"):
        segments.append((tile, src[name][0]))
        cols[name] = off
        tile += -(-src[name][1] // tn_f)
        off = tile * tn_f
    assert n_ssm_heads <= LANES <= tn_f and src["dt"][0] + tn_f <= w_t.shape[0] and off % PROJ_TN == 0
    wa = w_attn_out[l].astype(BF16)
    ws = w_ssm_out[l].astype(BF16)
    wo = w_out[l].astype(BF16)
    g_pre = norm_mix_pre[l].reshape(1, d_model)
    g_post = norm_mix_post[l].reshape(1, d_model)
    g_fpre = norm_ffn_pre[l].reshape(1, d_model)
    g_fpost = norm_ffn_post[l].reshape(1, d_model)
    cw = conv_w[l]
    cb = conv_b[l].reshape(1, conv_dim)
    dtb_r, dtb_c = _pad_lanes(dt_bias[l])
    alog_r, alog_c = _pad_lanes(a_log[l])
    dskip = jnp.repeat(d_skip[l].astype(F32), SSM_HEADDIM).reshape(1, d_inner)
    snorm = ssm_norm[l].reshape(1, d_inner)


    xs = x_sample.reshape(b_s, d_model)
    proj_s, w_t_b = _in_proj(xs, g_pre, w_t, tm=b_s, tn=tn_f, segments=tuple(segments),
                             pad=(tile - 1, n_ssm_heads))
    group = lambda name, w: proj_s[:, cols[name]:cols[name] + w]
    q_s, k_s, v_s = group("q", attn_w), group("k", attn_w), group("v", attn_w)
    z_s, xbc_s, dt_s = group("z", d_inner), group("xbc", conv_dim), group("dt", LANES)
    hd = (b_s, n_heads_attn, 1, HEAD_DIM)
    q4, k4, v4 = q_s.reshape(hd), k_s.reshape(hd), v_s.reshape(hd)
    q_bcast = jnp.broadcast_to(q_s.reshape(b_s, n_heads_attn, HEAD_DIM, 1),
                               (b_s, n_heads_attn, HEAD_DIM, PAGE_SIZE))
    ck = jnp.transpose(cache_k.reshape(cache_k.shape[1:]), (0, 2, 3, 1))
    cv = jnp.transpose(cache_v.reshape(cache_v.shape[1:]), (0, 2, 3, 1))

    m_p = b_p * seq
    xp = x_prompt.reshape(m_p, d_model)
    proj = _in_proj(xp, g_pre, w_t_b, tm=1024, tn=PROJ_TN)
    attn, kt, vt, gates = _moba_prompt(proj, cols, attn_w, b_p, seq, q_bcast, ck, page_table)
    u, conv_p, ssm_p = _ssd_prompt(proj, cols, cw, cb, dtb_r, dtb_c, alog_r, alog_c, dskip, snorm, b_p, seq)

    idx = _moba_topk(gates, n_sel)
    attn_s = _moba_sample(q4, k4, v4, ck, cv, page_table, idx.reshape(b_s, n_sel * n_heads_attn))
    u_s, conv_s, ssm_s = _ssd_step(xbc_s, z_s, dt_s, state_conv[l], state_ssm[l], cw, cb,
                                   dtb_r, alog_r, dskip, snorm)
    x1_s, h2_s = _merge(attn_s.reshape(b_s, attn_w).astype(BF16), u_s, proj_s, cols, xs,
                        wa, ws, wo, g_post, g_fpre, b_s)
    y_s, w_g, w_u, w_d = _ffn(h2_s, x1_s, w_gate[l], w_up[l], w_down[l], g_fpost, b_s, emit_bf16=True)

    x1_p, h2_p = _merge(attn, u, proj, cols, xp, wa, ws, wo, g_post, g_fpre, 256)
    y_p = _ffn(h2_p, x1_p, w_g, w_u, w_d, g_fpost, 1024)

    def kv_prompt(t):
        return jnp.transpose(t.reshape(b_p, n_heads_attn, HEAD_DIM, seq), (0, 3, 1, 2))[None]

    kv_s = (1, b_s, 1, n_heads_attn, HEAD_DIM)
    return (y_p.reshape(b_p, seq, d_model), y_s.reshape(b_s, 1, d_model),
            kv_prompt(kt), kv_prompt(vt), conv_p[None], ssm_p[None],
            k_s.reshape(kv_s), v_s.reshape(kv_s), conv_s[None], ssm_s[None])
```

```python
import functools
import math

import jax
import jax.numpy as jnp
from jax import lax
from jax.experimental import pallas as pl
from jax.experimental.pallas import tpu as pltpu

F32 = jnp.float32
BF16 = jnp.bfloat16

EPS = 1e-6
HEAD_DIM = 64
MOBA_BLOCK = 256
MOBA_TOPK = 3
PAGE_SIZE = 128
SSM_HEADDIM = 64
N_SSM_GROUPS = 8
D_STATE = 128
CONV_WIDTH = 4
SSD_CHUNK = 128

LANES = 128
SUBLANES = 8
VMEM_LIMIT = 56 * 1024 * 1024
PROJ_TM = 1024
MERGE_TM = 256
FFN_TM = 1024
PROJ_TN = 1536
PROJ_TN_F32 = 512
FFN_TF = 512
FFN_RESIDUAL_ROWS = 128

NEG_BIG = -(2.0 ** 100)
ROW_GROUP = 16
SCAN_RING = 8

_NT = (((1,), (1,)), ((), ()))
_TN = (((0,), (0,)), ((), ()))


def _cparams(sem):
    return pltpu.CompilerParams(dimension_semantics=sem, vmem_limit_bytes=VMEM_LIMIT)


def _rms(x, g):
    return x * lax.rsqrt(jnp.mean(x * x, axis=-1, keepdims=True) + EPS) * g


def _silu(x):
    return x * jax.nn.sigmoid(x)


def _softplus(x):
    return jnp.maximum(x, 0.0) + jnp.log1p(jnp.exp(-jnp.abs(x)))


def _eye(n):
    return lax.broadcasted_iota(jnp.int32, (n, n), 0) == lax.broadcasted_iota(jnp.int32, (n, n), 1)


def _row_to_col(row):
    n = row.shape[1]
    return jnp.sum(jnp.where(_eye(n), jnp.broadcast_to(row, (n, n)), 0.0), axis=1, keepdims=True)


def _col_to_row(col):
    n = col.shape[0]
    return jnp.sum(jnp.where(_eye(n), jnp.broadcast_to(col, (n, n)), 0.0), axis=0, keepdims=True)


def _in_proj_kernel(x_ref, g_ref, w_ref, o_ref, *rest, pad):
    h_scr = rest[-1]
    j = pl.program_id(1)

    @pl.when(j == 0)
    def _():
        h_scr[...] = _rms(x_ref[...], g_ref[...]).astype(BF16)

    w = w_ref[...].astype(BF16)
    if pad is not None:
        rows = lax.broadcasted_iota(jnp.int32, (w.shape[0], 1), 0)
        w = jnp.where((j == pad[0]) & (rows >= pad[1]), jnp.zeros_like(w), w)
    if len(rest) == 2:
        rest[0][...] = w
    o_ref[...] = lax.dot_general(h_scr[...], w, _NT, preferred_element_type=F32)


def _in_proj(x, g, w_t, tm, tn, segments=None, pad=None):
    M, D = x.shape
    if segments is None:
        assert w_t.shape[0] % tn == 0
        n_tiles = w_t.shape[0] // tn
        w_spec = pl.BlockSpec((tn, D), lambda i, j: (j, 0))
    else:
        n_tiles = pad[0] + 1
        assert M == tm, "one row tile: every weight tile is visited, and copied, once"

        assert all(first_row % SUBLANES == 0 for _, first_row in segments) and tn % SUBLANES == 0

        def row_offset(i, j):
            off = 0
            for first_tile, first_row in segments:
                off = jnp.where(j >= first_tile, first_row + (j - first_tile) * tn, off)
            return pl.multiple_of(off, SUBLANES), 0

        w_spec = pl.BlockSpec((pl.Element(tn), pl.Element(D)), row_offset)
    out_specs = [pl.BlockSpec((tm, tn), lambda i, j: (i, j))]
    out_shape = [jax.ShapeDtypeStruct((M, n_tiles * tn), F32)]
    if segments is not None:
        out_specs.append(pl.BlockSpec((tn, D), lambda i, j: (j, 0)))
        out_shape.append(jax.ShapeDtypeStruct((n_tiles * tn, D), BF16))
    outs = pl.pallas_call(
        functools.partial(_in_proj_kernel, pad=pad),
        grid=(M // tm, n_tiles),
        in_specs=[pl.BlockSpec((tm, D), lambda i, j: (i, 0)),
                  pl.BlockSpec((1, D), lambda i, j: (0, 0)),
                  w_spec],
        out_specs=out_specs,
        out_shape=out_shape,
        scratch_shapes=[pltpu.VMEM((tm, D), BF16)],
        compiler_params=_cparams(("arbitrary", "arbitrary")),
        name="in_proj",
    )(x, g, w_t)
    return outs if segments is not None else outs[0]


def _moba_prompt_kernel(pt_ref, q_ref, k_ref, v_ref, qs_ref, ck_ref, o_ref, kt_ref, vt_ref, g_ref,
                        s_scr, p_scr, o_scr, scan_buf, scan_sem, *, L, scale, n_pages):
    BS = MOBA_BLOCK
    nb = L // BS
    assert nb <= HEAD_DIM
    n_sub = 2 * nb
    ppb = BS // PAGE_SIZE
    ch = n_pages // n_sub
    n_buf = scan_buf.shape[0]
    assert n_pages % n_sub == 0 and ch % ppb == 0 and n_sub % n_buf == 0
    step = pl.program_id(0) * pl.num_programs(1) + pl.program_id(1)
    n_steps = pl.num_programs(0) * pl.num_programs(1)
    _, n_h, d_h, T = qs_ref.shape

    def scan_start(j):
        seq, jj = (step, j) if j < n_sub else (step + 1, j - n_sub)

        def go():
            for t in range(ch):
                pltpu.make_async_copy(ck_ref.at[pt_ref[seq, jj * ch + t]], scan_buf.at[j % n_buf, t],
                                      scan_sem.at[j % n_buf]).start()

        if j < n_sub:
            go()
        else:
            pl.when(step + 1 < n_steps)(go)

    def scan_chunk(j):
        slot = j % n_buf
        scan_start(j + n_buf - 1)
        for t in range(ch):
            pltpu.make_async_copy(ck_ref.at[pt_ref[step, j * ch + t]], scan_buf.at[slot, t],
                                  scan_sem.at[slot]).wait()
        for h in range(n_h):
            qh = qs_ref[0, h]
            for u in range(ch // ppb):
                pages = scan_buf[slot, u * ppb, h]
                for t in range(1, ppb):
                    pages = pages + scan_buf[slot, u * ppb + t, h]
                part = jnp.sum((pages * qh).reshape(d_h // SUBLANES, SUBLANES, T), axis=0)
                n = j * (ch // ppb) + u
                g_ref[0, h:h + 1, n:n + 1] = jnp.sum(part, keepdims=True) * (1.0 / BS)

    @pl.when(step == 0)
    def _():
        for j in range(n_buf - 1):
            scan_start(j)

    q = q_ref[...]
    k = k_ref[...]
    v = v_ref[...]
    kt_ref[0] = k.T
    vt_ref[0] = v.T
    lane = lax.broadcasted_iota(jnp.int32, (1, LANES), 1)
    first = lane < HEAD_DIM

    kmean = jnp.concatenate(
        [jnp.mean(k[n * BS:(n + 1) * BS], axis=0, keepdims=True) for n in range(nb)], axis=0)

    blk = lax.broadcasted_iota(jnp.int32, (nb, L), 0)
    qblk = lax.broadcasted_iota(jnp.int32, (nb, L), 1) // BS
    past = blk < qblk
    row = lax.broadcasted_iota(jnp.int32, (BS, BS), 0)
    col = lax.broadcasted_iota(jnp.int32, (BS, BS), 1)
    causal_bias = jnp.where(col <= row, 0.0, -jnp.inf)
    key_blk = lax.broadcasted_iota(jnp.int32, (L, LANES), 0) // BS
    key_lane = lax.broadcasted_iota(jnp.int32, (L, LANES), 1)
    c = scale * math.log2(math.e)

    def scores(q_aug, k_aug, qb):
        qs = slice(qb * BS, (qb + 1) * BS)
        n_past = qb * BS
        if qb > 0:
            s_scr[qb % 2, :, :n_past] = lax.dot_general(q_aug[qs], k_aug[:n_past], _NT,
                                                        preferred_element_type=F32)
        s_scr[qb % 2, :, n_past:n_past + BS] = lax.dot_general(
            q_aug[qs], k_aug[n_past:n_past + BS], _NT, preferred_element_type=F32) + causal_bias

    def softmax(qb):
        nk = (qb + 1) * BS
        sb = qb % 2

        def rows(r, carry):
            rs = pl.ds(pl.multiple_of(r * ROW_GROUP, ROW_GROUP), ROW_GROUP)
            s = s_scr[sb, rs, :nk]
            p_scr[sb, rs, :nk] = jnp.exp2(s - jnp.max(s, axis=1, keepdims=True)).astype(BF16)
            return carry

        lax.fori_loop(0, BS // ROW_GROUP, rows, 0, unroll=True)

    def weighted_values(head, v_aug, qb):
        qs = slice(qb * BS, (qb + 1) * BS)
        nk = (qb + 1) * BS
        o = jnp.dot(p_scr[qb % 2, :, :nk], v_aug[:nk], preferred_element_type=F32)
        o = o / pltpu.roll(o, HEAD_DIM, axis=1)
        if head == 0:
            o_scr[qs, :] = o
        else:
            o_ref[qs, :] = jnp.where(first, o_scr[qs, :], o).astype(o_ref.dtype)

    for head, (own, off) in enumerate(((first, HEAD_DIM), (~first, 0))):
        gate = lax.dot_general(jnp.where(own, kmean, 0.0), q, _NT,
                               precision=lax.Precision.HIGHEST, preferred_element_type=F32)
        gate = jnp.where(past, gate, -jnp.inf)
        rank = jnp.zeros((nb, L), jnp.int32)
        for m in range(nb):
            gm = gate[m:m + 1, :]
            beats = (gm > gate) | ((gm == gate) & (m < blk))
            rank = rank + beats.astype(jnp.int32)
        allowed = (past & (rank < MOBA_TOPK)) | (blk == qblk)
        bias_t = jnp.where(allowed, 0.0, NEG_BIG)
        pieces = [jnp.zeros((off, L), F32), bias_t, jnp.zeros((LANES - off - nb, L), F32)]
        pad_t = jnp.concatenate([t for t in pieces if t.shape[0]], axis=0)
        q_aug = jnp.where(own, q * c, pad_t.T).astype(BF16)
        k_aug = jnp.where(own, k, (key_lane - off == key_blk).astype(F32)).astype(BF16)
        v_aug = jnp.where(own, v, 1.0).astype(BF16)

        for qb in range(nb + 1):
            if qb < nb:
                scan_chunk(head * nb + qb)
            if qb == 0:
                scores(q_aug, k_aug, 0)
            if qb + 1 < nb:
                scores(q_aug, k_aug, qb + 1)
            if qb < nb:
                softmax(qb)
            if qb > 0:
                weighted_values(head, v_aug, qb - 1)


def _col_block(off, width):
    assert off % width == 0
    return off // width


def _moba_prompt(proj, cols, W, batch, L, q_sample, cache_kt, page_table):
    M = proj.shape[0]
    B, H, d, T = q_sample.shape
    n_pages = page_table.shape[1]
    n_hp = W // LANES
    n_blocks = n_pages * PAGE_SIZE // MOBA_BLOCK
    assert L % MOBA_BLOCK == 0 and W % LANES == 0 and LANES == 2 * HEAD_DIM
    assert batch * n_hp == B, "one sample sequence is scanned per grid step"
    ch = n_pages // (2 * (L // MOBA_BLOCK))

    def in_spec(name):
        c0 = _col_block(cols[name], LANES)
        return pl.BlockSpec((L, LANES), lambda b, hp, pt: (b, c0 + hp))

    seq4 = lambda b, hp, pt: (b * n_hp + hp, 0, 0, 0)
    spec_t = pl.BlockSpec((1, LANES, L), lambda b, hp, pt: (b, hp, 0))
    kv_t = jax.ShapeDtypeStruct((batch, W, L), F32)
    grid_spec = pltpu.PrefetchScalarGridSpec(
        num_scalar_prefetch=1,
        grid=(batch, n_hp),
        in_specs=[in_spec("q"), in_spec("k"), in_spec("v"),
                  pl.BlockSpec((1, H, d, T), seq4), pl.BlockSpec(memory_space=pl.ANY)],
        out_specs=[pl.BlockSpec((L, LANES), lambda b, hp, pt: (b, hp)), spec_t, spec_t,
                   pl.BlockSpec((1, H, n_blocks), lambda b, hp, pt: (b * n_hp + hp, 0, 0))],
        scratch_shapes=[pltpu.VMEM((2, MOBA_BLOCK, L), F32), pltpu.VMEM((2, MOBA_BLOCK, L), BF16),
                        pltpu.VMEM((L, LANES), F32),
                        pltpu.VMEM((SCAN_RING, ch, H, d, T), F32),
                        pltpu.SemaphoreType.DMA((SCAN_RING,))],
    )
    return pl.pallas_call(
        functools.partial(_moba_prompt_kernel, L=L, scale=HEAD_DIM ** -0.5, n_pages=n_pages),
        grid_spec=grid_spec,
        out_shape=[jax.ShapeDtypeStruct((M, W), BF16), kv_t, kv_t,
                   jax.ShapeDtypeStruct((B, H, n_blocks), F32)],
        compiler_params=_cparams(("arbitrary", "arbitrary")),
        name="moba_prompt",
    )(page_table, proj, proj, proj, q_sample, cache_kt)


def _ssd_prompt_kernel(*refs, d_inner, n_x, n_z):
    xbc_refs, z_refs = refs[:n_x], refs[n_x:n_x + n_z]
    (dt_ref, cw_ref, cb_ref, dtb_r_ref, dtb_c_ref, alog_r_ref, alog_c_ref, dskip_ref, norm_ref,
     u_ref, conv_ref, st_ref, xs_scr) = refs[n_x + n_z:]
    bw = xbc_refs[0].shape[1]
    Q = SSD_CHUNK
    P = SSM_HEADDIM
    N = D_STATE
    G = N_SSM_GROUPS
    R = d_inner // P // G
    assert R % 2 == 0 and 2 * P == LANES and N == LANES
    c = pl.program_id(1)
    tail = CONV_WIDTH - 1

    @pl.when(c == 0)
    def _():
        xs_scr[...] = jnp.zeros(xs_scr.shape, F32)
        st_ref[...] = jnp.zeros(st_ref.shape, F32)

    x = jnp.concatenate([r[...] for r in xbc_refs], axis=1)
    prev = xs_scr[...]
    row8 = lax.broadcasted_iota(jnp.int32, (SUBLANES, 1), 0)

    def shifted(k):
        xk = pltpu.roll(x, k, axis=0)
        head = jnp.where(row8 < k, pltpu.roll(prev, k, axis=0), xk[0:SUBLANES])
        return jnp.concatenate([head, xk[SUBLANES:]], axis=0)

    acc = cw_ref[0:1, :] * shifted(tail)
    for i in range(1, tail):
        acc = acc + cw_ref[i:i + 1, :] * shifted(tail - i)
    acc = acc + cw_ref[tail:tail + 1, :] * x
    xc = _silu(cb_ref[...] + acc)
    xs_scr[...] = x[Q - SUBLANES:Q, :]
    conv_ref[0] = x[Q - tail:Q, :]

    raw = dt_ref[...]
    dt = _softplus(raw + dtb_r_ref[...])
    dt_t = _softplus(raw.T + dtb_c_ref[...])
    a = dt * (-jnp.exp(alog_r_ref[...]))
    a_t = dt_t * (-jnp.exp(alog_c_ref[...]))
    ri = lax.broadcasted_iota(jnp.int32, (Q, Q), 0)
    ci = lax.broadcasted_iota(jnp.int32, (Q, Q), 1)
    causal = ci <= ri
    acum = jnp.dot(causal.astype(F32), a, precision=lax.Precision.HIGHEST,
                   preferred_element_type=F32)
    acum_t = jnp.dot(a_t, (ri <= ci).astype(F32), precision=lax.Precision.HIGHEST,
                     preferred_element_type=F32)
    e_acum = jnp.exp(acum)
    d_end = jnp.exp(acum[Q - 1:Q, :] - acum)
    lane = lax.broadcasted_iota(jnp.int32, (1, LANES), 1)
    first = lane < P

    def pair_cols(t, h0):
        return jnp.where(first, t[:, h0:h0 + 1], t[:, h0 + 1:h0 + 2])

    y_parts = []
    for g in range(G):
        b_g = xc[:, d_inner + g * N:d_inner + (g + 1) * N].astype(BF16)
        c_g = xc[:, d_inner + G * N + g * N:d_inner + G * N + (g + 1) * N].astype(BF16)
        cb = lax.dot_general(c_g, b_g, _NT, preferred_element_type=F32)
        st_g = st_ref[0, g * R:(g + 1) * R].reshape(R * P, N)
        y_off = lax.dot_general(c_g, st_g.astype(BF16), _NT, preferred_element_type=F32)
        xdtd_parts, cd_parts = [], []
        for pr in range(R // 2):
            h0 = g * R + 2 * pr
            x_p = xc[:, h0 * P:h0 * P + LANES]
            xdt = x_p * pair_cols(dt, h0)
            xdt_b = xdt.astype(BF16)
            yd = []
            for hh in (h0, h0 + 1):
                seg = acum[:, hh:hh + 1] - acum_t[hh:hh + 1, :]
                dec = jnp.exp(jnp.where(causal, seg, -jnp.inf))
                yd.append(jnp.dot((cb * dec).astype(BF16), xdt_b, preferred_element_type=F32))
                cd_parts.append(jnp.broadcast_to(jnp.exp(acum_t[hh:hh + 1, Q - 1:Q]), (P, N)))
            y_p = (jnp.where(first, yd[0], yd[1])
                   + y_off[:, 2 * pr * P:2 * pr * P + LANES] * pair_cols(e_acum, h0)
                   + dskip_ref[:, h0 * P:h0 * P + LANES] * x_p)
            y_parts.append(y_p)
            xdtd_parts.append((xdt * pair_cols(d_end, h0)).astype(BF16))
        xdtd = jnp.concatenate(xdtd_parts, axis=1)
        s_new = lax.dot_general(xdtd, b_g, _TN, preferred_element_type=F32)
        st_new = jnp.concatenate(cd_parts, axis=0) * st_g + s_new
        st_ref[0, g * R:(g + 1) * R] = st_new.reshape(R, P, N)

    y = jnp.concatenate(y_parts, axis=1)
    u = y * _silu(jnp.concatenate([r[...] for r in z_refs], axis=1))
    gw = d_inner // G
    u_parts = []
    for g in range(G):
        ug = u[:, g * gw:(g + 1) * gw]
        u_parts.append(ug * lax.rsqrt(jnp.mean(ug * ug, axis=-1, keepdims=True) + EPS))
    u_ref[...] = (jnp.concatenate(u_parts, axis=1) * norm_ref[...]).astype(u_ref.dtype)


def _ssd_prompt(proj, cols, conv_w, conv_b, dtb_r, dtb_c, alog_r, alog_c, dskip, norm, batch, L):
    M = proj.shape[0]
    conv_dim = conv_w.shape[1]
    d_inner = norm.shape[1]
    n_heads = d_inner // SSM_HEADDIM
    assert L % SSD_CHUNK == 0
    nc = L // SSD_CHUNK
    Q = SSD_CHUNK
    tail = CONV_WIDTH - 1
    row = lambda b, c: (b * nc + c, 0)
    const = lambda b, c: (0, 0)

    def in_spec(off, width):
        c0 = _col_block(off, width)
        return pl.BlockSpec((Q, width), lambda b, c: (b * nc + c, c0))

    bw = math.gcd(cols["xbc"], cols["z"], conv_dim, d_inner)
    assert bw % LANES == 0
    n_x, n_z = conv_dim // bw, d_inner // bw
    return pl.pallas_call(
        functools.partial(_ssd_prompt_kernel, d_inner=d_inner, n_x=n_x, n_z=n_z),
        grid=(batch, nc),
        in_specs=[in_spec(cols["xbc"] + i * bw, bw) for i in range(n_x)]
        + [in_spec(cols["z"] + i * bw, bw) for i in range(n_z)]
        + [
            in_spec(cols["dt"], LANES),
            pl.BlockSpec((CONV_WIDTH, conv_dim), const),
            pl.BlockSpec((1, conv_dim), const),
            pl.BlockSpec((1, LANES), const),
            pl.BlockSpec((LANES, 1), const),
            pl.BlockSpec((1, LANES), const),
            pl.BlockSpec((LANES, 1), const),
            pl.BlockSpec((1, d_inner), const),
            pl.BlockSpec((1, d_inner), const),
        ],
        out_specs=[
            pl.BlockSpec((Q, d_inner), row),
            pl.BlockSpec((1, tail, conv_dim), lambda b, c: (b, 0, 0)),
            pl.BlockSpec((1, n_heads, SSM_HEADDIM, D_STATE), lambda b, c: (b, 0, 0, 0)),
        ],
        out_shape=[
            jax.ShapeDtypeStruct((M, d_inner), BF16),
            jax.ShapeDtypeStruct((batch, tail, conv_dim), F32),
            jax.ShapeDtypeStruct((batch, n_heads, SSM_HEADDIM, D_STATE), F32),
        ],
        scratch_shapes=[pltpu.VMEM((SUBLANES, conv_dim), F32)],
        compiler_params=_cparams(("arbitrary", "arbitrary")),
        name="ssd_prompt",
    )(*([proj] * (n_x + n_z + 1)), conv_w, conv_b, dtb_r, dtb_c, alog_r, alog_c, dskip, norm)


def _ssd_step_kernel(xbc_ref, z_ref, dt_ref, cprev_ref, sprev_ref, cw_ref, cb_ref, dtb_ref,
                     alog_ref, dskip_ref, norm_ref, u_ref, conv_ref, st_ref, *, d_inner):
    P = SSM_HEADDIM
    N = D_STATE
    G = N_SSM_GROUPS
    n_heads = d_inner // P
    R = n_heads // G
    assert R % 2 == 0 and 2 * P == LANES and N == LANES
    tail = CONV_WIDTH - 1
    x_new = xbc_ref[0]
    prev = cprev_ref[0]
    acc = cw_ref[0:1, :] * prev[0:1, :]
    for i in range(1, tail):
        acc = acc + cw_ref[i:i + 1, :] * prev[i:i + 1, :]
    acc = acc + cw_ref[tail:tail + 1, :] * x_new
    xc = _silu(cb_ref[...] + acc)
    conv_ref[0, 0:tail - 1, :] = prev[1:tail, :]
    conv_ref[0, tail - 1:tail, :] = x_new

    dt = _softplus(dt_ref[0] + dtb_ref[...])
    decay = jnp.exp(dt * (-jnp.exp(alog_ref[...])))
    first_rows = lax.broadcasted_iota(jnp.int32, (LANES, 1), 0) < P
    first = lax.broadcasted_iota(jnp.int32, (1, LANES), 1) < P
    pad_rows = lambda row: jnp.broadcast_to(row, (SUBLANES, row.shape[1])).astype(BF16)

    y_parts = []
    for hp in range(n_heads // 2):
        h0 = 2 * hp
        g = h0 // R
        b_g = xc[:, d_inner + g * N:d_inner + (g + 1) * N]
        c_g = xc[:, d_inner + G * N + g * N:d_inner + G * N + (g + 1) * N]
        x_row = xc[:, h0 * P:h0 * P + LANES]
        dt_row = jnp.where(first, dt[:, h0:h0 + 1], dt[:, h0 + 1:h0 + 2])
        dec_col = jnp.where(first_rows, decay[:, h0:h0 + 1], decay[:, h0 + 1:h0 + 2])
        st = sprev_ref[0, h0:h0 + 2].reshape(2 * P, N)
        upd = lax.dot_general(pad_rows(dt_row * x_row), pad_rows(b_g), _TN,
                              preferred_element_type=F32) * (1.0 / SUBLANES)
        st_new = dec_col * st + upd
        st_ref[0, h0:h0 + 2] = st_new.reshape(2, P, N)
        y_row = lax.dot_general(pad_rows(c_g), st_new.astype(BF16), _NT,
                                preferred_element_type=F32)[0:1]
        y_parts.append(y_row + dskip_ref[:, h0 * P:h0 * P + LANES] * x_row)
    y = jnp.concatenate(y_parts, axis=1)
    u = y * _silu(z_ref[0])
    gw = d_inner // G
    u_parts = []
    for g in range(G):
        ug = u[:, g * gw:(g + 1) * gw]
        u_parts.append(ug * lax.rsqrt(jnp.mean(ug * ug, axis=-1, keepdims=True) + EPS))
    u_ref[0] = (jnp.concatenate(u_parts, axis=1) * norm_ref[...]).astype(u_ref.dtype)


def _ssd_step(xbc, z, dt_raw, conv_prev, ssm_prev, conv_w, conv_b, dtb_r, alog_r, dskip, norm):
    B, conv_dim = xbc.shape
    d_inner = z.shape[1]
    n_heads = d_inner // SSM_HEADDIM
    tail = CONV_WIDTH - 1
    const = lambda b: (0, 0)
    b3 = lambda b: (b, 0, 0)
    b4 = lambda b: (b, 0, 0, 0)
    u, conv_new, ssm_new = pl.pallas_call(
        functools.partial(_ssd_step_kernel, d_inner=d_inner),
        grid=(B,),
        in_specs=[
            pl.BlockSpec((1, 1, conv_dim), b3),
            pl.BlockSpec((1, 1, d_inner), b3),
            pl.BlockSpec((1, 1, LANES), b3),
            pl.BlockSpec((1, tail, conv_dim), b3),
            pl.BlockSpec((1, n_heads, SSM_HEADDIM, D_STATE), b4),
            pl.BlockSpec((CONV_WIDTH, conv_dim), const),
            pl.BlockSpec((1, conv_dim), const),
            pl.BlockSpec((1, LANES), const),
            pl.BlockSpec((1, LANES), const),
            pl.BlockSpec((1, d_inner), const),
            pl.BlockSpec((1, d_inner), const),
        ],
        out_specs=[
            pl.BlockSpec((1, 1, d_inner), b3),
            pl.BlockSpec((1, tail, conv_dim), b3),
            pl.BlockSpec((1, n_heads, SSM_HEADDIM, D_STATE), b4),
        ],
        out_shape=[
            jax.ShapeDtypeStruct((B, 1, d_inner), BF16),
            jax.ShapeDtypeStruct((B, tail, conv_dim), F32),
            jax.ShapeDtypeStruct((B, n_heads, SSM_HEADDIM, D_STATE), F32),
        ],
        compiler_params=_cparams(("arbitrary",)),
        name="ssd_step",
    )(xbc.reshape(B, 1, conv_dim), z.reshape(B, 1, d_inner), dt_raw.reshape(B, 1, LANES),
      conv_prev, ssm_prev, conv_w, conv_b, dtb_r, alog_r, dskip, norm)
    return u.reshape(B, d_inner), conv_new, ssm_new


def _moba_topk_kernel(g_ref, idx_ref, *, n_sel):
    gate = g_ref[...]
    n_blocks = gate.shape[2]
    blk = lax.broadcasted_iota(jnp.int32, gate.shape, 2)
    rank = jnp.zeros(gate.shape, jnp.int32)
    for m in range(n_blocks):
        gm = gate[:, :, m:m + 1]
        beats = (gm > gate) | ((gm == gate) & (m < blk))
        rank = rank + beats.astype(jnp.int32)
    for r in range(n_sel):
        idx_ref[:, r] = jnp.sum(jnp.where(rank == r, blk, 0), axis=2, keepdims=True)


def _moba_topk(gates, n_sel):
    B, H, n_blocks = gates.shape
    return pl.pallas_call(
        functools.partial(_moba_topk_kernel, n_sel=n_sel),
        grid=(1,),
        in_specs=[pl.BlockSpec((B, H, n_blocks), lambda b: (0, 0, 0))],
        out_specs=pl.BlockSpec((B, n_sel, H, 1), lambda b: (0, 0, 0, 0)),
        out_shape=jax.ShapeDtypeStruct((B, n_sel, H, 1), jnp.int32),
        compiler_params=_cparams(("arbitrary",)),
        name="moba_topk",
    )(gates)


def _moba_sample_kernel(pt_ref, idx_ref, q_ref, kn_ref, vn_ref, ck_ref, cv_ref, o_ref,
                        kbuf, vbuf, sem, *, n_sel, n_heads, scale):
    b = pl.program_id(0)
    n_b = pl.num_programs(0)
    ppb = MOBA_BLOCK // PAGE_SIZE

    def copies(bb, slot):
        out = []
        for h in range(n_heads):
            for r in range(n_sel):
                blk = idx_ref[bb, r * n_heads + h]
                for t in range(ppb):
                    page = pt_ref[bb, blk * ppb + t]
                    j = r * ppb + t
                    out.append(pltpu.make_async_copy(ck_ref.at[page, h], kbuf.at[slot, h, j], sem.at[0, slot]))
                    out.append(pltpu.make_async_copy(cv_ref.at[page, h], vbuf.at[slot, h, j], sem.at[1, slot]))
        return out

    slot = b % 2

    @pl.when(b == 0)
    def _():
        for cp in copies(b, slot):
            cp.start()

    @pl.when(b + 1 < n_b)
    def _():
        for cp in copies(b + 1, 1 - slot):
            cp.start()

    for cp in copies(b, slot):
        cp.wait()

    def body(h, carry):
        q_row = q_ref[0, h]
        qc = _row_to_col(q_row)
        kh = kbuf[slot, h]
        vh = vbuf[slot, h]
        s = jnp.sum(kh * qc, axis=1, keepdims=True) * scale
        s_own = jnp.sum(kn_ref[0, h] * q_row, axis=-1, keepdims=True) * scale
        m = jnp.maximum(jnp.max(jnp.max(s, axis=0), axis=-1, keepdims=True), s_own)
        p = jnp.exp(s - m)
        p_own = jnp.exp(s_own - m)
        l = jnp.sum(jnp.sum(p, axis=0), axis=-1, keepdims=True) + p_own
        o_past = jnp.sum(jnp.sum(vh * p, axis=0), axis=-1, keepdims=True)
        o_ref[0, h] = (_col_to_row(o_past) + p_own * vn_ref[0, h]) / l
        return carry

    lax.fori_loop(0, n_heads, body, 0, unroll=True)


def _moba_sample(q4, k4, v4, cache_kt, cache_vt, page_table, idx):
    B, H, _, d = q4.shape
    n_sel = idx.shape[1] // H
    n_slabs = n_sel * (MOBA_BLOCK // PAGE_SIZE)
    new = pl.BlockSpec((1, H, 1, d), lambda b, pt, ix: (b, 0, 0, 0))
    grid_spec = pltpu.PrefetchScalarGridSpec(
        num_scalar_prefetch=2,
        grid=(B,),
        in_specs=[new, new, new, pl.BlockSpec(memory_space=pl.ANY), pl.BlockSpec(memory_space=pl.ANY)],
        out_specs=new,
        scratch_shapes=[pltpu.VMEM((2, H, n_slabs, d, PAGE_SIZE), F32),
                        pltpu.VMEM((2, H, n_slabs, d, PAGE_SIZE), F32),
                        pltpu.SemaphoreType.DMA((2, 2))],
    )
    return pl.pallas_call(
        functools.partial(_moba_sample_kernel, n_sel=n_sel, n_heads=H, scale=d ** -0.5),
        grid_spec=grid_spec,
        out_shape=jax.ShapeDtypeStruct((B, H, 1, d), F32),
        compiler_params=_cparams(("arbitrary",)),
        name="moba_sample",
    )(page_table, idx, q4, k4, v4, cache_kt, cache_vt)


def _merge_kernel(*refs, n_g):
    attn_ref, ssm_ref = refs[:2]
    ga_refs, gs_refs = refs[2:2 + n_g], refs[2 + n_g:2 + 2 * n_g]
    x_ref, wa_ref, ws_ref, wo_ref, npost_ref, npre_ref, x1_ref, h2_ref = refs[2 + 2 * n_g:]
    a = jnp.dot(attn_ref[...], wa_ref[...], preferred_element_type=F32)
    s = jnp.dot(ssm_ref[...], ws_ref[...], preferred_element_type=F32)
    ga = jnp.concatenate([r[...] for r in ga_refs], axis=1)
    gs = jnp.concatenate([r[...] for r in gs_refs], axis=1)
    merged = jax.nn.sigmoid(ga) * a + jax.nn.sigmoid(gs) * s
    o = jnp.dot(merged.astype(BF16), wo_ref[...], preferred_element_type=F32)
    x1 = x_ref[...] + _rms(o, npost_ref[...])
    x1_ref[...] = x1
    h2_ref[...] = _rms(x1, npre_ref[...]).astype(BF16)


def _merge(attn, ssm, proj, cols, x, wa, ws, wo, npost, npre, tm):
    M, D = x.shape
    rows = lambda w: pl.BlockSpec((tm, w), lambda i: (i, 0))
    whole = lambda a: pl.BlockSpec(a.shape, lambda i: (0, 0), pipeline_mode=pl.Buffered(1))

    bw = math.gcd(cols["ga"], cols["gs"], D)
    assert bw % LANES == 0
    n_g = D // bw

    def gate_specs(name):
        c0 = _col_block(cols[name], bw)
        return [pl.BlockSpec((tm, bw), lambda i, c=c0 + k: (i, c)) for k in range(n_g)]

    return pl.pallas_call(
        functools.partial(_merge_kernel, n_g=n_g),
        grid=(M // tm,),
        in_specs=[rows(attn.shape[1]), rows(ssm.shape[1])] + gate_specs("ga") + gate_specs("gs")
        + [rows(D), whole(wa), whole(ws), whole(wo), whole(npost), whole(npre)],
        out_specs=[rows(D), rows(D)],
        out_shape=[jax.ShapeDtypeStruct((M, D), F32), jax.ShapeDtypeStruct((M, D), BF16)],
        compiler_params=_cparams(("arbitrary",)),
        name="merge",
    )(attn, ssm, *([proj] * (2 * n_g)), x, wa, ws, wo, npost, npre)


def _ffn_kernel(h_ref, x_hbm, wg_ref, wu_ref, wd_ref, npost_ref, y_ref, *rest):
    *copies, x_buf, x_sem = rest
    i = pl.program_id(0)
    f = pl.program_id(1)
    last = pl.num_programs(1) - 1
    tm = y_ref.shape[0]
    rc = x_buf.shape[1]

    def x_copy(c):
        return pltpu.make_async_copy(x_hbm.at[pl.ds(i * tm + c * rc, rc), :], x_buf.at[c % 2],
                                     x_sem.at[c % 2])

    @pl.when(f == last)
    def _():
        x_copy(0).start()

    h = h_ref[...]
    wg, wu, wd = (r[...].astype(BF16) for r in (wg_ref, wu_ref, wd_ref))
    for copy_ref, w in zip(copies, (wg, wu, wd)):
        copy_ref[...] = w
    act = _silu(jnp.dot(h, wg, preferred_element_type=F32)) * \
        jnp.dot(h, wu, preferred_element_type=F32)

    @pl.when(f == 0)
    def _():
        y_ref[...] = jnp.zeros(y_ref.shape, F32)

    y_ref[...] += jnp.dot(act.astype(BF16), wd, preferred_element_type=F32)

    @pl.when(f == last)
    def _():
        for c in range(tm // rc):
            if c + 1 < tm // rc:
                x_copy(c + 1).start()
            x_copy(c).wait()
            rows = slice(c * rc, (c + 1) * rc)
            y_ref[rows, :] = x_buf[c % 2] + _rms(y_ref[rows, :], npost_ref[...])


def _ffn(h2, x1, w_g, w_u, w_d, npost, tm, emit_bf16=False):
    M, D = x1.shape
    d_ff = w_g.shape[1]
    tf = FFN_TF
    rc = min(tm, FFN_RESIDUAL_ROWS)
    assert d_ff % tf == 0 and tm % rc == 0
    w_specs = [pl.BlockSpec((D, tf), lambda i, f: (0, f)), pl.BlockSpec((D, tf), lambda i, f: (0, f)),
               pl.BlockSpec((tf, D), lambda i, f: (f, 0))]
    out_specs = [pl.BlockSpec((tm, D), lambda i, f: (i, 0))]
    out_shape = [jax.ShapeDtypeStruct((M, D), F32)]
    if emit_bf16:
        assert M == tm
        out_specs += w_specs
        out_shape += [jax.ShapeDtypeStruct(w.shape, BF16) for w in (w_g, w_u, w_d)]
    outs = pl.pallas_call(
        _ffn_kernel,
        grid=(M // tm, d_ff // tf),
        in_specs=[
            pl.BlockSpec((tm, D), lambda i, f: (i, 0)),
            pl.BlockSpec(memory_space=pl.ANY),
            *w_specs,
            pl.BlockSpec((1, D), lambda i, f: (0, 0)),
        ],
        out_specs=out_specs,
        out_shape=out_shape,
        scratch_shapes=[pltpu.VMEM((2, rc, D), F32), pltpu.SemaphoreType.DMA((2,))],
        compiler_params=_cparams(("arbitrary", "arbitrary")),
        name="ffn",
    )(h2, x1, w_g, w_u, w_d, npost)
    return outs if emit_bf16 else outs[0]


def _pad_lanes(v):
    row = jnp.zeros((1, LANES), F32).at[0, :v.shape[0]].set(v.astype(F32))
    return row, row.reshape(LANES, 1)


def kernel(x_prompt, x_sample, cache_k, cache_v, state_conv, state_ssm, page_table, norm_mix_pre, w_in, conv_w, conv_b, dt_bias, a_log, d_skip, ssm_norm, w_attn_out, w_ssm_out, w_out, norm_mix_post, norm_ffn_pre, w_gate, w_up, w_down, norm_ffn_post):
    depth = w_in.shape[0]
    assert depth == 1, "single trunk layer"
    b_p, seq, d_model = x_prompt.shape
    b_s, dec_seq, _ = x_sample.shape
    assert dec_seq == 1
    n_heads_attn = cache_k.shape[3]
    attn_w = n_heads_attn * cache_k.shape[4]
    assert cache_k.shape[4] == HEAD_DIM and cache_k.shape[2] == PAGE_SIZE
    conv_dim = conv_w.shape[2]
    n_ssm_heads = dt_bias.shape[1]
    d_inner = n_ssm_heads * SSM_HEADDIM
    assert conv_dim == d_inner + 2 * N_SSM_GROUPS * D_STATE and n_ssm_heads <= LANES
    n_pages = page_table.shape[1]
    assert (n_pages * PAGE_SIZE) % MOBA_BLOCK == 0
    n_sel = min(MOBA_TOPK, n_pages * PAGE_SIZE // MOBA_BLOCK)
    assert n_sel == MOBA_TOPK

    l = 0
    src, off = {}, 0
    for name, w in (("q", attn_w), ("k", attn_w), ("v", attn_w), ("z", d_inner), ("xbc", conv_dim),
                    ("dt", n_ssm_heads), ("ga", d_model), ("gs", d_model)):
        src[name] = (off, w)
        off += w
    assert off == w_in.shape[2]
    w_t = jnp.transpose(w_in[l])
    tn_f = PROJ_TN_F32
    cols = {name: src[name][0] for name in ("q", "k", "v", "z", "xbc")}
    segments, tile, off = [(0, 0)], src["dt"][0] // tn_f, src["dt"][0]
    assert off % tn_f == 0 and d_model % tn_f == 0
    for name in ("ga", "gs", "dt"):
        segments.append((tile, src[name][0]))
        cols[name] = off
        tile += -(-src[name][1] // tn_f)
        off = tile * tn_f
    assert n_ssm_heads <= LANES <= tn_f and src["dt"][0] + tn_f <= w_t.shape[0] and off % PROJ_TN == 0
    wa = w_attn_out[l].astype(BF16)
    ws = w_ssm_out[l].astype(BF16)
    wo = w_out[l].astype(BF16)
    g_pre = norm_mix_pre[l].reshape(1, d_model)
    g_post = norm_mix_post[l].reshape(1, d_model)
    g_fpre = norm_ffn_pre[l].reshape(1, d_model)
    g_fpost = norm_ffn_post[l].reshape(1, d_model)
    cw = conv_w[l]
    cb = conv_b[l].reshape(1, conv_dim)
    dtb_r, dtb_c = _pad_lanes(dt_bias[l])
    alog_r, alog_c = _pad_lanes(a_log[l])
    dskip = jnp.repeat(d_skip[l].astype(F32), SSM_HEADDIM).reshape(1, d_inner)
    snorm = ssm_norm[l].reshape(1, d_inner)


    xs = x_sample.reshape(b_s, d_model)
    proj_s, w_t_b = _in_proj(xs, g_pre, w_t, tm=b_s, tn=tn_f, segments=tuple(segments),
                             pad=(tile - 1, n_ssm_heads))
    group = lambda name, w: proj_s[:, cols[name]:cols[name] + w]
    q_s, k_s, v_s = group("q", attn_w), group("k", attn_w), group("v", attn_w)
    z_s, xbc_s, dt_s = group("z", d_inner), group("xbc", conv_dim), group("dt", LANES)
    hd = (b_s, n_heads_attn, 1, HEAD_DIM)
    q4, k4, v4 = q_s.reshape(hd), k_s.reshape(hd), v_s.reshape(hd)
    q_bcast = jnp.broadcast_to(q_s.reshape(b_s, n_heads_attn, HEAD_DIM, 1),
                               (b_s, n_heads_attn, HEAD_DIM, PAGE_SIZE))
    ck = jnp.transpose(cache_k.reshape(cache_k.shape[1:]), (0, 2, 3, 1))
    cv = jnp.transpose(cache_v.reshape(cache_v.shape[1:]), (0, 2, 3, 1))

    m_p = b_p * seq
    xp = x_prompt.reshape(m_p, d_model)
    proj = _in_proj(xp, g_pre, w_t_b, tm=min(PROJ_TM, m_p), tn=PROJ_TN)
    attn, kt, vt, gates = _moba_prompt(proj, cols, attn_w, b_p, seq, q_bcast, ck, page_table)
    u, conv_p, ssm_p = _ssd_prompt(proj, cols, cw, cb, dtb_r, dtb_c, alog_r, alog_c, dskip, snorm, b_p, seq)

    idx = _moba_topk(gates, n_sel)
    attn_s = _moba_sample(q4, k4, v4, ck, cv, page_table, idx.reshape(b_s, n_sel * n_heads_attn))
    u_s, conv_s, ssm_s = _ssd_step(xbc_s, z_s, dt_s, state_conv[l], state_ssm[l], cw, cb,
                                   dtb_r, alog_r, dskip, snorm)
    x1_s, h2_s = _merge(attn_s.reshape(b_s, attn_w).astype(BF16), u_s, proj_s, cols, xs,
                        wa, ws, wo, g_post, g_fpre, b_s)
    y_s, w_g, w_u, w_d = _ffn(h2_s, x1_s, w_gate[l], w_up[l], w_down[l], g_fpost, b_s, emit_bf16=True)

    x1_p, h2_p = _merge(attn, u, proj, cols, xp, wa, ws, wo, g_post, g_fpre, min(MERGE_TM, m_p))
    y_p = _ffn(h2_p, x1_p, w_g, w_u, w_d, g_fpost, min(FFN_TM, m_p))

    def kv_prompt(t):
        return jnp.transpose(t.reshape(b_p, n_heads_attn, HEAD_DIM, seq), (0, 3, 1, 2))[None]

    kv_s = (1, b_s, 1, n_heads_attn, HEAD_DIM)
    return (y_p.reshape(b_p, seq, d_model), y_s.reshape(b_s, 1, d_model),
            kv_prompt(kt), kv_prompt(vt), conv_p[None], ssm_p[None],
            k_s.reshape(kv_s), v_s.reshape(kv_s), conv_s[None], ssm_s[None])
```

```python
import functools
import math

import jax
import jax.numpy as jnp
from jax import lax
from jax.experimental import pallas as pl
from jax.experimental.pallas import tpu as pltpu

F32 = jnp.float32
BF16 = jnp.bfloat16

EPS = 1e-6
HEAD_DIM = 64
MOBA_BLOCK = 256
MOBA_TOPK = 3
PAGE_SIZE = 128
SSM_HEADDIM = 64
N_SSM_GROUPS = 8
D_STATE = 128
CONV_WIDTH = 4
SSD_CHUNK = 128

LANES = 128
SUBLANES = 8
VMEM_LIMIT = 56 * 1024 * 1024
PROJ_TM = 1024
MERGE_TM = 256
FFN_TM = 1024
PROJ_TN = 1536
PROJ_TN_F32 = 512
FFN_TF = 512
FFN_RESIDUAL_ROWS = 128

NEG_BIG = -(2.0 ** 100)
ROW_GROUP = 16
SCAN_RING = 8

_NT = (((1,), (1,)), ((), ()))
_TN = (((0,), (0,)), ((), ()))


def _cparams(sem):
    return pltpu.CompilerParams(dimension_semantics=sem, vmem_limit_bytes=VMEM_LIMIT)


def _rms(x, g):
    return x * lax.rsqrt(jnp.mean(x * x, axis=-1, keepdims=True) + EPS) * g


def _silu(x):
    return x * jax.nn.sigmoid(x)


def _softplus(x):
    return jnp.maximum(x, 0.0) + jnp.log1p(jnp.exp(-jnp.abs(x)))


def _eye(n):
    return lax.broadcasted_iota(jnp.int32, (n, n), 0) == lax.broadcasted_iota(jnp.int32, (n, n), 1)


def _row_to_col(row):
    n = row.shape[1]
    return jnp.sum(jnp.where(_eye(n), jnp.broadcast_to(row, (n, n)), 0.0), axis=1, keepdims=True)


def _col_to_row(col):
    n = col.shape[0]
    return jnp.sum(jnp.where(_eye(n), jnp.broadcast_to(col, (n, n)), 0.0), axis=0, keepdims=True)


def _in_proj_kernel(x_ref, g_ref, w_ref, o_ref, *rest, pad):
    h_scr = rest[-1]
    j = pl.program_id(1)

    @pl.when(j == 0)
    def _():
        h_scr[...] = _rms(x_ref[...], g_ref[...]).astype(BF16)

    w = w_ref[...].astype(BF16)
    if pad is not None:
        rows = lax.broadcasted_iota(jnp.int32, (w.shape[0], 1), 0)
        w = jnp.where((j == pad[0]) & (rows >= pad[1]), jnp.zeros_like(w), w)
    if len(rest) == 2:
        rest[0][...] = w
    o_ref[...] = lax.dot_general(h_scr[...], w, _NT, preferred_element_type=F32)


def _in_proj(x, g, w_t, tm, tn, segments=None, pad=None):
    M, D = x.shape
    if segments is None:
        assert w_t.shape[0] % tn == 0
        n_tiles = w_t.shape[0] // tn
        w_spec = pl.BlockSpec((tn, D), lambda i, j: (j, 0))
    else:
        n_tiles = pad[0] + 1
        assert M == tm, "one row tile: every weight tile is visited, and copied, once"

        assert all(first_row % SUBLANES == 0 for _, first_row in segments) and tn % SUBLANES == 0

        def row_offset(i, j):
            off = 0
            for first_tile, first_row in segments:
                off = jnp.where(j >= first_tile, first_row + (j - first_tile) * tn, off)
            return pl.multiple_of(off, SUBLANES), 0

        w_spec = pl.BlockSpec((pl.Element(tn), pl.Element(D)), row_offset)
    out_specs = [pl.BlockSpec((tm, tn), lambda i, j: (i, j))]
    out_shape = [jax.ShapeDtypeStruct((M, n_tiles * tn), F32)]
    if segments is not None:
        out_specs.append(pl.BlockSpec((tn, D), lambda i, j: (j, 0)))
        out_shape.append(jax.ShapeDtypeStruct((n_tiles * tn, D), BF16))
    outs = pl.pallas_call(
        functools.partial(_in_proj_kernel, pad=pad),
        grid=(M // tm, n_tiles),
        in_specs=[pl.BlockSpec((tm, D), lambda i, j: (i, 0)),
                  pl.BlockSpec((1, D), lambda i, j: (0, 0)),
                  w_spec],
        out_specs=out_specs,
        out_shape=out_shape,
        scratch_shapes=[pltpu.VMEM((tm, D), BF16)],
        compiler_params=_cparams(("arbitrary", "arbitrary")),
        name="in_proj",
    )(x, g, w_t)
    return outs if segments is not None else outs[0]


def _moba_prompt_kernel(pt_ref, q_ref, k_ref, v_ref, qs_ref, ck_ref, o_ref, kt_ref, vt_ref, g_ref,
                        s_scr, p_scr, o_scr, scan_buf, scan_sem, *, L, scale, n_pages):
    BS = MOBA_BLOCK
    nb = L // BS
    assert nb <= HEAD_DIM
    n_sub = 2 * nb
    ppb = BS // PAGE_SIZE
    ch = n_pages // n_sub
    n_buf = scan_buf.shape[0]
    assert n_pages % n_sub == 0 and ch % ppb == 0 and n_sub % n_buf == 0
    step = pl.program_id(0) * pl.num_programs(1) + pl.program_id(1)
    n_steps = pl.num_programs(0) * pl.num_programs(1)
    _, n_h, d_h, T = qs_ref.shape

    def scan_start(j):
        seq, jj = (step, j) if j < n_sub else (step + 1, j - n_sub)

        def go():
            for t in range(ch):
                pltpu.make_async_copy(ck_ref.at[pt_ref[seq, jj * ch + t]], scan_buf.at[j % n_buf, t],
                                      scan_sem.at[j % n_buf]).start()

        if j < n_sub:
            go()
        else:
            pl.when(step + 1 < n_steps)(go)

    def scan_chunk(j):
        slot = j % n_buf
        scan_start(j + n_buf - 1)
        for t in range(ch):
            pltpu.make_async_copy(ck_ref.at[pt_ref[step, j * ch + t]], scan_buf.at[slot, t],
                                  scan_sem.at[slot]).wait()
        for h in range(n_h):
            qh = qs_ref[0, h]
            for u in range(ch // ppb):
                pages = scan_buf[slot, u * ppb, h]
                for t in range(1, ppb):
                    pages = pages + scan_buf[slot, u * ppb + t, h]
                part = jnp.sum((pages * qh).reshape(d_h // SUBLANES, SUBLANES, T), axis=0)
                n = j * (ch // ppb) + u
                g_ref[0, h:h + 1, n:n + 1] = jnp.sum(part, keepdims=True) * (1.0 / BS)

    @pl.when(step == 0)
    def _():
        for j in range(n_buf - 1):
            scan_start(j)

    q = q_ref[...]
    k = k_ref[...]
    v = v_ref[...]
    kt_ref[0] = k.T
    vt_ref[0] = v.T
    lane = lax.broadcasted_iota(jnp.int32, (1, LANES), 1)
    first = lane < HEAD_DIM

    kmean = jnp.concatenate(
        [jnp.mean(k[n * BS:(n + 1) * BS], axis=0, keepdims=True) for n in range(nb)], axis=0)

    blk = lax.broadcasted_iota(jnp.int32, (nb, L), 0)
    qblk = lax.broadcasted_iota(jnp.int32, (nb, L), 1) // BS
    past = blk < qblk
    row = lax.broadcasted_iota(jnp.int32, (BS, BS), 0)
    col = lax.broadcasted_iota(jnp.int32, (BS, BS), 1)
    causal_bias = jnp.where(col <= row, 0.0, -jnp.inf)
    key_blk = lax.broadcasted_iota(jnp.int32, (L, LANES), 0) // BS
    key_lane = lax.broadcasted_iota(jnp.int32, (L, LANES), 1)
    c = scale * math.log2(math.e)

    def scores(q_aug, k_aug, qb):
        qs = slice(qb * BS, (qb + 1) * BS)
        n_past = qb * BS
        if qb > 0:
            s_scr[qb % 2, :, :n_past] = lax.dot_general(q_aug[qs], k_aug[:n_past], _NT,
                                                        preferred_element_type=F32)
        s_scr[qb % 2, :, n_past:n_past + BS] = lax.dot_general(
            q_aug[qs], k_aug[n_past:n_past + BS], _NT, preferred_element_type=F32) + causal_bias

    def softmax(qb):
        nk = (qb + 1) * BS
        sb = qb % 2

        def rows(r, carry):
            rs = pl.ds(pl.multiple_of(r * ROW_GROUP, ROW_GROUP), ROW_GROUP)
            s = s_scr[sb, rs, :nk]
            p_scr[sb, rs, :nk] = jnp.exp2(s - jnp.max(s, axis=1, keepdims=True)).astype(BF16)
            return carry

        lax.fori_loop(0, BS // ROW_GROUP, rows, 0, unroll=True)

    def weighted_values(head, v_aug, qb):
        qs = slice(qb * BS, (qb + 1) * BS)
        nk = (qb + 1) * BS
        o = jnp.dot(p_scr[qb % 2, :, :nk], v_aug[:nk], preferred_element_type=F32)
        o = o / pltpu.roll(o, HEAD_DIM, axis=1)
        if head == 0:
            o_scr[qs, :] = o
        else:
            o_ref[qs, :] = jnp.where(first, o_scr[qs, :], o).astype(o_ref.dtype)

    for head, (own, off) in enumerate(((first, HEAD_DIM), (~first, 0))):
        gate = lax.dot_general(jnp.where(own, kmean, 0.0), q, _NT,
                               precision=lax.Precision.HIGHEST, preferred_element_type=F32)
        gate = jnp.where(past, gate, -jnp.inf)
        rank = jnp.zeros((nb, L), jnp.int32)
        for m in range(nb):
            gm = gate[m:m + 1, :]
            beats = (gm > gate) | ((gm == gate) & (m < blk))
            rank = rank + beats.astype(jnp.int32)
        allowed = (past & (rank < MOBA_TOPK)) | (blk == qblk)
        bias_t = jnp.where(allowed, 0.0, NEG_BIG)
        pieces = [jnp.zeros((off, L), F32), bias_t, jnp.zeros((LANES - off - nb, L), F32)]
        pad_t = jnp.concatenate([t for t in pieces if t.shape[0]], axis=0)
        q_aug = jnp.where(own, q * c, pad_t.T).astype(BF16)
        k_aug = jnp.where(own, k, (key_lane - off == key_blk).astype(F32)).astype(BF16)
        v_aug = jnp.where(own, v, 1.0).astype(BF16)

        for qb in range(nb + 1):
            if qb < nb:
                scan_chunk(head * nb + qb)
            if qb == 0:
                scores(q_aug, k_aug, 0)
            if qb + 1 < nb:
                scores(q_aug, k_aug, qb + 1)
            if qb < nb:
                softmax(qb)
            if qb > 0:
                weighted_values(head, v_aug, qb - 1)


def _col_block(off, width):
    assert off % width == 0
    return off // width


def _moba_prompt(proj, cols, W, batch, L, q_sample, cache_kt, page_table):
    M = proj.shape[0]
    B, H, d, T = q_sample.shape
    n_pages = page_table.shape[1]
    n_hp = W // LANES
    n_blocks = n_pages * PAGE_SIZE // MOBA_BLOCK
    assert L % MOBA_BLOCK == 0 and W % LANES == 0 and LANES == 2 * HEAD_DIM
    assert batch * n_hp == B, "one sample sequence is scanned per grid step"
    ch = n_pages // (2 * (L // MOBA_BLOCK))

    def in_spec(name):
        c0 = _col_block(cols[name], LANES)
        return pl.BlockSpec((L, LANES), lambda b, hp, pt: (b, c0 + hp))

    seq4 = lambda b, hp, pt: (b * n_hp + hp, 0, 0, 0)
    spec_t = pl.BlockSpec((1, LANES, L), lambda b, hp, pt: (b, hp, 0))
    kv_t = jax.ShapeDtypeStruct((batch, W, L), F32)
    grid_spec = pltpu.PrefetchScalarGridSpec(
        num_scalar_prefetch=1,
        grid=(batch, n_hp),
        in_specs=[in_spec("q"), in_spec("k"), in_spec("v"),
                  pl.BlockSpec((1, H, d, T), seq4), pl.BlockSpec(memory_space=pl.ANY)],
        out_specs=[pl.BlockSpec((L, LANES), lambda b, hp, pt: (b, hp)), spec_t, spec_t,
                   pl.BlockSpec((1, H, n_blocks), lambda b, hp, pt: (b * n_hp + hp, 0, 0))],
        scratch_shapes=[pltpu.VMEM((2, MOBA_BLOCK, L), F32), pltpu.VMEM((2, MOBA_BLOCK, L), BF16),
                        pltpu.VMEM((L, LANES), F32),
                        pltpu.VMEM((SCAN_RING, ch, H, d, T), F32),
                        pltpu.SemaphoreType.DMA((SCAN_RING,))],
    )
    return pl.pallas_call(
        functools.partial(_moba_prompt_kernel, L=L, scale=HEAD_DIM ** -0.5, n_pages=n_pages),
        grid_spec=grid_spec,
        out_shape=[jax.ShapeDtypeStruct((M, W), BF16), kv_t, kv_t,
                   jax.ShapeDtypeStruct((B, H, n_blocks), F32)],
        compiler_params=_cparams(("arbitrary", "arbitrary")),
        name="moba_prompt",
    )(page_table, proj, proj, proj, q_sample, cache_kt)


def _ssd_prompt_kernel(*refs, d_inner, n_x, n_z, n_w):
    xbc_refs, z_refs = refs[:n_x], refs[n_x:n_x + n_z]
    rest = refs[n_x + n_z:]
    (dt_ref, cw_ref, cb_ref, dtb_r_ref, dtb_c_ref, alog_r_ref, alog_c_ref, dskip_ref,
     norm_ref) = rest[:9]
    w_in_refs = rest[9:9 + n_w]
    u_ref, conv_ref, st_ref = rest[9 + n_w:12 + n_w]
    w_out_refs = rest[12 + n_w:12 + 2 * n_w]
    xs_scr = rest[12 + 2 * n_w]
    for w_in_ref, w_out_ref in zip(w_in_refs, w_out_refs):
        w_out_ref[...] = w_in_ref[...].astype(BF16)
    bw = xbc_refs[0].shape[1]
    Q = SSD_CHUNK
    P = SSM_HEADDIM
    N = D_STATE
    G = N_SSM_GROUPS
    R = d_inner // P // G
    assert R % 2 == 0 and 2 * P == LANES and N == LANES
    c = pl.program_id(1)
    tail = CONV_WIDTH - 1

    @pl.when(c == 0)
    def _():
        xs_scr[...] = jnp.zeros(xs_scr.shape, F32)
        st_ref[...] = jnp.zeros(st_ref.shape, F32)

    x = jnp.concatenate([r[...] for r in xbc_refs], axis=1)
    prev = xs_scr[...]
    row8 = lax.broadcasted_iota(jnp.int32, (SUBLANES, 1), 0)

    def shifted(k):
        xk = pltpu.roll(x, k, axis=0)
        head = jnp.where(row8 < k, pltpu.roll(prev, k, axis=0), xk[0:SUBLANES])
        return jnp.concatenate([head, xk[SUBLANES:]], axis=0)

    acc = cw_ref[0:1, :] * shifted(tail)
    for i in range(1, tail):
        acc = acc + cw_ref[i:i + 1, :] * shifted(tail - i)
    acc = acc + cw_ref[tail:tail + 1, :] * x
    xc = _silu(cb_ref[...] + acc)
    xs_scr[...] = x[Q - SUBLANES:Q, :]
    conv_ref[0] = x[Q - tail:Q, :]

    raw = dt_ref[...]
    dt = _softplus(raw + dtb_r_ref[...])
    dt_t = _softplus(raw.T + dtb_c_ref[...])
    a = dt * (-jnp.exp(alog_r_ref[...]))
    a_t = dt_t * (-jnp.exp(alog_c_ref[...]))
    ri = lax.broadcasted_iota(jnp.int32, (Q, Q), 0)
    ci = lax.broadcasted_iota(jnp.int32, (Q, Q), 1)
    causal = ci <= ri
    acum = jnp.dot(causal.astype(F32), a, precision=lax.Precision.HIGHEST,
                   preferred_element_type=F32)
    acum_t = jnp.dot(a_t, (ri <= ci).astype(F32), precision=lax.Precision.HIGHEST,
                     preferred_element_type=F32)
    e_acum = jnp.exp(acum)
    d_end = jnp.exp(acum[Q - 1:Q, :] - acum)
    lane = lax.broadcasted_iota(jnp.int32, (1, LANES), 1)
    first = lane < P

    def pair_cols(t, h0):
        return jnp.where(first, t[:, h0:h0 + 1], t[:, h0 + 1:h0 + 2])

    y_parts = []
    for g in range(G):
        b_g = xc[:, d_inner + g * N:d_inner + (g + 1) * N].astype(BF16)
        c_g = xc[:, d_inner + G * N + g * N:d_inner + G * N + (g + 1) * N].astype(BF16)
        cb = lax.dot_general(c_g, b_g, _NT, preferred_element_type=F32)
        st_g = st_ref[0, g * R:(g + 1) * R].reshape(R * P, N)
        y_off = lax.dot_general(c_g, st_g.astype(BF16), _NT, preferred_element_type=F32)
        xdtd_parts, cd_parts = [], []
        for pr in range(R // 2):
            h0 = g * R + 2 * pr
            x_p = xc[:, h0 * P:h0 * P + LANES]
            xdt = x_p * pair_cols(dt, h0)
            xdt_b = xdt.astype(BF16)
            yd = []
            for hh in (h0, h0 + 1):
                seg = acum[:, hh:hh + 1] - acum_t[hh:hh + 1, :]
                dec = jnp.exp(jnp.where(causal, seg, -jnp.inf))
                yd.append(jnp.dot((cb * dec).astype(BF16), xdt_b, preferred_element_type=F32))
                cd_parts.append(jnp.broadcast_to(jnp.exp(acum_t[hh:hh + 1, Q - 1:Q]), (P, N)))
            y_p = (jnp.where(first, yd[0], yd[1])
                   + y_off[:, 2 * pr * P:2 * pr * P + LANES] * pair_cols(e_acum, h0)
                   + dskip_ref[:, h0 * P:h0 * P + LANES] * x_p)
            y_parts.append(y_p)
            xdtd_parts.append((xdt * pair_cols(d_end, h0)).astype(BF16))
        xdtd = jnp.concatenate(xdtd_parts, axis=1)
        s_new = lax.dot_general(xdtd, b_g, _TN, preferred_element_type=F32)
        st_new = jnp.concatenate(cd_parts, axis=0) * st_g + s_new
        st_ref[0, g * R:(g + 1) * R] = st_new.reshape(R, P, N)

    y = jnp.concatenate(y_parts, axis=1)
    u = y * _silu(jnp.concatenate([r[...] for r in z_refs], axis=1))
    gw = d_inner // G
    u_parts = []
    for g in range(G):
        ug = u[:, g * gw:(g + 1) * gw]
        u_parts.append(ug * lax.rsqrt(jnp.mean(ug * ug, axis=-1, keepdims=True) + EPS))
    u_ref[...] = (jnp.concatenate(u_parts, axis=1) * norm_ref[...]).astype(u_ref.dtype)


def _ssd_prompt(proj, cols, conv_w, conv_b, dtb_r, dtb_c, alog_r, alog_c, dskip, norm, batch, L,
                cast_weights=()):
    M = proj.shape[0]
    conv_dim = conv_w.shape[1]
    d_inner = norm.shape[1]
    n_heads = d_inner // SSM_HEADDIM
    assert L % SSD_CHUNK == 0
    nc = L // SSD_CHUNK
    Q = SSD_CHUNK
    tail = CONV_WIDTH - 1
    row = lambda b, c: (b * nc + c, 0)
    const = lambda b, c: (0, 0)

    def in_spec(off, width):
        c0 = _col_block(off, width)
        return pl.BlockSpec((Q, width), lambda b, c: (b * nc + c, c0))

    bw = math.gcd(cols["xbc"], cols["z"], conv_dim, d_inner)
    assert bw % LANES == 0
    n_x, n_z = conv_dim // bw, d_inner // bw

    def slab_spec(w):
        n_steps = batch * nc
        assert w.shape[0] % n_steps == 0
        r = w.shape[0] // n_steps
        g = next(g for g in (1, 2, 4, 8, 16) if (r * g) % (2 * SUBLANES) == 0 and n_steps % g == 0)
        return pl.BlockSpec((r * g, w.shape[1]), lambda b, c: ((b * nc + c) // g, 0))

    w_specs = [slab_spec(w) for w in cast_weights]
    return pl.pallas_call(
        functools.partial(_ssd_prompt_kernel, d_inner=d_inner, n_x=n_x, n_z=n_z,
                          n_w=len(cast_weights)),
        grid=(batch, nc),
        in_specs=[in_spec(cols["xbc"] + i * bw, bw) for i in range(n_x)]
        + [in_spec(cols["z"] + i * bw, bw) for i in range(n_z)]
        + [
            in_spec(cols["dt"], LANES),
            pl.BlockSpec((CONV_WIDTH, conv_dim), const),
            pl.BlockSpec((1, conv_dim), const),
            pl.BlockSpec((1, LANES), const),
            pl.BlockSpec((LANES, 1), const),
            pl.BlockSpec((1, LANES), const),
            pl.BlockSpec((LANES, 1), const),
            pl.BlockSpec((1, d_inner), const),
            pl.BlockSpec((1, d_inner), const),
        ] + w_specs,
        out_specs=[
            pl.BlockSpec((Q, d_inner), row),
            pl.BlockSpec((1, tail, conv_dim), lambda b, c: (b, 0, 0)),
            pl.BlockSpec((1, n_heads, SSM_HEADDIM, D_STATE), lambda b, c: (b, 0, 0, 0)),
        ] + w_specs,
        out_shape=[
            jax.ShapeDtypeStruct((M, d_inner), BF16),
            jax.ShapeDtypeStruct((batch, tail, conv_dim), F32),
            jax.ShapeDtypeStruct((batch, n_heads, SSM_HEADDIM, D_STATE), F32),
        ] + [jax.ShapeDtypeStruct(w.shape, BF16) for w in cast_weights],
        scratch_shapes=[pltpu.VMEM((SUBLANES, conv_dim), F32)],
        compiler_params=_cparams(("arbitrary", "arbitrary")),
        name="ssd_prompt",
    )(*([proj] * (n_x + n_z + 1)), conv_w, conv_b, dtb_r, dtb_c, alog_r, alog_c, dskip, norm,
      *cast_weights)


def _ssd_step_kernel(xbc_ref, z_ref, dt_ref, cprev_ref, sprev_ref, cw_ref, cb_ref, dtb_ref,
                     alog_ref, dskip_ref, norm_ref, u_ref, conv_ref, st_ref, *, d_inner):
    P = SSM_HEADDIM
    N = D_STATE
    G = N_SSM_GROUPS
    n_heads = d_inner // P
    R = n_heads // G
    assert R % 2 == 0 and 2 * P == LANES and N == LANES
    tail = CONV_WIDTH - 1
    x_new = xbc_ref[0]
    prev = cprev_ref[0]
    acc = cw_ref[0:1, :] * prev[0:1, :]
    for i in range(1, tail):
        acc = acc + cw_ref[i:i + 1, :] * prev[i:i + 1, :]
    acc = acc + cw_ref[tail:tail + 1, :] * x_new
    xc = _silu(cb_ref[...] + acc)
    conv_ref[0, 0:tail - 1, :] = prev[1:tail, :]
    conv_ref[0, tail - 1:tail, :] = x_new

    dt = _softplus(dt_ref[0] + dtb_ref[...])
    decay = jnp.exp(dt * (-jnp.exp(alog_ref[...])))
    first_rows = lax.broadcasted_iota(jnp.int32, (LANES, 1), 0) < P
    first = lax.broadcasted_iota(jnp.int32, (1, LANES), 1) < P
    pad_rows = lambda row: jnp.broadcast_to(row, (SUBLANES, row.shape[1])).astype(BF16)

    y_parts = []
    for hp in range(n_heads // 2):
        h0 = 2 * hp
        g = h0 // R
        b_g = xc[:, d_inner + g * N:d_inner + (g + 1) * N]
        c_g = xc[:, d_inner + G * N + g * N:d_inner + G * N + (g + 1) * N]
        x_row = xc[:, h0 * P:h0 * P + LANES]
        dt_row = jnp.where(first, dt[:, h0:h0 + 1], dt[:, h0 + 1:h0 + 2])
        dec_col = jnp.where(first_rows, decay[:, h0:h0 + 1], decay[:, h0 + 1:h0 + 2])
        st = sprev_ref[0, h0:h0 + 2].reshape(2 * P, N)
        upd = lax.dot_general(pad_rows(dt_row * x_row), pad_rows(b_g), _TN,
                              preferred_element_type=F32) * (1.0 / SUBLANES)
        st_new = dec_col * st + upd
        st_ref[0, h0:h0 + 2] = st_new.reshape(2, P, N)
        y_row = lax.dot_general(pad_rows(c_g), st_new.astype(BF16), _NT,
                                preferred_element_type=F32)[0:1]
        y_parts.append(y_row + dskip_ref[:, h0 * P:h0 * P + LANES] * x_row)
    y = jnp.concatenate(y_parts, axis=1)
    u = y * _silu(z_ref[0])
    gw = d_inner // G
    u_parts = []
    for g in range(G):
        ug = u[:, g * gw:(g + 1) * gw]
        u_parts.append(ug * lax.rsqrt(jnp.mean(ug * ug, axis=-1, keepdims=True) + EPS))
    u_ref[0] = (jnp.concatenate(u_parts, axis=1) * norm_ref[...]).astype(u_ref.dtype)


def _ssd_step(xbc, z, dt_raw, conv_prev, ssm_prev, conv_w, conv_b, dtb_r, alog_r, dskip, norm):
    B, conv_dim = xbc.shape
    d_inner = z.shape[1]
    n_heads = d_inner // SSM_HEADDIM
    tail = CONV_WIDTH - 1
    const = lambda b: (0, 0)
    b3 = lambda b: (b, 0, 0)
    b4 = lambda b: (b, 0, 0, 0)
    u, conv_new, ssm_new = pl.pallas_call(
        functools.partial(_ssd_step_kernel, d_inner=d_inner),
        grid=(B,),
        in_specs=[
            pl.BlockSpec((1, 1, conv_dim), b3),
            pl.BlockSpec((1, 1, d_inner), b3),
            pl.BlockSpec((1, 1, LANES), b3),
            pl.BlockSpec((1, tail, conv_dim), b3),
            pl.BlockSpec((1, n_heads, SSM_HEADDIM, D_STATE), b4),
            pl.BlockSpec((CONV_WIDTH, conv_dim), const),
            pl.BlockSpec((1, conv_dim), const),
            pl.BlockSpec((1, LANES), const),
            pl.BlockSpec((1, LANES), const),
            pl.BlockSpec((1, d_inner), const),
            pl.BlockSpec((1, d_inner), const),
        ],
        out_specs=[
            pl.BlockSpec((1, 1, d_inner), b3),
            pl.BlockSpec((1, tail, conv_dim), b3),
            pl.BlockSpec((1, n_heads, SSM_HEADDIM, D_STATE), b4),
        ],
        out_shape=[
            jax.ShapeDtypeStruct((B, 1, d_inner), BF16),
            jax.ShapeDtypeStruct((B, tail, conv_dim), F32),
            jax.ShapeDtypeStruct((B, n_heads, SSM_HEADDIM, D_STATE), F32),
        ],
        compiler_params=_cparams(("arbitrary",)),
        name="ssd_step",
    )(xbc.reshape(B, 1, conv_dim), z.reshape(B, 1, d_inner), dt_raw.reshape(B, 1, LANES),
      conv_prev, ssm_prev, conv_w, conv_b, dtb_r, alog_r, dskip, norm)
    return u.reshape(B, d_inner), conv_new, ssm_new


def _moba_topk_kernel(g_ref, idx_ref, *, n_sel):
    gate = g_ref[...]
    n_blocks = gate.shape[2]
    blk = lax.broadcasted_iota(jnp.int32, gate.shape, 2)
    rank = jnp.zeros(gate.shape, jnp.int32)
    for m in range(n_blocks):
        gm = gate[:, :, m:m + 1]
        beats = (gm > gate) | ((gm == gate) & (m < blk))
        rank = rank + beats.astype(jnp.int32)
    for r in range(n_sel):
        idx_ref[:, r] = jnp.sum(jnp.where(rank == r, blk, 0), axis=2, keepdims=True)


def _moba_topk(gates, n_sel):
    B, H, n_blocks = gates.shape
    return pl.pallas_call(
        functools.partial(_moba_topk_kernel, n_sel=n_sel),
        grid=(1,),
        in_specs=[pl.BlockSpec((B, H, n_blocks), lambda b: (0, 0, 0))],
        out_specs=pl.BlockSpec((B, n_sel, H, 1), lambda b: (0, 0, 0, 0)),
        out_shape=jax.ShapeDtypeStruct((B, n_sel, H, 1), jnp.int32),
        compiler_params=_cparams(("arbitrary",)),
        name="moba_topk",
    )(gates)


def _moba_sample_kernel(pt_ref, idx_ref, q_ref, kn_ref, vn_ref, ck_ref, cv_ref, o_ref,
                        kbuf, vbuf, sem, *, n_sel, n_heads, scale):
    b = pl.program_id(0)
    n_b = pl.num_programs(0)
    ppb = MOBA_BLOCK // PAGE_SIZE

    def copies(bb, slot):
        out = []
        for h in range(n_heads):
            for r in range(n_sel):
                blk = idx_ref[bb, r * n_heads + h]
                for t in range(ppb):
                    page = pt_ref[bb, blk * ppb + t]
                    j = r * ppb + t
                    out.append(pltpu.make_async_copy(ck_ref.at[page, h], kbuf.at[slot, h, j], sem.at[0, slot]))
                    out.append(pltpu.make_async_copy(cv_ref.at[page, h], vbuf.at[slot, h, j], sem.at[1, slot]))
        return out

    slot = b % 2

    @pl.when(b == 0)
    def _():
        for cp in copies(b, slot):
            cp.start()

    @pl.when(b + 1 < n_b)
    def _():
        for cp in copies(b + 1, 1 - slot):
            cp.start()

    for cp in copies(b, slot):
        cp.wait()

    def body(h, carry):
        q_row = q_ref[0, h]
        qc = _row_to_col(q_row)
        kh = kbuf[slot, h]
        vh = vbuf[slot, h]
        s = jnp.sum(kh * qc, axis=1, keepdims=True) * scale
        s_own = jnp.sum(kn_ref[0, h] * q_row, axis=-1, keepdims=True) * scale
        m = jnp.maximum(jnp.max(jnp.max(s, axis=0), axis=-1, keepdims=True), s_own)
        p = jnp.exp(s - m)
        p_own = jnp.exp(s_own - m)
        l = jnp.sum(jnp.sum(p, axis=0), axis=-1, keepdims=True) + p_own
        o_past = jnp.sum(jnp.sum(vh * p, axis=0), axis=-1, keepdims=True)
        o_ref[0, h] = (_col_to_row(o_past) + p_own * vn_ref[0, h]) / l
        return carry

    lax.fori_loop(0, n_heads, body, 0, unroll=True)


def _moba_sample(q4, k4, v4, cache_kt, cache_vt, page_table, idx):
    B, H, _, d = q4.shape
    n_sel = idx.shape[1] // H
    n_slabs = n_sel * (MOBA_BLOCK // PAGE_SIZE)
    new = pl.BlockSpec((1, H, 1, d), lambda b, pt, ix: (b, 0, 0, 0))
    grid_spec = pltpu.PrefetchScalarGridSpec(
        num_scalar_prefetch=2,
        grid=(B,),
        in_specs=[new, new, new, pl.BlockSpec(memory_space=pl.ANY), pl.BlockSpec(memory_space=pl.ANY)],
        out_specs=new,
        scratch_shapes=[pltpu.VMEM((2, H, n_slabs, d, PAGE_SIZE), F32),
                        pltpu.VMEM((2, H, n_slabs, d, PAGE_SIZE), F32),
                        pltpu.SemaphoreType.DMA((2, 2))],
    )
    return pl.pallas_call(
        functools.partial(_moba_sample_kernel, n_sel=n_sel, n_heads=H, scale=d ** -0.5),
        grid_spec=grid_spec,
        out_shape=jax.ShapeDtypeStruct((B, H, 1, d), F32),
        compiler_params=_cparams(("arbitrary",)),
        name="moba_sample",
    )(page_table, idx, q4, k4, v4, cache_kt, cache_vt)


def _merge_kernel(*refs, n_g):
    attn_ref, ssm_ref = refs[:2]
    ga_refs, gs_refs = refs[2:2 + n_g], refs[2 + n_g:2 + 2 * n_g]
    x_ref, wa_ref, ws_ref, wo_ref, npost_ref, npre_ref, x1_ref, h2_ref = refs[2 + 2 * n_g:]
    a = jnp.dot(attn_ref[...], wa_ref[...], preferred_element_type=F32)
    s = jnp.dot(ssm_ref[...], ws_ref[...], preferred_element_type=F32)
    ga = jnp.concatenate([r[...] for r in ga_refs], axis=1)
    gs = jnp.concatenate([r[...] for r in gs_refs], axis=1)
    merged = jax.nn.sigmoid(ga) * a + jax.nn.sigmoid(gs) * s
    o = jnp.dot(merged.astype(BF16), wo_ref[...], preferred_element_type=F32)
    x1 = x_ref[...] + _rms(o, npost_ref[...])
    x1_ref[...] = x1
    h2_ref[...] = _rms(x1, npre_ref[...]).astype(BF16)


def _merge(attn, ssm, proj, cols, x, wa, ws, wo, npost, npre, tm):
    M, D = x.shape
    rows = lambda w: pl.BlockSpec((tm, w), lambda i: (i, 0))
    whole = lambda a: pl.BlockSpec(a.shape, lambda i: (0, 0), pipeline_mode=pl.Buffered(1))

    bw = math.gcd(cols["ga"], cols["gs"], D)
    assert bw % LANES == 0
    n_g = D // bw

    def gate_specs(name):
        c0 = _col_block(cols[name], bw)
        return [pl.BlockSpec((tm, bw), lambda i, c=c0 + k: (i, c)) for k in range(n_g)]

    return pl.pallas_call(
        functools.partial(_merge_kernel, n_g=n_g),
        grid=(M // tm,),
        in_specs=[rows(attn.shape[1]), rows(ssm.shape[1])] + gate_specs("ga") + gate_specs("gs")
        + [rows(D), whole(wa), whole(ws), whole(wo), whole(npost), whole(npre)],
        out_specs=[rows(D), rows(D)],
        out_shape=[jax.ShapeDtypeStruct((M, D), F32), jax.ShapeDtypeStruct((M, D), BF16)],
        compiler_params=_cparams(("arbitrary",)),
        name="merge",
    )(attn, ssm, *([proj] * (2 * n_g)), x, wa, ws, wo, npost, npre)


def _ffn_kernel(h_ref, x_hbm, wg_ref, wu_ref, wd_ref, npost_ref, y_ref, x_buf, x_sem):
    i = pl.program_id(0)
    f = pl.program_id(1)
    last = pl.num_programs(1) - 1
    tm = y_ref.shape[0]
    rc = x_buf.shape[1]

    def x_copy(c):
        return pltpu.make_async_copy(x_hbm.at[pl.ds(i * tm + c * rc, rc), :], x_buf.at[c % 2],
                                     x_sem.at[c % 2])

    @pl.when(f == last)
    def _():
        x_copy(0).start()

    h = h_ref[...]
    act = _silu(jnp.dot(h, wg_ref[...], preferred_element_type=F32)) * \
        jnp.dot(h, wu_ref[...], preferred_element_type=F32)

    @pl.when(f == 0)
    def _():
        y_ref[...] = jnp.zeros(y_ref.shape, F32)

    y_ref[...] += jnp.dot(act.astype(BF16), wd_ref[...], preferred_element_type=F32)

    @pl.when(f == last)
    def _():
        for c in range(tm // rc):
            if c + 1 < tm // rc:
                x_copy(c + 1).start()
            x_copy(c).wait()
            rows = slice(c * rc, (c + 1) * rc)
            y_ref[rows, :] = x_buf[c % 2] + _rms(y_ref[rows, :], npost_ref[...])


def _ffn(h2, x1, w_g, w_u, w_d, npost, tm):
    M, D = x1.shape
    d_ff = w_g.shape[1]
    tf = FFN_TF
    rc = min(tm, FFN_RESIDUAL_ROWS)
    assert d_ff % tf == 0 and tm % rc == 0
    return pl.pallas_call(
        _ffn_kernel,
        grid=(M // tm, d_ff // tf),
        in_specs=[
            pl.BlockSpec((tm, D), lambda i, f: (i, 0)),
            pl.BlockSpec(memory_space=pl.ANY),
            pl.BlockSpec((D, tf), lambda i, f: (0, f)),
            pl.BlockSpec((D, tf), lambda i, f: (0, f)),
            pl.BlockSpec((tf, D), lambda i, f: (f, 0)),
            pl.BlockSpec((1, D), lambda i, f: (0, 0)),
        ],
        out_specs=pl.BlockSpec((tm, D), lambda i, f: (i, 0)),
        out_shape=jax.ShapeDtypeStruct((M, D), F32),
        scratch_shapes=[pltpu.VMEM((2, rc, D), F32), pltpu.SemaphoreType.DMA((2,))],
        compiler_params=_cparams(("arbitrary", "arbitrary")),
        name="ffn",
    )(h2, x1, w_g, w_u, w_d, npost)


def _pad_lanes(v):
    row = jnp.zeros((1, LANES), F32).at[0, :v.shape[0]].set(v.astype(F32))
    return row, row.reshape(LANES, 1)


def kernel(x_prompt, x_sample, cache_k, cache_v, state_conv, state_ssm, page_table, norm_mix_pre, w_in, conv_w, conv_b, dt_bias, a_log, d_skip, ssm_norm, w_attn_out, w_ssm_out, w_out, norm_mix_post, norm_ffn_pre, w_gate, w_up, w_down, norm_ffn_post):
    depth = w_in.shape[0]
    assert depth == 1, "single trunk layer"
    b_p, seq, d_model = x_prompt.shape
    b_s, dec_seq, _ = x_sample.shape
    assert dec_seq == 1
    n_heads_attn = cache_k.shape[3]
    attn_w = n_heads_attn * cache_k.shape[4]
    assert cache_k.shape[4] == HEAD_DIM and cache_k.shape[2] == PAGE_SIZE
    conv_dim = conv_w.shape[2]
    n_ssm_heads = dt_bias.shape[1]
    d_inner = n_ssm_heads * SSM_HEADDIM
    assert conv_dim == d_inner + 2 * N_SSM_GROUPS * D_STATE and n_ssm_heads <= LANES
    n_pages = page_table.shape[1]
    assert (n_pages * PAGE_SIZE) % MOBA_BLOCK == 0
    n_sel = min(MOBA_TOPK, n_pages * PAGE_SIZE // MOBA_BLOCK)
    assert n_sel == MOBA_TOPK

    l = 0
    src, off = {}, 0
    for name, w in (("q", attn_w), ("k", attn_w), ("v", attn_w), ("z", d_inner), ("xbc", conv_dim),
                    ("dt", n_ssm_heads), ("ga", d_model), ("gs", d_model)):
        src[name] = (off, w)
        off += w
    assert off == w_in.shape[2]
    w_t = jnp.transpose(w_in[l])
    tn_f = PROJ_TN_F32
    cols = {name: src[name][0] for name in ("q", "k", "v", "z", "xbc")}
    segments, tile, off = [(0, 0)], src["dt"][0] // tn_f, src["dt"][0]
    assert off % tn_f == 0 and d_model % tn_f == 0
    for name in ("ga", "gs", "dt"):
        segments.append((tile, src[name][0]))
        cols[name] = off
        tile += -(-src[name][1] // tn_f)
        off = tile * tn_f
    assert n_ssm_heads <= LANES <= tn_f and src["dt"][0] + tn_f <= w_t.shape[0] and off % PROJ_TN == 0
    g_pre = norm_mix_pre[l].reshape(1, d_model)
    g_post = norm_mix_post[l].reshape(1, d_model)
    g_fpre = norm_ffn_pre[l].reshape(1, d_model)
    g_fpost = norm_ffn_post[l].reshape(1, d_model)
    cw = conv_w[l]
    cb = conv_b[l].reshape(1, conv_dim)
    dtb_r, dtb_c = _pad_lanes(dt_bias[l])
    alog_r, alog_c = _pad_lanes(a_log[l])
    dskip = jnp.repeat(d_skip[l].astype(F32), SSM_HEADDIM).reshape(1, d_inner)
    snorm = ssm_norm[l].reshape(1, d_inner)


    xs = x_sample.reshape(b_s, d_model)
    proj_s, w_t_b = _in_proj(xs, g_pre, w_t, tm=b_s, tn=tn_f, segments=tuple(segments),
                             pad=(tile - 1, n_ssm_heads))
    group = lambda name, w: proj_s[:, cols[name]:cols[name] + w]
    q_s, k_s, v_s = group("q", attn_w), group("k", attn_w), group("v", attn_w)
    z_s, xbc_s, dt_s = group("z", d_inner), group("xbc", conv_dim), group("dt", LANES)
    hd = (b_s, n_heads_attn, 1, HEAD_DIM)
    q4, k4, v4 = q_s.reshape(hd), k_s.reshape(hd), v_s.reshape(hd)
    q_bcast = jnp.broadcast_to(q_s.reshape(b_s, n_heads_attn, HEAD_DIM, 1),
                               (b_s, n_heads_attn, HEAD_DIM, PAGE_SIZE))
    ck = jnp.transpose(cache_k.reshape(cache_k.shape[1:]), (0, 2, 3, 1))
    cv = jnp.transpose(cache_v.reshape(cache_v.shape[1:]), (0, 2, 3, 1))

    m_p = b_p * seq
    xp = x_prompt.reshape(m_p, d_model)
    proj = _in_proj(xp, g_pre, w_t_b, tm=min(PROJ_TM, m_p), tn=PROJ_TN)
    attn, kt, vt, gates = _moba_prompt(proj, cols, attn_w, b_p, seq, q_bcast, ck, page_table)
    u, conv_p, ssm_p, wa, ws, wo, w_g, w_u, w_d = _ssd_prompt(
        proj, cols, cw, cb, dtb_r, dtb_c, alog_r, alog_c, dskip, snorm, b_p, seq,
        cast_weights=(w_attn_out[l], w_ssm_out[l], w_out[l], w_gate[l], w_up[l], w_down[l]))

    idx = _moba_topk(gates, n_sel)
    attn_s = _moba_sample(q4, k4, v4, ck, cv, page_table, idx.reshape(b_s, n_sel * n_heads_attn))
    u_s, conv_s, ssm_s = _ssd_step(xbc_s, z_s, dt_s, state_conv[l], state_ssm[l], cw, cb,
                                   dtb_r, alog_r, dskip, snorm)
    x1_s, h2_s = _merge(attn_s.reshape(b_s, attn_w).astype(BF16), u_s, proj_s, cols, xs,
                        wa, ws, wo, g_post, g_fpre, b_s)
    y_s = _ffn(h2_s, x1_s, w_g, w_u, w_d, g_fpost, b_s)

    x1_p, h2_p = _merge(attn, u, proj, cols, xp, wa, ws, wo, g_post, g_fpre, min(MERGE_TM, m_p))
    y_p = _ffn(h2_p, x1_p, w_g, w_u, w_d, g_fpost, min(FFN_TM, m_p))

    def kv_prompt(t):
        return jnp.transpose(t.reshape(b_p, n_heads_attn, HEAD_DIM, seq), (0, 3, 1, 2))[None]

    kv_s = (1, b_s, 1, n_heads_attn, HEAD_DIM)
    return (y_p.reshape(b_p, seq, d_model), y_s.reshape(b_s, 1, d_model),
            kv_prompt(kt), kv_prompt(vt), conv_p[None], ssm_p[None],
            k_s.reshape(kv_s), v_s.reshape(kv_s), conv_s[None], ssm_s[None])
```

```python
import functools
import math

import jax
import jax.numpy as jnp
from jax import lax
from jax.experimental import pallas as pl
from jax.experimental.pallas import tpu as pltpu

F32 = jnp.float32
BF16 = jnp.bfloat16

EPS = 1e-6
HEAD_DIM = 64
MOBA_BLOCK = 256
MOBA_TOPK = 3
PAGE_SIZE = 128
SSM_HEADDIM = 64
N_SSM_GROUPS = 8
D_STATE = 128
CONV_WIDTH = 4
SSD_CHUNK = 128

LANES = 128
SUBLANES = 8
VMEM_LIMIT = 56 * 1024 * 1024
PROJ_TM = 1024
MERGE_TM = 256
FFN_TM = 1024
PROJ_TN = 1536
PROJ_TN_F32 = 512
FFN_TF = 512
FFN_RESIDUAL_ROWS = 128

NEG_BIG = -(2.0 ** 100)
ROW_GROUP = 16
SCAN_RING = 8

_NT = (((1,), (1,)), ((), ()))
_TN = (((0,), (0,)), ((), ()))


def _cparams(sem):
    return pltpu.CompilerParams(dimension_semantics=sem, vmem_limit_bytes=VMEM_LIMIT)


def _rms(x, g):
    return x * lax.rsqrt(jnp.mean(x * x, axis=-1, keepdims=True) + EPS) * g


def _silu(x):
    return x * jax.nn.sigmoid(x)


def _softplus(x):
    return jnp.maximum(x, 0.0) + jnp.log1p(jnp.exp(-jnp.abs(x)))


def _eye(n):
    return lax.broadcasted_iota(jnp.int32, (n, n), 0) == lax.broadcasted_iota(jnp.int32, (n, n), 1)


def _row_to_col(row):
    n = row.shape[1]
    return jnp.sum(jnp.where(_eye(n), jnp.broadcast_to(row, (n, n)), 0.0), axis=1, keepdims=True)


def _col_to_row(col):
    n = col.shape[0]
    return jnp.sum(jnp.where(_eye(n), jnp.broadcast_to(col, (n, n)), 0.0), axis=0, keepdims=True)


def _in_proj_kernel(x_ref, g_ref, w_ref, o_ref, *rest, pad):
    h_scr = rest[-1]
    j = pl.program_id(1)

    @pl.when(j == 0)
    def _():
        h_scr[...] = _rms(x_ref[...], g_ref[...]).astype(BF16)

    w = w_ref[...].astype(BF16)
    if pad is not None:
        rows = lax.broadcasted_iota(jnp.int32, (w.shape[0], 1), 0)
        w = jnp.where((j == pad[0]) & (rows >= pad[1]), jnp.zeros_like(w), w)
    if len(rest) == 2:
        rest[0][...] = w
    o_ref[...] = lax.dot_general(h_scr[...], w, _NT, preferred_element_type=F32)


def _in_proj(x, g, w_t, tm, tn, segments=None, pad=None):
    M, D = x.shape
    if segments is None:
        assert w_t.shape[0] % tn == 0
        n_tiles = w_t.shape[0] // tn
        w_spec = pl.BlockSpec((tn, D), lambda i, j: (j, 0))
    else:
        n_tiles = pad[0] + 1
        assert M == tm, "one row tile: every weight tile is visited, and copied, once"

        assert all(first_row % SUBLANES == 0 for _, first_row in segments) and tn % SUBLANES == 0

        def row_offset(i, j):
            off = 0
            for first_tile, first_row in segments:
                off = jnp.where(j >= first_tile, first_row + (j - first_tile) * tn, off)
            return pl.multiple_of(off, SUBLANES), 0

        w_spec = pl.BlockSpec((pl.Element(tn), pl.Element(D)), row_offset)
    out_specs = [pl.BlockSpec((tm, tn), lambda i, j: (i, j))]
    out_shape = [jax.ShapeDtypeStruct((M, n_tiles * tn), F32)]
    if segments is not None:
        out_specs.append(pl.BlockSpec((tn, D), lambda i, j: (j, 0)))
        out_shape.append(jax.ShapeDtypeStruct((n_tiles * tn, D), BF16))
    outs = pl.pallas_call(
        functools.partial(_in_proj_kernel, pad=pad),
        grid=(M // tm, n_tiles),
        in_specs=[pl.BlockSpec((tm, D), lambda i, j: (i, 0)),
                  pl.BlockSpec((1, D), lambda i, j: (0, 0)),
                  w_spec],
        out_specs=out_specs,
        out_shape=out_shape,
        scratch_shapes=[pltpu.VMEM((tm, D), BF16)],
        compiler_params=_cparams(("arbitrary", "arbitrary")),
        name="in_proj",
    )(x, g, w_t)
    return outs if segments is not None else outs[0]


def _moba_prompt_kernel(pt_ref, q_ref, k_ref, v_ref, qs_ref, ck_ref, o_ref, kt_ref, vt_ref, g_ref,
                        s_scr, p_scr, o_scr, scan_buf, scan_sem, *, L, scale, n_pages):
    BS = MOBA_BLOCK
    nb = L // BS
    assert nb <= HEAD_DIM
    n_sub = 2 * nb
    ppb = BS // PAGE_SIZE
    ch = n_pages // n_sub
    n_buf = scan_buf.shape[0]
    assert n_pages % n_sub == 0 and ch % ppb == 0 and n_sub % n_buf == 0
    step = pl.program_id(0) * pl.num_programs(1) + pl.program_id(1)
    n_steps = pl.num_programs(0) * pl.num_programs(1)
    _, n_h, d_h, T = qs_ref.shape

    def scan_start(j):
        seq, jj = (step, j) if j < n_sub else (step + 1, j - n_sub)

        def go():
            for t in range(ch):
                pltpu.make_async_copy(ck_ref.at[pt_ref[seq, jj * ch + t]], scan_buf.at[j % n_buf, t],
                                      scan_sem.at[j % n_buf]).start()

        if j < n_sub:
            go()
        else:
            pl.when(step + 1 < n_steps)(go)

    def scan_chunk(j):
        slot = j % n_buf
        scan_start(j + n_buf - 1)
        for t in range(ch):
            pltpu.make_async_copy(ck_ref.at[pt_ref[step, j * ch + t]], scan_buf.at[slot, t],
                                  scan_sem.at[slot]).wait()
        for h in range(n_h):
            qh = qs_ref[0, h]
            for u in range(ch // ppb):
                pages = scan_buf[slot, u * ppb, h]
                for t in range(1, ppb):
                    pages = pages + scan_buf[slot, u * ppb + t, h]
                part = jnp.sum((pages * qh).reshape(d_h // SUBLANES, SUBLANES, T), axis=0)
                n = j * (ch // ppb) + u
                g_ref[0, h:h + 1, n:n + 1] = jnp.sum(part, keepdims=True) * (1.0 / BS)

    @pl.when(step == 0)
    def _():
        for j in range(n_buf - 1):
            scan_start(j)

    q = q_ref[...]
    k = k_ref[...]
    v = v_ref[...]
    kt_ref[0] = k.T
    vt_ref[0] = v.T
    lane = lax.broadcasted_iota(jnp.int32, (1, LANES), 1)
    first = lane < HEAD_DIM

    kmean = jnp.concatenate(
        [jnp.mean(k[n * BS:(n + 1) * BS], axis=0, keepdims=True) for n in range(nb)], axis=0)

    blk = lax.broadcasted_iota(jnp.int32, (nb, L), 0)
    qblk = lax.broadcasted_iota(jnp.int32, (nb, L), 1) // BS
    past = blk < qblk
    row = lax.broadcasted_iota(jnp.int32, (BS, BS), 0)
    col = lax.broadcasted_iota(jnp.int32, (BS, BS), 1)
    causal_bias = jnp.where(col <= row, 0.0, -jnp.inf)
    key_blk = lax.broadcasted_iota(jnp.int32, (L, LANES), 0) // BS
    key_lane = lax.broadcasted_iota(jnp.int32, (L, LANES), 1)
    c = scale * math.log2(math.e)

    def scores(q_aug, k_aug, qb):
        qs = slice(qb * BS, (qb + 1) * BS)
        n_past = qb * BS
        if qb > 0:
            s_scr[qb % 2, :, :n_past] = lax.dot_general(q_aug[qs], k_aug[:n_past], _NT,
                                                        preferred_element_type=F32)
        s_scr[qb % 2, :, n_past:n_past + BS] = lax.dot_general(
            q_aug[qs], k_aug[n_past:n_past + BS], _NT, preferred_element_type=F32) + causal_bias

    def softmax(qb):
        nk = (qb + 1) * BS
        sb = qb % 2

        def rows(r, carry):
            rs = pl.ds(pl.multiple_of(r * ROW_GROUP, ROW_GROUP), ROW_GROUP)
            s = s_scr[sb, rs, :nk]
            p_scr[sb, rs, :nk] = jnp.exp2(s - jnp.max(s, axis=1, keepdims=True)).astype(BF16)
            return carry

        lax.fori_loop(0, BS // ROW_GROUP, rows, 0, unroll=True)

    def weighted_values(head, v_aug, qb):
        qs = slice(qb * BS, (qb + 1) * BS)
        nk = (qb + 1) * BS
        o = jnp.dot(p_scr[qb % 2, :, :nk], v_aug[:nk], preferred_element_type=F32)
        o = o / pltpu.roll(o, HEAD_DIM, axis=1)
        if head == 0:
            o_scr[qs, :] = o
        else:
            o_ref[qs, :] = jnp.where(first, o_scr[qs, :], o).astype(o_ref.dtype)

    for head, (own, off) in enumerate(((first, HEAD_DIM), (~first, 0))):
        gate = lax.dot_general(jnp.where(own, kmean, 0.0), q, _NT,
                               precision=lax.Precision.HIGHEST, preferred_element_type=F32)
        gate = jnp.where(past, gate, -jnp.inf)
        rank = jnp.zeros((nb, L), jnp.int32)
        for m in range(nb):
            gm = gate[m:m + 1, :]
            beats = (gm > gate) | ((gm == gate) & (m < blk))
            rank = rank + beats.astype(jnp.int32)
        allowed = (past & (rank < MOBA_TOPK)) | (blk == qblk)
        bias_t = jnp.where(allowed, 0.0, NEG_BIG)
        pieces = [jnp.zeros((off, L), F32), bias_t, jnp.zeros((LANES - off - nb, L), F32)]
        pad_t = jnp.concatenate([t for t in pieces if t.shape[0]], axis=0)
        q_aug = jnp.where(own, q * c, pad_t.T).astype(BF16)
        k_aug = jnp.where(own, k, (key_lane - off == key_blk).astype(F32)).astype(BF16)
        v_aug = jnp.where(own, v, 1.0).astype(BF16)

        for qb in range(nb + 1):
            if qb == 0:
                scores(q_aug, k_aug, 0)
            if qb + 1 < nb:
                scores(q_aug, k_aug, qb + 1)
            if qb > 0:
                weighted_values(head, v_aug, qb - 1)
            if qb < nb:
                scan_chunk(head * nb + qb)
                softmax(qb)


def _col_block(off, width):
    assert off % width == 0
    return off // width


def _moba_prompt(proj, cols, W, batch, L, q_sample, cache_kt, page_table):
    M = proj.shape[0]
    B, H, d, T = q_sample.shape
    n_pages = page_table.shape[1]
    n_hp = W // LANES
    n_blocks = n_pages * PAGE_SIZE // MOBA_BLOCK
    assert L % MOBA_BLOCK == 0 and W % LANES == 0 and LANES == 2 * HEAD_DIM
    assert batch * n_hp == B, "one sample sequence is scanned per grid step"
    ch = n_pages // (2 * (L // MOBA_BLOCK))

    def in_spec(name):
        c0 = _col_block(cols[name], LANES)
        return pl.BlockSpec((L, LANES), lambda b, hp, pt: (b, c0 + hp))

    seq4 = lambda b, hp, pt: (b * n_hp + hp, 0, 0, 0)
    spec_t = pl.BlockSpec((1, LANES, L), lambda b, hp, pt: (b, hp, 0))
    kv_t = jax.ShapeDtypeStruct((batch, W, L), F32)
    grid_spec = pltpu.PrefetchScalarGridSpec(
        num_scalar_prefetch=1,
        grid=(batch, n_hp),
        in_specs=[in_spec("q"), in_spec("k"), in_spec("v"),
                  pl.BlockSpec((1, H, d, T), seq4), pl.BlockSpec(memory_space=pl.ANY)],
        out_specs=[pl.BlockSpec((L, LANES), lambda b, hp, pt: (b, hp)), spec_t, spec_t,
                   pl.BlockSpec((1, H, n_blocks), lambda b, hp, pt: (b * n_hp + hp, 0, 0))],
        scratch_shapes=[pltpu.VMEM((2, MOBA_BLOCK, L), F32), pltpu.VMEM((2, MOBA_BLOCK, L), BF16),
                        pltpu.VMEM((L, LANES), F32),
                        pltpu.VMEM((SCAN_RING, ch, H, d, T), F32),
                        pltpu.SemaphoreType.DMA((SCAN_RING,))],
    )
    return pl.pallas_call(
        functools.partial(_moba_prompt_kernel, L=L, scale=HEAD_DIM ** -0.5, n_pages=n_pages),
        grid_spec=grid_spec,
        out_shape=[jax.ShapeDtypeStruct((M, W), BF16), kv_t, kv_t,
                   jax.ShapeDtypeStruct((B, H, n_blocks), F32)],
        compiler_params=_cparams(("arbitrary", "arbitrary")),
        name="moba_prompt",
    )(page_table, proj, proj, proj, q_sample, cache_kt)


def _ssd_prompt_kernel(*refs, d_inner, n_x, n_z, n_w):
    xbc_refs, z_refs = refs[:n_x], refs[n_x:n_x + n_z]
    rest = refs[n_x + n_z:]
    (dt_ref, cw_ref, cb_ref, dtb_r_ref, dtb_c_ref, alog_r_ref, alog_c_ref, dskip_ref,
     norm_ref) = rest[:9]
    w_in_refs = rest[9:9 + n_w]
    u_ref, conv_ref, st_ref = rest[9 + n_w:12 + n_w]
    w_out_refs = rest[12 + n_w:12 + 2 * n_w]
    xs_scr = rest[12 + 2 * n_w]
    for w_in_ref, w_out_ref in zip(w_in_refs, w_out_refs):
        w_out_ref[...] = w_in_ref[...].astype(BF16)
    bw = xbc_refs[0].shape[1]
    Q = SSD_CHUNK
    P = SSM_HEADDIM
    N = D_STATE
    G = N_SSM_GROUPS
    R = d_inner // P // G
    assert R % 2 == 0 and 2 * P == LANES and N == LANES
    c = pl.program_id(1)
    tail = CONV_WIDTH - 1

    @pl.when(c == 0)
    def _():
        xs_scr[...] = jnp.zeros(xs_scr.shape, F32)
        st_ref[...] = jnp.zeros(st_ref.shape, F32)

    x = jnp.concatenate([r[...] for r in xbc_refs], axis=1)
    prev = xs_scr[...]
    row8 = lax.broadcasted_iota(jnp.int32, (SUBLANES, 1), 0)

    def shifted(k):
        xk = pltpu.roll(x, k, axis=0)
        head = jnp.where(row8 < k, pltpu.roll(prev, k, axis=0), xk[0:SUBLANES])
        return jnp.concatenate([head, xk[SUBLANES:]], axis=0)

    acc = cw_ref[0:1, :] * shifted(tail)
    for i in range(1, tail):
        acc = acc + cw_ref[i:i + 1, :] * shifted(tail - i)
    acc = acc + cw_ref[tail:tail + 1, :] * x
    xc = _silu(cb_ref[...] + acc)
    xs_scr[...] = x[Q - SUBLANES:Q, :]
    conv_ref[0] = x[Q - tail:Q, :]

    raw = dt_ref[...]
    dt = _softplus(raw + dtb_r_ref[...])
    dt_t = _softplus(raw.T + dtb_c_ref[...])
    a = dt * (-jnp.exp(alog_r_ref[...]))
    a_t = dt_t * (-jnp.exp(alog_c_ref[...]))
    ri = lax.broadcasted_iota(jnp.int32, (Q, Q), 0)
    ci = lax.broadcasted_iota(jnp.int32, (Q, Q), 1)
    causal = ci <= ri
    acum = jnp.dot(causal.astype(F32), a, precision=lax.Precision.HIGHEST,
                   preferred_element_type=F32)
    acum_t = jnp.dot(a_t, (ri <= ci).astype(F32), precision=lax.Precision.HIGHEST,
                     preferred_element_type=F32)
    e_acum = jnp.exp(acum)
    d_end = jnp.exp(acum[Q - 1:Q, :] - acum)
    lane = lax.broadcasted_iota(jnp.int32, (1, LANES), 1)
    first = lane < P

    def pair_cols(t, h0):
        return jnp.where(first, t[:, h0:h0 + 1], t[:, h0 + 1:h0 + 2])

    y_parts = []
    for g in range(G):
        b_g = xc[:, d_inner + g * N:d_inner + (g + 1) * N].astype(BF16)
        c_g = xc[:, d_inner + G * N + g * N:d_inner + G * N + (g + 1) * N].astype(BF16)
        cb = lax.dot_general(c_g, b_g, _NT, preferred_element_type=F32)
        st_g = st_ref[0, g * R:(g + 1) * R].reshape(R * P, N)
        y_off = lax.dot_general(c_g, st_g.astype(BF16), _NT, preferred_element_type=F32)
        xdtd_parts, cd_parts = [], []
        for pr in range(R // 2):
            h0 = g * R + 2 * pr
            x_p = xc[:, h0 * P:h0 * P + LANES]
            xdt = x_p * pair_cols(dt, h0)
            xdt_b = xdt.astype(BF16)
            yd = []
            for hh in (h0, h0 + 1):
                seg = acum[:, hh:hh + 1] - acum_t[hh:hh + 1, :]
                dec = jnp.exp(jnp.where(causal, seg, -jnp.inf))
                yd.append(jnp.dot((cb * dec).astype(BF16), xdt_b, preferred_element_type=F32))
                cd_parts.append(jnp.broadcast_to(jnp.exp(acum_t[hh:hh + 1, Q - 1:Q]), (P, N)))
            y_p = (jnp.where(first, yd[0], yd[1])
                   + y_off[:, 2 * pr * P:2 * pr * P + LANES] * pair_cols(e_acum, h0)
                   + dskip_ref[:, h0 * P:h0 * P + LANES] * x_p)
            y_parts.append(y_p)
            xdtd_parts.append((xdt * pair_cols(d_end, h0)).astype(BF16))
        xdtd = jnp.concatenate(xdtd_parts, axis=1)
        s_new = lax.dot_general(xdtd, b_g, _TN, preferred_element_type=F32)
        st_new = jnp.concatenate(cd_parts, axis=0) * st_g + s_new
        st_ref[0, g * R:(g + 1) * R] = st_new.reshape(R, P, N)

    y = jnp.concatenate(y_parts, axis=1)
    u = y * _silu(jnp.concatenate([r[...] for r in z_refs], axis=1))
    gw = d_inner // G
    u_parts = []
    for g in range(G):
        ug = u[:, g * gw:(g + 1) * gw]
        u_parts.append(ug * lax.rsqrt(jnp.mean(ug * ug, axis=-1, keepdims=True) + EPS))
    u_ref[...] = (jnp.concatenate(u_parts, axis=1) * norm_ref[...]).astype(u_ref.dtype)


def _ssd_prompt(proj, cols, conv_w, conv_b, dtb_r, dtb_c, alog_r, alog_c, dskip, norm, batch, L,
                cast_weights=()):
    M = proj.shape[0]
    conv_dim = conv_w.shape[1]
    d_inner = norm.shape[1]
    n_heads = d_inner // SSM_HEADDIM
    assert L % SSD_CHUNK == 0
    nc = L // SSD_CHUNK
    Q = SSD_CHUNK
    tail = CONV_WIDTH - 1
    row = lambda b, c: (b * nc + c, 0)
    const = lambda b, c: (0, 0)

    def in_spec(off, width):
        c0 = _col_block(off, width)
        return pl.BlockSpec((Q, width), lambda b, c: (b * nc + c, c0))

    bw = math.gcd(cols["xbc"], cols["z"], conv_dim, d_inner)
    assert bw % LANES == 0
    n_x, n_z = conv_dim // bw, d_inner // bw

    def slab_spec(w):
        n_steps = batch * nc
        assert w.shape[0] % n_steps == 0
        r = w.shape[0] // n_steps
        g = next(g for g in (1, 2, 4, 8, 16) if (r * g) % (2 * SUBLANES) == 0 and n_steps % g == 0)
        return pl.BlockSpec((r * g, w.shape[1]), lambda b, c: ((b * nc + c) // g, 0))

    w_specs = [slab_spec(w) for w in cast_weights]
    return pl.pallas_call(
        functools.partial(_ssd_prompt_kernel, d_inner=d_inner, n_x=n_x, n_z=n_z,
                          n_w=len(cast_weights)),
        grid=(batch, nc),
        in_specs=[in_spec(cols["xbc"] + i * bw, bw) for i in range(n_x)]
        + [in_spec(cols["z"] + i * bw, bw) for i in range(n_z)]
        + [
            in_spec(cols["dt"], LANES),
            pl.BlockSpec((CONV_WIDTH, conv_dim), const),
            pl.BlockSpec((1, conv_dim), const),
            pl.BlockSpec((1, LANES), const),
            pl.BlockSpec((LANES, 1), const),
            pl.BlockSpec((1, LANES), const),
            pl.BlockSpec((LANES, 1), const),
            pl.BlockSpec((1, d_inner), const),
            pl.BlockSpec((1, d_inner), const),
        ] + w_specs,
        out_specs=[
            pl.BlockSpec((Q, d_inner), row),
            pl.BlockSpec((1, tail, conv_dim), lambda b, c: (b, 0, 0)),
            pl.BlockSpec((1, n_heads, SSM_HEADDIM, D_STATE), lambda b, c: (b, 0, 0, 0)),
        ] + w_specs,
        out_shape=[
            jax.ShapeDtypeStruct((M, d_inner), BF16),
            jax.ShapeDtypeStruct((batch, tail, conv_dim), F32),
            jax.ShapeDtypeStruct((batch, n_heads, SSM_HEADDIM, D_STATE), F32),
        ] + [jax.ShapeDtypeStruct(w.shape, BF16) for w in cast_weights],
        scratch_shapes=[pltpu.VMEM((SUBLANES, conv_dim), F32)],
        compiler_params=_cparams(("arbitrary", "arbitrary")),
        name="ssd_prompt",
    )(*([proj] * (n_x + n_z + 1)), conv_w, conv_b, dtb_r, dtb_c, alog_r, alog_c, dskip, norm,
      *cast_weights)


def _ssd_step_kernel(xbc_ref, z_ref, dt_ref, cprev_ref, sprev_ref, cw_ref, cb_ref, dtb_ref,
                     alog_ref, dskip_ref, norm_ref, u_ref, conv_ref, st_ref, *, d_inner):
    P = SSM_HEADDIM
    N = D_STATE
    G = N_SSM_GROUPS
    n_heads = d_inner // P
    R = n_heads // G
    assert R % 2 == 0 and 2 * P == LANES and N == LANES
    tail = CONV_WIDTH - 1
    x_new = xbc_ref[0]
    prev = cprev_ref[0]
    acc = cw_ref[0:1, :] * prev[0:1, :]
    for i in range(1, tail):
        acc = acc + cw_ref[i:i + 1, :] * prev[i:i + 1, :]
    acc = acc + cw_ref[tail:tail + 1, :] * x_new
    xc = _silu(cb_ref[...] + acc)
    conv_ref[0, 0:tail - 1, :] = prev[1:tail, :]
    conv_ref[0, tail - 1:tail, :] = x_new

    dt = _softplus(dt_ref[0] + dtb_ref[...])
    decay = jnp.exp(dt * (-jnp.exp(alog_ref[...])))
    first_rows = lax.broadcasted_iota(jnp.int32, (LANES, 1), 0) < P
    first = lax.broadcasted_iota(jnp.int32, (1, LANES), 1) < P
    pad_rows = lambda row: jnp.broadcast_to(row, (SUBLANES, row.shape[1])).astype(BF16)

    y_parts = []
    for hp in range(n_heads // 2):
        h0 = 2 * hp
        g = h0 // R
        b_g = xc[:, d_inner + g * N:d_inner + (g + 1) * N]
        c_g = xc[:, d_inner + G * N + g * N:d_inner + G * N + (g + 1) * N]
        x_row = xc[:, h0 * P:h0 * P + LANES]
        dt_row = jnp.where(first, dt[:, h0:h0 + 1], dt[:, h0 + 1:h0 + 2])
        dec_col = jnp.where(first_rows, decay[:, h0:h0 + 1], decay[:, h0 + 1:h0 + 2])
        st = sprev_ref[0, h0:h0 + 2].reshape(2 * P, N)
        upd = lax.dot_general(pad_rows(dt_row * x_row), pad_rows(b_g), _TN,
                              preferred_element_type=F32) * (1.0 / SUBLANES)
        st_new = dec_col * st + upd
        st_ref[0, h0:h0 + 2] = st_new.reshape(2, P, N)
        y_row = lax.dot_general(pad_rows(c_g), st_new.astype(BF16), _NT,
                                preferred_element_type=F32)[0:1]
        y_parts.append(y_row + dskip_ref[:, h0 * P:h0 * P + LANES] * x_row)
    y = jnp.concatenate(y_parts, axis=1)
    u = y * _silu(z_ref[0])
    gw = d_inner // G
    u_parts = []
    for g in range(G):
        ug = u[:, g * gw:(g + 1) * gw]
        u_parts.append(ug * lax.rsqrt(jnp.mean(ug * ug, axis=-1, keepdims=True) + EPS))
    u_ref[0] = (jnp.concatenate(u_parts, axis=1) * norm_ref[...]).astype(u_ref.dtype)


def _ssd_step(xbc, z, dt_raw, conv_prev, ssm_prev, conv_w, conv_b, dtb_r, alog_r, dskip, norm):
    B, conv_dim = xbc.shape
    d_inner = z.shape[1]
    n_heads = d_inner // SSM_HEADDIM
    tail = CONV_WIDTH - 1
    const = lambda b: (0, 0)
    b3 = lambda b: (b, 0, 0)
    b4 = lambda b: (b, 0, 0, 0)
    u, conv_new, ssm_new = pl.pallas_call(
        functools.partial(_ssd_step_kernel, d_inner=d_inner),
        grid=(B,),
        in_specs=[
            pl.BlockSpec((1, 1, conv_dim), b3),
            pl.BlockSpec((1, 1, d_inner), b3),
            pl.BlockSpec((1, 1, LANES), b3),
            pl.BlockSpec((1, tail, conv_dim), b3),
            pl.BlockSpec((1, n_heads, SSM_HEADDIM, D_STATE), b4),
            pl.BlockSpec((CONV_WIDTH, conv_dim), const),
            pl.BlockSpec((1, conv_dim), const),
            pl.BlockSpec((1, LANES), const),
            pl.BlockSpec((1, LANES), const),
            pl.BlockSpec((1, d_inner), const),
            pl.BlockSpec((1, d_inner), const),
        ],
        out_specs=[
            pl.BlockSpec((1, 1, d_inner), b3),
            pl.BlockSpec((1, tail, conv_dim), b3),
            pl.BlockSpec((1, n_heads, SSM_HEADDIM, D_STATE), b4),
        ],
        out_shape=[
            jax.ShapeDtypeStruct((B, 1, d_inner), BF16),
            jax.ShapeDtypeStruct((B, tail, conv_dim), F32),
            jax.ShapeDtypeStruct((B, n_heads, SSM_HEADDIM, D_STATE), F32),
        ],
        compiler_params=_cparams(("arbitrary",)),
        name="ssd_step",
    )(xbc.reshape(B, 1, conv_dim), z.reshape(B, 1, d_inner), dt_raw.reshape(B, 1, LANES),
      conv_prev, ssm_prev, conv_w, conv_b, dtb_r, alog_r, dskip, norm)
    return u.reshape(B, d_inner), conv_new, ssm_new


def _moba_topk_kernel(g_ref, idx_ref, *, n_sel):
    gate = g_ref[...]
    n_blocks = gate.shape[2]
    blk = lax.broadcasted_iota(jnp.int32, gate.shape, 2)
    rank = jnp.zeros(gate.shape, jnp.int32)
    for m in range(n_blocks):
        gm = gate[:, :, m:m + 1]
        beats = (gm > gate) | ((gm == gate) & (m < blk))
        rank = rank + beats.astype(jnp.int32)
    for r in range(n_sel):
        idx_ref[:, r] = jnp.sum(jnp.where(rank == r, blk, 0), axis=2, keepdims=True)


def _moba_topk(gates, n_sel):
    B, H, n_blocks = gates.shape
    return pl.pallas_call(
        functools.partial(_moba_topk_kernel, n_sel=n_sel),
        grid=(1,),
        in_specs=[pl.BlockSpec((B, H, n_blocks), lambda b: (0, 0, 0))],
        out_specs=pl.BlockSpec((B, n_sel, H, 1), lambda b: (0, 0, 0, 0)),
        out_shape=jax.ShapeDtypeStruct((B, n_sel, H, 1), jnp.int32),
        compiler_params=_cparams(("arbitrary",)),
        name="moba_topk",
    )(gates)


def _moba_sample_kernel(pt_ref, idx_ref, q_ref, kn_ref, vn_ref, ck_ref, cv_ref, o_ref,
                        kbuf, vbuf, sem, *, n_sel, n_heads, scale):
    b = pl.program_id(0)
    n_b = pl.num_programs(0)
    ppb = MOBA_BLOCK // PAGE_SIZE

    def copies(bb, slot):
        out = []
        for h in range(n_heads):
            for r in range(n_sel):
                blk = idx_ref[bb, r * n_heads + h]
                for t in range(ppb):
                    page = pt_ref[bb, blk * ppb + t]
                    j = r * ppb + t
                    out.append(pltpu.make_async_copy(ck_ref.at[page, h], kbuf.at[slot, h, j], sem.at[0, slot]))
                    out.append(pltpu.make_async_copy(cv_ref.at[page, h], vbuf.at[slot, h, j], sem.at[1, slot]))
        return out

    slot = b % 2

    @pl.when(b == 0)
    def _():
        for cp in copies(b, slot):
            cp.start()

    @pl.when(b + 1 < n_b)
    def _():
        for cp in copies(b + 1, 1 - slot):
            cp.start()

    for cp in copies(b, slot):
        cp.wait()

    def body(h, carry):
        q_row = q_ref[0, h]
        qc = _row_to_col(q_row)
        kh = kbuf[slot, h]
        vh = vbuf[slot, h]
        s = jnp.sum(kh * qc, axis=1, keepdims=True) * scale
        s_own = jnp.sum(kn_ref[0, h] * q_row, axis=-1, keepdims=True) * scale
        m = jnp.maximum(jnp.max(jnp.max(s, axis=0), axis=-1, keepdims=True), s_own)
        p = jnp.exp(s - m)
        p_own = jnp.exp(s_own - m)
        l = jnp.sum(jnp.sum(p, axis=0), axis=-1, keepdims=True) + p_own
        o_past = jnp.sum(jnp.sum(vh * p, axis=0), axis=-1, keepdims=True)
        o_ref[0, h] = (_col_to_row(o_past) + p_own * vn_ref[0, h]) / l
        return carry

    lax.fori_loop(0, n_heads, body, 0, unroll=True)


def _moba_sample(q4, k4, v4, cache_kt, cache_vt, page_table, idx):
    B, H, _, d = q4.shape
    n_sel = idx.shape[1] // H
    n_slabs = n_sel * (MOBA_BLOCK // PAGE_SIZE)
    new = pl.BlockSpec((1, H, 1, d), lambda b, pt, ix: (b, 0, 0, 0))
    grid_spec = pltpu.PrefetchScalarGridSpec(
        num_scalar_prefetch=2,
        grid=(B,),
        in_specs=[new, new, new, pl.BlockSpec(memory_space=pl.ANY), pl.BlockSpec(memory_space=pl.ANY)],
        out_specs=new,
        scratch_shapes=[pltpu.VMEM((2, H, n_slabs, d, PAGE_SIZE), F32),
                        pltpu.VMEM((2, H, n_slabs, d, PAGE_SIZE), F32),
                        pltpu.SemaphoreType.DMA((2, 2))],
    )
    return pl.pallas_call(
        functools.partial(_moba_sample_kernel, n_sel=n_sel, n_heads=H, scale=d ** -0.5),
        grid_spec=grid_spec,
        out_shape=jax.ShapeDtypeStruct((B, H, 1, d), F32),
        compiler_params=_cparams(("arbitrary",)),
        name="moba_sample",
    )(page_table, idx, q4, k4, v4, cache_kt, cache_vt)


def _merge_kernel(*refs, n_g):
    attn_ref, ssm_ref = refs[:2]
    ga_refs, gs_refs = refs[2:2 + n_g], refs[2 + n_g:2 + 2 * n_g]
    x_ref, wa_ref, ws_ref, wo_ref, npost_ref, npre_ref, x1_ref, h2_ref = refs[2 + 2 * n_g:]
    a = jnp.dot(attn_ref[...], wa_ref[...], preferred_element_type=F32)
    s = jnp.dot(ssm_ref[...], ws_ref[...], preferred_element_type=F32)
    ga = jnp.concatenate([r[...] for r in ga_refs], axis=1)
    gs = jnp.concatenate([r[...] for r in gs_refs], axis=1)
    merged = jax.nn.sigmoid(ga) * a + jax.nn.sigmoid(gs) * s
    o = jnp.dot(merged.astype(BF16), wo_ref[...], preferred_element_type=F32)
    x1 = x_ref[...] + _rms(o, npost_ref[...])
    x1_ref[...] = x1
    h2_ref[...] = _rms(x1, npre_ref[...]).astype(BF16)


def _merge(attn, ssm, proj, cols, x, wa, ws, wo, npost, npre, tm):
    M, D = x.shape
    rows = lambda w: pl.BlockSpec((tm, w), lambda i: (i, 0))
    whole = lambda a: pl.BlockSpec(a.shape, lambda i: (0, 0), pipeline_mode=pl.Buffered(1))

    bw = math.gcd(cols["ga"], cols["gs"], D)
    assert bw % LANES == 0
    n_g = D // bw

    def gate_specs(name):
        c0 = _col_block(cols[name], bw)
        return [pl.BlockSpec((tm, bw), lambda i, c=c0 + k: (i, c)) for k in range(n_g)]

    return pl.pallas_call(
        functools.partial(_merge_kernel, n_g=n_g),
        grid=(M // tm,),
        in_specs=[rows(attn.shape[1]), rows(ssm.shape[1])] + gate_specs("ga") + gate_specs("gs")
        + [rows(D), whole(wa), whole(ws), whole(wo), whole(npost), whole(npre)],
        out_specs=[rows(D), rows(D)],
        out_shape=[jax.ShapeDtypeStruct((M, D), F32), jax.ShapeDtypeStruct((M, D), BF16)],
        compiler_params=_cparams(("arbitrary",)),
        name="merge",
    )(attn, ssm, *([proj] * (2 * n_g)), x, wa, ws, wo, npost, npre)


def _ffn_kernel(h_ref, x_hbm, wg_ref, wu_ref, wd_ref, npost_ref, y_ref, x_buf, x_sem):
    i = pl.program_id(0)
    f = pl.program_id(1)
    last = pl.num_programs(1) - 1
    tm = y_ref.shape[0]
    rc = x_buf.shape[1]

    def x_copy(c):
        return pltpu.make_async_copy(x_hbm.at[pl.ds(i * tm + c * rc, rc), :], x_buf.at[c % 2],
                                     x_sem.at[c % 2])

    @pl.when(f == last)
    def _():
        x_copy(0).start()

    h = h_ref[...]
    act = _silu(jnp.dot(h, wg_ref[...], preferred_element_type=F32)) * \
        jnp.dot(h, wu_ref[...], preferred_element_type=F32)

    @pl.when(f == 0)
    def _():
        y_ref[...] = jnp.zeros(y_ref.shape, F32)

    y_ref[...] += jnp.dot(act.astype(BF16), wd_ref[...], preferred_element_type=F32)

    @pl.when(f == last)
    def _():
        for c in range(tm // rc):
            if c + 1 < tm // rc:
                x_copy(c + 1).start()
            x_copy(c).wait()
            rows = slice(c * rc, (c + 1) * rc)
            y_ref[rows, :] = x_buf[c % 2] + _rms(y_ref[rows, :], npost_ref[...])


def _ffn(h2, x1, w_g, w_u, w_d, npost, tm):
    M, D = x1.shape
    d_ff = w_g.shape[1]
    tf = FFN_TF
    rc = min(tm, FFN_RESIDUAL_ROWS)
    assert d_ff % tf == 0 and tm % rc == 0
    return pl.pallas_call(
        _ffn_kernel,
        grid=(M // tm, d_ff // tf),
        in_specs=[
            pl.BlockSpec((tm, D), lambda i, f: (i, 0)),
            pl.BlockSpec(memory_space=pl.ANY),
            pl.BlockSpec((D, tf), lambda i, f: (0, f)),
            pl.BlockSpec((D, tf), lambda i, f: (0, f)),
            pl.BlockSpec((tf, D), lambda i, f: (f, 0)),
            pl.BlockSpec((1, D), lambda i, f: (0, 0)),
        ],
        out_specs=pl.BlockSpec((tm, D), lambda i, f: (i, 0)),
        out_shape=jax.ShapeDtypeStruct((M, D), F32),
        scratch_shapes=[pltpu.VMEM((2, rc, D), F32), pltpu.SemaphoreType.DMA((2,))],
        compiler_params=_cparams(("arbitrary", "arbitrary")),
        name="ffn",
    )(h2, x1, w_g, w_u, w_d, npost)


def _pad_lanes(v):
    row = jnp.zeros((1, LANES), F32).at[0, :v.shape[0]].set(v.astype(F32))
    return row, row.reshape(LANES, 1)


def kernel(x_prompt, x_sample, cache_k, cache_v, state_conv, state_ssm, page_table, norm_mix_pre, w_in, conv_w, conv_b, dt_bias, a_log, d_skip, ssm_norm, w_attn_out, w_ssm_out, w_out, norm_mix_post, norm_ffn_pre, w_gate, w_up, w_down, norm_ffn_post):
    depth = w_in.shape[0]
    assert depth == 1, "single trunk layer"
    b_p, seq, d_model = x_prompt.shape
    b_s, dec_seq, _ = x_sample.shape
    assert dec_seq == 1
    n_heads_attn = cache_k.shape[3]
    attn_w = n_heads_attn * cache_k.shape[4]
    assert cache_k.shape[4] == HEAD_DIM and cache_k.shape[2] == PAGE_SIZE
    conv_dim = conv_w.shape[2]
    n_ssm_heads = dt_bias.shape[1]
    d_inner = n_ssm_heads * SSM_HEADDIM
    assert conv_dim == d_inner + 2 * N_SSM_GROUPS * D_STATE and n_ssm_heads <= LANES
    n_pages = page_table.shape[1]
    assert (n_pages * PAGE_SIZE) % MOBA_BLOCK == 0
    n_sel = min(MOBA_TOPK, n_pages * PAGE_SIZE // MOBA_BLOCK)
    assert n_sel == MOBA_TOPK

    l = 0
    src, off = {}, 0
    for name, w in (("q", attn_w), ("k", attn_w), ("v", attn_w), ("z", d_inner), ("xbc", conv_dim),
                    ("dt", n_ssm_heads), ("ga", d_model), ("gs", d_model)):
        src[name] = (off, w)
        off += w
    assert off == w_in.shape[2]
    w_t = jnp.transpose(w_in[l])
    tn_f = PROJ_TN_F32
    cols = {name: src[name][0] for name in ("q", "k", "v", "z", "xbc")}
    segments, tile, off = [(0, 0)], src["dt"][0] // tn_f, src["dt"][0]
    assert off % tn_f == 0 and d_model % tn_f == 0
    for name in ("ga", "gs", "dt"):
        segments.append((tile, src[name][0]))
        cols[name] = off
        tile += -(-src[name][1] // tn_f)
        off = tile * tn_f
    assert n_ssm_heads <= LANES <= tn_f and src["dt"][0] + tn_f <= w_t.shape[0] and off % PROJ_TN == 0
    g_pre = norm_mix_pre[l].reshape(1, d_model)
    g_post = norm_mix_post[l].reshape(1, d_model)
    g_fpre = norm_ffn_pre[l].reshape(1, d_model)
    g_fpost = norm_ffn_post[l].reshape(1, d_model)
    cw = conv_w[l]
    cb = conv_b[l].reshape(1, conv_dim)
    dtb_r, dtb_c = _pad_lanes(dt_bias[l])
    alog_r, alog_c = _pad_lanes(a_log[l])
    dskip = jnp.repeat(d_skip[l].astype(F32), SSM_HEADDIM).reshape(1, d_inner)
    snorm = ssm_norm[l].reshape(1, d_inner)


    xs = x_sample.reshape(b_s, d_model)
    proj_s, w_t_b = _in_proj(xs, g_pre, w_t, tm=b_s, tn=tn_f, segments=tuple(segments),
                             pad=(tile - 1, n_ssm_heads))
    group = lambda name, w: proj_s[:, cols[name]:cols[name] + w]
    q_s, k_s, v_s = group("q", attn_w), group("k", attn_w), group("v", attn_w)
    z_s, xbc_s, dt_s = group("z", d_inner), group("xbc", conv_dim), group("dt", LANES)
    hd = (b_s, n_heads_attn, 1, HEAD_DIM)
    q4, k4, v4 = q_s.reshape(hd), k_s.reshape(hd), v_s.reshape(hd)
    q_bcast = jnp.broadcast_to(q_s.reshape(b_s, n_heads_attn, HEAD_DIM, 1),
                               (b_s, n_heads_attn, HEAD_DIM, PAGE_SIZE))
    ck = jnp.transpose(cache_k.reshape(cache_k.shape[1:]), (0, 2, 3, 1))
    cv = jnp.transpose(cache_v.reshape(cache_v.shape[1:]), (0, 2, 3, 1))

    m_p = b_p * seq
    xp = x_prompt.reshape(m_p, d_model)
    proj = _in_proj(xp, g_pre, w_t_b, tm=min(PROJ_TM, m_p), tn=PROJ_TN)
    attn, kt, vt, gates = _moba_prompt(proj, cols, attn_w, b_p, seq, q_bcast, ck, page_table)
    u, conv_p, ssm_p, wa, ws, wo, w_g, w_u, w_d = _ssd_prompt(
        proj, cols, cw, cb, dtb_r, dtb_c, alog_r, alog_c, dskip, snorm, b_p, seq,
        cast_weights=(w_attn_out[l], w_ssm_out[l], w_out[l], w_gate[l], w_up[l], w_down[l]))

    idx = _moba_topk(gates, n_sel)
    attn_s = _moba_sample(q4, k4, v4, ck, cv, page_table, idx.reshape(b_s, n_sel * n_heads_attn))
    u_s, conv_s, ssm_s = _ssd_step(xbc_s, z_s, dt_s, state_conv[l], state_ssm[l], cw, cb,
                                   dtb_r, alog_r, dskip, snorm)
    x1_s, h2_s = _merge(attn_s.reshape(b_s, attn_w).astype(BF16), u_s, proj_s, cols, xs,
                        wa, ws, wo, g_post, g_fpre, b_s)
    y_s = _ffn(h2_s, x1_s, w_g, w_u, w_d, g_fpost, b_s)

    x1_p, h2_p = _merge(attn, u, proj, cols, xp, wa, ws, wo, g_post, g_fpre, min(MERGE_TM, m_p))
    y_p = _ffn(h2_p, x1_p, w_g, w_u, w_d, g_fpost, min(FFN_TM, m_p))

    def kv_prompt(t):
        return jnp.transpose(t.reshape(b_p, n_heads_attn, HEAD_DIM, seq), (0, 3, 1, 2))[None]

    kv_s = (1, b_s, 1, n_heads_attn, HEAD_DIM)
    return (y_p.reshape(b_p, seq, d_model), y_s.reshape(b_s, 1, d_model),
            kv_prompt(kt), kv_prompt(vt), conv_p[None], ssm_p[None],
            k_s.reshape(kv_s), v_s.reshape(kv_s), conv_s[None], ssm_s[None])
```

```python
import functools
import math

import jax
import jax.numpy as jnp
from jax import lax
from jax.experimental import pallas as pl
from jax.experimental.pallas import tpu as pltpu

F32 = jnp.float32
BF16 = jnp.bfloat16

EPS = 1e-6
HEAD_DIM = 64
MOBA_BLOCK = 256
MOBA_TOPK = 3
PAGE_SIZE = 128
SSM_HEADDIM = 64
N_SSM_GROUPS = 8
D_STATE = 128
CONV_WIDTH = 4
SSD_CHUNK = 128

LANES = 128
SUBLANES = 8
VMEM_LIMIT = 56 * 1024 * 1024
PROJ_TM = 1024
MERGE_TM = 256
FFN_TM = 1024
PROJ_TN = 1536
PROJ_TN_F32 = 512
FFN_TF = 512
FFN_RESIDUAL_ROWS = 128

NEG_BIG = -(2.0 ** 100)
ROW_GROUP = 16
SCAN_RING = 8

_NT = (((1,), (1,)), ((), ()))
_TN = (((0,), (0,)), ((), ()))


def _cparams(sem):
    return pltpu.CompilerParams(dimension_semantics=sem, vmem_limit_bytes=VMEM_LIMIT)


def _rms(x, g):
    return x * lax.rsqrt(jnp.mean(x * x, axis=-1, keepdims=True) + EPS) * g


def _silu(x):
    return x * jax.nn.sigmoid(x)


def _softplus(x):
    return jnp.maximum(x, 0.0) + jnp.log1p(jnp.exp(-jnp.abs(x)))


def _eye(n):
    return lax.broadcasted_iota(jnp.int32, (n, n), 0) == lax.broadcasted_iota(jnp.int32, (n, n), 1)


def _row_to_col(row):
    n = row.shape[1]
    return jnp.sum(jnp.where(_eye(n), jnp.broadcast_to(row, (n, n)), 0.0), axis=1, keepdims=True)


def _col_to_row(col):
    n = col.shape[0]
    return jnp.sum(jnp.where(_eye(n), jnp.broadcast_to(col, (n, n)), 0.0), axis=0, keepdims=True)


def _in_proj_kernel(x_ref, g_ref, w_ref, o_ref, *rest, pad):
    h_scr = rest[-1]
    j = pl.program_id(1)

    @pl.when(j == 0)
    def _():
        h_scr[...] = _rms(x_ref[...], g_ref[...]).astype(BF16)

    w = w_ref[...].astype(BF16)
    if pad is not None:
        rows = lax.broadcasted_iota(jnp.int32, (w.shape[0], 1), 0)
        w = jnp.where((j == pad[0]) & (rows >= pad[1]), jnp.zeros_like(w), w)
    if len(rest) == 2:
        rest[0][...] = w
    o_ref[...] = lax.dot_general(h_scr[...], w, _NT, preferred_element_type=F32)


def _in_proj(x, g, w_t, tm, tn, segments=None, pad=None):
    M, D = x.shape
    if segments is None:
        assert w_t.shape[0] % tn == 0
        n_tiles = w_t.shape[0] // tn
        w_spec = pl.BlockSpec((tn, D), lambda i, j: (j, 0))
    else:
        n_tiles = pad[0] + 1
        assert M == tm, "one row tile: every weight tile is visited, and copied, once"

        assert all(first_row % SUBLANES == 0 for _, first_row in segments) and tn % SUBLANES == 0

        def row_offset(i, j):
            off = 0
            for first_tile, first_row in segments:
                off = jnp.where(j >= first_tile, first_row + (j - first_tile) * tn, off)
            return pl.multiple_of(off, SUBLANES), 0

        w_spec = pl.BlockSpec((pl.Element(tn), pl.Element(D)), row_offset)
    out_specs = [pl.BlockSpec((tm, tn), lambda i, j: (i, j))]
    out_shape = [jax.ShapeDtypeStruct((M, n_tiles * tn), F32)]
    if segments is not None:
        out_specs.append(pl.BlockSpec((tn, D), lambda i, j: (j, 0)))
        out_shape.append(jax.ShapeDtypeStruct((n_tiles * tn, D), BF16))
    outs = pl.pallas_call(
        functools.partial(_in_proj_kernel, pad=pad),
        grid=(M // tm, n_tiles),
        in_specs=[pl.BlockSpec((tm, D), lambda i, j: (i, 0)),
                  pl.BlockSpec((1, D), lambda i, j: (0, 0)),
                  w_spec],
        out_specs=out_specs,
        out_shape=out_shape,
        scratch_shapes=[pltpu.VMEM((tm, D), BF16)],
        compiler_params=_cparams(("arbitrary", "arbitrary")),
        name="in_proj",
    )(x, g, w_t)
    return outs if segments is not None else outs[0]


def _moba_prompt_kernel(pt_ref, q_ref, k_ref, v_ref, qs_ref, ck_ref, o_ref, kt_ref, vt_ref, g_ref,
                        s_scr, p_scr, o_scr, scan_buf, scan_sem, *, L, scale, n_pages):
    BS = MOBA_BLOCK
    nb = L // BS
    assert nb <= HEAD_DIM
    n_sub = 2 * nb
    ppb = BS // PAGE_SIZE
    ch = n_pages // n_sub
    n_buf = scan_buf.shape[0]
    assert n_pages % n_sub == 0 and ch % ppb == 0 and n_sub % n_buf == 0
    step = pl.program_id(0) * pl.num_programs(1) + pl.program_id(1)
    n_steps = pl.num_programs(0) * pl.num_programs(1)
    _, n_h, d_h, T = qs_ref.shape

    def scan_start(j):
        seq, jj = (step, j) if j < n_sub else (step + 1, j - n_sub)

        def go():
            for t in range(ch):
                pltpu.make_async_copy(ck_ref.at[pt_ref[seq, jj * ch + t]], scan_buf.at[j % n_buf, t],
                                      scan_sem.at[j % n_buf]).start()

        if j < n_sub:
            go()
        else:
            pl.when(step + 1 < n_steps)(go)

    def scan_chunk(j):
        slot = j % n_buf
        scan_start(j + n_buf - 1)
        for t in range(ch):
            pltpu.make_async_copy(ck_ref.at[pt_ref[step, j * ch + t]], scan_buf.at[slot, t],
                                  scan_sem.at[slot]).wait()
        for h in range(n_h):
            qh = qs_ref[0, h]
            for u in range(ch // ppb):
                pages = scan_buf[slot, u * ppb, h]
                for t in range(1, ppb):
                    pages = pages + scan_buf[slot, u * ppb + t, h]
                part = jnp.sum((pages * qh).reshape(d_h // SUBLANES, SUBLANES, T), axis=0)
                n = j * (ch // ppb) + u
                g_ref[0, h:h + 1, n:n + 1] = jnp.sum(part, keepdims=True) * (1.0 / BS)

    @pl.when(step == 0)
    def _():
        for j in range(n_buf - 1):
            scan_start(j)

    q = q_ref[...]
    k = k_ref[...]
    v = v_ref[...]
    kt_ref[0] = k.T
    vt_ref[0] = v.T
    lane = lax.broadcasted_iota(jnp.int32, (1, LANES), 1)
    first = lane < HEAD_DIM

    kmean = jnp.concatenate(
        [jnp.mean(k[n * BS:(n + 1) * BS], axis=0, keepdims=True) for n in range(nb)], axis=0)

    blk = lax.broadcasted_iota(jnp.int32, (nb, L), 0)
    qblk = lax.broadcasted_iota(jnp.int32, (nb, L), 1) // BS
    past = blk < qblk
    row = lax.broadcasted_iota(jnp.int32, (BS, BS), 0)
    col = lax.broadcasted_iota(jnp.int32, (BS, BS), 1)
    causal_bias = jnp.where(col <= row, 0.0, -jnp.inf)
    key_blk = lax.broadcasted_iota(jnp.int32, (L, LANES), 0) // BS
    key_lane = lax.broadcasted_iota(jnp.int32, (L, LANES), 1)
    c = scale * math.log2(math.e)

    def scores(q_aug, k_aug, qb):
        qs = slice(qb * BS, (qb + 1) * BS)
        n_past = qb * BS
        if qb > 0:
            s_scr[qb % 2, :, :n_past] = lax.dot_general(q_aug[qs], k_aug[:n_past], _NT,
                                                        preferred_element_type=F32)
        s_scr[qb % 2, :, n_past:n_past + BS] = lax.dot_general(
            q_aug[qs], k_aug[n_past:n_past + BS], _NT, preferred_element_type=F32) + causal_bias

    def softmax(qb):
        nk = (qb + 1) * BS
        sb = qb % 2

        def rows(r, carry):
            rs = pl.ds(pl.multiple_of(r * ROW_GROUP, ROW_GROUP), ROW_GROUP)
            s = s_scr[sb, rs, :nk]
            p_scr[sb, rs, :nk] = jnp.exp2(s - jnp.max(s, axis=1, keepdims=True)).astype(BF16)
            return carry

        lax.fori_loop(0, BS // ROW_GROUP, rows, 0, unroll=True)

    def weighted_values(head, v_aug, qb):
        qs = slice(qb * BS, (qb + 1) * BS)
        nk = (qb + 1) * BS
        o = jnp.dot(p_scr[qb % 2, :, :nk], v_aug[:nk], preferred_element_type=F32)
        o = o / pltpu.roll(o, HEAD_DIM, axis=1)
        if head == 0:
            o_scr[qs, :] = o
        else:
            o_ref[qs, :] = jnp.where(first, o_scr[qs, :], o).astype(o_ref.dtype)

    for head, (own, off) in enumerate(((first, HEAD_DIM), (~first, 0))):
        gate = lax.dot_general(jnp.where(own, kmean, 0.0), q, _NT,
                               precision=lax.Precision.HIGHEST, preferred_element_type=F32)
        gate = jnp.where(past, gate, -jnp.inf)
        rank = jnp.zeros((nb, L), jnp.int32)
        for m in range(nb):
            gm = gate[m:m + 1, :]
            beats = (gm > gate) | ((gm == gate) & (m < blk))
            rank = rank + beats.astype(jnp.int32)
        allowed = (past & (rank < MOBA_TOPK)) | (blk == qblk)
        bias_t = jnp.where(allowed, 0.0, NEG_BIG)
        pieces = [jnp.zeros((off, L), F32), bias_t, jnp.zeros((LANES - off - nb, L), F32)]
        pad_t = jnp.concatenate([t for t in pieces if t.shape[0]], axis=0)
        q_aug = jnp.where(own, q * c, pad_t.T).astype(BF16)
        k_aug = jnp.where(own, k, (key_lane - off == key_blk).astype(F32)).astype(BF16)
        v_aug = jnp.where(own, v, 1.0).astype(BF16)

        for qb in range(nb + 1):
            if qb == 0:
                scores(q_aug, k_aug, 0)
            if qb + 1 < nb:
                scores(q_aug, k_aug, qb + 1)
            if qb > 0:
                weighted_values(head, v_aug, qb - 1)
            if qb < nb:
                scan_chunk(head * nb + qb)
                softmax(qb)


def _col_block(off, width):
    assert off % width == 0
    return off // width


def _moba_prompt(proj, cols, W, batch, L, q_sample, cache_kt, page_table):
    M = proj.shape[0]
    B, H, d, T = q_sample.shape
    n_pages = page_table.shape[1]
    n_hp = W // LANES
    n_blocks = n_pages * PAGE_SIZE // MOBA_BLOCK
    assert L % MOBA_BLOCK == 0 and W % LANES == 0 and LANES == 2 * HEAD_DIM
    assert batch * n_hp == B, "one sample sequence is scanned per grid step"
    ch = n_pages // (2 * (L // MOBA_BLOCK))

    def in_spec(name):
        c0 = _col_block(cols[name], LANES)
        return pl.BlockSpec((L, LANES), lambda b, hp, pt: (b, c0 + hp))

    seq4 = lambda b, hp, pt: (b * n_hp + hp, 0, 0, 0)
    spec_t = pl.BlockSpec((1, LANES, L), lambda b, hp, pt: (b, hp, 0))
    kv_t = jax.ShapeDtypeStruct((batch, W, L), F32)
    grid_spec = pltpu.PrefetchScalarGridSpec(
        num_scalar_prefetch=1,
        grid=(batch, n_hp),
        in_specs=[in_spec("q"), in_spec("k"), in_spec("v"),
                  pl.BlockSpec((1, H, d, T), seq4), pl.BlockSpec(memory_space=pl.ANY)],
        out_specs=[pl.BlockSpec((L, LANES), lambda b, hp, pt: (b, hp)), spec_t, spec_t,
                   pl.BlockSpec((1, H, n_blocks), lambda b, hp, pt: (b * n_hp + hp, 0, 0))],
        scratch_shapes=[pltpu.VMEM((2, MOBA_BLOCK, L), F32), pltpu.VMEM((2, MOBA_BLOCK, L), BF16),
                        pltpu.VMEM((L, LANES), F32),
                        pltpu.VMEM((SCAN_RING, ch, H, d, T), F32),
                        pltpu.SemaphoreType.DMA((SCAN_RING,))],
    )
    return pl.pallas_call(
        functools.partial(_moba_prompt_kernel, L=L, scale=HEAD_DIM ** -0.5, n_pages=n_pages),
        grid_spec=grid_spec,
        out_shape=[jax.ShapeDtypeStruct((M, W), BF16), kv_t, kv_t,
                   jax.ShapeDtypeStruct((B, H, n_blocks), F32)],
        compiler_params=_cparams(("arbitrary", "arbitrary")),
        name="moba_prompt",
    )(page_table, proj, proj, proj, q_sample, cache_kt)


def _ssd_prompt_kernel(*refs, d_inner, n_x, n_z, n_w):
    xbc_refs, z_refs = refs[:n_x], refs[n_x:n_x + n_z]
    rest = refs[n_x + n_z:]
    (dt_ref, cw_ref, cb_ref, dtb_r_ref, dtb_c_ref, alog_r_ref, alog_c_ref, dskip_ref,
     norm_ref) = rest[:9]
    w_in_refs = rest[9:9 + n_w]
    u_ref, conv_ref, st_ref = rest[9 + n_w:12 + n_w]
    w_out_refs = rest[12 + n_w:12 + 2 * n_w]
    xs_scr = rest[12 + 2 * n_w]
    for w_in_ref, w_out_ref in zip(w_in_refs, w_out_refs):
        w_out_ref[...] = w_in_ref[...].astype(BF16)
    bw = xbc_refs[0].shape[1]
    Q = SSD_CHUNK
    P = SSM_HEADDIM
    N = D_STATE
    G = N_SSM_GROUPS
    R = d_inner // P // G
    assert R % 2 == 0 and 2 * P == LANES and N == LANES
    c = pl.program_id(1)
    tail = CONV_WIDTH - 1

    @pl.when(c == 0)
    def _():
        xs_scr[...] = jnp.zeros(xs_scr.shape, F32)
        st_ref[...] = jnp.zeros(st_ref.shape, F32)

    x = jnp.concatenate([r[...] for r in xbc_refs], axis=1)
    prev = xs_scr[...]
    row8 = lax.broadcasted_iota(jnp.int32, (SUBLANES, 1), 0)

    def shifted(k):
        xk = pltpu.roll(x, k, axis=0)
        head = jnp.where(row8 < k, pltpu.roll(prev, k, axis=0), xk[0:SUBLANES])
        return jnp.concatenate([head, xk[SUBLANES:]], axis=0)

    acc = cw_ref[0:1, :] * shifted(tail)
    for i in range(1, tail):
        acc = acc + cw_ref[i:i + 1, :] * shifted(tail - i)
    acc = acc + cw_ref[tail:tail + 1, :] * x
    xc = _silu(cb_ref[...] + acc)
    xs_scr[...] = x[Q - SUBLANES:Q, :]
    conv_ref[0] = x[Q - tail:Q, :]

    raw = dt_ref[...]
    dt = _softplus(raw + dtb_r_ref[...])
    dt_t = _softplus(raw.T + dtb_c_ref[...])
    a = dt * (-jnp.exp(alog_r_ref[...]))
    a_t = dt_t * (-jnp.exp(alog_c_ref[...]))
    ri = lax.broadcasted_iota(jnp.int32, (Q, Q), 0)
    ci = lax.broadcasted_iota(jnp.int32, (Q, Q), 1)
    causal = ci <= ri
    acum = jnp.dot(causal.astype(F32), a, precision=lax.Precision.HIGHEST,
                   preferred_element_type=F32)
    acum_t = jnp.dot(a_t, (ri <= ci).astype(F32), precision=lax.Precision.HIGHEST,
                     preferred_element_type=F32)
    e_acum = jnp.exp(acum)
    d_end = jnp.exp(acum[Q - 1:Q, :] - acum)
    lane = lax.broadcasted_iota(jnp.int32, (1, LANES), 1)
    first = lane < P

    def pair_cols(t, h0):
        return jnp.where(first, t[:, h0:h0 + 1], t[:, h0 + 1:h0 + 2])

    y_parts = []
    for g in range(G):
        b_g = xc[:, d_inner + g * N:d_inner + (g + 1) * N].astype(BF16)
        c_g = xc[:, d_inner + G * N + g * N:d_inner + G * N + (g + 1) * N].astype(BF16)
        cb = lax.dot_general(c_g, b_g, _NT, preferred_element_type=F32)
        st_g = st_ref[0, g * R:(g + 1) * R].reshape(R * P, N)
        y_off = lax.dot_general(c_g, st_g.astype(BF16), _NT, preferred_element_type=F32)
        xdtd_parts, cd_parts = [], []
        for pr in range(R // 2):
            h0 = g * R + 2 * pr
            x_p = xc[:, h0 * P:h0 * P + LANES]
            xdt = x_p * pair_cols(dt, h0)
            xdt_b = xdt.astype(BF16)
            yd = []
            for hh in (h0, h0 + 1):
                seg = acum[:, hh:hh + 1] - acum_t[hh:hh + 1, :]
                dec = jnp.exp(jnp.where(causal, seg, -jnp.inf))
                yd.append(jnp.dot((cb * dec).astype(BF16), xdt_b, preferred_element_type=F32))
                cd_parts.append(jnp.broadcast_to(jnp.exp(acum_t[hh:hh + 1, Q - 1:Q]), (P, N)))
            y_p = (jnp.where(first, yd[0], yd[1])
                   + y_off[:, 2 * pr * P:2 * pr * P + LANES] * pair_cols(e_acum, h0)
                   + dskip_ref[:, h0 * P:h0 * P + LANES] * x_p)
            y_parts.append(y_p)
            xdtd_parts.append((xdt * pair_cols(d_end, h0)).astype(BF16))
        xdtd = jnp.concatenate(xdtd_parts, axis=1)
        s_new = lax.dot_general(xdtd, b_g, _TN, preferred_element_type=F32)
        st_new = jnp.concatenate(cd_parts, axis=0) * st_g + s_new
        st_ref[0, g * R:(g + 1) * R] = st_new.reshape(R, P, N)

    y = jnp.concatenate(y_parts, axis=1)
    u = y * _silu(jnp.concatenate([r[...] for r in z_refs], axis=1))
    gw = d_inner // G
    u_parts = []
    for g in range(G):
        ug = u[:, g * gw:(g + 1) * gw]
        u_parts.append(ug * lax.rsqrt(jnp.mean(ug * ug, axis=-1, keepdims=True) + EPS))
    u_ref[...] = (jnp.concatenate(u_parts, axis=1) * norm_ref[...]).astype(u_ref.dtype)


def _ssd_prompt(proj, cols, conv_w, conv_b, dtb_r, dtb_c, alog_r, alog_c, dskip, norm, batch, L,
                cast_weights=()):
    M = proj.shape[0]
    conv_dim = conv_w.shape[1]
    d_inner = norm.shape[1]
    n_heads = d_inner // SSM_HEADDIM
    assert L % SSD_CHUNK == 0
    nc = L // SSD_CHUNK
    Q = SSD_CHUNK
    tail = CONV_WIDTH - 1
    row = lambda b, c: (b * nc + c, 0)
    const = lambda b, c: (0, 0)

    def in_spec(off, width):
        c0 = _col_block(off, width)
        return pl.BlockSpec((Q, width), lambda b, c: (b * nc + c, c0))

    bw = math.gcd(cols["xbc"], cols["z"], conv_dim, d_inner)
    assert bw % LANES == 0
    n_x, n_z = conv_dim // bw, d_inner // bw

    def slab_spec(w):
        n_steps = batch * nc
        assert w.shape[0] % n_steps == 0
        r = w.shape[0] // n_steps
        g = next(g for g in (1, 2, 4, 8, 16) if (r * g) % (2 * SUBLANES) == 0 and n_steps % g == 0)
        return pl.BlockSpec((r * g, w.shape[1]), lambda b, c: ((b * nc + c) // g, 0))

    w_specs = [slab_spec(w) for w in cast_weights]
    return pl.pallas_call(
        functools.partial(_ssd_prompt_kernel, d_inner=d_inner, n_x=n_x, n_z=n_z,
                          n_w=len(cast_weights)),
        grid=(batch, nc),
        in_specs=[in_spec(cols["xbc"] + i * bw, bw) for i in range(n_x)]
        + [in_spec(cols["z"] + i * bw, bw) for i in range(n_z)]
        + [
            in_spec(cols["dt"], LANES),
            pl.BlockSpec((CONV_WIDTH, conv_dim), const),
            pl.BlockSpec((1, conv_dim), const),
            pl.BlockSpec((1, LANES), const),
            pl.BlockSpec((LANES, 1), const),
            pl.BlockSpec((1, LANES), const),
            pl.BlockSpec((LANES, 1), const),
            pl.BlockSpec((1, d_inner), const),
            pl.BlockSpec((1, d_inner), const),
        ] + w_specs,
        out_specs=[
            pl.BlockSpec((Q, d_inner), row),
            pl.BlockSpec((1, tail, conv_dim), lambda b, c: (b, 0, 0)),
            pl.BlockSpec((1, n_heads, SSM_HEADDIM, D_STATE), lambda b, c: (b, 0, 0, 0)),
        ] + w_specs,
        out_shape=[
            jax.ShapeDtypeStruct((M, d_inner), BF16),
            jax.ShapeDtypeStruct((batch, tail, conv_dim), F32),
            jax.ShapeDtypeStruct((batch, n_heads, SSM_HEADDIM, D_STATE), F32),
        ] + [jax.ShapeDtypeStruct(w.shape, BF16) for w in cast_weights],
        scratch_shapes=[pltpu.VMEM((SUBLANES, conv_dim), F32)],
        compiler_params=_cparams(("arbitrary", "arbitrary")),
        name="ssd_prompt",
    )(*([proj] * (n_x + n_z + 1)), conv_w, conv_b, dtb_r, dtb_c, alog_r, alog_c, dskip, norm,
      *cast_weights)


def _ssd_step_kernel(xbc_ref, z_ref, dt_ref, cprev_ref, sprev_ref, cw_ref, cb_ref, dtb_ref,
                     alog_ref, dskip_ref, norm_ref, u_ref, conv_ref, st_ref, *, d_inner):
    P = SSM_HEADDIM
    N = D_STATE
    G = N_SSM_GROUPS
    n_heads = d_inner // P
    R = n_heads // G
    assert R % 2 == 0 and 2 * P == LANES and N == LANES
    tail = CONV_WIDTH - 1
    x_new = xbc_ref[0]
    prev = cprev_ref[0]
    acc = cw_ref[0:1, :] * prev[0:1, :]
    for i in range(1, tail):
        acc = acc + cw_ref[i:i + 1, :] * prev[i:i + 1, :]
    acc = acc + cw_ref[tail:tail + 1, :] * x_new
    xc = _silu(cb_ref[...] + acc)
    conv_ref[0, 0:tail - 1, :] = prev[1:tail, :]
    conv_ref[0, tail - 1:tail, :] = x_new

    dt = _softplus(dt_ref[0] + dtb_ref[...])
    decay = jnp.exp(dt * (-jnp.exp(alog_ref[...])))
    first_rows = lax.broadcasted_iota(jnp.int32, (LANES, 1), 0) < P
    first = lax.broadcasted_iota(jnp.int32, (1, LANES), 1) < P
    pad_rows = lambda row: jnp.broadcast_to(row, (SUBLANES, row.shape[1])).astype(BF16)

    y_parts = []
    for hp in range(n_heads // 2):
        h0 = 2 * hp
        g = h0 // R
        b_g = xc[:, d_inner + g * N:d_inner + (g + 1) * N]
        c_g = xc[:, d_inner + G * N + g * N:d_inner + G * N + (g + 1) * N]
        x_row = xc[:, h0 * P:h0 * P + LANES]
        dt_row = jnp.where(first, dt[:, h0:h0 + 1], dt[:, h0 + 1:h0 + 2])
        dec_col = jnp.where(first_rows, decay[:, h0:h0 + 1], decay[:, h0 + 1:h0 + 2])
        st = sprev_ref[0, h0:h0 + 2].reshape(2 * P, N)
        upd = lax.dot_general(pad_rows(dt_row * x_row), pad_rows(b_g), _TN,
                              preferred_element_type=F32) * (1.0 / SUBLANES)
        st_new = dec_col * st + upd
        st_ref[0, h0:h0 + 2] = st_new.reshape(2, P, N)
        y_row = lax.dot_general(pad_rows(c_g), st_new.astype(BF16), _NT,
                                preferred_element_type=F32)[0:1]
        y_parts.append(y_row + dskip_ref[:, h0 * P:h0 * P + LANES] * x_row)
    y = jnp.concatenate(y_parts, axis=1)
    u = y * _silu(z_ref[0])
    gw = d_inner // G
    u_parts = []
    for g in range(G):
        ug = u[:, g * gw:(g + 1) * gw]
        u_parts.append(ug * lax.rsqrt(jnp.mean(ug * ug, axis=-1, keepdims=True) + EPS))
    u_ref[0] = (jnp.concatenate(u_parts, axis=1) * norm_ref[...]).astype(u_ref.dtype)


def _ssd_step(xbc, z, dt_raw, conv_prev, ssm_prev, conv_w, conv_b, dtb_r, alog_r, dskip, norm):
    B, conv_dim = xbc.shape
    d_inner = z.shape[1]
    n_heads = d_inner // SSM_HEADDIM
    tail = CONV_WIDTH - 1
    const = lambda b: (0, 0)
    b3 = lambda b: (b, 0, 0)
    b4 = lambda b: (b, 0, 0, 0)
    u, conv_new, ssm_new = pl.pallas_call(
        functools.partial(_ssd_step_kernel, d_inner=d_inner),
        grid=(B,),
        in_specs=[
            pl.BlockSpec((1, 1, conv_dim), b3),
            pl.BlockSpec((1, 1, d_inner), b3),
            pl.BlockSpec((1, 1, LANES), b3),
            pl.BlockSpec((1, tail, conv_dim), b3),
            pl.BlockSpec((1, n_heads, SSM_HEADDIM, D_STATE), b4),
            pl.BlockSpec((CONV_WIDTH, conv_dim), const),
            pl.BlockSpec((1, conv_dim), const),
            pl.BlockSpec((1, LANES), const),
            pl.BlockSpec((1, LANES), const),
            pl.BlockSpec((1, d_inner), const),
            pl.BlockSpec((1, d_inner), const),
        ],
        out_specs=[
            pl.BlockSpec((1, 1, d_inner), b3),
            pl.BlockSpec((1, tail, conv_dim), b3),
            pl.BlockSpec((1, n_heads, SSM_HEADDIM, D_STATE), b4),
        ],
        out_shape=[
            jax.ShapeDtypeStruct((B, 1, d_inner), BF16),
            jax.ShapeDtypeStruct((B, tail, conv_dim), F32),
            jax.ShapeDtypeStruct((B, n_heads, SSM_HEADDIM, D_STATE), F32),
        ],
        compiler_params=_cparams(("arbitrary",)),
        name="ssd_step",
    )(xbc.reshape(B, 1, conv_dim), z.reshape(B, 1, d_inner), dt_raw.reshape(B, 1, LANES),
      conv_prev, ssm_prev, conv_w, conv_b, dtb_r, alog_r, dskip, norm)
    return u.reshape(B, d_inner), conv_new, ssm_new


def _moba_topk_kernel(g_ref, idx_ref, *, n_sel):
    gate = g_ref[...]
    n_blocks = gate.shape[2]
    blk = lax.broadcasted_iota(jnp.int32, gate.shape, 2)
    rank = jnp.zeros(gate.shape, jnp.int32)
    for m in range(n_blocks):
        gm = gate[:, :, m:m + 1]
        beats = (gm > gate) | ((gm == gate) & (m < blk))
        rank = rank + beats.astype(jnp.int32)
    for r in range(n_sel):
        idx_ref[:, r] = jnp.sum(jnp.where(rank == r, blk, 0), axis=2, keepdims=True)


def _moba_topk(gates, n_sel):
    B, H, n_blocks = gates.shape
    return pl.pallas_call(
        functools.partial(_moba_topk_kernel, n_sel=n_sel),
        grid=(1,),
        in_specs=[pl.BlockSpec((B, H, n_blocks), lambda b: (0, 0, 0))],
        out_specs=pl.BlockSpec((B, n_sel, H, 1), lambda b: (0, 0, 0, 0)),
        out_shape=jax.ShapeDtypeStruct((B, n_sel, H, 1), jnp.int32),
        compiler_params=_cparams(("arbitrary",)),
        name="moba_topk",
    )(gates)


def _moba_sample_kernel(pt_ref, idx_ref, q_ref, kn_ref, vn_ref, ck_ref, cv_ref, o_ref,
                        kbuf, vbuf, sem, *, n_sel, n_heads, scale):
    b = pl.program_id(0)
    n_b = pl.num_programs(0)
    ppb = MOBA_BLOCK // PAGE_SIZE

    def copies(bb, slot):
        out = []
        for h in range(n_heads):
            for r in range(n_sel):
                blk = idx_ref[bb, r * n_heads + h]
                for t in range(ppb):
                    page = pt_ref[bb, blk * ppb + t]
                    j = r * ppb + t
                    out.append(pltpu.make_async_copy(ck_ref.at[page, h], kbuf.at[slot, h, j], sem.at[0, slot]))
                    out.append(pltpu.make_async_copy(cv_ref.at[page, h], vbuf.at[slot, h, j], sem.at[1, slot]))
        return out

    slot = b % 2

    def start_all(cps):
        for n, cp in enumerate(cps):
            cp.start(priority=n % 2)

    @pl.when(b == 0)
    def _():
        start_all(copies(b, slot))

    @pl.when(b + 1 < n_b)
    def _():
        start_all(copies(b + 1, 1 - slot))

    for cp in copies(b, slot):
        cp.wait()

    def body(h, carry):
        q_row = q_ref[0, h]
        qc = _row_to_col(q_row)
        kh = kbuf[slot, h]
        vh = vbuf[slot, h]
        s = jnp.sum(kh * qc, axis=1, keepdims=True) * scale
        s_own = jnp.sum(kn_ref[0, h] * q_row, axis=-1, keepdims=True) * scale
        m = jnp.maximum(jnp.max(jnp.max(s, axis=0), axis=-1, keepdims=True), s_own)
        p = jnp.exp(s - m)
        p_own = jnp.exp(s_own - m)
        l = jnp.sum(jnp.sum(p, axis=0), axis=-1, keepdims=True) + p_own
        o_past = jnp.sum(jnp.sum(vh * p, axis=0), axis=-1, keepdims=True)
        o_ref[0, h] = (_col_to_row(o_past) + p_own * vn_ref[0, h]) / l
        return carry

    lax.fori_loop(0, n_heads, body, 0, unroll=True)


def _moba_sample(q4, k4, v4, cache_kt, cache_vt, page_table, idx):
    B, H, _, d = q4.shape
    n_sel = idx.shape[1] // H
    n_slabs = n_sel * (MOBA_BLOCK // PAGE_SIZE)
    new = pl.BlockSpec((1, H, 1, d), lambda b, pt, ix: (b, 0, 0, 0))
    grid_spec = pltpu.PrefetchScalarGridSpec(
        num_scalar_prefetch=2,
        grid=(B,),
        in_specs=[new, new, new, pl.BlockSpec(memory_space=pl.ANY), pl.BlockSpec(memory_space=pl.ANY)],
        out_specs=new,
        scratch_shapes=[pltpu.VMEM((2, H, n_slabs, d, PAGE_SIZE), F32),
                        pltpu.VMEM((2, H, n_slabs, d, PAGE_SIZE), F32),
                        pltpu.SemaphoreType.DMA((2, 2))],
    )
    return pl.pallas_call(
        functools.partial(_moba_sample_kernel, n_sel=n_sel, n_heads=H, scale=d ** -0.5),
        grid_spec=grid_spec,
        out_shape=jax.ShapeDtypeStruct((B, H, 1, d), F32),
        compiler_params=_cparams(("arbitrary",)),
        name="moba_sample",
    )(page_table, idx, q4, k4, v4, cache_kt, cache_vt)


def _merge_kernel(*refs, n_g):
    attn_ref, ssm_ref = refs[:2]
    ga_refs, gs_refs = refs[2:2 + n_g], refs[2 + n_g:2 + 2 * n_g]
    x_ref, wa_ref, ws_ref, wo_ref, npost_ref, npre_ref, x1_ref, h2_ref = refs[2 + 2 * n_g:]
    a = jnp.dot(attn_ref[...], wa_ref[...], preferred_element_type=F32)
    s = jnp.dot(ssm_ref[...], ws_ref[...], preferred_element_type=F32)
    ga = jnp.concatenate([r[...] for r in ga_refs], axis=1)
    gs = jnp.concatenate([r[...] for r in gs_refs], axis=1)
    merged = jax.nn.sigmoid(ga) * a + jax.nn.sigmoid(gs) * s
    o = jnp.dot(merged.astype(BF16), wo_ref[...], preferred_element_type=F32)
    x1 = x_ref[...] + _rms(o, npost_ref[...])
    x1_ref[...] = x1
    h2_ref[...] = _rms(x1, npre_ref[...]).astype(BF16)


def _merge(attn, ssm, proj, cols, x, wa, ws, wo, npost, npre, tm):
    M, D = x.shape
    rows = lambda w: pl.BlockSpec((tm, w), lambda i: (i, 0))
    whole = lambda a: pl.BlockSpec(a.shape, lambda i: (0, 0), pipeline_mode=pl.Buffered(1))

    bw = math.gcd(cols["ga"], cols["gs"], D)
    assert bw % LANES == 0
    n_g = D // bw

    def gate_specs(name):
        c0 = _col_block(cols[name], bw)
        return [pl.BlockSpec((tm, bw), lambda i, c=c0 + k: (i, c)) for k in range(n_g)]

    return pl.pallas_call(
        functools.partial(_merge_kernel, n_g=n_g),
        grid=(M // tm,),
        in_specs=[rows(attn.shape[1]), rows(ssm.shape[1])] + gate_specs("ga") + gate_specs("gs")
        + [rows(D), whole(wa), whole(ws), whole(wo), whole(npost), whole(npre)],
        out_specs=[rows(D), rows(D)],
        out_shape=[jax.ShapeDtypeStruct((M, D), F32), jax.ShapeDtypeStruct((M, D), BF16)],
        compiler_params=_cparams(("arbitrary",)),
        name="merge",
    )(attn, ssm, *([proj] * (2 * n_g)), x, wa, ws, wo, npost, npre)


def _ffn_kernel(h_ref, x_hbm, wg_ref, wu_ref, wd_ref, npost_ref, y_ref, x_buf, x_sem):
    i = pl.program_id(0)
    f = pl.program_id(1)
    last = pl.num_programs(1) - 1
    tm = y_ref.shape[0]
    rc = x_buf.shape[1]

    def x_copy(c):
        return pltpu.make_async_copy(x_hbm.at[pl.ds(i * tm + c * rc, rc), :], x_buf.at[c % 2],
                                     x_sem.at[c % 2])

    @pl.when(f == last)
    def _():
        x_copy(0).start()

    h = h_ref[...]
    act = _silu(jnp.dot(h, wg_ref[...], preferred_element_type=F32)) * \
        jnp.dot(h, wu_ref[...], preferred_element_type=F32)

    @pl.when(f == 0)
    def _():
        y_ref[...] = jnp.zeros(y_ref.shape, F32)

    y_ref[...] += jnp.dot(act.astype(BF16), wd_ref[...], preferred_element_type=F32)

    @pl.when(f == last)
    def _():
        for c in range(tm // rc):
            if c + 1 < tm // rc:
                x_copy(c + 1).start()
            x_copy(c).wait()
            rows = slice(c * rc, (c + 1) * rc)
            y_ref[rows, :] = x_buf[c % 2] + _rms(y_ref[rows, :], npost_ref[...])


def _ffn(h2, x1, w_g, w_u, w_d, npost, tm):
    M, D = x1.shape
    d_ff = w_g.shape[1]
    tf = FFN_TF
    rc = min(tm, FFN_RESIDUAL_ROWS)
    assert d_ff % tf == 0 and tm % rc == 0
    return pl.pallas_call(
        _ffn_kernel,
        grid=(M // tm, d_ff // tf),
        in_specs=[
            pl.BlockSpec((tm, D), lambda i, f: (i, 0)),
            pl.BlockSpec(memory_space=pl.ANY),
            pl.BlockSpec((D, tf), lambda i, f: (0, f)),
            pl.BlockSpec((D, tf), lambda i, f: (0, f)),
            pl.BlockSpec((tf, D), lambda i, f: (f, 0)),
            pl.BlockSpec((1, D), lambda i, f: (0, 0)),
        ],
        out_specs=pl.BlockSpec((tm, D), lambda i, f: (i, 0)),
        out_shape=jax.ShapeDtypeStruct((M, D), F32),
        scratch_shapes=[pltpu.VMEM((2, rc, D), F32), pltpu.SemaphoreType.DMA((2,))],
        compiler_params=_cparams(("arbitrary", "arbitrary")),
        name="ffn",
    )(h2, x1, w_g, w_u, w_d, npost)


def _pad_lanes(v):
    row = jnp.zeros((1, LANES), F32).at[0, :v.shape[0]].set(v.astype(F32))
    return row, row.reshape(LANES, 1)


def kernel(x_prompt, x_sample, cache_k, cache_v, state_conv, state_ssm, page_table, norm_mix_pre, w_in, conv_w, conv_b, dt_bias, a_log, d_skip, ssm_norm, w_attn_out, w_ssm_out, w_out, norm_mix_post, norm_ffn_pre, w_gate, w_up, w_down, norm_ffn_post):
    depth = w_in.shape[0]
    assert depth == 1, "single trunk layer"
    b_p, seq, d_model = x_prompt.shape
    b_s, dec_seq, _ = x_sample.shape
    assert dec_seq == 1
    n_heads_attn = cache_k.shape[3]
    attn_w = n_heads_attn * cache_k.shape[4]
    assert cache_k.shape[4] == HEAD_DIM and cache_k.shape[2] == PAGE_SIZE
    conv_dim = conv_w.shape[2]
    n_ssm_heads = dt_bias.shape[1]
    d_inner = n_ssm_heads * SSM_HEADDIM
    assert conv_dim == d_inner + 2 * N_SSM_GROUPS * D_STATE and n_ssm_heads <= LANES
    n_pages = page_table.shape[1]
    assert (n_pages * PAGE_SIZE) % MOBA_BLOCK == 0
    n_sel = min(MOBA_TOPK, n_pages * PAGE_SIZE // MOBA_BLOCK)
    assert n_sel == MOBA_TOPK

    l = 0
    src, off = {}, 0
    for name, w in (("q", attn_w), ("k", attn_w), ("v", attn_w), ("z", d_inner), ("xbc", conv_dim),
                    ("dt", n_ssm_heads), ("ga", d_model), ("gs", d_model)):
        src[name] = (off, w)
        off += w
    assert off == w_in.shape[2]
    w_t = jnp.transpose(w_in[l])
    tn_f = PROJ_TN_F32
    cols = {name: src[name][0] for name in ("q", "k", "v", "z", "xbc")}
    segments, tile, off = [(0, 0)], src["dt"][0] // tn_f, src["dt"][0]
    assert off % tn_f == 0 and d_model % tn_f == 0
    for name in ("ga", "gs", "dt"):
        segments.append((tile, src[name][0]))
        cols[name] = off
        tile += -(-src[name][1] // tn_f)
        off = tile * tn_f
    assert n_ssm_heads <= LANES <= tn_f and src["dt"][0] + tn_f <= w_t.shape[0] and off % PROJ_TN == 0
    g_pre = norm_mix_pre[l].reshape(1, d_model)
    g_post = norm_mix_post[l].reshape(1, d_model)
    g_fpre = norm_ffn_pre[l].reshape(1, d_model)
    g_fpost = norm_ffn_post[l].reshape(1, d_model)
    cw = conv_w[l]
    cb = conv_b[l].reshape(1, conv_dim)
    dtb_r, dtb_c = _pad_lanes(dt_bias[l])
    alog_r, alog_c = _pad_lanes(a_log[l])
    dskip = jnp.repeat(d_skip[l].astype(F32), SSM_HEADDIM).reshape(1, d_inner)
    snorm = ssm_norm[l].reshape(1, d_inner)


    xs = x_sample.reshape(b_s, d_model)
    proj_s, w_t_b = _in_proj(xs, g_pre, w_t, tm=b_s, tn=tn_f, segments=tuple(segments),
                             pad=(tile - 1, n_ssm_heads))
    group = lambda name, w: proj_s[:, cols[name]:cols[name] + w]
    q_s, k_s, v_s = group("q", attn_w), group("k", attn_w), group("v", attn_w)
    z_s, xbc_s, dt_s = group("z", d_inner), group("xbc", conv_dim), group("dt", LANES)
    hd = (b_s, n_heads_attn, 1, HEAD_DIM)
    q4, k4, v4 = q_s.reshape(hd), k_s.reshape(hd), v_s.reshape(hd)
    q_bcast = jnp.broadcast_to(q_s.reshape(b_s, n_heads_attn, HEAD_DIM, 1),
                               (b_s, n_heads_attn, HEAD_DIM, PAGE_SIZE))
    ck = jnp.transpose(cache_k.reshape(cache_k.shape[1:]), (0, 2, 3, 1))
    cv = jnp.transpose(cache_v.reshape(cache_v.shape[1:]), (0, 2, 3, 1))

    m_p = b_p * seq
    xp = x_prompt.reshape(m_p, d_model)
    proj = _in_proj(xp, g_pre, w_t_b, tm=min(PROJ_TM, m_p), tn=PROJ_TN)
    attn, kt, vt, gates = _moba_prompt(proj, cols, attn_w, b_p, seq, q_bcast, ck, page_table)
    u, conv_p, ssm_p, wa, ws, wo, w_g, w_u, w_d = _ssd_prompt(
        proj, cols, cw, cb, dtb_r, dtb_c, alog_r, alog_c, dskip, snorm, b_p, seq,
        cast_weights=(w_attn_out[l], w_ssm_out[l], w_out[l], w_gate[l], w_up[l], w_down[l]))

    idx = _moba_topk(gates, n_sel)
    attn_s = _moba_sample(q4, k4, v4, ck, cv, page_table, idx.reshape(b_s, n_sel * n_heads_attn))
    u_s, conv_s, ssm_s = _ssd_step(xbc_s, z_s, dt_s, state_conv[l], state_ssm[l], cw, cb,
                                   dtb_r, alog_r, dskip, snorm)
    x1_s, h2_s = _merge(attn_s.reshape(b_s, attn_w).astype(BF16), u_s, proj_s, cols, xs,
                        wa, ws, wo, g_post, g_fpre, b_s)
    y_s = _ffn(h2_s, x1_s, w_g, w_u, w_d, g_fpost, b_s)

    x1_p, h2_p = _merge(attn, u, proj, cols, xp, wa, ws, wo, g_post, g_fpre, min(MERGE_TM, m_p))
    y_p = _ffn(h2_p, x1_p, w_g, w_u, w_d, g_fpost, min(FFN_TM, m_p))

    def kv_prompt(t):
        return jnp.transpose(t.reshape(b_p, n_heads_attn, HEAD_DIM, seq), (0, 3, 1, 2))[None]

    kv_s = (1, b_s, 1, n_heads_attn, HEAD_DIM)
    return (y_p.reshape(b_p, seq, d_model), y_s.reshape(b_s, 1, d_model),
            kv_prompt(kt), kv_prompt(vt), conv_p[None], ssm_p[None],
            k_s.reshape(kv_s), v_s.reshape(kv_s), conv_s[None], ssm_s[None])
```
